```python
import math
import jax, jax.numpy as jnp
from jax import lax
import numpy as np

D_MODEL = 2048
BATCH = 8
SEQ = 8192
DEPTH = 4

N_MIXERS = 3
N_CONV_LAYERS = (DEPTH + 2) // 3
N_ATTN_LAYERS = (DEPTH + 1) // 3
N_GDN_LAYERS = DEPTH // 3

N_META = 16
BLOCK = 128
FRONT = BLOCK
PAD_LEN = FRONT - N_META

NORM_EPS = 1e-6
LN_EPS = 1e-5
NEG_INF = -1e30

CONV_KERNEL = 31

HEAD_DIM = 64
N_HEADS = D_MODEL // HEAD_DIM
N_KV_HEADS = N_HEADS // 8
GROUP = N_HEADS // N_KV_HEADS
WINDOW = 128
ROPE_THETA = 10000.0
Q_WIDTH = N_HEADS * HEAD_DIM
KV_WIDTH = N_KV_HEADS * HEAD_DIM
QKV_WIDTH = Q_WIDTH + 2 * KV_WIDTH

GDN_K_DIM = 128
GDN_V_DIM = 128
GDN_K_HEADS = D_MODEL // 128
GDN_V_HEADS = 2 * GDN_K_HEADS
GDN_KEY_WIDTH = GDN_K_HEADS * GDN_K_DIM
GDN_VAL_WIDTH = GDN_V_HEADS * GDN_V_DIM
GDN_CONV_DIM = 2 * GDN_KEY_WIDTH + GDN_VAL_WIDTH
GDN_IN_WIDTH = GDN_CONV_DIM + GDN_VAL_WIDTH + 2 * GDN_V_HEADS
GDN_CONV = 4
GDN_CHUNK = 64

FFN_HIDDEN = ((8 * D_MODEL // 3 + 255) // 256) * 256

kernel_name = "hybrid_conformer_swa_sink_gdn_trunk"


def rms_norm(x, w):
    x32 = x.astype(jnp.float32)
    y = x32 * lax.rsqrt(jnp.mean(x32 * x32, axis=-1, keepdims=True) + NORM_EPS)
    return (y * w.astype(jnp.float32)).astype(x.dtype)


def layer_norm(x, g, b):
    x32 = x.astype(jnp.float32)
    mu = jnp.mean(x32, axis=-1, keepdims=True)
    xc = x32 - mu
    var = jnp.mean(xc * xc, axis=-1, keepdims=True)
    y = xc * lax.rsqrt(var + LN_EPS) * g.astype(jnp.float32) + b.astype(jnp.float32)
    return y.astype(x.dtype)


def l2_normalize(x):
    return x * lax.rsqrt(jnp.sum(x * x, axis=-1, keepdims=True) + 1e-6)


def causal_depthwise_conv(u, w):
    k = w.shape[0]
    return lax.conv_general_dilated(
        u, w[:, None, :].astype(u.dtype), window_strides=(1,), padding=[(k - 1, 0)],
        dimension_numbers=("NWC", "WIO", "NWC"), feature_group_count=u.shape[-1])


def swiglu_ffn(h, w_gate, w_up, w_down):
    return (jax.nn.silu(h @ w_gate) * (h @ w_up)) @ w_down


def conformer_conv(h, valid, w_pw1, b_pw1, w_dw, b_dw, ln_g, ln_b, w_pw2, b_pw2):
    u = h @ w_pw1 + b_pw1
    val, gate = jnp.split(u, 2, axis=-1)
    u = val * jax.nn.sigmoid(gate)
    u = jnp.where(valid[None, :, None], u, 0)
    u = causal_depthwise_conv(u, w_dw) + b_dw
    u = jax.nn.silu(layer_norm(u, ln_g, ln_b))
    return u @ w_pw2 + b_pw2


def apply_rope(x, cos, sin):
    x1, x2 = jnp.split(x, 2, axis=-1)
    c = cos[None, :, None, :].astype(x.dtype)
    s = sin[None, :, None, :].astype(x.dtype)
    return jnp.concatenate([x1 * c - x2 * s, x2 * c + x1 * s], axis=-1)


def sliding_window_sink_attention(h, cos, sin, w_qkv, b_qkv, sinks, w_o, b_o):
    bsz, length, _ = h.shape
    n_blocks = length // BLOCK
    qkv = h @ w_qkv + b_qkv
    q = qkv[..., :Q_WIDTH].reshape(bsz, length, N_HEADS, HEAD_DIM)
    k = qkv[..., Q_WIDTH:Q_WIDTH + KV_WIDTH].reshape(bsz, length, N_KV_HEADS, HEAD_DIM)
    v = qkv[..., Q_WIDTH + KV_WIDTH:].reshape(bsz, length, N_KV_HEADS, HEAD_DIM)
    q = apply_rope(q, cos, sin)
    k = apply_rope(k, cos, sin)
    q = q.reshape(bsz, n_blocks, BLOCK, N_KV_HEADS, GROUP, HEAD_DIM)
    k = k.reshape(bsz, n_blocks, BLOCK, N_KV_HEADS, HEAD_DIM)
    v = v.reshape(bsz, n_blocks, BLOCK, N_KV_HEADS, HEAD_DIM)

    def band(t):
        prev = jnp.pad(t, ((0, 0), (1, 0), (0, 0), (0, 0), (0, 0)))[:, :-1]
        return jnp.concatenate([prev, t], axis=2)

    kb, vb = band(k), band(v)
    scores = jnp.einsum("bnqkgd,bnskd->bnkgqs", q, kb).astype(jnp.float32) * (HEAD_DIM ** -0.5)
    blk = jnp.arange(n_blocks)
    q_idx = blk[:, None] * BLOCK + jnp.arange(BLOCK)[None, :]
    k_idx = (blk[:, None] - 1) * BLOCK + jnp.arange(2 * BLOCK)[None, :]
    dist = q_idx[:, :, None] - k_idx[:, None, :]
    allowed = (dist >= 0) & (dist < WINDOW) & (k_idx[:, None, :] >= PAD_LEN)
    scores = jnp.where(allowed[None, :, None, None], scores, NEG_INF)
    sink = sinks.astype(jnp.float32).reshape(N_KV_HEADS, GROUP)[None, None, :, :, None, None]
    m = jnp.maximum(jnp.max(scores, axis=-1, keepdims=True), sink)
    e = jnp.exp(scores - m)
    probs = e / (jnp.sum(e, axis=-1, keepdims=True) + jnp.exp(sink - m))
    out = jnp.einsum("bnkgqs,bnskd->bnqkgd", probs.astype(vb.dtype), vb)
    out = out.reshape(bsz, length, Q_WIDTH)
    return out @ w_o + b_o


def chunk_gated_delta_rule(q, k, v, g, beta):
    bsz, length, heads, dk = q.shape
    dv = v.shape[-1]
    n = length // GDN_CHUNK
    q = l2_normalize(q) * (dk ** -0.5)
    k = l2_normalize(k)

    def to_chunks(t):
        return t.reshape(bsz, n, GDN_CHUNK, heads, t.shape[-1]).transpose(1, 0, 3, 2, 4)

    q, k, v = to_chunks(q), to_chunks(k), to_chunks(v)
    g = g.reshape(bsz, n, GDN_CHUNK, heads).transpose(1, 0, 3, 2)
    beta = beta.reshape(bsz, n, GDN_CHUNK, heads).transpose(1, 0, 3, 2)
    gc = jnp.cumsum(g, axis=-1)
    idx = jnp.arange(GDN_CHUNK)
    causal = idx[:, None] >= idx[None, :]
    strict = idx[:, None] > idx[None, :]
    decay = jnp.exp(jnp.where(causal, gc[..., :, None] - gc[..., None, :], -jnp.inf))
    k_beta = k * beta[..., None]
    lower = jnp.where(strict, jnp.einsum("nbhid,nbhjd->nbhij", k_beta, k) * decay, 0.0)
    tri = jnp.eye(GDN_CHUNK, dtype=jnp.float32) + lower
    rhs = jnp.concatenate([v * beta[..., None], k_beta * jnp.exp(gc)[..., None]], axis=-1)
    sol = lax.linalg.triangular_solve(tri, rhs, left_side=True, lower=True)
    u, w = sol[..., :dv], sol[..., dv:]
    intra = jnp.einsum("nbhid,nbhjd->nbhij", q, k) * decay
    q_dec = q * jnp.exp(gc)[..., None]
    k_dec = k * jnp.exp(gc[..., -1:] - gc)[..., None]
    chunk_decay = jnp.exp(gc[..., -1])

    def step(state, xs):
        q_c, k_c, u_c, w_c, a_c, cd = xs
        v_new = u_c - jnp.einsum("bhck,bhkv->bhcv", w_c, state)
        out = (jnp.einsum("bhck,bhkv->bhcv", q_c, state)
               + jnp.einsum("bhcs,bhsv->bhcv", a_c, v_new))
        state = state * cd[..., None, None] + jnp.einsum("bhck,bhcv->bhkv", k_c, v_new)
        return state, out

    state0 = jnp.zeros((bsz, heads, dk, dv), jnp.float32)
    _, out = lax.scan(step, state0, (q_dec, k_dec, u, w, intra, chunk_decay))
    return out.transpose(1, 0, 3, 2, 4).reshape(bsz, length, heads, dv)


def gated_deltanet(h, valid, w_in, conv_w, a_log, dt_bias, norm_w, w_out):
    bsz, length, _ = h.shape
    proj = h @ w_in
    qkv = proj[..., :GDN_CONV_DIM]
    z = proj[..., GDN_CONV_DIM:GDN_CONV_DIM + GDN_VAL_WIDTH]
    b = proj[..., GDN_CONV_DIM + GDN_VAL_WIDTH:GDN_CONV_DIM + GDN_VAL_WIDTH + GDN_V_HEADS]
    a = proj[..., GDN_CONV_DIM + GDN_VAL_WIDTH + GDN_V_HEADS:]
    mask = valid[None, :, None]
    qkv = jnp.where(mask, qkv, 0)
    qkv = jax.nn.silu(causal_depthwise_conv(qkv, conv_w)).astype(jnp.float32)
    q = qkv[..., :GDN_KEY_WIDTH].reshape(bsz, length, GDN_K_HEADS, GDN_K_DIM)
    k = qkv[..., GDN_KEY_WIDTH:2 * GDN_KEY_WIDTH].reshape(bsz, length, GDN_K_HEADS, GDN_K_DIM)
    v = qkv[..., 2 * GDN_KEY_WIDTH:].reshape(bsz, length, GDN_V_HEADS, GDN_V_DIM)
    rep = GDN_V_HEADS // GDN_K_HEADS
    q = jnp.repeat(q, rep, axis=2)
    k = jnp.repeat(k, rep, axis=2)
    beta = jnp.where(mask, jax.nn.sigmoid(b.astype(jnp.float32)), 0.0)
    g = jnp.where(mask, -jnp.exp(a_log.astype(jnp.float32))
                  * jax.nn.softplus(a.astype(jnp.float32) + dt_bias.astype(jnp.float32)), 0.0)
    o = chunk_gated_delta_rule(q, k, v, g, beta)
    zg = jax.nn.silu(z.astype(jnp.float32).reshape(bsz, length, GDN_V_HEADS, GDN_V_DIM))
    o = rms_norm(o, norm_w) * zg
    return o.reshape(bsz, length, GDN_VAL_WIDTH).astype(h.dtype) @ w_out


def _fwd_setup_inputs(seed: int = 0) -> dict:
    key = jax.random.key(seed)
    ks = iter(jax.random.split(key, 40))
    f32 = jnp.float32
    D = D_MODEL

    def nrm(shape, scale):
        return jax.random.normal(next(ks), shape, f32) * scale

    def gain(shape):
        return 1.0 + nrm(shape, 0.02)

    dt = jnp.exp(jax.random.uniform(next(ks), (N_GDN_LAYERS, GDN_V_HEADS), f32,
                                    math.log(1e-3), math.log(1e-1)))
    return {
        "x": nrm((BATCH, SEQ, D), 1.0),
        "meta_tokens": nrm((N_META, D), 1.0),
        "norm_mix": gain((DEPTH, D)),
        "norm_ffn": gain((DEPTH, D)),
        "norm_final": gain((D,)),
        "conv_w_pw1": nrm((N_CONV_LAYERS, D, 2 * D), D ** -0.5),
        "conv_b_pw1": nrm((N_CONV_LAYERS, 2 * D), 0.01),
        "conv_w_dw": nrm((N_CONV_LAYERS, CONV_KERNEL, D), CONV_KERNEL ** -0.5),
        "conv_b_dw": nrm((N_CONV_LAYERS, D), 0.01),
        "conv_ln_g": gain((N_CONV_LAYERS, D)),
        "conv_ln_b": nrm((N_CONV_LAYERS, D), 0.01),
        "conv_w_pw2": nrm((N_CONV_LAYERS, D, D), D ** -0.5),
        "conv_b_pw2": nrm((N_CONV_LAYERS, D), 0.01),
        "attn_w_qkv": nrm((N_ATTN_LAYERS, D, QKV_WIDTH), D ** -0.5),
        "attn_b_qkv": nrm((N_ATTN_LAYERS, QKV_WIDTH), 0.01),
        "attn_sinks": nrm((N_ATTN_LAYERS, N_HEADS), 1.0),
        "attn_w_o": nrm((N_ATTN_LAYERS, Q_WIDTH, D), Q_WIDTH ** -0.5),
        "attn_b_o": nrm((N_ATTN_LAYERS, D), 0.01),
        "gdn_w_in": nrm((N_GDN_LAYERS, D, GDN_IN_WIDTH), D ** -0.5),
        "gdn_conv_w": nrm((N_GDN_LAYERS, GDN_CONV, GDN_CONV_DIM), GDN_CONV ** -0.5),
        "gdn_a_log": jnp.log(jax.random.uniform(next(ks), (N_GDN_LAYERS, GDN_V_HEADS), f32, 1.0, 16.0)),
        "gdn_dt_bias": dt + jnp.log(-jnp.expm1(-dt)),
        "gdn_norm_w": gain((N_GDN_LAYERS, GDN_V_DIM)),
        "gdn_w_out": nrm((N_GDN_LAYERS, GDN_VAL_WIDTH, D), GDN_VAL_WIDTH ** -0.5),
        "ffn_w_gate": nrm((DEPTH, D, FFN_HIDDEN), D ** -0.5),
        "ffn_w_up": nrm((DEPTH, D, FFN_HIDDEN), D ** -0.5),
        "ffn_w_down": nrm((DEPTH, FFN_HIDDEN, D), FFN_HIDDEN ** -0.5),
    }


def _fwd_reference(x, meta_tokens, norm_mix, norm_ffn, norm_final,
              conv_w_pw1, conv_b_pw1, conv_w_dw, conv_b_dw, conv_ln_g, conv_ln_b,
              conv_w_pw2, conv_b_pw2,
              attn_w_qkv, attn_b_qkv, attn_sinks, attn_w_o, attn_b_o,
              gdn_w_in, gdn_conv_w, gdn_a_log, gdn_dt_bias, gdn_norm_w, gdn_w_out,
              ffn_w_gate, ffn_w_up, ffn_w_down):
    bsz = x.shape[0]
    meta = jnp.broadcast_to(meta_tokens.astype(x.dtype)[None], (bsz, N_META, D_MODEL))
    h = jnp.concatenate([jnp.zeros((bsz, PAD_LEN, D_MODEL), x.dtype), meta, x], axis=1)
    length = h.shape[1]
    valid = jnp.arange(length) >= PAD_LEN
    pos = (jnp.arange(length) - PAD_LEN).astype(jnp.float32)
    inv_freq = ROPE_THETA ** (-jnp.arange(0, HEAD_DIM, 2, dtype=jnp.float32) / HEAD_DIM)
    ang = pos[:, None] * inv_freq[None, :]
    cos, sin = jnp.cos(ang), jnp.sin(ang)

    for i in range(DEPTH):
        kind, j = i % N_MIXERS, i // N_MIXERS
        hn = rms_norm(h, norm_mix[i])
        if kind == 0:
            mix = conformer_conv(hn, valid, conv_w_pw1[j], conv_b_pw1[j], conv_w_dw[j], conv_b_dw[j],
                                 conv_ln_g[j], conv_ln_b[j], conv_w_pw2[j], conv_b_pw2[j])
        elif kind == 1:
            mix = sliding_window_sink_attention(hn, cos, sin, attn_w_qkv[j], attn_b_qkv[j],
                                                attn_sinks[j], attn_w_o[j], attn_b_o[j])
        else:
            mix = gated_deltanet(hn, valid, gdn_w_in[j], gdn_conv_w[j], gdn_a_log[j],
                                 gdn_dt_bias[j], gdn_norm_w[j], gdn_w_out[j])
        h = h + mix.astype(h.dtype)
        h = h + swiglu_ffn(rms_norm(h, norm_ffn[i]), ffn_w_gate[i], ffn_w_up[i], ffn_w_down[i])

    return rms_norm(h, norm_final)[:, FRONT:]


import jax as _jax
import jax.numpy as _jnp

TWIN_FORMAT = 'train_step'
FWD_PARAMS = ['x', 'meta_tokens', 'norm_mix', 'norm_ffn', 'norm_final', 'conv_w_pw1', 'conv_b_pw1', 'conv_w_dw', 'conv_b_dw', 'conv_ln_g', 'conv_ln_b', 'conv_w_pw2', 'conv_b_pw2', 'attn_w_qkv', 'attn_b_qkv', 'attn_sinks', 'attn_w_o', 'attn_b_o', 'gdn_w_in', 'gdn_conv_w', 'gdn_a_log', 'gdn_dt_bias', 'gdn_norm_w', 'gdn_w_out', 'ffn_w_gate', 'ffn_w_up', 'ffn_w_down']
TWIN_WEIGHTS = ['meta_tokens', 'norm_mix', 'norm_ffn', 'norm_final', 'conv_w_pw1', 'conv_b_pw1', 'conv_w_dw', 'conv_b_dw', 'conv_ln_g', 'conv_ln_b', 'conv_w_pw2', 'conv_b_pw2', 'attn_w_qkv', 'attn_b_qkv', 'attn_sinks', 'attn_w_o', 'attn_b_o', 'gdn_w_in', 'gdn_conv_w', 'gdn_a_log', 'gdn_dt_bias', 'gdn_norm_w', 'gdn_w_out', 'ffn_w_gate', 'ffn_w_up', 'ffn_w_down']
TWIN_DIFF_INPUT = 'x'
TWIN_INPUTS = ['x', 'meta_tokens', 'norm_mix', 'norm_ffn', 'norm_final', 'conv_w_pw1', 'conv_b_pw1', 'conv_w_dw', 'conv_b_dw', 'conv_ln_g', 'conv_ln_b', 'conv_w_pw2', 'conv_b_pw2', 'attn_w_qkv', 'attn_b_qkv', 'attn_sinks', 'attn_w_o', 'attn_b_o', 'gdn_w_in', 'gdn_conv_w', 'gdn_a_log', 'gdn_dt_bias', 'gdn_norm_w', 'gdn_w_out', 'ffn_w_gate', 'ffn_w_up', 'ffn_w_down', 'loss_target', 'm_meta_tokens', 'm_norm_mix', 'm_norm_ffn', 'm_norm_final', 'm_conv_w_pw1', 'm_conv_b_pw1', 'm_conv_w_dw', 'm_conv_b_dw', 'm_conv_ln_g', 'm_conv_ln_b', 'm_conv_w_pw2', 'm_conv_b_pw2', 'm_attn_w_qkv', 'm_attn_b_qkv', 'm_attn_sinks', 'm_attn_w_o', 'm_attn_b_o', 'm_gdn_w_in', 'm_gdn_conv_w', 'm_gdn_a_log', 'm_gdn_dt_bias', 'm_gdn_norm_w', 'm_gdn_w_out', 'm_ffn_w_gate', 'm_ffn_w_up', 'm_ffn_w_down', 'v_meta_tokens', 'v_norm_mix', 'v_norm_ffn', 'v_norm_final', 'v_conv_w_pw1', 'v_conv_b_pw1', 'v_conv_w_dw', 'v_conv_b_dw', 'v_conv_ln_g', 'v_conv_ln_b', 'v_conv_w_pw2', 'v_conv_b_pw2', 'v_attn_w_qkv', 'v_attn_b_qkv', 'v_attn_sinks', 'v_attn_w_o', 'v_attn_b_o', 'v_gdn_w_in', 'v_gdn_conv_w', 'v_gdn_a_log', 'v_gdn_dt_bias', 'v_gdn_norm_w', 'v_gdn_w_out', 'v_ffn_w_gate', 'v_ffn_w_up', 'v_ffn_w_down']
TWIN_OUTPUTS = ['loss', 'grad_x', 'grad_meta_tokens', 'grad_norm_mix', 'grad_norm_ffn', 'grad_norm_final', 'grad_conv_w_pw1', 'grad_conv_b_pw1', 'grad_conv_w_dw', 'grad_conv_b_dw', 'grad_conv_ln_g', 'grad_conv_ln_b', 'grad_conv_w_pw2', 'grad_conv_b_pw2', 'grad_attn_w_qkv', 'grad_attn_b_qkv', 'grad_attn_sinks', 'grad_attn_w_o', 'grad_attn_b_o', 'grad_gdn_w_in', 'grad_gdn_conv_w', 'grad_gdn_a_log', 'grad_gdn_dt_bias', 'grad_gdn_norm_w', 'grad_gdn_w_out', 'grad_ffn_w_gate', 'grad_ffn_w_up', 'grad_ffn_w_down', 'delta_meta_tokens', 'delta_norm_mix', 'delta_norm_ffn', 'delta_norm_final', 'delta_conv_w_pw1', 'delta_conv_b_pw1', 'delta_conv_w_dw', 'delta_conv_b_dw', 'delta_conv_ln_g', 'delta_conv_ln_b', 'delta_conv_w_pw2', 'delta_conv_b_pw2', 'delta_attn_w_qkv', 'delta_attn_b_qkv', 'delta_attn_sinks', 'delta_attn_w_o', 'delta_attn_b_o', 'delta_gdn_w_in', 'delta_gdn_conv_w', 'delta_gdn_a_log', 'delta_gdn_dt_bias', 'delta_gdn_norm_w', 'delta_gdn_w_out', 'delta_ffn_w_gate', 'delta_ffn_w_up', 'delta_ffn_w_down', 'new_m_meta_tokens', 'new_m_norm_mix', 'new_m_norm_ffn', 'new_m_norm_final', 'new_m_conv_w_pw1', 'new_m_conv_b_pw1', 'new_m_conv_w_dw', 'new_m_conv_b_dw', 'new_m_conv_ln_g', 'new_m_conv_ln_b', 'new_m_conv_w_pw2', 'new_m_conv_b_pw2', 'new_m_attn_w_qkv', 'new_m_attn_b_qkv', 'new_m_attn_sinks', 'new_m_attn_w_o', 'new_m_attn_b_o', 'new_m_gdn_w_in', 'new_m_gdn_conv_w', 'new_m_gdn_a_log', 'new_m_gdn_dt_bias', 'new_m_gdn_norm_w', 'new_m_gdn_w_out', 'new_m_ffn_w_gate', 'new_m_ffn_w_up', 'new_m_ffn_w_down', 'new_v_meta_tokens', 'new_v_norm_mix', 'new_v_norm_ffn', 'new_v_norm_final', 'new_v_conv_w_pw1', 'new_v_conv_b_pw1', 'new_v_conv_w_dw', 'new_v_conv_b_dw', 'new_v_conv_ln_g', 'new_v_conv_ln_b', 'new_v_conv_w_pw2', 'new_v_conv_b_pw2', 'new_v_attn_w_qkv', 'new_v_attn_b_qkv', 'new_v_attn_sinks', 'new_v_attn_w_o', 'new_v_attn_b_o', 'new_v_gdn_w_in', 'new_v_gdn_conv_w', 'new_v_gdn_a_log', 'new_v_gdn_dt_bias', 'new_v_gdn_norm_w', 'new_v_gdn_w_out', 'new_v_ffn_w_gate', 'new_v_ffn_w_up', 'new_v_ffn_w_down']
TWIN_LEAF_KINDS = {'loss': 'loss', 'grad_x': 'grad_x', 'grad_meta_tokens': 'grad_w', 'grad_norm_mix': 'grad_w', 'grad_norm_ffn': 'grad_w', 'grad_norm_final': 'grad_w', 'grad_conv_w_pw1': 'grad_w', 'grad_conv_b_pw1': 'grad_w', 'grad_conv_w_dw': 'grad_w', 'grad_conv_b_dw': 'grad_w', 'grad_conv_ln_g': 'grad_w', 'grad_conv_ln_b': 'grad_w', 'grad_conv_w_pw2': 'grad_w', 'grad_conv_b_pw2': 'grad_w', 'grad_attn_w_qkv': 'grad_w', 'grad_attn_b_qkv': 'grad_w', 'grad_attn_sinks': 'grad_w', 'grad_attn_w_o': 'grad_w', 'grad_attn_b_o': 'grad_w', 'grad_gdn_w_in': 'grad_w', 'grad_gdn_conv_w': 'grad_w', 'grad_gdn_a_log': 'grad_w', 'grad_gdn_dt_bias': 'grad_w', 'grad_gdn_norm_w': 'grad_w', 'grad_gdn_w_out': 'grad_w', 'grad_ffn_w_gate': 'grad_w', 'grad_ffn_w_up': 'grad_w', 'grad_ffn_w_down': 'grad_w', 'delta_meta_tokens': 'delta_w', 'delta_norm_mix': 'delta_w', 'delta_norm_ffn': 'delta_w', 'delta_norm_final': 'delta_w', 'delta_conv_w_pw1': 'delta_w', 'delta_conv_b_pw1': 'delta_w', 'delta_conv_w_dw': 'delta_w', 'delta_conv_b_dw': 'delta_w', 'delta_conv_ln_g': 'delta_w', 'delta_conv_ln_b': 'delta_w', 'delta_conv_w_pw2': 'delta_w', 'delta_conv_b_pw2': 'delta_w', 'delta_attn_w_qkv': 'delta_w', 'delta_attn_b_qkv': 'delta_w', 'delta_attn_sinks': 'delta_w', 'delta_attn_w_o': 'delta_w', 'delta_attn_b_o': 'delta_w', 'delta_gdn_w_in': 'delta_w', 'delta_gdn_conv_w': 'delta_w', 'delta_gdn_a_log': 'delta_w', 'delta_gdn_dt_bias': 'delta_w', 'delta_gdn_norm_w': 'delta_w', 'delta_gdn_w_out': 'delta_w', 'delta_ffn_w_gate': 'delta_w', 'delta_ffn_w_up': 'delta_w', 'delta_ffn_w_down': 'delta_w', 'new_m_meta_tokens': 'new_m', 'new_m_norm_mix': 'new_m', 'new_m_norm_ffn': 'new_m', 'new_m_norm_final': 'new_m', 'new_m_conv_w_pw1': 'new_m', 'new_m_conv_b_pw1': 'new_m', 'new_m_conv_w_dw': 'new_m', 'new_m_conv_b_dw': 'new_m', 'new_m_conv_ln_g': 'new_m', 'new_m_conv_ln_b': 'new_m', 'new_m_conv_w_pw2': 'new_m', 'new_m_conv_b_pw2': 'new_m', 'new_m_attn_w_qkv': 'new_m', 'new_m_attn_b_qkv': 'new_m', 'new_m_attn_sinks': 'new_m', 'new_m_attn_w_o': 'new_m', 'new_m_attn_b_o': 'new_m', 'new_m_gdn_w_in': 'new_m', 'new_m_gdn_conv_w': 'new_m', 'new_m_gdn_a_log': 'new_m', 'new_m_gdn_dt_bias': 'new_m', 'new_m_gdn_norm_w': 'new_m', 'new_m_gdn_w_out': 'new_m', 'new_m_ffn_w_gate': 'new_m', 'new_m_ffn_w_up': 'new_m', 'new_m_ffn_w_down': 'new_m', 'new_v_meta_tokens': 'new_v', 'new_v_norm_mix': 'new_v', 'new_v_norm_ffn': 'new_v', 'new_v_norm_final': 'new_v', 'new_v_conv_w_pw1': 'new_v', 'new_v_conv_b_pw1': 'new_v', 'new_v_conv_w_dw': 'new_v', 'new_v_conv_b_dw': 'new_v', 'new_v_conv_ln_g': 'new_v', 'new_v_conv_ln_b': 'new_v', 'new_v_conv_w_pw2': 'new_v', 'new_v_conv_b_pw2': 'new_v', 'new_v_attn_w_qkv': 'new_v', 'new_v_attn_b_qkv': 'new_v', 'new_v_attn_sinks': 'new_v', 'new_v_attn_w_o': 'new_v', 'new_v_attn_b_o': 'new_v', 'new_v_gdn_w_in': 'new_v', 'new_v_gdn_conv_w': 'new_v', 'new_v_gdn_a_log': 'new_v', 'new_v_gdn_dt_bias': 'new_v', 'new_v_gdn_norm_w': 'new_v', 'new_v_gdn_w_out': 'new_v', 'new_v_ffn_w_gate': 'new_v', 'new_v_ffn_w_up': 'new_v', 'new_v_ffn_w_down': 'new_v'}


def _forward(args):
    return _fwd_reference(*[args[k] for k in FWD_PARAMS])


def _output_shape():
    def fwd():
        inp = _fwd_setup_inputs(0)
        return _fwd_reference(*[inp[k] for k in FWD_PARAMS])
    out = _jax.eval_shape(fwd)
    return out.shape, out.dtype

N_MICROBATCH = 1
ADAM_LR = 0.001
ADAM_B1 = 0.9
ADAM_B2 = 0.999
ADAM_EPS = 1e-08
ADAM_WD = 0.01
ADAM_STEP = 10
PER_EXAMPLE_BATCH_AXIS = {'x': 0, 'loss_target': 0}
SHARED_INPUTS = []
_WEIGHT_DTYPES = {'meta_tokens': _jnp.float32, 'norm_mix': _jnp.float32, 'norm_ffn': _jnp.float32, 'norm_final': _jnp.float32, 'conv_w_pw1': _jnp.float32, 'conv_b_pw1': _jnp.float32, 'conv_w_dw': _jnp.float32, 'conv_b_dw': _jnp.float32, 'conv_ln_g': _jnp.float32, 'conv_ln_b': _jnp.float32, 'conv_w_pw2': _jnp.float32, 'conv_b_pw2': _jnp.float32, 'attn_w_qkv': _jnp.float32, 'attn_b_qkv': _jnp.float32, 'attn_sinks': _jnp.float32, 'attn_w_o': _jnp.float32, 'attn_b_o': _jnp.float32, 'gdn_w_in': _jnp.float32, 'gdn_conv_w': _jnp.float32, 'gdn_a_log': _jnp.float32, 'gdn_dt_bias': _jnp.float32, 'gdn_norm_w': _jnp.float32, 'gdn_w_out': _jnp.float32, 'ffn_w_gate': _jnp.float32, 'ffn_w_up': _jnp.float32, 'ffn_w_down': _jnp.float32}
MOMENT_SCALE = {'meta_tokens': 9.821733e-03, 'norm_mix': 7.834598e-02, 'norm_ffn': 8.708571e-02, 'norm_final': 3.206087e+01, 'conv_w_pw1': 6.048329e-02, 'conv_b_pw1': 7.058588e-02, 'conv_w_dw': 8.031473e-02, 'conv_b_dw': 1.801271e-01, 'conv_ln_g': 1.037136e-01, 'conv_ln_b': 1.017030e-01, 'conv_w_pw2': 8.154518e-02, 'conv_b_pw2': 1.821675e-01, 'attn_w_qkv': 3.920534e-02, 'attn_b_qkv': 1.306400e-01, 'attn_sinks': 9.539214e-03, 'attn_w_o': 3.287191e-02, 'attn_b_o': 1.640286e-01, 'gdn_w_in': 3.756697e-02, 'gdn_conv_w': 3.612232e-02, 'gdn_a_log': 2.682753e-01, 'gdn_dt_bias': 2.623283e-01, 'gdn_norm_w': 2.627877e-01, 'gdn_w_out': 5.415733e-02, 'ffn_w_gate': 3.754438e-02, 'ffn_w_up': 3.645760e-02, 'ffn_w_down': 6.047440e-02}


def _to_microbatches(a, axis):
    t = _jnp.moveaxis(a, axis, 0)
    t = t.reshape((N_MICROBATCH, t.shape[0] // N_MICROBATCH) + t.shape[1:])
    return _jnp.moveaxis(t, 1, axis + 1)


def setup_inputs(seed: int = 0) -> dict:
    inp = _fwd_setup_inputs(seed)
    key = _jax.random.fold_in(_jax.random.key(seed), 7919)
    shape, _ = _output_shape()
    out = dict(inp)
    out["loss_target"] = _jax.random.normal(_jax.random.fold_in(key, 0), shape, _jnp.float32)
    for i, name in enumerate(TWIN_WEIGHTS):
        w = inp[name].astype(_jnp.float32)
        if MOMENT_SCALE is None:
            s = _jnp.sqrt(_jnp.mean(_jnp.square(w)) + 1e-30)
        else:
            s = MOMENT_SCALE[name]
        km, kv = _jax.random.split(_jax.random.fold_in(key, i + 1))
        out[name] = w
        out["m_" + name] = s * _jax.random.normal(km, w.shape, _jnp.float32)
        out["v_" + name] = (s * s) * _jax.random.uniform(kv, w.shape, _jnp.float32, 0.5, 1.5)
    if N_MICROBATCH > 1:
        for name, axis in PER_EXAMPLE_BATCH_AXIS.items():
            out[name] = _to_microbatches(out[name], axis)
    return {'x': out['x'], 'meta_tokens': out['meta_tokens'], 'norm_mix': out['norm_mix'], 'norm_ffn': out['norm_ffn'], 'norm_final': out['norm_final'], 'conv_w_pw1': out['conv_w_pw1'], 'conv_b_pw1': out['conv_b_pw1'], 'conv_w_dw': out['conv_w_dw'], 'conv_b_dw': out['conv_b_dw'], 'conv_ln_g': out['conv_ln_g'], 'conv_ln_b': out['conv_ln_b'], 'conv_w_pw2': out['conv_w_pw2'], 'conv_b_pw2': out['conv_b_pw2'], 'attn_w_qkv': out['attn_w_qkv'], 'attn_b_qkv': out['attn_b_qkv'], 'attn_sinks': out['attn_sinks'], 'attn_w_o': out['attn_w_o'], 'attn_b_o': out['attn_b_o'], 'gdn_w_in': out['gdn_w_in'], 'gdn_conv_w': out['gdn_conv_w'], 'gdn_a_log': out['gdn_a_log'], 'gdn_dt_bias': out['gdn_dt_bias'], 'gdn_norm_w': out['gdn_norm_w'], 'gdn_w_out': out['gdn_w_out'], 'ffn_w_gate': out['ffn_w_gate'], 'ffn_w_up': out['ffn_w_up'], 'ffn_w_down': out['ffn_w_down'], 'loss_target': out['loss_target'], 'm_meta_tokens': out['m_meta_tokens'], 'm_norm_mix': out['m_norm_mix'], 'm_norm_ffn': out['m_norm_ffn'], 'm_norm_final': out['m_norm_final'], 'm_conv_w_pw1': out['m_conv_w_pw1'], 'm_conv_b_pw1': out['m_conv_b_pw1'], 'm_conv_w_dw': out['m_conv_w_dw'], 'm_conv_b_dw': out['m_conv_b_dw'], 'm_conv_ln_g': out['m_conv_ln_g'], 'm_conv_ln_b': out['m_conv_ln_b'], 'm_conv_w_pw2': out['m_conv_w_pw2'], 'm_conv_b_pw2': out['m_conv_b_pw2'], 'm_attn_w_qkv': out['m_attn_w_qkv'], 'm_attn_b_qkv': out['m_attn_b_qkv'], 'm_attn_sinks': out['m_attn_sinks'], 'm_attn_w_o': out['m_attn_w_o'], 'm_attn_b_o': out['m_attn_b_o'], 'm_gdn_w_in': out['m_gdn_w_in'], 'm_gdn_conv_w': out['m_gdn_conv_w'], 'm_gdn_a_log': out['m_gdn_a_log'], 'm_gdn_dt_bias': out['m_gdn_dt_bias'], 'm_gdn_norm_w': out['m_gdn_norm_w'], 'm_gdn_w_out': out['m_gdn_w_out'], 'm_ffn_w_gate': out['m_ffn_w_gate'], 'm_ffn_w_up': out['m_ffn_w_up'], 'm_ffn_w_down': out['m_ffn_w_down'], 'v_meta_tokens': out['v_meta_tokens'], 'v_norm_mix': out['v_norm_mix'], 'v_norm_ffn': out['v_norm_ffn'], 'v_norm_final': out['v_norm_final'], 'v_conv_w_pw1': out['v_conv_w_pw1'], 'v_conv_b_pw1': out['v_conv_b_pw1'], 'v_conv_w_dw': out['v_conv_w_dw'], 'v_conv_b_dw': out['v_conv_b_dw'], 'v_conv_ln_g': out['v_conv_ln_g'], 'v_conv_ln_b': out['v_conv_ln_b'], 'v_conv_w_pw2': out['v_conv_w_pw2'], 'v_conv_b_pw2': out['v_conv_b_pw2'], 'v_attn_w_qkv': out['v_attn_w_qkv'], 'v_attn_b_qkv': out['v_attn_b_qkv'], 'v_attn_sinks': out['v_attn_sinks'], 'v_attn_w_o': out['v_attn_w_o'], 'v_attn_b_o': out['v_attn_b_o'], 'v_gdn_w_in': out['v_gdn_w_in'], 'v_gdn_conv_w': out['v_gdn_conv_w'], 'v_gdn_a_log': out['v_gdn_a_log'], 'v_gdn_dt_bias': out['v_gdn_dt_bias'], 'v_gdn_norm_w': out['v_gdn_norm_w'], 'v_gdn_w_out': out['v_gdn_w_out'], 'v_ffn_w_gate': out['v_ffn_w_gate'], 'v_ffn_w_up': out['v_ffn_w_up'], 'v_ffn_w_down': out['v_ffn_w_down']}


def _loss(weights, diff, rest, loss_target):
    with _jax.named_scope("forward"):
        args = {**rest, TWIN_DIFF_INPUT: diff, **{k: w.astype(_WEIGHT_DTYPES[k]) for k, w in weights.items()}}
        y = _forward(args)
    with _jax.named_scope("loss_head"):
        err = _jnp.square(y.astype(_jnp.float32) - loss_target)
        return 0.5 * _jnp.sum(_jnp.mean(err, axis=-1)) if err.ndim else 0.5 * err


def _adamw(w, g, m, v):
    m = ADAM_B1 * m + (1.0 - ADAM_B1) * g
    v = ADAM_B2 * v + (1.0 - ADAM_B2) * _jnp.square(g)
    m_hat = m / (1.0 - ADAM_B1 ** ADAM_STEP)
    v_hat = v / (1.0 - ADAM_B2 ** ADAM_STEP)
    delta = -ADAM_LR * (m_hat / (_jnp.sqrt(v_hat) + ADAM_EPS) + ADAM_WD * w)
    return delta, m, v


def reference(x, meta_tokens, norm_mix, norm_ffn, norm_final, conv_w_pw1, conv_b_pw1, conv_w_dw, conv_b_dw, conv_ln_g, conv_ln_b, conv_w_pw2, conv_b_pw2, attn_w_qkv, attn_b_qkv, attn_sinks, attn_w_o, attn_b_o, gdn_w_in, gdn_conv_w, gdn_a_log, gdn_dt_bias, gdn_norm_w, gdn_w_out, ffn_w_gate, ffn_w_up, ffn_w_down, loss_target, m_meta_tokens, m_norm_mix, m_norm_ffn, m_norm_final, m_conv_w_pw1, m_conv_b_pw1, m_conv_w_dw, m_conv_b_dw, m_conv_ln_g, m_conv_ln_b, m_conv_w_pw2, m_conv_b_pw2, m_attn_w_qkv, m_attn_b_qkv, m_attn_sinks, m_attn_w_o, m_attn_b_o, m_gdn_w_in, m_gdn_conv_w, m_gdn_a_log, m_gdn_dt_bias, m_gdn_norm_w, m_gdn_w_out, m_ffn_w_gate, m_ffn_w_up, m_ffn_w_down, v_meta_tokens, v_norm_mix, v_norm_ffn, v_norm_final, v_conv_w_pw1, v_conv_b_pw1, v_conv_w_dw, v_conv_b_dw, v_conv_ln_g, v_conv_ln_b, v_conv_w_pw2, v_conv_b_pw2, v_attn_w_qkv, v_attn_b_qkv, v_attn_sinks, v_attn_w_o, v_attn_b_o, v_gdn_w_in, v_gdn_conv_w, v_gdn_a_log, v_gdn_dt_bias, v_gdn_norm_w, v_gdn_w_out, v_ffn_w_gate, v_ffn_w_up, v_ffn_w_down):
    given = dict(x=x, meta_tokens=meta_tokens, norm_mix=norm_mix, norm_ffn=norm_ffn, norm_final=norm_final, conv_w_pw1=conv_w_pw1, conv_b_pw1=conv_b_pw1, conv_w_dw=conv_w_dw, conv_b_dw=conv_b_dw, conv_ln_g=conv_ln_g, conv_ln_b=conv_ln_b, conv_w_pw2=conv_w_pw2, conv_b_pw2=conv_b_pw2, attn_w_qkv=attn_w_qkv, attn_b_qkv=attn_b_qkv, attn_sinks=attn_sinks, attn_w_o=attn_w_o, attn_b_o=attn_b_o, gdn_w_in=gdn_w_in, gdn_conv_w=gdn_conv_w, gdn_a_log=gdn_a_log, gdn_dt_bias=gdn_dt_bias, gdn_norm_w=gdn_norm_w, gdn_w_out=gdn_w_out, ffn_w_gate=ffn_w_gate, ffn_w_up=ffn_w_up, ffn_w_down=ffn_w_down, loss_target=loss_target, m_meta_tokens=m_meta_tokens, m_norm_mix=m_norm_mix, m_norm_ffn=m_norm_ffn, m_norm_final=m_norm_final, m_conv_w_pw1=m_conv_w_pw1, m_conv_b_pw1=m_conv_b_pw1, m_conv_w_dw=m_conv_w_dw, m_conv_b_dw=m_conv_b_dw, m_conv_ln_g=m_conv_ln_g, m_conv_ln_b=m_conv_ln_b, m_conv_w_pw2=m_conv_w_pw2, m_conv_b_pw2=m_conv_b_pw2, m_attn_w_qkv=m_attn_w_qkv, m_attn_b_qkv=m_attn_b_qkv, m_attn_sinks=m_attn_sinks, m_attn_w_o=m_attn_w_o, m_attn_b_o=m_attn_b_o, m_gdn_w_in=m_gdn_w_in, m_gdn_conv_w=m_gdn_conv_w, m_gdn_a_log=m_gdn_a_log, m_gdn_dt_bias=m_gdn_dt_bias, m_gdn_norm_w=m_gdn_norm_w, m_gdn_w_out=m_gdn_w_out, m_ffn_w_gate=m_ffn_w_gate, m_ffn_w_up=m_ffn_w_up, m_ffn_w_down=m_ffn_w_down, v_meta_tokens=v_meta_tokens, v_norm_mix=v_norm_mix, v_norm_ffn=v_norm_ffn, v_norm_final=v_norm_final, v_conv_w_pw1=v_conv_w_pw1, v_conv_b_pw1=v_conv_b_pw1, v_conv_w_dw=v_conv_w_dw, v_conv_b_dw=v_conv_b_dw, v_conv_ln_g=v_conv_ln_g, v_conv_ln_b=v_conv_ln_b, v_conv_w_pw2=v_conv_w_pw2, v_conv_b_pw2=v_conv_b_pw2, v_attn_w_qkv=v_attn_w_qkv, v_attn_b_qkv=v_attn_b_qkv, v_attn_sinks=v_attn_sinks, v_attn_w_o=v_attn_w_o, v_attn_b_o=v_attn_b_o, v_gdn_w_in=v_gdn_w_in, v_gdn_conv_w=v_gdn_conv_w, v_gdn_a_log=v_gdn_a_log, v_gdn_dt_bias=v_gdn_dt_bias, v_gdn_norm_w=v_gdn_norm_w, v_gdn_w_out=v_gdn_w_out, v_ffn_w_gate=v_ffn_w_gate, v_ffn_w_up=v_ffn_w_up, v_ffn_w_down=v_ffn_w_down)
    weights = {n: given[n] for n in TWIN_WEIGHTS}
    shared = {n: given[n] for n in SHARED_INPUTS}
    per_example = {n: given[n] for n in ['x']}
    grad_fn = _jax.value_and_grad(_loss, argnums=(0, 1))

    def one_microbatch(ex, loss_target):
        ex = dict(ex)
        diff = ex.pop(TWIN_DIFF_INPUT)
        return grad_fn(weights, diff, {**shared, **ex}, loss_target)

    if N_MICROBATCH == 1:
        loss, (grad_w, grad_x) = one_microbatch(per_example, given["loss_target"])
    else:
        def body(carry, xs):
            loss_sum, grad_sum = carry
            l_k, (gw_k, gx_k) = one_microbatch(xs[0], xs[1])
            with _jax.named_scope("update"):
                return (loss_sum + l_k, _jax.tree.map(_jnp.add, grad_sum, gw_k)), gx_k

        init = (_jnp.zeros((), _jnp.float32), _jax.tree.map(_jnp.zeros_like, weights))
        (loss, grad_w), grad_x = _jax.lax.scan(body, init, (per_example, given["loss_target"]))
    with _jax.named_scope("update"):
        delta_w, new_m, new_v = {}, {}, {}
        for n in TWIN_WEIGHTS:
            delta_w[n], new_m[n], new_v[n] = _adamw(weights[n], grad_w[n], given["m_" + n], given["v_" + n])
    return (loss, grad_x, *[grad_w[n] for n in TWIN_WEIGHTS], *[delta_w[n] for n in TWIN_WEIGHTS],
            *[new_m[n] for n in TWIN_WEIGHTS], *[new_v[n] for n in TWIN_WEIGHTS])
```

```python
import functools
import math

import jax
import jax.numpy as jnp
from jax import lax
from jax.experimental import pallas as pl
from jax.experimental.pallas import tpu as pltpu

F32 = jnp.float32
BF16 = jnp.bfloat16
HIGHEST = lax.Precision.HIGHEST

AXES = ("x", "y", "c")
N_DEV = 8

N_META = 16
FRONT = 128
PAD_LEN = FRONT - N_META
NORM_EPS = 1e-6
LN_EPS = 1e-5
NEG_INF = -1e30
CONV_KERNEL = 31
HEAD_DIM = 64
GROUP = 8
BLOCK = 128
ROPE_THETA = 10000.0
GDN_DIM = 128
GDN_CONV = 4
GDN_CHUNK = 64
ADAM_LR, ADAM_B1, ADAM_B2, ADAM_EPS, ADAM_WD, ADAM_STEP = 0.001, 0.9, 0.999, 1e-08, 0.01, 10

VMEM_LIMIT_BYTES = 52 * 1024 * 1024
LANES = 128
CONV_HALO = 32


def _tile(n, pref, align=128):
    best = None
    for t in range(align, min(n, pref) + 1, align):
        if n % t == 0:
            best = t
    return best if best is not None else n


def _cparams(sem):
    return pltpu.CompilerParams(dimension_semantics=sem, vmem_limit_bytes=VMEM_LIMIT_BYTES)


def _mm(a, b, mode, out_dtype, name, bias=None, residual=None):
    if mode == "nn":
        (m, k), (k2, n) = a.shape, b.shape
    elif mode == "nt":
        (m, k), (n, k2) = a.shape, b.shape
    else:
        (k, m), (k2, n) = a.shape, b.shape
    assert k == k2, (a.shape, b.shape, mode)
    tm = _tile(m, 1024 if mode == "tn" else 640)
    tn = _tile(n, 1280)
    tk = _tile(k, 2048)
    nk = k // tk
    dims = {"nn": (((1,), (0,)), ((), ())), "nt": (((1,), (1,)), ((), ())), "tn": (((0,), (0,)), ((), ()))}[mode]
    a_spec = pl.BlockSpec((tk, tm), lambda i, j, kk: (kk, i)) if mode == "tn" else pl.BlockSpec((tm, tk), lambda i, j, kk: (i, kk))
    b_spec = pl.BlockSpec((tn, tk), lambda i, j, kk: (j, kk)) if mode == "nt" else pl.BlockSpec((tk, tn), lambda i, j, kk: (kk, j))
    ins, specs = [a, b], [a_spec, b_spec]
    if bias is not None:
        ins.append(bias.reshape(1, n).astype(F32))
        specs.append(pl.BlockSpec((1, tn), lambda i, j, kk: (0, j)))
    if residual is not None:
        ins.append(residual)
        specs.append(pl.BlockSpec((tm, tn), lambda i, j, kk: (i, j)))
    has_bias, has_res = bias is not None, residual is not None

    def body(*refs):
        a_ref, b_ref = refs[0], refs[1]
        o_ref, acc = refs[-2], refs[-1]
        kk = pl.program_id(2)

        @pl.when(kk == 0)
        def _():
            acc[...] = jnp.zeros_like(acc)

        acc[...] += lax.dot_general(a_ref[...].astype(BF16), b_ref[...].astype(BF16), dims, preferred_element_type=F32)

        @pl.when(kk == nk - 1)
        def _():
            r = acc[...]
            pos = 2
            if has_bias:
                r = r + refs[pos][...]
                pos += 1
            if has_res:
                r = r + refs[pos][...].astype(F32)
            o_ref[...] = r.astype(o_ref.dtype)

    return pl.pallas_call(
        body, name=name, grid=(m // tm, n // tn, nk), in_specs=specs,
        out_specs=pl.BlockSpec((tm, tn), lambda i, j, kk: (i, j)),
        out_shape=jax.ShapeDtypeStruct((m, n), out_dtype),
        scratch_shapes=[pltpu.VMEM((tm, tn), F32)],
        compiler_params=_cparams(("parallel", "parallel", "arbitrary")),
    )(*ins)


def _rowwise(fn, rows, consts, out_rows, out_accs, name, tm=None):
    rows = [r if isinstance(r, tuple) else (r, r.shape[1], 0) for r in rows]
    n_rows = rows[0][0].shape[0]
    widest = max([w for _, w, _ in rows] + [w for w, _ in out_rows])
    if tm is None:
        tm = _tile(n_rows, 640 if widest <= 2560 else 128, 8)
    steps = n_rows // tm
    in_specs = [pl.BlockSpec((tm, w), functools.partial(lambda i, c: (i, c), c=cb)) for _, w, cb in rows]
    in_specs += [pl.BlockSpec(c.shape, lambda i: (0, 0)) for c in consts]
    out_specs = [pl.BlockSpec((tm, w), lambda i: (i, 0)) for w, _ in out_rows]
    out_specs += [pl.BlockSpec(s, lambda i: (0, 0)) for s in out_accs]
    out_shape = [jax.ShapeDtypeStruct((n_rows, w), d) for w, d in out_rows]
    out_shape += [jax.ShapeDtypeStruct(s, F32) for s in out_accs]
    n_in, n_or = len(rows) + len(consts), len(out_rows)

    def body(*refs):
        i = pl.program_id(0)
        vals = fn(i * tm, *[r[...] for r in refs[:n_in]])
        outs = refs[n_in:]
        for o_ref, v in zip(outs[:n_or], vals[:n_or]):
            o_ref[...] = v.astype(o_ref.dtype)
        if out_accs:
            @pl.when(i == 0)
            def _():
                for a_ref in outs[n_or:]:
                    a_ref[...] = jnp.zeros_like(a_ref)

            for a_ref, v in zip(outs[n_or:], vals[n_or:]):
                a_ref[...] += v

    res = pl.pallas_call(
        body, name=name, grid=(steps,), in_specs=in_specs, out_specs=out_specs, out_shape=out_shape,
        compiler_params=_cparams(("arbitrary",) if out_accs else ("parallel",)),
    )(*[r[0] for r in rows], *consts)
    return res


def _colsum(v):
    return jnp.sum(v, axis=0, keepdims=True)


def _rms(h, w):
    return h * lax.rsqrt(jnp.mean(h * h, axis=-1, keepdims=True) + NORM_EPS) * w


def _rms_fwd(h, w, name):
    (hn,) = _rowwise(lambda r0, hv, wv: (_rms(hv, wv),), [h], [w.reshape(1, -1)], [(h.shape[1], BF16)], [], name)
    return hn


def _rms_bwd(h, w, dhn, dh_in, name):
    d = h.shape[1]

    def fn(r0, hv, dv, rv, wv):
        _, vjp = jax.vjp(_rms, hv, wv)
        dh, dw = vjp(dv.astype(F32))
        return dh + rv, dw

    dh, dw = _rowwise(fn, [h, dhn, dh_in], [w.reshape(1, -1)], [(d, F32)], [(1, d)], name)
    return dh, dw[0]


def _swiglu(gu):
    f = gu.shape[1] // 2
    g, u = gu[:, :f].astype(F32), gu[:, f:].astype(F32)
    return jax.nn.silu(g) * u


def _swiglu_fwd(gu, name):
    (a,) = _rowwise(lambda r0, v: (_swiglu(v),), [gu], [], [(gu.shape[1] // 2, BF16)], [], name)
    return a


def _swiglu_bwd(gu, da, name):
    def fn(r0, v, dv):
        _, vjp = jax.vjp(_swiglu, v.astype(F32))
        return vjp(dv.astype(F32))

    (dgu,) = _rowwise(fn, [gu, da], [], [(gu.shape[1], BF16)], [], name)
    return dgu


def _glu(p, b):
    t = p + b
    d = t.shape[1] // 2
    return t[:, :d] * jax.nn.sigmoid(t[:, d:])


def _glu_fwd(p, b, name):
    (u,) = _rowwise(lambda r0, v, bv: (_glu(v, bv),), [p], [b.reshape(1, -1)], [(p.shape[1] // 2, F32)], [], name)
    return u


def _glu_bwd(p, b, du, name):
    def fn(r0, v, dv, bv):
        _, vjp = jax.vjp(_glu, v, bv)
        dp, db = vjp(dv)
        return dp, db

    dp, db = _rowwise(fn, [p, du], [b.reshape(1, -1)], [(p.shape[1], BF16)], [(1, p.shape[1])], name)
    return dp, db[0]


def _ln_silu(c, g, b):
    mu = jnp.mean(c, axis=-1, keepdims=True)
    xc = c - mu
    var = jnp.mean(xc * xc, axis=-1, keepdims=True)
    return jax.nn.silu(xc * lax.rsqrt(var + LN_EPS) * g + b)


def _ln_silu_fwd(c, g, b, name):
    (s,) = _rowwise(lambda r0, v, gv, bv: (_ln_silu(v, gv, bv),), [c], [g.reshape(1, -1), b.reshape(1, -1)],
                    [(c.shape[1], BF16)], [], name)
    return s


def _ln_silu_bwd(c, g, b, ds, name):
    d = c.shape[1]

    def fn(r0, v, dv, gv, bv):
        _, vjp = jax.vjp(_ln_silu, v, gv, bv)
        return vjp(dv.astype(F32))

    dc, dg, db = _rowwise(fn, [c, ds], [g.reshape(1, -1), b.reshape(1, -1)], [(d, F32)], [(1, d), (1, d)], name)
    return dc, dg[0], db[0]


def _colsum_rows(v, name):
    (s,) = _rowwise(lambda r0, t: (_colsum(t.astype(F32)),), [v], [], [], [(1, v.shape[1])], name)
    return s[0]


def _rot_half(x):
    w = x.shape[1]
    lane = lax.broadcasted_iota(jnp.int32, x.shape, 1)
    lo = (lane % HEAD_DIM) < (HEAD_DIM // 2)
    return jnp.where(lo, -pltpu.roll(x, w - HEAD_DIM // 2, axis=1), pltpu.roll(x, HEAD_DIM // 2, axis=1))


def _qkv_post_fwd(pre, b, cos, sin, qw, kw, name):
    reps = (qw + kw) // LANES

    def fn(r0, pv, cv, sv, bv):
        t = pv + bv
        tq = t[:, :qw + kw]
        y = tq * jnp.tile(cv, (1, reps)) + _rot_half(tq) * jnp.tile(sv, (1, reps))
        return y[:, :qw], y[:, qw:], t[:, qw + kw:]

    return _rowwise(fn, [pre, cos, sin], [b.reshape(1, -1)], [(qw, BF16), (kw, BF16), (kw, BF16)], [], name)


def _qkv_post_bwd(dq, dk, dv, cos, sin, name):
    qw, kw = dq.shape[1], dk.shape[1]
    reps = (qw + kw) // LANES
    width = qw + 2 * kw

    def fn(r0, dqv, dkv, dvv, cv, sv):
        dy = jnp.concatenate([dqv.astype(F32), dkv.astype(F32)], axis=1)
        dt = dy * jnp.tile(cv, (1, reps)) - _rot_half(dy * jnp.tile(sv, (1, reps)))
        dpre = jnp.concatenate([dt, dvv.astype(F32)], axis=1)
        return dpre, _colsum(dpre)

    dpre, db = _rowwise(fn, [dq, dk, dv, cos, sin], [], [(width, BF16)], [(1, width)], name)
    return dpre, db[0]


def _dw_tiles(n_rows, c):
    return _tile(n_rows, 640, 8), _tile(c, 512)


def _row_mask(r0, n, width):
    row = r0 + lax.broadcasted_iota(jnp.int32, (n, width), 0)
    return row >= PAD_LEN


def _dwconv_fwd(u, w, bias, c, name):
    n_rows, taps = u.shape[0], w.shape[0]
    tr, cb = _dw_tiles(n_rows, c)
    kp = -(-taps // 8) * 8
    wp = jnp.zeros((kp, c), F32).at[:taps].set(w)
    bp = jnp.zeros((1, c), F32) if bias is None else bias.reshape(1, c).astype(F32)

    def body(cur_ref, prev_ref, w_ref, b_ref, o_ref):
        r = pl.program_id(1)
        cur = jnp.where(_row_mask(r * tr, tr, cb), cur_ref[...], 0.0)
        tail = prev_ref[tr - CONV_HALO:, :]
        tail = jnp.where(_row_mask(r * tr - CONV_HALO, CONV_HALO, cb) & (r > 0), tail, 0.0)
        win = jnp.concatenate([tail, cur], axis=0)
        acc = jnp.zeros((tr, cb), F32) + b_ref[...]
        for k in range(taps):
            off = CONV_HALO - (taps - 1) + k
            acc = acc + w_ref[k:k + 1, :] * win[off:off + tr, :]
        o_ref[...] = acc

    return pl.pallas_call(
        body, name=name, grid=(c // cb, n_rows // tr),
        in_specs=[pl.BlockSpec((tr, cb), lambda j, r: (r, j)),
                  pl.BlockSpec((tr, cb), lambda j, r: (jnp.maximum(r - 1, 0), j)),
                  pl.BlockSpec((kp, cb), lambda j, r: (0, j)),
                  pl.BlockSpec((1, cb), lambda j, r: (0, j))],
        out_specs=pl.BlockSpec((tr, cb), lambda j, r: (r, j)),
        out_shape=jax.ShapeDtypeStruct((n_rows, c), F32),
        compiler_params=_cparams(("parallel", "parallel")),
    )(u, u, wp, bp)


def _dwconv_bwd(u, w, dc, c, name):
    n_rows, taps = u.shape[0], w.shape[0]
    tr, cb = _dw_tiles(n_rows, c)
    nr = n_rows // tr
    kp = -(-taps // 8) * 8
    wp = jnp.zeros((kp, c), F32).at[:taps].set(w)

    def body(cur_ref, prev_ref, d_ref, dnext_ref, w_ref, du_ref, dw_ref, db_ref):
        r = pl.program_id(1)
        cur = jnp.where(_row_mask(r * tr, tr, cb), cur_ref[...], 0.0)
        tail = prev_ref[tr - CONV_HALO:, :]
        tail = jnp.where(_row_mask(r * tr - CONV_HALO, CONV_HALO, cb) & (r > 0), tail, 0.0)
        win_u = jnp.concatenate([tail, cur], axis=0)
        d = d_ref[...]
        head = jnp.where(r < nr - 1, dnext_ref[:CONV_HALO, :], 0.0)
        win_d = jnp.concatenate([d, head], axis=0)

        @pl.when(r == 0)
        def _():
            dw_ref[...] = jnp.zeros_like(dw_ref)
            db_ref[...] = jnp.zeros_like(db_ref)

        du = jnp.zeros((tr, cb), F32)
        for k in range(taps):
            off = CONV_HALO - (taps - 1) + k
            du = du + w_ref[k:k + 1, :] * win_d[taps - 1 - k:taps - 1 - k + tr, :]
            dw_ref[k:k + 1, :] += _colsum(d * win_u[off:off + tr, :])
        du_ref[...] = jnp.where(_row_mask(r * tr, tr, cb), du, 0.0)
        db_ref[...] += _colsum(d)

    du, dw, db = pl.pallas_call(
        body, name=name, grid=(c // cb, nr),
        in_specs=[pl.BlockSpec((tr, cb), lambda j, r: (r, j)),
                  pl.BlockSpec((tr, cb), lambda j, r: (jnp.maximum(r - 1, 0), j)),
                  pl.BlockSpec((tr, cb), lambda j, r: (r, j)),
                  pl.BlockSpec((tr, cb), lambda j, r: (jnp.minimum(r + 1, nr - 1), j)),
                  pl.BlockSpec((kp, cb), lambda j, r: (0, j))],
        out_specs=[pl.BlockSpec((tr, cb), lambda j, r: (r, j)),
                   pl.BlockSpec((kp, cb), lambda j, r: (0, j)),
                   pl.BlockSpec((1, cb), lambda j, r: (0, j))],
        out_shape=[jax.ShapeDtypeStruct((n_rows, c), F32), jax.ShapeDtypeStruct((kp, c), F32),
                   jax.ShapeDtypeStruct((1, c), F32)],
        compiler_params=_cparams(("parallel", "arbitrary")),
    )(u, u, dc, dc, wp)
    return du, dw[:taps], db[0]


def _attn_block(q, kprev, kcur, vprev, vcur, sink, n):
    qf = q.reshape(GROUP * BLOCK, HEAD_DIM).astype(BF16)
    kb = jnp.concatenate([kprev, kcur], axis=0).astype(BF16)
    vb = jnp.concatenate([vprev, vcur], axis=0).astype(BF16)
    s = lax.dot_general(qf, kb, (((1,), (1,)), ((), ())), preferred_element_type=F32) * (HEAD_DIM ** -0.5)
    s = s.reshape(GROUP, BLOCK, 2 * BLOCK)
    qi = lax.broadcasted_iota(jnp.int32, (BLOCK, 2 * BLOCK), 0)
    kj = lax.broadcasted_iota(jnp.int32, (BLOCK, 2 * BLOCK), 1)
    dist = qi + BLOCK - kj
    allowed = (dist >= 0) & (dist < BLOCK) & ((n - 1) * BLOCK + kj >= PAD_LEN)
    s = jnp.where(allowed[None], s, NEG_INF)
    m = lax.stop_gradient(jnp.maximum(jnp.max(s, axis=-1, keepdims=True), sink))
    e = jnp.exp(s - m)
    p = e / (jnp.sum(e, axis=-1, keepdims=True) + jnp.exp(sink - m))
    o = jnp.dot(p.reshape(GROUP * BLOCK, 2 * BLOCK).astype(BF16), vb, preferred_element_type=F32)
    return o.reshape(GROUP, BLOCK, HEAD_DIM)


def _attn_specs(nb):
    q_spec = pl.BlockSpec((GROUP, BLOCK, HEAD_DIM), lambda g, n: (g, n, 0))
    cur = pl.BlockSpec((1, BLOCK, HEAD_DIM), lambda g, n: (g, n, 0))
    prev = pl.BlockSpec((1, BLOCK, HEAD_DIM), lambda g, n: (g, jnp.maximum(n - 1, 0), 0))
    sink = pl.BlockSpec((1, GROUP, 1, 1), lambda g, n: (g, 0, 0, 0))
    return q_spec, cur, prev, sink


def _attn_fwd(q, k, v, sinks, name):
    heads, n_rows, _ = q.shape
    nb = n_rows // BLOCK
    q_spec, cur, prev, sink = _attn_specs(nb)

    def body(q_ref, kp_ref, kc_ref, vp_ref, vc_ref, s_ref, o_ref):
        n = pl.program_id(1)
        o = _attn_block(q_ref[...].astype(F32), kp_ref[0].astype(F32), kc_ref[0].astype(F32), vp_ref[0].astype(F32),
                        vc_ref[0].astype(F32), s_ref[0], n)
        o_ref[...] = o.astype(o_ref.dtype)

    return pl.pallas_call(
        body, name=name, grid=(heads // GROUP, nb), in_specs=[q_spec, prev, cur, prev, cur, sink], out_specs=q_spec,
        out_shape=jax.ShapeDtypeStruct(q.shape, BF16), compiler_params=_cparams(("parallel", "parallel")),
    )(q, k, k, v, v, sinks.reshape(heads // GROUP, GROUP, 1, 1))


def _attn_bwd(q, k, v, sinks, do, name):
    heads, n_rows, _ = q.shape
    kvh = heads // GROUP
    nb = n_rows // BLOCK
    q_spec, cur, prev, sink = _attn_specs(nb)
    part = pl.BlockSpec((1, 1, BLOCK, HEAD_DIM), lambda g, n: (g, n, 0, 0))
    part_shape = jax.ShapeDtypeStruct((kvh, nb, BLOCK, HEAD_DIM), F32)

    def body(q_ref, kp_ref, kc_ref, vp_ref, vc_ref, s_ref, do_ref, dq_ref, dkp_ref, dkc_ref, dvp_ref, dvc_ref, ds_ref):
        n = pl.program_id(1)
        f = functools.partial(_attn_block, n=n)
        _, vjp = jax.vjp(f, q_ref[...].astype(F32), kp_ref[0].astype(F32), kc_ref[0].astype(F32), vp_ref[0].astype(F32),
                         vc_ref[0].astype(F32), s_ref[0])
        dq, dkp, dkc, dvp, dvc, ds = vjp(do_ref[...].astype(F32))
        dq_ref[...] = dq.astype(dq_ref.dtype)
        dkp_ref[0, 0], dkc_ref[0, 0], dvp_ref[0, 0], dvc_ref[0, 0] = dkp, dkc, dvp, dvc

        @pl.when(n == 0)
        def _():
            ds_ref[...] = jnp.zeros_like(ds_ref)

        ds_ref[0] += ds

    return pl.pallas_call(
        body, name=name, grid=(kvh, nb), in_specs=[q_spec, prev, cur, prev, cur, sink, q_spec],
        out_specs=[q_spec, part, part, part, part, sink],
        out_shape=[jax.ShapeDtypeStruct(q.shape, BF16), part_shape, part_shape, part_shape, part_shape,
                   jax.ShapeDtypeStruct((kvh, GROUP, 1, 1), F32)],
        compiler_params=_cparams(("parallel", "arbitrary")),
    )(q, k, k, v, v, sinks.reshape(kvh, GROUP, 1, 1), do)


def _shift_add(own, to_prev, name):
    kvh, nb = own.shape[:2]
    blk = (1, 1, BLOCK, HEAD_DIM)

    def body(a_ref, b_ref, o_ref):
        n = pl.program_id(1)
        o_ref[...] = (a_ref[...] + jnp.where(n < nb - 1, b_ref[...], 0.0)).astype(o_ref.dtype)

    return pl.pallas_call(
        body, name=name, grid=(kvh, nb),
        in_specs=[pl.BlockSpec(blk, lambda g, n: (g, n, 0, 0)),
                  pl.BlockSpec(blk, lambda g, n: (g, jnp.minimum(n + 1, nb - 1), 0, 0))],
        out_specs=pl.BlockSpec(blk, lambda g, n: (g, n, 0, 0)),
        out_shape=jax.ShapeDtypeStruct(own.shape, BF16), compiler_params=_cparams(("parallel", "parallel")),
    )(own, to_prev)


def _gdn_gates(ba, alog, dt, r0, hv):
    lane = lax.broadcasted_iota(jnp.int32, ba.shape, 1)
    t = ba + dt
    softplus = jnp.maximum(t, 0.0) + jnp.log(1.0 + jnp.exp(-jnp.abs(t)))
    val = jnp.where(lane < hv, jax.nn.sigmoid(ba), jnp.where(lane < 2 * hv, -jnp.exp(alog) * softplus, 0.0))
    return jnp.where(_row_mask(r0, ba.shape[0], ba.shape[1]), val, 0.0)


def _l2n(x):
    return x * lax.rsqrt(jnp.sum(x * x, axis=-1, keepdims=True) + 1e-6)


def _bdot(a, b, dims=(((1,), (0,)), ((), ()))):
    return lax.dot_general(a.astype(BF16), b.astype(BF16), dims, preferred_element_type=F32)


def _hdot(a, b):
    return jnp.dot(a, b, precision=HIGHEST, preferred_element_type=F32)


_NT = (((1,), (1,)), ((), ()))
_TN = (((0,), (0,)), ((), ()))


def _gdn_head(state, q_raw, k_raw, v, z, g_col, beta_col, norm_w):
    c = GDN_CHUNK
    q = _l2n(jax.nn.silu(q_raw)) * (GDN_DIM ** -0.5)
    k = _l2n(jax.nn.silu(k_raw))
    v = jax.nn.silu(v)
    i = lax.broadcasted_iota(jnp.int32, (c, c), 0)
    j = lax.broadcasted_iota(jnp.int32, (c, c), 1)
    causal, strict = i >= j, i > j
    ones = jnp.ones((c, c), F32)
    g_b = jnp.broadcast_to(g_col, (c, GDN_DIM))
    gc = _hdot(causal.astype(F32), g_b)
    gc_i = gc[:, :c]
    gc_j = _hdot(ones, jnp.where(i <= j, g_b[:, :c], 0.0))
    gc_last_sq = _hdot(jnp.ones((GDN_DIM, c), F32), g_b)
    gc_last = gc_last_sq[:c]
    decay = jnp.where(causal, jnp.exp(jnp.where(causal, gc_i - gc_j, 0.0)), 0.0)
    k_beta = k * beta_col
    lower = jnp.where(strict, _bdot(k_beta, k, _NT) * decay, 0.0)
    eye = (i == j).astype(F32)
    neg = -lower
    inv = eye + neg
    power = neg
    for _ in range(5):
        power = _hdot(power, power)
        inv = _hdot(inv, eye + power)
    rhs = jnp.concatenate([v * beta_col, k_beta * jnp.exp(gc)], axis=1)
    sol = _hdot(inv, rhs)
    u, w = sol[:, :GDN_DIM], sol[:, GDN_DIM:]
    intra = jnp.where(causal, _bdot(q, k, _NT) * decay, 0.0)
    q_dec = q * jnp.exp(gc)
    k_dec = k * jnp.exp(gc_last - gc)
    v_new = u - _bdot(w, state)
    o = _bdot(q_dec, state) + _bdot(intra, v_new)
    new_state = state * jnp.exp(gc_last_sq) + _bdot(k_dec, v_new, _TN)
    y = _rms(o, norm_w) * jax.nn.silu(z)
    return y, new_state


def _gdn_pair(states, qkv, z, gates, norm_w, hk, hv_total):
    lane = lax.broadcasted_iota(jnp.int32, gates.shape, 1)
    ys, new = [], []
    for t in range(2):
        hv = 2 * hk + t
        beta_col = jnp.sum(jnp.where(lane == hv, gates, 0.0), axis=1, keepdims=True)
        g_col = jnp.sum(jnp.where(lane == hv_total + hv, gates, 0.0), axis=1, keepdims=True)
        y, s = _gdn_head(states[t], qkv[:, :GDN_DIM], qkv[:, GDN_DIM:2 * GDN_DIM],
                         qkv[:, (2 + t) * GDN_DIM:(3 + t) * GDN_DIM], z[:, t * GDN_DIM:(t + 1) * GDN_DIM], g_col, beta_col, norm_w)
        ys.append(y)
        new.append(s)
    return jnp.concatenate(ys, axis=1), jnp.stack(new)


def _gdn_fwd(cq, proj, gates, norm_w, hk_total, name):
    n_rows = cq.shape[0]
    nc, hv_total = n_rows // GDN_CHUNK, 2 * hk_total
    zblk0 = cq.shape[1] // (2 * GDN_DIM)

    def body(cq_ref, z_ref, g_ref, w_ref, y_ref, save_ref, state):
        n, hk = pl.program_id(0), pl.program_id(1)

        @pl.when(n == 0)
        def _():
            state[pl.ds(2 * hk, 2)] = jnp.zeros((2, GDN_DIM, GDN_DIM), F32)

        s_in = state[pl.ds(2 * hk, 2)]
        save_ref[0] = s_in
        y, s_out = _gdn_pair(s_in, cq_ref[...], z_ref[...], g_ref[...], w_ref[...], hk, hv_total)
        y_ref[...] = y.astype(y_ref.dtype)
        state[pl.ds(2 * hk, 2)] = s_out

    return pl.pallas_call(
        body, name=name, grid=(nc, hk_total),
        in_specs=[pl.BlockSpec((GDN_CHUNK, 4 * GDN_DIM), lambda n, h: (n, h)),
                  pl.BlockSpec((GDN_CHUNK, 2 * GDN_DIM), lambda n, h: (n, zblk0 + h)),
                  pl.BlockSpec((GDN_CHUNK, LANES), lambda n, h: (n, 0)),
                  pl.BlockSpec((1, GDN_DIM), lambda n, h: (0, 0))],
        out_specs=[pl.BlockSpec((GDN_CHUNK, 2 * GDN_DIM), lambda n, h: (n, h)),
                   pl.BlockSpec((1, 2, GDN_DIM, GDN_DIM), lambda n, h: (n, h, 0, 0))],
        out_shape=[jax.ShapeDtypeStruct((n_rows, hv_total * GDN_DIM), BF16),
                   jax.ShapeDtypeStruct((nc, hv_total, GDN_DIM, GDN_DIM), F32)],
        scratch_shapes=[pltpu.VMEM((hv_total, GDN_DIM, GDN_DIM), F32)],
        compiler_params=_cparams(("arbitrary", "arbitrary")),
    )(cq, proj, gates, norm_w.reshape(1, GDN_DIM))


def _gdn_bwd(cq, proj, gates, norm_w, saved, dy, hk_total, name):
    n_rows = cq.shape[0]
    nc, hv_total = n_rows // GDN_CHUNK, 2 * hk_total
    zblk0 = cq.shape[1] // (2 * GDN_DIM)

    def body(cq_ref, z_ref, g_ref, w_ref, save_ref, dy_ref, dcq_ref, dz_ref, dg_ref, dw_ref, dstate):
        n, hk = pl.program_id(0), pl.program_id(1)

        @pl.when(n == 0)
        def _():
            dstate[pl.ds(2 * hk, 2)] = jnp.zeros((2, GDN_DIM, GDN_DIM), F32)

        @pl.when((n == 0) & (hk == 0))
        def _():
            dw_ref[...] = jnp.zeros_like(dw_ref)

        @pl.when(hk == 0)
        def _():
            dg_ref[...] = jnp.zeros_like(dg_ref)

        f = functools.partial(_gdn_pair, hk=hk, hv_total=hv_total)
        _, vjp = jax.vjp(f, save_ref[0], cq_ref[...], z_ref[...], g_ref[...], w_ref[...])
        ds, dcq, dz, dg, dw = vjp((dy_ref[...].astype(F32), dstate[pl.ds(2 * hk, 2)]))
        dstate[pl.ds(2 * hk, 2)] = ds
        dcq_ref[...] = dcq
        dz_ref[...] = dz.astype(dz_ref.dtype)
        dg_ref[...] += dg
        dw_ref[...] += dw

    rev = lambda n: nc - 1 - n
    return pl.pallas_call(
        body, name=name, grid=(nc, hk_total),
        in_specs=[pl.BlockSpec((GDN_CHUNK, 4 * GDN_DIM), lambda n, h: (rev(n), h)),
                  pl.BlockSpec((GDN_CHUNK, 2 * GDN_DIM), lambda n, h: (rev(n), zblk0 + h)),
                  pl.BlockSpec((GDN_CHUNK, LANES), lambda n, h: (rev(n), 0)),
                  pl.BlockSpec((1, GDN_DIM), lambda n, h: (0, 0)),
                  pl.BlockSpec((1, 2, GDN_DIM, GDN_DIM), lambda n, h: (rev(n), h, 0, 0)),
                  pl.BlockSpec((GDN_CHUNK, 2 * GDN_DIM), lambda n, h: (rev(n), h))],
        out_specs=[pl.BlockSpec((GDN_CHUNK, 4 * GDN_DIM), lambda n, h: (rev(n), h)),
                   pl.BlockSpec((GDN_CHUNK, 2 * GDN_DIM), lambda n, h: (rev(n), h)),
                   pl.BlockSpec((GDN_CHUNK, LANES), lambda n, h: (rev(n), 0)),
                   pl.BlockSpec((1, GDN_DIM), lambda n, h: (0, 0))],
        out_shape=[jax.ShapeDtypeStruct(cq.shape, F32), jax.ShapeDtypeStruct((n_rows, hv_total * GDN_DIM), BF16),
                   jax.ShapeDtypeStruct((n_rows, LANES), F32), jax.ShapeDtypeStruct((1, GDN_DIM), F32)],
        scratch_shapes=[pltpu.VMEM((hv_total, GDN_DIM, GDN_DIM), F32)],
        compiler_params=_cparams(("arbitrary", "arbitrary")),
    )(cq, proj, gates, norm_w.reshape(1, GDN_DIM), saved, dy)


def _final_loss(h, w, target, name):
    d = h.shape[1]

    def fn(r0, hv, tv, wv):
        def loss_of(hh, ww):
            err = jnp.where(_row_mask(r0, hh.shape[0], d) & (r0 + lax.broadcasted_iota(jnp.int32, hh.shape, 0) >= FRONT),
                            _rms(hh, ww) - tv, 0.0)
            return 0.5 * jnp.sum(jnp.sum(err * err, axis=1, keepdims=True) / d)

        loss, vjp = jax.vjp(loss_of, hv, wv)
        dh, dw = vjp(jnp.ones((), F32))
        return dh, jnp.zeros((1, LANES), F32) + loss, dw

    dh, loss, dw = _rowwise(fn, [h, target], [w.reshape(1, -1)], [(d, F32)], [(1, LANES), (1, d)], name)
    return loss[0, 0], dh, dw[0]


def _rope_tables(n_rows):
    pos = (jnp.arange(n_rows) - PAD_LEN).astype(F32)
    inv_freq = ROPE_THETA ** (-jnp.arange(0, HEAD_DIM, 2, dtype=F32) / HEAD_DIM)
    ang = pos[:, None] * inv_freq[None, :]
    reps = LANES // (HEAD_DIM // 2)
    return jnp.tile(jnp.cos(ang), (1, reps)), jnp.tile(jnp.sin(ang), (1, reps))


def _to_heads(t):
    n_rows, w = t.shape
    return t.reshape(n_rows, w // HEAD_DIM, HEAD_DIM).transpose(1, 0, 2)


def _from_heads(t):
    heads, n_rows, _ = t.shape
    return t.transpose(1, 0, 2).reshape(n_rows, heads * HEAD_DIM)


def _local_step(h0, target, p):
    n_rows, d = h0.shape
    depth = p["norm_mix"].shape[0]
    cos, sin = _rope_tables(n_rows)
    hk_total = d // GDN_DIM
    hv_total = 2 * hk_total
    conv_dim = 4 * hk_total * GDN_DIM
    qw, kw = d, d // GROUP
    saved = []
    h = h0
    for i in range(depth):
        kind, j = i % 3, i // 3
        tag = f"l{i}"
        s = {"h": h}
        hn = _rms_fwd(h, p["norm_mix"][i], "rms_fwd")
        s["hn"] = hn
        if kind == 0:
            pre = _mm(hn, p["conv_w_pw1"][j], "nn", F32, "mm_pw1")
            u1 = _glu_fwd(pre, p["conv_b_pw1"][j], "glu_fwd")
            c = _dwconv_fwd(u1, p["conv_w_dw"][j], p["conv_b_dw"][j], d, "dwconv31_fwd")
            sv = _ln_silu_fwd(c, p["conv_ln_g"][j], p["conv_ln_b"][j], "ln_silu_fwd")
            h = _mm(sv, p["conv_w_pw2"][j], "nn", F32, "mm_d_d_res", bias=p["conv_b_pw2"][j], residual=h)
            s.update(pre=pre, u1=u1, c=c, sv=sv)
        elif kind == 1:
            pre = _mm(hn, p["attn_w_qkv"][j], "nn", F32, "mm_qkv")
            q, k, v = _qkv_post_fwd(pre, p["attn_b_qkv"][j], cos, sin, qw, kw, "qkv_post_fwd")
            qh, kh, vh = _to_heads(q), _to_heads(k), _to_heads(v)
            o = _from_heads(_attn_fwd(qh, kh, vh, p["attn_sinks"][j], "attn_fwd"))
            h = _mm(o, p["attn_w_o"][j], "nn", F32, "mm_d_d_res", bias=p["attn_b_o"][j], residual=h)
            s.update(qh=qh, kh=kh, vh=vh, o=o)
        else:
            proj = _mm(hn, p["gdn_w_in"][j], "nn", F32, "mm_gdn_in")
            cq = _dwconv_fwd(proj, p["gdn_conv_w"][j], None, conv_dim, "dwconv4_fwd")
            vec = _gate_vectors(p["gdn_a_log"][j], p["gdn_dt_bias"][j], hv_total)
            ba_blk = (conv_dim + hv_total * GDN_DIM) // LANES
            (gates,) = _rowwise(lambda r0, bav, av, dv: (_gdn_gates(bav, av, dv, r0, hv_total),), [(proj, LANES, ba_blk)],
                                [vec[0], vec[1]], [(LANES, F32)], [], "gdn_gates_fwd")
            y, states = _gdn_fwd(cq, proj, gates, p["gdn_norm_w"][j], hk_total, "gdn_fwd")
            h = _mm(y, p["gdn_w_out"][j], "nn", F32, "mm_gdn_out_res", residual=h)
            s.update(proj=proj, cq=cq, gates=gates, y=y, states=states)
        s["h1"] = h
        hn2 = _rms_fwd(h, p["norm_ffn"][i], "rms_fwd")
        gu = _mm(hn2, p["ffn_w_gu"][i], "nn", BF16, "mm_ffn_gu")
        a = _swiglu_fwd(gu, "swiglu_fwd")
        h = _mm(a, p["ffn_w_down"][i], "nn", F32, "mm_ffn_down_res", residual=h)
        s.update(hn2=hn2, gu=gu, a=a)
        saved.append(s)

    loss, dh, g_final = _final_loss(h, p["norm_final"], target, "final_loss")
    g = {k: [None] * v.shape[0] for k, v in p.items() if k != "norm_final"}
    g["norm_final"] = g_final
    for i in reversed(range(depth)):
        kind, j = i % 3, i // 3
        s = saved[i]
        g["ffn_w_down"][i] = _mm(s["a"], dh, "tn", BF16, "mm_dw_down")
        da = _mm(dh, p["ffn_w_down"][i], "nt", F32, "mm_da")
        dgu = _swiglu_bwd(s["gu"], da, "swiglu_bwd")
        g["ffn_w_gu"][i] = _mm(s["hn2"], dgu, "tn", BF16, "mm_dw_gu")
        dhn2 = _mm(dgu, p["ffn_w_gu"][i], "nt", F32, "mm_dhn2")
        dh, g["norm_ffn"][i] = _rms_bwd(s["h1"], p["norm_ffn"][i], dhn2, dh, "rms_bwd")
        if kind == 0:
            g["conv_b_pw2"][j] = _colsum_rows(dh, "colsum_d")
            g["conv_w_pw2"][j] = _mm(s["sv"], dh, "tn", BF16, "mm_dw_d_d")
            dsv = _mm(dh, p["conv_w_pw2"][j], "nt", F32, "mm_dx_d_d")
            dc, g["conv_ln_g"][j], g["conv_ln_b"][j] = _ln_silu_bwd(s["c"], p["conv_ln_g"][j], p["conv_ln_b"][j], dsv, "ln_silu_bwd")
            du1, g["conv_w_dw"][j], g["conv_b_dw"][j] = _dwconv_bwd(s["u1"], p["conv_w_dw"][j], dc, d, "dwconv31_bwd")
            dpre, g["conv_b_pw1"][j] = _glu_bwd(s["pre"], p["conv_b_pw1"][j], du1, "glu_bwd")
            g["conv_w_pw1"][j] = _mm(s["hn"], dpre, "tn", BF16, "mm_dw_pw1")
            dhn = _mm(dpre, p["conv_w_pw1"][j], "nt", F32, "mm_dx_pw1")
        elif kind == 1:
            g["attn_b_o"][j] = _colsum_rows(dh, "colsum_d")
            g["attn_w_o"][j] = _mm(s["o"], dh, "tn", BF16, "mm_dw_d_d")
            do = _to_heads(_mm(dh, p["attn_w_o"][j], "nt", BF16, "mm_dx_d_d_bf16"))
            dq, dkp, dkc, dvp, dvc, dsink = _attn_bwd(s["qh"], s["kh"], s["vh"], p["attn_sinks"][j], do, "attn_bwd")
            g["attn_sinks"][j] = dsink.reshape(-1)
            kvh = kw // HEAD_DIM
            dk = _shift_add(dkc, dkp, "attn_shift_add").reshape(kvh, n_rows, HEAD_DIM)
            dv = _shift_add(dvc, dvp, "attn_shift_add").reshape(kvh, n_rows, HEAD_DIM)
            dpre, g["attn_b_qkv"][j] = _qkv_post_bwd(_from_heads(dq), _from_heads(dk), _from_heads(dv), cos, sin, "qkv_post_bwd")
            g["attn_w_qkv"][j] = _mm(s["hn"], dpre, "tn", BF16, "mm_dw_qkv")
            dhn = _mm(dpre, p["attn_w_qkv"][j], "nt", F32, "mm_dx_qkv")
        else:
            g["gdn_w_out"][j] = _mm(s["y"], dh, "tn", BF16, "mm_dw_gdn_out")
            dy = _mm(dh, p["gdn_w_out"][j], "nt", BF16, "mm_dx_gdn_out")
            dcq, dz, dgates, g_nw = _gdn_bwd(s["cq"], s["proj"], s["gates"], p["gdn_norm_w"][j], s["states"], dy, hk_total, "gdn_bwd")
            g["gdn_norm_w"][j] = g_nw[0]
            vec = _gate_vectors(p["gdn_a_log"][j], p["gdn_dt_bias"][j], hv_total)
            ba_blk = (conv_dim + hv_total * GDN_DIM) // LANES

            def gates_bwd(r0, bav, dgv, av, dv):
                _, vjp = jax.vjp(functools.partial(_gdn_gates, r0=r0, hv=hv_total), bav, av, dv)
                return vjp(dgv)

            dba, d_alog, d_dt = _rowwise(gates_bwd, [(s["proj"], LANES, ba_blk), dgates], [vec[0], vec[1]], [(LANES, BF16)],
                                         [(1, LANES), (1, LANES)], "gdn_gates_bwd")
            g["gdn_a_log"][j] = d_alog[0, hv_total:2 * hv_total]
            g["gdn_dt_bias"][j] = d_dt[0, hv_total:2 * hv_total]
            dconv_in, g["gdn_conv_w"][j], _ = _dwconv_bwd(s["proj"], p["gdn_conv_w"][j], dcq, conv_dim, "dwconv4_bwd")
            width = p["gdn_w_in"][j].shape[1]
            pad = jnp.zeros((n_rows, width - conv_dim - hv_total * GDN_DIM - LANES), BF16)
            dproj = jnp.concatenate([dconv_in.astype(BF16), dz, dba, pad], axis=1)
            g["gdn_w_in"][j] = _mm(s["hn"], dproj, "tn", BF16, "mm_dw_gdn_in")
            dhn = _mm(dproj, p["gdn_w_in"][j], "nt", F32, "mm_dx_gdn_in")
        dh, g["norm_mix"][i] = _rms_bwd(s["h"], p["norm_mix"][i], dhn, dh, "rms_bwd")
    g = {k: (jnp.stack(v) if isinstance(v, list) else v) for k, v in g.items()}
    return loss, dh, g


def _gate_vectors(a_log, dt_bias, hv):
    z = jnp.zeros((1, LANES), F32)
    return z.at[0, hv:2 * hv].set(a_log), z.at[0, hv:2 * hv].set(dt_bias)


GDN_IN_ALIGN = 512


def _gdn_group(w, hk):
    lead, kw = w.shape[:-1], hk * GDN_DIM
    q = w[..., :kw].reshape(*lead, hk, 1, GDN_DIM)
    k = w[..., kw:2 * kw].reshape(*lead, hk, 1, GDN_DIM)
    v = w[..., 2 * kw:4 * kw].reshape(*lead, hk, 2, GDN_DIM)
    return jnp.concatenate([q, k, v], axis=-2).reshape(*lead, 4 * kw)


def _gdn_ungroup(w, hk):
    lead, kw = w.shape[:-1], hk * GDN_DIM
    t = w.reshape(*lead, hk, 4, GDN_DIM)
    return jnp.concatenate([t[..., 0, :].reshape(*lead, kw), t[..., 1, :].reshape(*lead, kw),
                            t[..., 2:, :].reshape(*lead, 2 * kw)], axis=-1)


def _gdn_in_layout(w, hk):
    conv_dim = 4 * hk * GDN_DIM
    width = -(-w.shape[-1] // GDN_IN_ALIGN) * GDN_IN_ALIGN
    pad = jnp.zeros(w.shape[:-1] + (width - w.shape[-1],), w.dtype)
    return jnp.concatenate([_gdn_group(w[..., :conv_dim], hk), w[..., conv_dim:], pad], axis=-1)


def _gdn_in_natural(w, hk, in_width):
    conv_dim = 4 * hk * GDN_DIM
    return jnp.concatenate([_gdn_ungroup(w[..., :conv_dim], hk), w[..., conv_dim:in_width]], axis=-1)


def _step_layout(full):
    hk = full["norm_final"].shape[0] // GDN_DIM
    p = {k: v for k, v in full.items() if k not in ("meta_tokens", "ffn_w_gate", "ffn_w_up", "gdn_w_in", "gdn_conv_w")}
    p["ffn_w_gu"] = jnp.concatenate([full["ffn_w_gate"], full["ffn_w_up"]], axis=-1)
    p["gdn_w_in"] = _gdn_in_layout(full["gdn_w_in"], hk)
    p["gdn_conv_w"] = _gdn_group(full["gdn_conv_w"], hk)
    return p


def _natural_grads(g, in_width):
    hk = g["norm_final"].shape[0] // GDN_DIM
    out = {k: v for k, v in g.items() if k not in ("ffn_w_gu", "gdn_w_in", "gdn_conv_w")}
    f = g["ffn_w_gu"].shape[-1] // 2
    out["ffn_w_gate"], out["ffn_w_up"] = g["ffn_w_gu"][..., :f], g["ffn_w_gu"][..., f:]
    out["gdn_w_in"] = _gdn_in_natural(g["gdn_w_in"], hk, in_width)
    out["gdn_conv_w"] = _gdn_ungroup(g["gdn_conv_w"], hk)
    return out


def _exchange(src, gather, name):
    n = src.shape[0]
    out_shape = (n, N_DEV) + src.shape[-2:]

    def body(src_ref, out_ref, send_sems, recv_sems, local_sems):
        x, y, c = lax.axis_index("x"), lax.axis_index("y"), lax.axis_index("c")
        me = 4 * x + 2 * y + c
        remote, local = [], []
        for l in range(n):
            own = pltpu.make_async_copy(src_ref.at[l] if gather else src_ref.at[l, me], out_ref.at[l, me], local_sems.at[l])
            own.start()
            local.append(own)
            for k in range(1, N_DEV):
                px = 1 - x if k & 4 else x
                py = 1 - y if k & 2 else y
                pc = 1 - c if k & 1 else c
                peer = 4 * px + 2 * py + pc
                cp = pltpu.make_async_remote_copy(
                    src_ref=src_ref.at[l] if gather else src_ref.at[l, peer], dst_ref=out_ref.at[l, me],
                    send_sem=send_sems.at[l, k - 1], recv_sem=recv_sems.at[l, k - 1],
                    device_id=(px, py, pc), device_id_type=pl.DeviceIdType.MESH)
                cp.start()
                remote.append(cp)
        for cp in remote:
            cp.wait_send()
        for cp in remote:
            cp.wait_recv()
        for cp in local:
            cp.wait()

    return pl.pallas_call(
        body, name=name, out_shape=jax.ShapeDtypeStruct(out_shape, src.dtype),
        in_specs=[pl.BlockSpec(memory_space=pl.ANY)], out_specs=pl.BlockSpec(memory_space=pl.ANY),
        scratch_shapes=[pltpu.SemaphoreType.DMA((n, N_DEV - 1)), pltpu.SemaphoreType.DMA((n, N_DEV - 1)),
                        pltpu.SemaphoreType.DMA((n,))],
        compiler_params=pltpu.CompilerParams(has_side_effects=True),
    )(src)


def _cast_bf16(w, name):
    n, r, c = w.shape
    tr = _tile(r, max(16, (1 << 20) // c), 16)

    def body(w_ref, o_ref):
        o_ref[...] = w_ref[...].astype(BF16)

    return pl.pallas_call(
        body, name=name, grid=(n, r // tr), in_specs=[pl.BlockSpec((1, tr, c), lambda l, i: (l, i, 0))],
        out_specs=pl.BlockSpec((1, tr, c), lambda l, i: (l, i, 0)), out_shape=jax.ShapeDtypeStruct(w.shape, BF16),
        compiler_params=_cparams(("parallel", "parallel")),
    )(w)


def _adamw(w, g, m, v):
    m = ADAM_B1 * m + (1.0 - ADAM_B1) * g
    v = ADAM_B2 * v + (1.0 - ADAM_B2) * jnp.square(g)
    m_hat = m / (1.0 - ADAM_B1 ** ADAM_STEP)
    v_hat = v / (1.0 - ADAM_B2 ** ADAM_STEP)
    delta = -ADAM_LR * (m_hat / (jnp.sqrt(v_hat) + ADAM_EPS) + ADAM_WD * w)
    return delta, m, v


def _sum8_adam(parts, w, m, v, name):
    n, _, r, c = parts.shape
    tr = _tile(r, max(16, (1 << 18) // c), 16)
    blk = pl.BlockSpec((1, tr, c), lambda l, i: (l, i, 0))

    def body(p_ref, w_ref, m_ref, v_ref, g_ref, d_ref, mo_ref, vo_ref):
        g = p_ref[0, 0].astype(F32)
        for s in range(1, N_DEV):
            g = g + p_ref[0, s].astype(F32)
        delta, m2, v2 = _adamw(w_ref[0], g, m_ref[0], v_ref[0])
        g_ref[0], d_ref[0], mo_ref[0], vo_ref[0] = g, delta, m2, v2

    shp = jax.ShapeDtypeStruct(w.shape, F32)
    return pl.pallas_call(
        body, name=name, grid=(n, r // tr),
        in_specs=[pl.BlockSpec((1, N_DEV, tr, c), lambda l, i: (l, 0, i, 0)), blk, blk, blk],
        out_specs=[blk, blk, blk, blk], out_shape=[shp, shp, shp, shp],
        compiler_params=_cparams(("parallel", "parallel")),
    )(parts, w, m, v)


PACK_ROWS = 8


def _pack(arrays):
    flat = jnp.concatenate([a.reshape(-1).astype(F32) for a in arrays])
    unit = PACK_ROWS * LANES
    total = -(-flat.shape[0] // unit) * unit
    return jnp.concatenate([flat, jnp.zeros((total - flat.shape[0],), F32)]).reshape(-1, LANES)


def _unpack(packed, shapes):
    flat, out, pos = packed.reshape(-1), [], 0
    for s in shapes:
        size = math.prod(s)
        out.append(flat[pos:pos + size].reshape(s))
        pos += size
    return out


BIG_COL = ("conv_w_pw1", "attn_w_qkv", "gdn_w_in", "ffn_w_gate", "ffn_w_up")
BIG_ROW = ("conv_w_pw2", "attn_w_o", "gdn_w_out", "ffn_w_down")
SMALL_SHARDED = ("meta_tokens", "conv_b_pw1", "conv_w_dw", "conv_b_dw", "conv_ln_g", "conv_ln_b", "conv_b_pw2", "gdn_conv_w")
REPLICATED = ("norm_mix", "norm_ffn", "norm_final", "attn_b_qkv", "attn_sinks", "attn_b_o", "gdn_a_log", "gdn_dt_bias", "gdn_norm_w")
WEIGHTS = ("meta_tokens", "norm_mix", "norm_ffn", "norm_final", "conv_w_pw1", "conv_b_pw1", "conv_w_dw", "conv_b_dw", "conv_ln_g",
           "conv_ln_b", "conv_w_pw2", "conv_b_pw2", "attn_w_qkv", "attn_b_qkv", "attn_sinks", "attn_w_o", "attn_b_o", "gdn_w_in",
           "gdn_conv_w", "gdn_a_log", "gdn_dt_bias", "gdn_norm_w", "gdn_w_out", "ffn_w_gate", "ffn_w_up", "ffn_w_down")


def kernel(x, meta_tokens, norm_mix, norm_ffn, norm_final, conv_w_pw1, conv_b_pw1, conv_w_dw, conv_b_dw, conv_ln_g, conv_ln_b, conv_w_pw2, conv_b_pw2, attn_w_qkv, attn_b_qkv, attn_sinks, attn_w_o, attn_b_o, gdn_w_in, gdn_conv_w, gdn_a_log, gdn_dt_bias, gdn_norm_w, gdn_w_out, ffn_w_gate, ffn_w_up, ffn_w_down, loss_target, m_meta_tokens, m_norm_mix, m_norm_ffn, m_norm_final, m_conv_w_pw1, m_conv_b_pw1, m_conv_w_dw, m_conv_b_dw, m_conv_ln_g, m_conv_ln_b, m_conv_w_pw2, m_conv_b_pw2, m_attn_w_qkv, m_attn_b_qkv, m_attn_sinks, m_attn_w_o, m_attn_b_o, m_gdn_w_in, m_gdn_conv_w, m_gdn_a_log, m_gdn_dt_bias, m_gdn_norm_w, m_gdn_w_out, m_ffn_w_gate, m_ffn_w_up, m_ffn_w_down, v_meta_tokens, v_norm_mix, v_norm_ffn, v_norm_final, v_conv_w_pw1, v_conv_b_pw1, v_conv_w_dw, v_conv_b_dw, v_conv_ln_g, v_conv_ln_b, v_conv_w_pw2, v_conv_b_pw2, v_attn_w_qkv, v_attn_b_qkv, v_attn_sinks, v_attn_w_o, v_attn_b_o, v_gdn_w_in, v_gdn_conv_w, v_gdn_a_log, v_gdn_dt_bias, v_gdn_norm_w, v_gdn_w_out, v_ffn_w_gate, v_ffn_w_up, v_ffn_w_down):
    a = dict(locals())
    me = 4 * lax.axis_index("x") + 2 * lax.axis_index("y") + lax.axis_index("c")
    d = x.shape[-1]

    full = {k: a[k] for k in REPLICATED}
    for k in BIG_COL + BIG_ROW:
        n, r, c = a[k].shape
        got = _exchange(_cast_bf16(a[k], "cast_bf16"), True, "gather_weights")
        full[k] = got.transpose(0, 2, 1, 3).reshape(n, r, N_DEV * c) if k in BIG_COL else got.reshape(n, N_DEV * r, c)
    shard_shapes = [a[k].shape for k in SMALL_SHARDED]
    got = _exchange(_pack([a[k] for k in SMALL_SHARDED])[None], True, "gather_small")[0]
    per_dev = [_unpack(got[s], shard_shapes) for s in range(N_DEV)]
    for i, k in enumerate(SMALL_SHARDED):
        st = jnp.stack([per_dev[s][i] for s in range(N_DEV)], axis=-2)
        full[k] = st.reshape(st.shape[:-2] + (N_DEV * st.shape[-1],))

    h0 = jnp.concatenate([jnp.zeros((PAD_LEN, d), F32), full["meta_tokens"], x[0]], axis=0)
    target = jnp.concatenate([jnp.zeros((FRONT, d), F32), loss_target[0]], axis=0)
    loss, dh0, g = _local_step(h0, target, _step_layout(full))
    g = _natural_grads(g, full["gdn_w_in"].shape[-1])
    g["meta_tokens"] = dh0[PAD_LEN:FRONT]
    loss = lax.psum(loss, AXES)
    grad_x = dh0[FRONT:][None]

    grads, deltas, new_m, new_v = {}, {}, {}, {}
    for k in BIG_COL + BIG_ROW:
        n, r, c = a[k].shape
        pieces = g[k].reshape(n, r, N_DEV, c).transpose(0, 2, 1, 3) if k in BIG_COL else g[k].reshape(n, N_DEV, r, c)
        parts = _exchange(pieces, False, "scatter_grads")
        grads[k], deltas[k], new_m[k], new_v[k] = _sum8_adam(parts, a[k], a["m_" + k], a["v_" + k], "sum8_adamw")

    small = SMALL_SHARDED + REPLICATED
    full_shapes = [full[k].shape for k in small]
    got = _exchange(_pack([g[k] for k in small])[None], True, "gather_small_grads")[0]
    (total,) = _rowwise(lambda r0, *t: (functools.reduce(lambda p, q: p + q, t),), [got[s] for s in range(N_DEV)], [],
                        [(LANES, F32)], [], "sum8_small", tm=got.shape[1])
    for k, t in zip(small, _unpack(total, full_shapes)):
        if k in SMALL_SHARDED:
            c = a[k].shape[-1]
            t = lax.dynamic_index_in_dim(t.reshape(t.shape[:-1] + (N_DEV, c)), me, axis=t.ndim - 1, keepdims=False)
        grads[k] = t
    shapes = [a[k].shape for k in small]
    packed = [_pack([src[k] for k in small]) for src in (grads, a, {k: a["m_" + k] for k in small}, {k: a["v_" + k] for k in small})]

    def small_adam(r0, gv, wv, mv, vv):
        return _adamw(wv, gv, mv, vv)

    outs = _rowwise(small_adam, packed, [], [(LANES, F32)] * 3, [], "adamw_small", tm=packed[0].shape[0])
    for dst, o in zip((deltas, new_m, new_v), outs):
        for k, t in zip(small, _unpack(o, shapes)):
            dst[k] = t

    return (loss, grad_x, *[grads[k] for k in WEIGHTS], *[deltas[k] for k in WEIGHTS], *[new_m[k] for k in WEIGHTS],
            *[new_v[k] for k in WEIGHTS])
```

```python
import functools
import math

import jax
import jax.numpy as jnp
from jax import lax
from jax.experimental import pallas as pl
from jax.experimental.pallas import tpu as pltpu

F32 = jnp.float32
BF16 = jnp.bfloat16

AXES = ("x", "y", "c")
N_DEV = 8

N_META = 16
FRONT = 128
PAD_LEN = FRONT - N_META
NORM_EPS = 1e-6
LN_EPS = 1e-5
NEG_INF = -1e30
CONV_KERNEL = 31
HEAD_DIM = 64
GROUP = 8
BLOCK = 128
ROPE_THETA = 10000.0
GDN_DIM = 128
GDN_CONV = 4
GDN_CHUNK = 64
ADAM_LR, ADAM_B1, ADAM_B2, ADAM_EPS, ADAM_WD, ADAM_STEP = 0.001, 0.9, 0.999, 1e-08, 0.01, 10

VMEM_LIMIT_BYTES = 52 * 1024 * 1024
LANES = 128
CONV_HALO = 32


def _tile(n, pref, align=128):
    best = None
    for t in range(align, min(n, pref) + 1, align):
        if n % t == 0:
            best = t
    return best if best is not None else n


def _cparams(sem):
    return pltpu.CompilerParams(dimension_semantics=sem, vmem_limit_bytes=VMEM_LIMIT_BYTES)


def _mm(a, b, mode, out_dtype, name, bias=None, residual=None):
    if mode == "nn":
        (m, k), (k2, n) = a.shape, b.shape
    elif mode == "nt":
        (m, k), (n, k2) = a.shape, b.shape
    else:
        (k, m), (k2, n) = a.shape, b.shape
    assert k == k2, (a.shape, b.shape, mode)
    tm = _tile(m, 1024 if mode == "tn" else 640)
    tn = _tile(n, 1280)
    tk = _tile(k, 2048)
    nk = k // tk
    dims = {"nn": (((1,), (0,)), ((), ())), "nt": (((1,), (1,)), ((), ())), "tn": (((0,), (0,)), ((), ()))}[mode]
    a_spec = pl.BlockSpec((tk, tm), lambda i, j, kk: (kk, i)) if mode == "tn" else pl.BlockSpec((tm, tk), lambda i, j, kk: (i, kk))
    b_spec = pl.BlockSpec((tn, tk), lambda i, j, kk: (j, kk)) if mode == "nt" else pl.BlockSpec((tk, tn), lambda i, j, kk: (kk, j))
    ins, specs = [a, b], [a_spec, b_spec]
    if bias is not None:
        ins.append(bias.reshape(1, n).astype(F32))
        specs.append(pl.BlockSpec((1, tn), lambda i, j, kk: (0, j)))
    if residual is not None:
        ins.append(residual)
        specs.append(pl.BlockSpec((tm, tn), lambda i, j, kk: (i, j)))
    has_bias, has_res = bias is not None, residual is not None

    def body(*refs):
        a_ref, b_ref = refs[0], refs[1]
        o_ref, acc = refs[-2], refs[-1]
        kk = pl.program_id(2)

        @pl.when(kk == 0)
        def _():
            acc[...] = jnp.zeros_like(acc)

        acc[...] += lax.dot_general(a_ref[...].astype(BF16), b_ref[...].astype(BF16), dims, preferred_element_type=F32)

        @pl.when(kk == nk - 1)
        def _():
            r = acc[...]
            pos = 2
            if has_bias:
                r = r + refs[pos][...]
                pos += 1
            if has_res:
                r = r + refs[pos][...].astype(F32)
            o_ref[...] = r.astype(o_ref.dtype)

    return pl.pallas_call(
        body, name=name, grid=(m // tm, n // tn, nk), in_specs=specs,
        out_specs=pl.BlockSpec((tm, tn), lambda i, j, kk: (i, j)),
        out_shape=jax.ShapeDtypeStruct((m, n), out_dtype),
        scratch_shapes=[pltpu.VMEM((tm, tn), F32)],
        compiler_params=_cparams(("parallel", "parallel", "arbitrary")),
    )(*ins)


def _rowwise(fn, rows, consts, out_rows, out_accs, name, tm=None):
    rows = [r if isinstance(r, tuple) else (r, r.shape[1], 0) for r in rows]
    n_rows = rows[0][0].shape[0]
    widest = max([w for _, w, _ in rows] + [w for w, _ in out_rows])
    if tm is None:
        tm = _tile(n_rows, 640 if widest <= 2560 else 128, 8)
    steps = n_rows // tm
    in_specs = [pl.BlockSpec((tm, w), functools.partial(lambda i, c: (i, c), c=cb)) for _, w, cb in rows]
    in_specs += [pl.BlockSpec(c.shape, lambda i: (0, 0)) for c in consts]
    out_specs = [pl.BlockSpec((tm, w), lambda i: (i, 0)) for w, _ in out_rows]
    out_specs += [pl.BlockSpec(s, lambda i: (0, 0)) for s in out_accs]
    out_shape = [jax.ShapeDtypeStruct((n_rows, w), d) for w, d in out_rows]
    out_shape += [jax.ShapeDtypeStruct(s, F32) for s in out_accs]
    n_in, n_or = len(rows) + len(consts), len(out_rows)

    def body(*refs):
        i = pl.program_id(0)
        vals = fn(i * tm, *[r[...] for r in refs[:n_in]])
        outs = refs[n_in:]
        for o_ref, v in zip(outs[:n_or], vals[:n_or]):
            o_ref[...] = v.astype(o_ref.dtype)
        if out_accs:
            @pl.when(i == 0)
            def _():
                for a_ref in outs[n_or:]:
                    a_ref[...] = jnp.zeros_like(a_ref)

            for a_ref, v in zip(outs[n_or:], vals[n_or:]):
                a_ref[...] += v

    res = pl.pallas_call(
        body, name=name, grid=(steps,), in_specs=in_specs, out_specs=out_specs, out_shape=out_shape,
        compiler_params=_cparams(("arbitrary",) if out_accs else ("parallel",)),
    )(*[r[0] for r in rows], *consts)
    return res


def _colsum(v):
    return jnp.sum(v, axis=0, keepdims=True)


def _rms(h, w):
    return h * lax.rsqrt(jnp.mean(h * h, axis=-1, keepdims=True) + NORM_EPS) * w


def _rms_fwd(h, w, name):
    (hn,) = _rowwise(lambda r0, hv, wv: (_rms(hv, wv),), [h], [w.reshape(1, -1)], [(h.shape[1], BF16)], [], name)
    return hn


def _rms_bwd(h, w, dhn, dh_in, name):
    d = h.shape[1]

    def fn(r0, hv, dv, rv, wv):
        _, vjp = jax.vjp(_rms, hv, wv)
        dh, dw = vjp(dv.astype(F32))
        return dh + rv, dw

    dh, dw = _rowwise(fn, [h, dhn, dh_in], [w.reshape(1, -1)], [(d, F32)], [(1, d)], name)
    return dh, dw[0]


def _swiglu(gu):
    f = gu.shape[1] // 2
    g, u = gu[:, :f].astype(F32), gu[:, f:].astype(F32)
    return jax.nn.silu(g) * u


def _swiglu_fwd(gu, name):
    (a,) = _rowwise(lambda r0, v: (_swiglu(v),), [gu], [], [(gu.shape[1] // 2, BF16)], [], name)
    return a


def _swiglu_bwd(gu, da, name):
    def fn(r0, v, dv):
        _, vjp = jax.vjp(_swiglu, v.astype(F32))
        return vjp(dv.astype(F32))

    (dgu,) = _rowwise(fn, [gu, da], [], [(gu.shape[1], BF16)], [], name)
    return dgu


def _glu(p, b):
    t = p + b
    d = t.shape[1] // 2
    return t[:, :d] * jax.nn.sigmoid(t[:, d:])


def _glu_fwd(p, b, name):
    (u,) = _rowwise(lambda r0, v, bv: (_glu(v, bv),), [p], [b.reshape(1, -1)], [(p.shape[1] // 2, F32)], [], name)
    return u


def _glu_bwd(p, b, du, name):
    def fn(r0, v, dv, bv):
        _, vjp = jax.vjp(_glu, v, bv)
        dp, db = vjp(dv)
        return dp, db

    dp, db = _rowwise(fn, [p, du], [b.reshape(1, -1)], [(p.shape[1], BF16)], [(1, p.shape[1])], name)
    return dp, db[0]


def _ln_silu(c, g, b):
    mu = jnp.mean(c, axis=-1, keepdims=True)
    xc = c - mu
    var = jnp.mean(xc * xc, axis=-1, keepdims=True)
    return jax.nn.silu(xc * lax.rsqrt(var + LN_EPS) * g + b)


def _ln_silu_fwd(c, g, b, name):
    (s,) = _rowwise(lambda r0, v, gv, bv: (_ln_silu(v, gv, bv),), [c], [g.reshape(1, -1), b.reshape(1, -1)],
                    [(c.shape[1], BF16)], [], name)
    return s


def _ln_silu_bwd(c, g, b, ds, name):
    d = c.shape[1]

    def fn(r0, v, dv, gv, bv):
        _, vjp = jax.vjp(_ln_silu, v, gv, bv)
        return vjp(dv.astype(F32))

    dc, dg, db = _rowwise(fn, [c, ds], [g.reshape(1, -1), b.reshape(1, -1)], [(d, F32)], [(1, d), (1, d)], name)
    return dc, dg[0], db[0]


def _colsum_rows(v, name):
    (s,) = _rowwise(lambda r0, t: (_colsum(t.astype(F32)),), [v], [], [], [(1, v.shape[1])], name)
    return s[0]


def _rot_half(x):
    w = x.shape[1]
    lane = lax.broadcasted_iota(jnp.int32, x.shape, 1)
    lo = (lane % HEAD_DIM) < (HEAD_DIM // 2)
    return jnp.where(lo, -pltpu.roll(x, w - HEAD_DIM // 2, axis=1), pltpu.roll(x, HEAD_DIM // 2, axis=1))


def _qkv_post_fwd(pre, b, cos, sin, qw, kw, name):
    reps = (qw + kw) // LANES

    def fn(r0, pv, cv, sv, bv):
        t = pv + bv
        tq = t[:, :qw + kw]
        y = tq * jnp.tile(cv, (1, reps)) + _rot_half(tq) * jnp.tile(sv, (1, reps))
        return y[:, :qw], y[:, qw:], t[:, qw + kw:]

    return _rowwise(fn, [pre, cos, sin], [b.reshape(1, -1)], [(qw, BF16), (kw, BF16), (kw, BF16)], [], name)


def _qkv_post_bwd(dq, dk, dv, cos, sin, name):
    qw, kw = dq.shape[1], dk.shape[1]
    reps = (qw + kw) // LANES
    width = qw + 2 * kw

    def fn(r0, dqv, dkv, dvv, cv, sv):
        dy = jnp.concatenate([dqv.astype(F32), dkv.astype(F32)], axis=1)
        dt = dy * jnp.tile(cv, (1, reps)) - _rot_half(dy * jnp.tile(sv, (1, reps)))
        dpre = jnp.concatenate([dt, dvv.astype(F32)], axis=1)
        return dpre, _colsum(dpre)

    dpre, db = _rowwise(fn, [dq, dk, dv, cos, sin], [], [(width, BF16)], [(1, width)], name)
    return dpre, db[0]


def _dw_tiles(n_rows, c):
    return _tile(n_rows, 640, 8), _tile(c, 512)


def _row_mask(r0, n, width):
    row = r0 + lax.broadcasted_iota(jnp.int32, (n, width), 0)
    return row >= PAD_LEN


def _dwconv_fwd(u, w, bias, c, name):
    n_rows, taps = u.shape[0], w.shape[0]
    tr, cb = _dw_tiles(n_rows, c)
    kp = -(-taps // 8) * 8
    wp = jnp.concatenate([w.astype(F32), jnp.zeros((kp - taps, c), F32)], axis=0)
    bp = jnp.zeros((1, c), F32) if bias is None else bias.reshape(1, c).astype(F32)

    def body(cur_ref, prev_ref, w_ref, b_ref, o_ref):
        r = pl.program_id(1)
        cur = jnp.where(_row_mask(r * tr, tr, cb), cur_ref[...], 0.0)
        tail = prev_ref[tr - CONV_HALO:, :]
        tail = jnp.where(_row_mask(r * tr - CONV_HALO, CONV_HALO, cb) & (r > 0), tail, 0.0)
        win = jnp.concatenate([tail, cur], axis=0)
        acc = jnp.zeros((tr, cb), F32) + b_ref[...]
        for k in range(taps):
            off = CONV_HALO - (taps - 1) + k
            acc = acc + w_ref[k:k + 1, :] * win[off:off + tr, :]
        o_ref[...] = acc

    return pl.pallas_call(
        body, name=name, grid=(c // cb, n_rows // tr),
        in_specs=[pl.BlockSpec((tr, cb), lambda j, r: (r, j)),
                  pl.BlockSpec((tr, cb), lambda j, r: (jnp.maximum(r - 1, 0), j)),
                  pl.BlockSpec((kp, cb), lambda j, r: (0, j)),
                  pl.BlockSpec((1, cb), lambda j, r: (0, j))],
        out_specs=pl.BlockSpec((tr, cb), lambda j, r: (r, j)),
        out_shape=jax.ShapeDtypeStruct((n_rows, c), F32),
        compiler_params=_cparams(("parallel", "parallel")),
    )(u, u, wp, bp)


def _dwconv_bwd(u, w, dc, c, name):
    n_rows, taps = u.shape[0], w.shape[0]
    tr, cb = _dw_tiles(n_rows, c)
    nr = n_rows // tr
    kp = -(-taps // 8) * 8
    wp = jnp.concatenate([w.astype(F32), jnp.zeros((kp - taps, c), F32)], axis=0)

    def body(cur_ref, prev_ref, d_ref, dnext_ref, w_ref, du_ref, dw_ref, db_ref):
        r = pl.program_id(1)
        cur = jnp.where(_row_mask(r * tr, tr, cb), cur_ref[...], 0.0)
        tail = prev_ref[tr - CONV_HALO:, :]
        tail = jnp.where(_row_mask(r * tr - CONV_HALO, CONV_HALO, cb) & (r > 0), tail, 0.0)
        win_u = jnp.concatenate([tail, cur], axis=0)
        d = d_ref[...]
        head = jnp.where(r < nr - 1, dnext_ref[:CONV_HALO, :], 0.0)
        win_d = jnp.concatenate([d, head], axis=0)

        @pl.when(r == 0)
        def _():
            dw_ref[...] = jnp.zeros_like(dw_ref)
            db_ref[...] = jnp.zeros_like(db_ref)

        du = jnp.zeros((tr, cb), F32)
        for k in range(taps):
            off = CONV_HALO - (taps - 1) + k
            du = du + w_ref[k:k + 1, :] * win_d[taps - 1 - k:taps - 1 - k + tr, :]
            dw_ref[k:k + 1, :] += _colsum(d * win_u[off:off + tr, :])
        du_ref[...] = jnp.where(_row_mask(r * tr, tr, cb), du, 0.0)
        db_ref[...] += _colsum(d)

    du, dw, db = pl.pallas_call(
        body, name=name, grid=(c // cb, nr),
        in_specs=[pl.BlockSpec((tr, cb), lambda j, r: (r, j)),
                  pl.BlockSpec((tr, cb), lambda j, r: (jnp.maximum(r - 1, 0), j)),
                  pl.BlockSpec((tr, cb), lambda j, r: (r, j)),
                  pl.BlockSpec((tr, cb), lambda j, r: (jnp.minimum(r + 1, nr - 1), j)),
                  pl.BlockSpec((kp, cb), lambda j, r: (0, j))],
        out_specs=[pl.BlockSpec((tr, cb), lambda j, r: (r, j)),
                   pl.BlockSpec((kp, cb), lambda j, r: (0, j)),
                   pl.BlockSpec((1, cb), lambda j, r: (0, j))],
        out_shape=[jax.ShapeDtypeStruct((n_rows, c), F32), jax.ShapeDtypeStruct((kp, c), F32),
                   jax.ShapeDtypeStruct((1, c), F32)],
        compiler_params=_cparams(("parallel", "arbitrary")),
    )(u, u, dc, dc, wp)
    return du, dw[:taps], db[0]


def _attn_block(q, kprev, kcur, vprev, vcur, sink, n):
    qf = q.reshape(GROUP * BLOCK, HEAD_DIM).astype(BF16)
    kb = jnp.concatenate([kprev, kcur], axis=0).astype(BF16)
    vb = jnp.concatenate([vprev, vcur], axis=0).astype(BF16)
    s = lax.dot_general(qf, kb, (((1,), (1,)), ((), ())), preferred_element_type=F32) * (HEAD_DIM ** -0.5)
    s = s.reshape(GROUP, BLOCK, 2 * BLOCK)
    qi = lax.broadcasted_iota(jnp.int32, (BLOCK, 2 * BLOCK), 0)
    kj = lax.broadcasted_iota(jnp.int32, (BLOCK, 2 * BLOCK), 1)
    dist = qi + BLOCK - kj
    allowed = (dist >= 0) & (dist < BLOCK) & ((n - 1) * BLOCK + kj >= PAD_LEN)
    s = jnp.where(allowed[None], s, NEG_INF)
    m = lax.stop_gradient(jnp.maximum(jnp.max(s, axis=-1, keepdims=True), sink))
    e = jnp.exp(s - m)
    p = e / (jnp.sum(e, axis=-1, keepdims=True) + jnp.exp(sink - m))
    o = jnp.dot(p.reshape(GROUP * BLOCK, 2 * BLOCK).astype(BF16), vb, preferred_element_type=F32)
    return o.reshape(GROUP, BLOCK, HEAD_DIM)


def _attn_specs(nb):
    q_spec = pl.BlockSpec((GROUP, BLOCK, HEAD_DIM), lambda g, n: (g, n, 0))
    cur = pl.BlockSpec((1, BLOCK, HEAD_DIM), lambda g, n: (g, n, 0))
    prev = pl.BlockSpec((1, BLOCK, HEAD_DIM), lambda g, n: (g, jnp.maximum(n - 1, 0), 0))
    sink = pl.BlockSpec((1, GROUP, 1, 1), lambda g, n: (g, 0, 0, 0))
    return q_spec, cur, prev, sink


def _attn_fwd(q, k, v, sinks, name):
    heads, n_rows, _ = q.shape
    nb = n_rows // BLOCK
    q_spec, cur, prev, sink = _attn_specs(nb)

    def body(q_ref, kp_ref, kc_ref, vp_ref, vc_ref, s_ref, o_ref):
        n = pl.program_id(1)
        o = _attn_block(q_ref[...].astype(F32), kp_ref[0].astype(F32), kc_ref[0].astype(F32), vp_ref[0].astype(F32),
                        vc_ref[0].astype(F32), s_ref[0], n)
        o_ref[...] = o.astype(o_ref.dtype)

    return pl.pallas_call(
        body, name=name, grid=(heads // GROUP, nb), in_specs=[q_spec, prev, cur, prev, cur, sink], out_specs=q_spec,
        out_shape=jax.ShapeDtypeStruct(q.shape, BF16), compiler_params=_cparams(("parallel", "parallel")),
    )(q, k, k, v, v, sinks.reshape(heads // GROUP, GROUP, 1, 1))


def _attn_bwd(q, k, v, sinks, do, name):
    heads, n_rows, _ = q.shape
    kvh = heads // GROUP
    nb = n_rows // BLOCK
    q_spec, cur, prev, sink = _attn_specs(nb)
    part = pl.BlockSpec((1, 1, BLOCK, HEAD_DIM), lambda g, n: (g, n, 0, 0))
    part_shape = jax.ShapeDtypeStruct((kvh, nb, BLOCK, HEAD_DIM), F32)

    def body(q_ref, kp_ref, kc_ref, vp_ref, vc_ref, s_ref, do_ref, dq_ref, dkp_ref, dkc_ref, dvp_ref, dvc_ref, ds_ref):
        n = pl.program_id(1)
        f = functools.partial(_attn_block, n=n)
        _, vjp = jax.vjp(f, q_ref[...].astype(F32), kp_ref[0].astype(F32), kc_ref[0].astype(F32), vp_ref[0].astype(F32),
                         vc_ref[0].astype(F32), s_ref[0])
        dq, dkp, dkc, dvp, dvc, ds = vjp(do_ref[...].astype(F32))
        dq_ref[...] = dq.astype(dq_ref.dtype)
        dkp_ref[0, 0], dkc_ref[0, 0], dvp_ref[0, 0], dvc_ref[0, 0] = dkp, dkc, dvp, dvc

        @pl.when(n == 0)
        def _():
            ds_ref[...] = jnp.zeros_like(ds_ref)

        ds_ref[0] += ds

    return pl.pallas_call(
        body, name=name, grid=(kvh, nb), in_specs=[q_spec, prev, cur, prev, cur, sink, q_spec],
        out_specs=[q_spec, part, part, part, part, sink],
        out_shape=[jax.ShapeDtypeStruct(q.shape, BF16), part_shape, part_shape, part_shape, part_shape,
                   jax.ShapeDtypeStruct((kvh, GROUP, 1, 1), F32)],
        compiler_params=_cparams(("parallel", "arbitrary")),
    )(q, k, k, v, v, sinks.reshape(kvh, GROUP, 1, 1), do)


def _shift_add(own, to_prev, name):
    kvh, nb = own.shape[:2]
    blk = (1, 1, BLOCK, HEAD_DIM)

    def body(a_ref, b_ref, o_ref):
        n = pl.program_id(1)
        o_ref[...] = (a_ref[...] + jnp.where(n < nb - 1, b_ref[...], 0.0)).astype(o_ref.dtype)

    return pl.pallas_call(
        body, name=name, grid=(kvh, nb),
        in_specs=[pl.BlockSpec(blk, lambda g, n: (g, n, 0, 0)),
                  pl.BlockSpec(blk, lambda g, n: (g, jnp.minimum(n + 1, nb - 1), 0, 0))],
        out_specs=pl.BlockSpec(blk, lambda g, n: (g, n, 0, 0)),
        out_shape=jax.ShapeDtypeStruct(own.shape, BF16), compiler_params=_cparams(("parallel", "parallel")),
    )(own, to_prev)


def _gdn_gates(ba, alog, dt, r0, hv):
    lane = lax.broadcasted_iota(jnp.int32, ba.shape, 1)
    t = ba + dt
    softplus = jnp.maximum(t, 0.0) + jnp.log(1.0 + jnp.exp(-jnp.abs(t)))
    val = jnp.where(lane < hv, jax.nn.sigmoid(ba), jnp.where(lane < 2 * hv, -jnp.exp(alog) * softplus, 0.0))
    return jnp.where(_row_mask(r0, ba.shape[0], ba.shape[1]), val, 0.0)


def _l2n(x):
    return x * lax.rsqrt(jnp.sum(x * x, axis=-1, keepdims=True) + 1e-6)


_NN = (((2,), (1,)), ((0,), (0,)))
_NT = (((2,), (2,)), ((0,), (0,)))
_TN = (((1,), (1,)), ((0,), (0,)))


def _bdot(a, b, dims=_NN):
    return lax.dot_general(a.astype(BF16), b.astype(BF16), dims, preferred_element_type=F32)


def _dot3(a, b, dims):
    ah, bh = a.astype(BF16), b.astype(BF16)
    al, bl = (a - ah.astype(F32)).astype(BF16), (b - bh.astype(F32)).astype(BF16)

    def d(p, q):
        return lax.dot_general(p, q, dims, preferred_element_type=F32)

    return d(ah, bh) + (d(ah, bl) + d(al, bh))


@jax.custom_vjp
def _pdot(a, b):
    return _dot3(a, b, _NN)


def _pdot_fwd(a, b):
    return _dot3(a, b, _NN), (a, b)


def _pdot_bwd(res, ct):
    a, b = res
    return _dot3(ct, b, _NT), _dot3(a, ct, _TN)


_pdot.defvjp(_pdot_fwd, _pdot_bwd)


def _scan_chunks(x, reverse):
    n, c = x.shape[0], GDN_CHUNK
    row = lax.broadcasted_iota(jnp.int32, x.shape, 0) % c
    s = 1
    while s < c:
        if reverse:
            x = x + jnp.where(row < c - s, pltpu.roll(x, n - s, axis=0), 0.0)
        else:
            x = x + jnp.where(row >= s, pltpu.roll(x, s, axis=0), 0.0)
        s *= 2
    return x


@jax.custom_vjp
def _cumsum_chunks(x):
    return _scan_chunks(x, False)


_cumsum_chunks.defvjp(lambda x: (_scan_chunks(x, False), None), lambda _, ct: (_scan_chunks(ct, True),))


def _gdn_heads(states, qkv, z, gates, norm_w, hk0, hv_total):
    c, h = GDN_CHUNK, states.shape[0]
    g = h // 2

    def cols(src, starts):
        return jnp.stack([src[:, s:s + GDN_DIM] for s in starts])

    q = _l2n(jax.nn.silu(cols(qkv, [4 * GDN_DIM * t for t in range(g)]))) * (GDN_DIM ** -0.5)
    k = _l2n(jax.nn.silu(cols(qkv, [4 * GDN_DIM * t + GDN_DIM for t in range(g)])))
    q, k = jnp.repeat(q, 2, axis=0), jnp.repeat(k, 2, axis=0)
    v = jax.nn.silu(cols(qkv, [4 * GDN_DIM * (t // 2) + (2 + t % 2) * GDN_DIM for t in range(h)]))
    zz = cols(z, [GDN_DIM * t for t in range(h)])
    lane = lax.broadcasted_iota(jnp.int32, gates.shape, 1)

    def col_of(first):
        return jnp.stack([jnp.sum(jnp.where(lane == first + t, gates, 0.0), axis=1, keepdims=True) for t in range(h)])

    beta_col, g_col = col_of(2 * hk0), col_of(hv_total + 2 * hk0)
    i = lax.broadcasted_iota(jnp.int32, (c, c), 0)
    j = lax.broadcasted_iota(jnp.int32, (c, c), 1)
    causal, strict = (i >= j)[None], (i > j)[None]
    gc = _cumsum_chunks(jnp.broadcast_to(g_col, (h, c, GDN_DIM)).reshape(h * c, GDN_DIM)).reshape(h, c, GDN_DIM)
    gc_i = gc[:, :, :c]
    gc_j = jnp.swapaxes(gc_i, 1, 2)
    gc_last = jnp.broadcast_to(gc[:, c - 1:c, :], (h, GDN_DIM, GDN_DIM))
    decay = jnp.where(causal, jnp.exp(jnp.where(causal, gc_i - gc_j, 0.0)), 0.0)
    k_beta = k * beta_col
    lower = jnp.where(strict, _bdot(k_beta, k, _NT) * decay, 0.0)
    eye = (i == j).astype(F32)[None]
    neg = -lower
    inv = eye + neg
    power = neg
    for _ in range(5):
        power = _pdot(power, power)
        inv = _pdot(inv, eye + power)
    sol = _pdot(inv, jnp.concatenate([v * beta_col, k_beta * jnp.exp(gc)], axis=2))
    u, w = sol[:, :, :GDN_DIM], sol[:, :, GDN_DIM:]
    intra = jnp.where(causal, _bdot(q, k, _NT) * decay, 0.0)
    q_dec = q * jnp.exp(gc)
    k_dec = k * jnp.exp(gc_last[:, :c] - gc)
    v_new = u - _bdot(w, states)
    o = _bdot(q_dec, states) + _bdot(intra, v_new)
    new_states = states * jnp.exp(gc_last) + _bdot(k_dec, v_new, _TN)
    y = _rms(o, norm_w) * jax.nn.silu(zz)
    return jnp.concatenate([y[t] for t in range(h)], axis=1), new_states


GDN_KEY_HEADS_PER_STEP = 4


def _key_heads_per_step(hk_total):
    return math.gcd(hk_total, GDN_KEY_HEADS_PER_STEP)


def _gdn_fwd(cq, proj, gates, norm_w, hk_total, name):
    n_rows = cq.shape[0]
    nc, hv_total = n_rows // GDN_CHUNK, 2 * hk_total
    grp = _key_heads_per_step(hk_total)
    heads = 2 * grp
    zblk0 = cq.shape[1] // (heads * GDN_DIM)

    def body(cq_ref, z_ref, g_ref, w_ref, y_ref, save_ref, state):
        n, hg = pl.program_id(0), pl.program_id(1)

        @pl.when(n == 0)
        def _():
            state[pl.ds(heads * hg, heads)] = jnp.zeros((heads, GDN_DIM, GDN_DIM), F32)

        s_in = state[pl.ds(heads * hg, heads)]
        save_ref[0] = s_in
        y, s_out = _gdn_heads(s_in, cq_ref[...], z_ref[...], g_ref[...], w_ref[...], grp * hg, hv_total)
        y_ref[...] = y.astype(y_ref.dtype)
        state[pl.ds(heads * hg, heads)] = s_out

    return pl.pallas_call(
        body, name=name, grid=(nc, hk_total // grp),
        in_specs=[pl.BlockSpec((GDN_CHUNK, 2 * heads * GDN_DIM), lambda n, h: (n, h)),
                  pl.BlockSpec((GDN_CHUNK, heads * GDN_DIM), lambda n, h: (n, zblk0 + h)),
                  pl.BlockSpec((GDN_CHUNK, LANES), lambda n, h: (n, 0)),
                  pl.BlockSpec((1, GDN_DIM), lambda n, h: (0, 0))],
        out_specs=[pl.BlockSpec((GDN_CHUNK, heads * GDN_DIM), lambda n, h: (n, h)),
                   pl.BlockSpec((1, heads, GDN_DIM, GDN_DIM), lambda n, h: (n, h, 0, 0))],
        out_shape=[jax.ShapeDtypeStruct((n_rows, hv_total * GDN_DIM), BF16),
                   jax.ShapeDtypeStruct((nc, hv_total, GDN_DIM, GDN_DIM), F32)],
        scratch_shapes=[pltpu.VMEM((hv_total, GDN_DIM, GDN_DIM), F32)],
        compiler_params=_cparams(("arbitrary", "arbitrary")),
    )(cq, proj, gates, norm_w.reshape(1, GDN_DIM))


def _gdn_bwd(cq, proj, gates, norm_w, saved, dy, hk_total, name):
    n_rows = cq.shape[0]
    nc, hv_total = n_rows // GDN_CHUNK, 2 * hk_total
    grp = _key_heads_per_step(hk_total)
    heads = 2 * grp
    zblk0 = cq.shape[1] // (heads * GDN_DIM)

    def body(cq_ref, z_ref, g_ref, w_ref, save_ref, dy_ref, dcq_ref, dz_ref, dg_ref, dw_ref, dstate):
        n, hg = pl.program_id(0), pl.program_id(1)

        @pl.when(n == 0)
        def _():
            dstate[pl.ds(heads * hg, heads)] = jnp.zeros((heads, GDN_DIM, GDN_DIM), F32)

        @pl.when((n == 0) & (hg == 0))
        def _():
            dw_ref[...] = jnp.zeros_like(dw_ref)

        @pl.when(hg == 0)
        def _():
            dg_ref[...] = jnp.zeros_like(dg_ref)

        f = functools.partial(_gdn_heads, hk0=grp * hg, hv_total=hv_total)
        _, vjp = jax.vjp(f, save_ref[0], cq_ref[...], z_ref[...], g_ref[...], w_ref[...])
        ds, dcq, dz, dg, dw = vjp((dy_ref[...].astype(F32), dstate[pl.ds(heads * hg, heads)]))
        dstate[pl.ds(heads * hg, heads)] = ds
        dcq_ref[...] = dcq
        dz_ref[...] = dz.astype(dz_ref.dtype)
        dg_ref[...] += dg
        dw_ref[...] += dw

    rev = lambda n: nc - 1 - n
    return pl.pallas_call(
        body, name=name, grid=(nc, hk_total // grp),
        in_specs=[pl.BlockSpec((GDN_CHUNK, 2 * heads * GDN_DIM), lambda n, h: (rev(n), h)),
                  pl.BlockSpec((GDN_CHUNK, heads * GDN_DIM), lambda n, h: (rev(n), zblk0 + h)),
                  pl.BlockSpec((GDN_CHUNK, LANES), lambda n, h: (rev(n), 0)),
                  pl.BlockSpec((1, GDN_DIM), lambda n, h: (0, 0)),
                  pl.BlockSpec((1, heads, GDN_DIM, GDN_DIM), lambda n, h: (rev(n), h, 0, 0)),
                  pl.BlockSpec((GDN_CHUNK, heads * GDN_DIM), lambda n, h: (rev(n), h))],
        out_specs=[pl.BlockSpec((GDN_CHUNK, 2 * heads * GDN_DIM), lambda n, h: (rev(n), h)),
                   pl.BlockSpec((GDN_CHUNK, heads * GDN_DIM), lambda n, h: (rev(n), h)),
                   pl.BlockSpec((GDN_CHUNK, LANES), lambda n, h: (rev(n), 0)),
                   pl.BlockSpec((1, GDN_DIM), lambda n, h: (0, 0))],
        out_shape=[jax.ShapeDtypeStruct(cq.shape, F32), jax.ShapeDtypeStruct((n_rows, hv_total * GDN_DIM), BF16),
                   jax.ShapeDtypeStruct((n_rows, LANES), F32), jax.ShapeDtypeStruct((1, GDN_DIM), F32)],
        scratch_shapes=[pltpu.VMEM((hv_total, GDN_DIM, GDN_DIM), F32)],
        compiler_params=_cparams(("arbitrary", "arbitrary")),
    )(cq, proj, gates, norm_w.reshape(1, GDN_DIM), saved, dy)


def _final_loss(h, w, target, name):
    d = h.shape[1]

    def fn(r0, hv, tv, wv):
        def loss_of(hh, ww):
            err = jnp.where(_row_mask(r0, hh.shape[0], d) & (r0 + lax.broadcasted_iota(jnp.int32, hh.shape, 0) >= FRONT),
                            _rms(hh, ww) - tv, 0.0)
            return 0.5 * jnp.sum(jnp.sum(err * err, axis=1, keepdims=True) / d)

        loss, vjp = jax.vjp(loss_of, hv, wv)
        dh, dw = vjp(jnp.ones((), F32))
        return dh, jnp.zeros((1, LANES), F32) + loss, dw

    dh, loss, dw = _rowwise(fn, [h, target], [w.reshape(1, -1)], [(d, F32)], [(1, LANES), (1, d)], name)
    return loss[0, 0], dh, dw[0]


def _rope_tables(n_rows):
    pos = (jnp.arange(n_rows) - PAD_LEN).astype(F32)
    inv_freq = ROPE_THETA ** (-jnp.arange(0, HEAD_DIM, 2, dtype=F32) / HEAD_DIM)
    ang = pos[:, None] * inv_freq[None, :]
    reps = LANES // (HEAD_DIM // 2)
    return jnp.tile(jnp.cos(ang), (1, reps)), jnp.tile(jnp.sin(ang), (1, reps))


def _to_heads(t):
    n_rows, w = t.shape
    return t.reshape(n_rows, w // HEAD_DIM, HEAD_DIM).transpose(1, 0, 2)


def _from_heads(t):
    heads, n_rows, _ = t.shape
    return t.transpose(1, 0, 2).reshape(n_rows, heads * HEAD_DIM)


def _local_step(h0, target, p):
    n_rows, d = h0.shape
    depth = p["norm_mix"].shape[0]
    cos, sin = _rope_tables(n_rows)
    hk_total = d // GDN_DIM
    hv_total = 2 * hk_total
    conv_dim = 4 * hk_total * GDN_DIM
    qw, kw = d, d // GROUP
    saved = []
    h = h0
    for i in range(depth):
        kind, j = i % 3, i // 3
        tag = f"l{i}"
        s = {"h": h}
        hn = _rms_fwd(h, p["norm_mix"][i], "rms_fwd")
        s["hn"] = hn
        if kind == 0:
            pre = _mm(hn, p["conv_w_pw1"][j], "nn", F32, "mm_pw1")
            u1 = _glu_fwd(pre, p["conv_b_pw1"][j], "glu_fwd")
            c = _dwconv_fwd(u1, p["conv_w_dw"][j], p["conv_b_dw"][j], d, "dwconv31_fwd")
            sv = _ln_silu_fwd(c, p["conv_ln_g"][j], p["conv_ln_b"][j], "ln_silu_fwd")
            h = _mm(sv, p["conv_w_pw2"][j], "nn", F32, "mm_d_d_res", bias=p["conv_b_pw2"][j], residual=h)
            s.update(pre=pre, u1=u1, c=c, sv=sv)
        elif kind == 1:
            pre = _mm(hn, p["attn_w_qkv"][j], "nn", F32, "mm_qkv")
            q, k, v = _qkv_post_fwd(pre, p["attn_b_qkv"][j], cos, sin, qw, kw, "qkv_post_fwd")
            qh, kh, vh = _to_heads(q), _to_heads(k), _to_heads(v)
            o = _from_heads(_attn_fwd(qh, kh, vh, p["attn_sinks"][j], "attn_fwd"))
            h = _mm(o, p["attn_w_o"][j], "nn", F32, "mm_d_d_res", bias=p["attn_b_o"][j], residual=h)
            s.update(qh=qh, kh=kh, vh=vh, o=o)
        else:
            proj = _mm(hn, p["gdn_w_in"][j], "nn", F32, "mm_gdn_in")
            cq = _dwconv_fwd(proj, p["gdn_conv_w"][j], None, conv_dim, "dwconv4_fwd")
            vec = _gate_vectors(p["gdn_a_log"][j], p["gdn_dt_bias"][j], hv_total)
            ba_blk = (conv_dim + hv_total * GDN_DIM) // LANES
            (gates,) = _rowwise(lambda r0, bav, av, dv: (_gdn_gates(bav, av, dv, r0, hv_total),), [(proj, LANES, ba_blk)],
                                [vec[0], vec[1]], [(LANES, F32)], [], "gdn_gates_fwd")
            y, states = _gdn_fwd(cq, proj, gates, p["gdn_norm_w"][j], hk_total, "gdn_fwd")
            h = _mm(y, p["gdn_w_out"][j], "nn", F32, "mm_gdn_out_res", residual=h)
            s.update(proj=proj, cq=cq, gates=gates, y=y, states=states)
        s["h1"] = h
        hn2 = _rms_fwd(h, p["norm_ffn"][i], "rms_fwd")
        gu = _mm(hn2, p["ffn_w_gu"][i], "nn", BF16, "mm_ffn_gu")
        a = _swiglu_fwd(gu, "swiglu_fwd")
        h = _mm(a, p["ffn_w_down"][i], "nn", F32, "mm_ffn_down_res", residual=h)
        s.update(hn2=hn2, gu=gu, a=a)
        saved.append(s)

    loss, dh, g_final = _final_loss(h, p["norm_final"], target, "final_loss")
    g = {k: [None] * v.shape[0] for k, v in p.items() if k != "norm_final"}
    g["norm_final"] = g_final
    for i in reversed(range(depth)):
        kind, j = i % 3, i // 3
        s = saved[i]
        g["ffn_w_down"][i] = _mm(s["a"], dh, "tn", BF16, "mm_dw_down")
        da = _mm(dh, p["ffn_w_down"][i], "nt", F32, "mm_da")
        dgu = _swiglu_bwd(s["gu"], da, "swiglu_bwd")
        g["ffn_w_gu"][i] = _mm(s["hn2"], dgu, "tn", BF16, "mm_dw_gu")
        dhn2 = _mm(dgu, p["ffn_w_gu"][i], "nt", F32, "mm_dhn2")
        dh, g["norm_ffn"][i] = _rms_bwd(s["h1"], p["norm_ffn"][i], dhn2, dh, "rms_bwd")
        if kind == 0:
            g["conv_b_pw2"][j] = _colsum_rows(dh, "colsum_d")
            g["conv_w_pw2"][j] = _mm(s["sv"], dh, "tn", BF16, "mm_dw_d_d")
            dsv = _mm(dh, p["conv_w_pw2"][j], "nt", F32, "mm_dx_d_d")
            dc, g["conv_ln_g"][j], g["conv_ln_b"][j] = _ln_silu_bwd(s["c"], p["conv_ln_g"][j], p["conv_ln_b"][j], dsv, "ln_silu_bwd")
            du1, g["conv_w_dw"][j], g["conv_b_dw"][j] = _dwconv_bwd(s["u1"], p["conv_w_dw"][j], dc, d, "dwconv31_bwd")
            dpre, g["conv_b_pw1"][j] = _glu_bwd(s["pre"], p["conv_b_pw1"][j], du1, "glu_bwd")
            g["conv_w_pw1"][j] = _mm(s["hn"], dpre, "tn", BF16, "mm_dw_pw1")
            dhn = _mm(dpre, p["conv_w_pw1"][j], "nt", F32, "mm_dx_pw1")
        elif kind == 1:
            g["attn_b_o"][j] = _colsum_rows(dh, "colsum_d")
            g["attn_w_o"][j] = _mm(s["o"], dh, "tn", BF16, "mm_dw_d_d")
            do = _to_heads(_mm(dh, p["attn_w_o"][j], "nt", BF16, "mm_dx_d_d_bf16"))
            dq, dkp, dkc, dvp, dvc, dsink = _attn_bwd(s["qh"], s["kh"], s["vh"], p["attn_sinks"][j], do, "attn_bwd")
            g["attn_sinks"][j] = dsink.reshape(-1)
            kvh = kw // HEAD_DIM
            dk = _shift_add(dkc, dkp, "attn_shift_add").reshape(kvh, n_rows, HEAD_DIM)
            dv = _shift_add(dvc, dvp, "attn_shift_add").reshape(kvh, n_rows, HEAD_DIM)
            dpre, g["attn_b_qkv"][j] = _qkv_post_bwd(_from_heads(dq), _from_heads(dk), _from_heads(dv), cos, sin, "qkv_post_bwd")
            g["attn_w_qkv"][j] = _mm(s["hn"], dpre, "tn", BF16, "mm_dw_qkv")
            dhn = _mm(dpre, p["attn_w_qkv"][j], "nt", F32, "mm_dx_qkv")
        else:
            g["gdn_w_out"][j] = _mm(s["y"], dh, "tn", BF16, "mm_dw_gdn_out")
            dy = _mm(dh, p["gdn_w_out"][j], "nt", BF16, "mm_dx_gdn_out")
            dcq, dz, dgates, g_nw = _gdn_bwd(s["cq"], s["proj"], s["gates"], p["gdn_norm_w"][j], s["states"], dy, hk_total, "gdn_bwd")
            g["gdn_norm_w"][j] = g_nw[0]
            vec = _gate_vectors(p["gdn_a_log"][j], p["gdn_dt_bias"][j], hv_total)
            ba_blk = (conv_dim + hv_total * GDN_DIM) // LANES

            def gates_bwd(r0, bav, dgv, av, dv):
                _, vjp = jax.vjp(functools.partial(_gdn_gates, r0=r0, hv=hv_total), bav, av, dv)
                return vjp(dgv)

            dba, d_alog, d_dt = _rowwise(gates_bwd, [(s["proj"], LANES, ba_blk), dgates], [vec[0], vec[1]], [(LANES, BF16)],
                                         [(1, LANES), (1, LANES)], "gdn_gates_bwd")
            g["gdn_a_log"][j] = d_alog[0, hv_total:2 * hv_total]
            g["gdn_dt_bias"][j] = d_dt[0, hv_total:2 * hv_total]
            dconv_in, g["gdn_conv_w"][j], _ = _dwconv_bwd(s["proj"], p["gdn_conv_w"][j], dcq, conv_dim, "dwconv4_bwd")
            width = p["gdn_w_in"][j].shape[1]
            pad = jnp.zeros((n_rows, width - conv_dim - hv_total * GDN_DIM - LANES), BF16)
            dproj = jnp.concatenate([dconv_in.astype(BF16), dz, dba, pad], axis=1)
            g["gdn_w_in"][j] = _mm(s["hn"], dproj, "tn", BF16, "mm_dw_gdn_in")
            dhn = _mm(dproj, p["gdn_w_in"][j], "nt", F32, "mm_dx_gdn_in")
        dh, g["norm_mix"][i] = _rms_bwd(s["h"], p["norm_mix"][i], dhn, dh, "rms_bwd")
    g = {k: (jnp.stack(v) if isinstance(v, list) else v) for k, v in g.items()}
    return loss, dh, g


def _gate_vectors(a_log, dt_bias, hv):
    def place(t):
        return jnp.concatenate([jnp.zeros((hv,), F32), t, jnp.zeros((LANES - 2 * hv,), F32)]).reshape(1, LANES)

    return place(a_log), place(dt_bias)


GDN_IN_ALIGN = 512


def _gdn_group(w, hk):
    lead, kw = w.shape[:-1], hk * GDN_DIM
    q = w[..., :kw].reshape(*lead, hk, 1, GDN_DIM)
    k = w[..., kw:2 * kw].reshape(*lead, hk, 1, GDN_DIM)
    v = w[..., 2 * kw:4 * kw].reshape(*lead, hk, 2, GDN_DIM)
    return jnp.concatenate([q, k, v], axis=-2).reshape(*lead, 4 * kw)


def _gdn_ungroup(w, hk):
    lead, kw = w.shape[:-1], hk * GDN_DIM
    t = w.reshape(*lead, hk, 4, GDN_DIM)
    return jnp.concatenate([t[..., 0, :].reshape(*lead, kw), t[..., 1, :].reshape(*lead, kw),
                            t[..., 2:, :].reshape(*lead, 2 * kw)], axis=-1)


def _gdn_in_layout(w, hk):
    conv_dim = 4 * hk * GDN_DIM
    width = -(-w.shape[-1] // GDN_IN_ALIGN) * GDN_IN_ALIGN
    pad = jnp.zeros(w.shape[:-1] + (width - w.shape[-1],), w.dtype)
    return jnp.concatenate([_gdn_group(w[..., :conv_dim], hk), w[..., conv_dim:], pad], axis=-1)


def _gdn_in_natural(w, hk, in_width):
    conv_dim = 4 * hk * GDN_DIM
    return jnp.concatenate([_gdn_ungroup(w[..., :conv_dim], hk), w[..., conv_dim:in_width]], axis=-1)


def _step_layout(full):
    hk = full["norm_final"].shape[0] // GDN_DIM
    p = {k: v for k, v in full.items() if k not in ("meta_tokens", "ffn_w_gate", "ffn_w_up", "gdn_w_in", "gdn_conv_w")}
    p["ffn_w_gu"] = jnp.concatenate([full["ffn_w_gate"], full["ffn_w_up"]], axis=-1)
    p["gdn_w_in"] = _gdn_in_layout(full["gdn_w_in"], hk)
    p["gdn_conv_w"] = _gdn_group(full["gdn_conv_w"], hk)
    return p


def _natural_grads(g, in_width):
    hk = g["norm_final"].shape[0] // GDN_DIM
    out = {k: v for k, v in g.items() if k not in ("ffn_w_gu", "gdn_w_in", "gdn_conv_w")}
    f = g["ffn_w_gu"].shape[-1] // 2
    out["ffn_w_gate"], out["ffn_w_up"] = g["ffn_w_gu"][..., :f], g["ffn_w_gu"][..., f:]
    out["gdn_w_in"] = _gdn_in_natural(g["gdn_w_in"], hk, in_width)
    out["gdn_conv_w"] = _gdn_ungroup(g["gdn_conv_w"], hk)
    return out


def _exchange(src, gather, name):
    n = src.shape[0]
    out_shape = (n, N_DEV) + src.shape[-2:]

    def body(src_ref, out_ref, send_sems, recv_sems, local_sems):
        x, y, c = lax.axis_index("x"), lax.axis_index("y"), lax.axis_index("c")
        me = 4 * x + 2 * y + c
        remote, local = [], []
        for l in range(n):
            own = pltpu.make_async_copy(src_ref.at[l] if gather else src_ref.at[l, me], out_ref.at[l, me], local_sems.at[l])
            own.start()
            local.append(own)
            for k in range(1, N_DEV):
                px = 1 - x if k & 4 else x
                py = 1 - y if k & 2 else y
                pc = 1 - c if k & 1 else c
                peer = 4 * px + 2 * py + pc
                cp = pltpu.make_async_remote_copy(
                    src_ref=src_ref.at[l] if gather else src_ref.at[l, peer], dst_ref=out_ref.at[l, me],
                    send_sem=send_sems.at[l, k - 1], recv_sem=recv_sems.at[l, k - 1],
                    device_id=(px, py, pc), device_id_type=pl.DeviceIdType.MESH)
                cp.start()
                remote.append(cp)
        for cp in remote:
            cp.wait_send()
        for cp in remote:
            cp.wait_recv()
        for cp in local:
            cp.wait()

    return pl.pallas_call(
        body, name=name, out_shape=jax.ShapeDtypeStruct(out_shape, src.dtype),
        in_specs=[pl.BlockSpec(memory_space=pl.ANY)], out_specs=pl.BlockSpec(memory_space=pl.ANY),
        scratch_shapes=[pltpu.SemaphoreType.DMA((n, N_DEV - 1)), pltpu.SemaphoreType.DMA((n, N_DEV - 1)),
                        pltpu.SemaphoreType.DMA((n,))],
        compiler_params=pltpu.CompilerParams(has_side_effects=True),
    )(src)


def _cast_bf16(w, name):
    n, r, c = w.shape
    tr = _tile(r, max(16, (1 << 20) // c), 16)

    def body(w_ref, o_ref):
        o_ref[...] = w_ref[...].astype(BF16)

    return pl.pallas_call(
        body, name=name, grid=(n, r // tr), in_specs=[pl.BlockSpec((1, tr, c), lambda l, i: (l, i, 0))],
        out_specs=pl.BlockSpec((1, tr, c), lambda l, i: (l, i, 0)), out_shape=jax.ShapeDtypeStruct(w.shape, BF16),
        compiler_params=_cparams(("parallel", "parallel")),
    )(w)


def _adamw(w, g, m, v):
    m = ADAM_B1 * m + (1.0 - ADAM_B1) * g
    v = ADAM_B2 * v + (1.0 - ADAM_B2) * jnp.square(g)
    m_hat = m / (1.0 - ADAM_B1 ** ADAM_STEP)
    v_hat = v / (1.0 - ADAM_B2 ** ADAM_STEP)
    delta = -ADAM_LR * (m_hat / (jnp.sqrt(v_hat) + ADAM_EPS) + ADAM_WD * w)
    return delta, m, v


def _sum8_adam(parts, w, m, v, name):
    n, _, r, c = parts.shape
    tr = _tile(r, max(16, (1 << 18) // c), 16)
    blk = pl.BlockSpec((1, tr, c), lambda l, i: (l, i, 0))

    def body(p_ref, w_ref, m_ref, v_ref, g_ref, d_ref, mo_ref, vo_ref):
        g = p_ref[0, 0].astype(F32)
        for s in range(1, N_DEV):
            g = g + p_ref[0, s].astype(F32)
        delta, m2, v2 = _adamw(w_ref[0], g, m_ref[0], v_ref[0])
        g_ref[0], d_ref[0], mo_ref[0], vo_ref[0] = g, delta, m2, v2

    shp = jax.ShapeDtypeStruct(w.shape, F32)
    return pl.pallas_call(
        body, name=name, grid=(n, r // tr),
        in_specs=[pl.BlockSpec((1, N_DEV, tr, c), lambda l, i: (l, 0, i, 0)), blk, blk, blk],
        out_specs=[blk, blk, blk, blk], out_shape=[shp, shp, shp, shp],
        compiler_params=_cparams(("parallel", "parallel")),
    )(parts, w, m, v)


PACK_ROWS = 8


def _pack(arrays):
    flat = jnp.concatenate([a.reshape(-1).astype(F32) for a in arrays])
    unit = PACK_ROWS * LANES
    total = -(-flat.shape[0] // unit) * unit
    return jnp.concatenate([flat, jnp.zeros((total - flat.shape[0],), F32)]).reshape(-1, LANES)


def _unpack(packed, shapes):
    flat, out, pos = packed.reshape(-1), [], 0
    for s in shapes:
        size = math.prod(s)
        out.append(flat[pos:pos + size].reshape(s))
        pos += size
    return out


BIG_COL = ("conv_w_pw1", "attn_w_qkv", "gdn_w_in", "ffn_w_gate", "ffn_w_up")
BIG_ROW = ("conv_w_pw2", "attn_w_o", "gdn_w_out", "ffn_w_down")
SMALL_SHARDED = ("meta_tokens", "conv_b_pw1", "conv_w_dw", "conv_b_dw", "conv_ln_g", "conv_ln_b", "conv_b_pw2", "gdn_conv_w")
REPLICATED = ("norm_mix", "norm_ffn", "norm_final", "attn_b_qkv", "attn_sinks", "attn_b_o", "gdn_a_log", "gdn_dt_bias", "gdn_norm_w")
WEIGHTS = ("meta_tokens", "norm_mix", "norm_ffn", "norm_final", "conv_w_pw1", "conv_b_pw1", "conv_w_dw", "conv_b_dw", "conv_ln_g",
           "conv_ln_b", "conv_w_pw2", "conv_b_pw2", "attn_w_qkv", "attn_b_qkv", "attn_sinks", "attn_w_o", "attn_b_o", "gdn_w_in",
           "gdn_conv_w", "gdn_a_log", "gdn_dt_bias", "gdn_norm_w", "gdn_w_out", "ffn_w_gate", "ffn_w_up", "ffn_w_down")


def kernel(x, meta_tokens, norm_mix, norm_ffn, norm_final, conv_w_pw1, conv_b_pw1, conv_w_dw, conv_b_dw, conv_ln_g, conv_ln_b, conv_w_pw2, conv_b_pw2, attn_w_qkv, attn_b_qkv, attn_sinks, attn_w_o, attn_b_o, gdn_w_in, gdn_conv_w, gdn_a_log, gdn_dt_bias, gdn_norm_w, gdn_w_out, ffn_w_gate, ffn_w_up, ffn_w_down, loss_target, m_meta_tokens, m_norm_mix, m_norm_ffn, m_norm_final, m_conv_w_pw1, m_conv_b_pw1, m_conv_w_dw, m_conv_b_dw, m_conv_ln_g, m_conv_ln_b, m_conv_w_pw2, m_conv_b_pw2, m_attn_w_qkv, m_attn_b_qkv, m_attn_sinks, m_attn_w_o, m_attn_b_o, m_gdn_w_in, m_gdn_conv_w, m_gdn_a_log, m_gdn_dt_bias, m_gdn_norm_w, m_gdn_w_out, m_ffn_w_gate, m_ffn_w_up, m_ffn_w_down, v_meta_tokens, v_norm_mix, v_norm_ffn, v_norm_final, v_conv_w_pw1, v_conv_b_pw1, v_conv_w_dw, v_conv_b_dw, v_conv_ln_g, v_conv_ln_b, v_conv_w_pw2, v_conv_b_pw2, v_attn_w_qkv, v_attn_b_qkv, v_attn_sinks, v_attn_w_o, v_attn_b_o, v_gdn_w_in, v_gdn_conv_w, v_gdn_a_log, v_gdn_dt_bias, v_gdn_norm_w, v_gdn_w_out, v_ffn_w_gate, v_ffn_w_up, v_ffn_w_down):
    a = dict(locals())
    me = 4 * lax.axis_index("x") + 2 * lax.axis_index("y") + lax.axis_index("c")
    d = x.shape[-1]

    full = {k: a[k] for k in REPLICATED}
    for k in BIG_COL + BIG_ROW:
        n, r, c = a[k].shape
        got = _exchange(_cast_bf16(a[k], "cast_bf16"), True, "gather_weights")
        full[k] = got.transpose(0, 2, 1, 3).reshape(n, r, N_DEV * c) if k in BIG_COL else got.reshape(n, N_DEV * r, c)
    shard_shapes = [a[k].shape for k in SMALL_SHARDED]
    got = _exchange(_pack([a[k] for k in SMALL_SHARDED])[None], True, "gather_small")[0]
    per_dev = [_unpack(got[s], shard_shapes) for s in range(N_DEV)]
    for i, k in enumerate(SMALL_SHARDED):
        st = jnp.stack([per_dev[s][i] for s in range(N_DEV)], axis=-2)
        full[k] = st.reshape(st.shape[:-2] + (N_DEV * st.shape[-1],))

    h0 = jnp.concatenate([jnp.zeros((PAD_LEN, d), F32), full["meta_tokens"], x[0]], axis=0)
    target = jnp.concatenate([jnp.zeros((FRONT, d), F32), loss_target[0]], axis=0)
    loss, dh0, g = _local_step(h0, target, _step_layout(full))
    g = _natural_grads(g, full["gdn_w_in"].shape[-1])
    g["meta_tokens"] = dh0[PAD_LEN:FRONT]
    loss = lax.psum(loss, AXES)
    grad_x = dh0[FRONT:][None]

    grads, deltas, new_m, new_v = {}, {}, {}, {}
    for k in BIG_COL + BIG_ROW:
        n, r, c = a[k].shape
        pieces = g[k].reshape(n, r, N_DEV, c).transpose(0, 2, 1, 3) if k in BIG_COL else g[k].reshape(n, N_DEV, r, c)
        parts = _exchange(pieces, False, "scatter_grads")
        grads[k], deltas[k], new_m[k], new_v[k] = _sum8_adam(parts, a[k], a["m_" + k], a["v_" + k], "sum8_adamw")

    small = SMALL_SHARDED + REPLICATED
    full_shapes = [full[k].shape for k in small]
    got = _exchange(_pack([g[k] for k in small])[None], True, "gather_small_grads")[0]
    (total,) = _rowwise(lambda r0, *t: (functools.reduce(lambda p, q: p + q, t),), [got[s] for s in range(N_DEV)], [],
                        [(LANES, F32)], [], "sum8_small", tm=got.shape[1])
    for k, t in zip(small, _unpack(total, full_shapes)):
        if k in SMALL_SHARDED:
            c = a[k].shape[-1]
            t = lax.dynamic_index_in_dim(t.reshape(t.shape[:-1] + (N_DEV, c)), me, axis=t.ndim - 1, keepdims=False)
        grads[k] = t
    shapes = [a[k].shape for k in small]
    packed = [_pack([src[k] for k in small]) for src in (grads, a, {k: a["m_" + k] for k in small}, {k: a["v_" + k] for k in small})]

    def small_adam(r0, gv, wv, mv, vv):
        return _adamw(wv, gv, mv, vv)

    outs = _rowwise(small_adam, packed, [], [(LANES, F32)] * 3, [], "adamw_small", tm=packed[0].shape[0])
    for dst, o in zip((deltas, new_m, new_v), outs):
        for k, t in zip(small, _unpack(o, shapes)):
            dst[k] = t

    return (loss, grad_x, *[grads[k] for k in WEIGHTS], *[deltas[k] for k in WEIGHTS], *[new_m[k] for k in WEIGHTS],
            *[new_v[k] for k in WEIGHTS])
```

```python
import functools
import math

import jax
import jax.numpy as jnp
from jax import lax
from jax.experimental import pallas as pl
from jax.experimental.pallas import tpu as pltpu

F32 = jnp.float32
BF16 = jnp.bfloat16

AXES = ("x", "y", "c")
N_DEV = 8

N_META = 16
FRONT = 128
PAD_LEN = FRONT - N_META
NORM_EPS = 1e-6
LN_EPS = 1e-5
NEG_INF = -1e30
CONV_KERNEL = 31
HEAD_DIM = 64
GROUP = 8
BLOCK = 128
ROPE_THETA = 10000.0
GDN_DIM = 128
GDN_CONV = 4
GDN_CHUNK = 64
ADAM_LR, ADAM_B1, ADAM_B2, ADAM_EPS, ADAM_WD, ADAM_STEP = 0.001, 0.9, 0.999, 1e-08, 0.01, 10

VMEM_LIMIT_BYTES = 52 * 1024 * 1024
LANES = 128
CONV_HALO = 32


def _tile(n, pref, align=128):
    best = None
    for t in range(align, min(n, pref) + 1, align):
        if n % t == 0:
            best = t
    return best if best is not None else n


def _cparams(sem):
    return pltpu.CompilerParams(dimension_semantics=sem, vmem_limit_bytes=VMEM_LIMIT_BYTES)


def _exchange_copies(src_ref, out_ref, send_sems, recv_sems, local_sems, gather):
    n = src_ref.shape[0]
    x, y, c = lax.axis_index("x"), lax.axis_index("y"), lax.axis_index("c")
    me = 4 * x + 2 * y + c
    remote, local = [], []
    for l in range(n):
        local.append(pltpu.make_async_copy(src_ref.at[l] if gather else src_ref.at[l, me], out_ref.at[l, me], local_sems.at[l]))
        for k in range(1, N_DEV):
            px = 1 - x if k & 4 else x
            py = 1 - y if k & 2 else y
            pc = 1 - c if k & 1 else c
            peer = 4 * px + 2 * py + pc
            remote.append(pltpu.make_async_remote_copy(
                src_ref=src_ref.at[l] if gather else src_ref.at[l, peer], dst_ref=out_ref.at[l, me],
                send_sem=send_sems.at[l, k - 1], recv_sem=recv_sems.at[l, k - 1],
                device_id=(px, py, pc), device_id_type=pl.DeviceIdType.MESH))
    return remote, local


def _exchange_start(copies):
    remote, local = copies
    for cp in local + remote:
        cp.start()


def _exchange_wait(copies):
    remote, local = copies
    for cp in remote:
        cp.wait_send()
    for cp in remote:
        cp.wait_recv()
    for cp in local:
        cp.wait()


def _exchange_out(src, gather):
    return jax.ShapeDtypeStruct((src.shape[0], N_DEV) + src.shape[-2:], src.dtype)


def _exchange_sems(src):
    n = src.shape[0]
    return [pltpu.SemaphoreType.DMA((n, N_DEV - 1)), pltpu.SemaphoreType.DMA((n, N_DEV - 1)), pltpu.SemaphoreType.DMA((n,))]


class _Ride:
    def __init__(self, items):
        self.items = items
        self.outs = None


def _pcall(body, args, *, name, grid, in_specs, out_specs, out_shape, sem, scratch_shapes=(), ride=None):
    if ride is None:
        return pl.pallas_call(body, name=name, grid=grid, in_specs=in_specs, out_specs=out_specs, out_shape=out_shape,
                              scratch_shapes=list(scratch_shapes), compiler_params=_cparams(sem))(*args)
    n_in, n_out, n_scr, n_ride = len(in_specs), len(out_specs), len(scratch_shapes), len(ride.items)
    any_spec = pl.BlockSpec(memory_space=pl.ANY)

    def with_ride(*refs):
        pos = 0
        ins, pos = refs[pos:pos + n_in], pos + n_in
        srcs, pos = refs[pos:pos + n_ride], pos + n_ride
        outs, pos = refs[pos:pos + n_out], pos + n_out
        dsts, pos = refs[pos:pos + n_ride], pos + n_ride
        scr, pos = refs[pos:pos + n_scr], pos + n_scr
        sems = refs[pos:]
        first = functools.reduce(lambda p, q: p & q, [pl.program_id(d) == 0 for d in range(len(grid))])
        last = functools.reduce(lambda p, q: p & q, [pl.program_id(d) == grid[d] - 1 for d in range(len(grid))])

        def copies():
            return [_exchange_copies(srcs[t], dsts[t], *sems[3 * t:3 * t + 3], gather=ride.items[t][1]) for t in range(n_ride)]

        @pl.when(first)
        def _():
            for cps in copies():
                _exchange_start(cps)

        body(*ins, *outs, *scr)

        @pl.when(last)
        def _():
            for cps in copies():
                _exchange_wait(cps)

    res = pl.pallas_call(
        with_ride, name=name, grid=grid, in_specs=list(in_specs) + [any_spec] * n_ride,
        out_specs=list(out_specs) + [any_spec] * n_ride,
        out_shape=list(out_shape) + [_exchange_out(s, g) for s, g in ride.items],
        scratch_shapes=list(scratch_shapes) + [s for src, _ in ride.items for s in _exchange_sems(src)],
        compiler_params=_cparams(("arbitrary",) * len(grid)),
    )(*args, *[s for s, _ in ride.items])
    ride.outs = list(res[n_out:])
    return list(res[:n_out])


def _mm(a, b, mode, out_dtype, name, bias=None, residual=None, ride=None):
    if mode == "nn":
        (m, k), (k2, n) = a.shape, b.shape
    elif mode == "nt":
        (m, k), (n, k2) = a.shape, b.shape
    else:
        (k, m), (k2, n) = a.shape, b.shape
    assert k == k2, (a.shape, b.shape, mode)
    tm = _tile(m, 1024 if mode == "tn" else 640)
    tn = _tile(n, 1280)
    tk = _tile(k, 2048)
    nk = k // tk
    dims = {"nn": (((1,), (0,)), ((), ())), "nt": (((1,), (1,)), ((), ())), "tn": (((0,), (0,)), ((), ()))}[mode]
    a_spec = pl.BlockSpec((tk, tm), lambda i, j, kk: (kk, i)) if mode == "tn" else pl.BlockSpec((tm, tk), lambda i, j, kk: (i, kk))
    b_spec = pl.BlockSpec((tn, tk), lambda i, j, kk: (j, kk)) if mode == "nt" else pl.BlockSpec((tk, tn), lambda i, j, kk: (kk, j))
    ins, specs = [a, b], [a_spec, b_spec]
    if bias is not None:
        ins.append(bias.reshape(1, n).astype(F32))
        specs.append(pl.BlockSpec((1, tn), lambda i, j, kk: (0, j)))
    if residual is not None:
        ins.append(residual)
        specs.append(pl.BlockSpec((tm, tn), lambda i, j, kk: (i, j)))
    has_bias, has_res = bias is not None, residual is not None

    def body(*refs):
        a_ref, b_ref = refs[0], refs[1]
        o_ref, acc = refs[-2], refs[-1]
        kk = pl.program_id(2)

        @pl.when(kk == 0)
        def _():
            acc[...] = jnp.zeros_like(acc)

        acc[...] += lax.dot_general(a_ref[...].astype(BF16), b_ref[...].astype(BF16), dims, preferred_element_type=F32)

        @pl.when(kk == nk - 1)
        def _():
            r = acc[...]
            pos = 2
            if has_bias:
                r = r + refs[pos][...]
                pos += 1
            if has_res:
                r = r + refs[pos][...].astype(F32)
            o_ref[...] = r.astype(o_ref.dtype)

    (out,) = _pcall(
        body, ins, name=name, grid=(m // tm, n // tn, nk), in_specs=specs,
        out_specs=[pl.BlockSpec((tm, tn), lambda i, j, kk: (i, j))],
        out_shape=[jax.ShapeDtypeStruct((m, n), out_dtype)],
        scratch_shapes=[pltpu.VMEM((tm, tn), F32)], sem=("parallel", "parallel", "arbitrary"), ride=ride)
    return out


def _rowwise(fn, rows, consts, out_rows, out_accs, name, tm=None, ride=None):
    rows = [r if isinstance(r, tuple) else (r, r.shape[1], 0) for r in rows]
    n_rows = rows[0][0].shape[0]
    widest = max([w for _, w, _ in rows] + [w for w, _ in out_rows])
    if tm is None:
        tm = _tile(n_rows, 640 if widest <= 2560 else 128, 8)
    steps = n_rows // tm
    in_specs = [pl.BlockSpec((tm, w), functools.partial(lambda i, c: (i, c), c=cb)) for _, w, cb in rows]
    in_specs += [pl.BlockSpec(c.shape, lambda i: (0, 0)) for c in consts]
    out_specs = [pl.BlockSpec((tm, w), lambda i: (i, 0)) for w, _ in out_rows]
    out_specs += [pl.BlockSpec(s, lambda i: (0, 0)) for s in out_accs]
    out_shape = [jax.ShapeDtypeStruct((n_rows, w), d) for w, d in out_rows]
    out_shape += [jax.ShapeDtypeStruct(s, F32) for s in out_accs]
    n_in, n_or = len(rows) + len(consts), len(out_rows)

    def body(*refs):
        i = pl.program_id(0)
        vals = fn(i * tm, *[r[...] for r in refs[:n_in]])
        outs = refs[n_in:]
        for o_ref, v in zip(outs[:n_or], vals[:n_or]):
            o_ref[...] = v.astype(o_ref.dtype)
        if out_accs:
            @pl.when(i == 0)
            def _():
                for a_ref in outs[n_or:]:
                    a_ref[...] = jnp.zeros_like(a_ref)

            for a_ref, v in zip(outs[n_or:], vals[n_or:]):
                a_ref[...] += v

    return _pcall(body, [r[0] for r in rows] + list(consts), name=name, grid=(steps,), in_specs=in_specs, out_specs=out_specs,
                  out_shape=out_shape, sem=("arbitrary",) if out_accs else ("parallel",), ride=ride)


def _colsum(v):
    return jnp.sum(v, axis=0, keepdims=True)


def _rms(h, w):
    return h * lax.rsqrt(jnp.mean(h * h, axis=-1, keepdims=True) + NORM_EPS) * w


def _rms_fwd(h, w, name):
    (hn,) = _rowwise(lambda r0, hv, wv: (_rms(hv, wv),), [h], [w.reshape(1, -1)], [(h.shape[1], BF16)], [], name)
    return hn


def _rms_bwd(h, w, dhn, dh_in, name):
    d = h.shape[1]

    def fn(r0, hv, dv, rv, wv):
        _, vjp = jax.vjp(_rms, hv, wv)
        dh, dw = vjp(dv.astype(F32))
        return dh + rv, dw

    dh, dw = _rowwise(fn, [h, dhn, dh_in], [w.reshape(1, -1)], [(d, F32)], [(1, d)], name)
    return dh, dw[0]


def _swiglu(gu):
    f = gu.shape[1] // 2
    g, u = gu[:, :f].astype(F32), gu[:, f:].astype(F32)
    return jax.nn.silu(g) * u


def _swiglu_fwd(gu, name, ride=None):
    (a,) = _rowwise(lambda r0, v: (_swiglu(v),), [gu], [], [(gu.shape[1] // 2, BF16)], [], name, ride=ride)
    return a


def _swiglu_bwd(gu, da, name, ride=None):
    def fn(r0, v, dv):
        _, vjp = jax.vjp(_swiglu, v.astype(F32))
        return vjp(dv.astype(F32))

    (dgu,) = _rowwise(fn, [gu, da], [], [(gu.shape[1], BF16)], [], name, ride=ride)
    return dgu


def _glu(p, b):
    t = p + b
    d = t.shape[1] // 2
    return t[:, :d] * jax.nn.sigmoid(t[:, d:])


def _glu_fwd(p, b, name, ride=None):
    (u,) = _rowwise(lambda r0, v, bv: (_glu(v, bv),), [p], [b.reshape(1, -1)], [(p.shape[1] // 2, F32)], [], name, ride=ride)
    return u


def _glu_bwd(p, b, du, name):
    def fn(r0, v, dv, bv):
        _, vjp = jax.vjp(_glu, v, bv)
        dp, db = vjp(dv)
        return dp, db

    dp, db = _rowwise(fn, [p, du], [b.reshape(1, -1)], [(p.shape[1], BF16)], [(1, p.shape[1])], name)
    return dp, db[0]


def _ln_silu(c, g, b):
    mu = jnp.mean(c, axis=-1, keepdims=True)
    xc = c - mu
    var = jnp.mean(xc * xc, axis=-1, keepdims=True)
    return jax.nn.silu(xc * lax.rsqrt(var + LN_EPS) * g + b)


def _ln_silu_fwd(c, g, b, name):
    (s,) = _rowwise(lambda r0, v, gv, bv: (_ln_silu(v, gv, bv),), [c], [g.reshape(1, -1), b.reshape(1, -1)],
                    [(c.shape[1], BF16)], [], name)
    return s


def _ln_silu_bwd(c, g, b, ds, name):
    d = c.shape[1]

    def fn(r0, v, dv, gv, bv):
        _, vjp = jax.vjp(_ln_silu, v, gv, bv)
        return vjp(dv.astype(F32))

    dc, dg, db = _rowwise(fn, [c, ds], [g.reshape(1, -1), b.reshape(1, -1)], [(d, F32)], [(1, d), (1, d)], name)
    return dc, dg[0], db[0]


def _colsum_rows(v, name):
    (s,) = _rowwise(lambda r0, t: (_colsum(t.astype(F32)),), [v], [], [], [(1, v.shape[1])], name)
    return s[0]


def _rot_half(x):
    w = x.shape[1]
    lane = lax.broadcasted_iota(jnp.int32, x.shape, 1)
    lo = (lane % HEAD_DIM) < (HEAD_DIM // 2)
    return jnp.where(lo, -pltpu.roll(x, w - HEAD_DIM // 2, axis=1), pltpu.roll(x, HEAD_DIM // 2, axis=1))


def _qkv_post_fwd(pre, b, cos, sin, qw, kw, name):
    reps = (qw + kw) // LANES

    def fn(r0, pv, cv, sv, bv):
        t = pv + bv
        tq = t[:, :qw + kw]
        y = tq * jnp.tile(cv, (1, reps)) + _rot_half(tq) * jnp.tile(sv, (1, reps))
        return y[:, :qw], y[:, qw:], t[:, qw + kw:]

    return _rowwise(fn, [pre, cos, sin], [b.reshape(1, -1)], [(qw, BF16), (kw, BF16), (kw, BF16)], [], name)


def _qkv_post_bwd(dq, dk, dv, cos, sin, name):
    qw, kw = dq.shape[1], dk.shape[1]
    reps = (qw + kw) // LANES
    width = qw + 2 * kw

    def fn(r0, dqv, dkv, dvv, cv, sv):
        dy = jnp.concatenate([dqv.astype(F32), dkv.astype(F32)], axis=1)
        dt = dy * jnp.tile(cv, (1, reps)) - _rot_half(dy * jnp.tile(sv, (1, reps)))
        dpre = jnp.concatenate([dt, dvv.astype(F32)], axis=1)
        return dpre, _colsum(dpre)

    dpre, db = _rowwise(fn, [dq, dk, dv, cos, sin], [], [(width, BF16)], [(1, width)], name)
    return dpre, db[0]


def _dw_tiles(n_rows, c):
    return _tile(n_rows, 640, 8), _tile(c, 512)


def _row_mask(r0, n, width):
    row = r0 + lax.broadcasted_iota(jnp.int32, (n, width), 0)
    return row >= PAD_LEN


def _dwconv_fwd(u, w, bias, c, name, ride=None):
    n_rows, taps = u.shape[0], w.shape[0]
    tr, cb = _dw_tiles(n_rows, c)
    kp = -(-taps // 8) * 8
    wp = jnp.concatenate([w.astype(F32), jnp.zeros((kp - taps, c), F32)], axis=0)
    bp = jnp.zeros((1, c), F32) if bias is None else bias.reshape(1, c).astype(F32)

    def body(cur_ref, prev_ref, w_ref, b_ref, o_ref):
        r = pl.program_id(1)
        cur = jnp.where(_row_mask(r * tr, tr, cb), cur_ref[...], 0.0)
        tail = prev_ref[tr - CONV_HALO:, :]
        tail = jnp.where(_row_mask(r * tr - CONV_HALO, CONV_HALO, cb) & (r > 0), tail, 0.0)
        win = jnp.concatenate([tail, cur], axis=0)
        acc = jnp.zeros((tr, cb), F32) + b_ref[...]
        for k in range(taps):
            off = CONV_HALO - (taps - 1) + k
            acc = acc + w_ref[k:k + 1, :] * win[off:off + tr, :]
        o_ref[...] = acc

    (out,) = _pcall(
        body, [u, u, wp, bp], name=name, grid=(c // cb, n_rows // tr),
        in_specs=[pl.BlockSpec((tr, cb), lambda j, r: (r, j)),
                  pl.BlockSpec((tr, cb), lambda j, r: (jnp.maximum(r - 1, 0), j)),
                  pl.BlockSpec((kp, cb), lambda j, r: (0, j)),
                  pl.BlockSpec((1, cb), lambda j, r: (0, j))],
        out_specs=[pl.BlockSpec((tr, cb), lambda j, r: (r, j))],
        out_shape=[jax.ShapeDtypeStruct((n_rows, c), F32)], sem=("parallel", "parallel"), ride=ride)
    return out


def _dwconv_bwd(u, w, dc, c, name, ride=None):
    n_rows, taps = u.shape[0], w.shape[0]
    tr, cb = _dw_tiles(n_rows, c)
    nr = n_rows // tr
    kp = -(-taps // 8) * 8
    wp = jnp.concatenate([w.astype(F32), jnp.zeros((kp - taps, c), F32)], axis=0)

    def body(cur_ref, prev_ref, d_ref, dnext_ref, w_ref, du_ref, dw_ref, db_ref):
        r = pl.program_id(1)
        cur = jnp.where(_row_mask(r * tr, tr, cb), cur_ref[...], 0.0)
        tail = prev_ref[tr - CONV_HALO:, :]
        tail = jnp.where(_row_mask(r * tr - CONV_HALO, CONV_HALO, cb) & (r > 0), tail, 0.0)
        win_u = jnp.concatenate([tail, cur], axis=0)
        d = d_ref[...]
        head = jnp.where(r < nr - 1, dnext_ref[:CONV_HALO, :], 0.0)
        win_d = jnp.concatenate([d, head], axis=0)

        @pl.when(r == 0)
        def _():
            dw_ref[...] = jnp.zeros_like(dw_ref)
            db_ref[...] = jnp.zeros_like(db_ref)

        du = jnp.zeros((tr, cb), F32)
        for k in range(taps):
            off = CONV_HALO - (taps - 1) + k
            du = du + w_ref[k:k + 1, :] * win_d[taps - 1 - k:taps - 1 - k + tr, :]
            dw_ref[k:k + 1, :] += _colsum(d * win_u[off:off + tr, :])
        du_ref[...] = jnp.where(_row_mask(r * tr, tr, cb), du, 0.0)
        db_ref[...] += _colsum(d)

    du, dw, db = _pcall(
        body, [u, u, dc, dc, wp], name=name, grid=(c // cb, nr),
        in_specs=[pl.BlockSpec((tr, cb), lambda j, r: (r, j)),
                  pl.BlockSpec((tr, cb), lambda j, r: (jnp.maximum(r - 1, 0), j)),
                  pl.BlockSpec((tr, cb), lambda j, r: (r, j)),
                  pl.BlockSpec((tr, cb), lambda j, r: (jnp.minimum(r + 1, nr - 1), j)),
                  pl.BlockSpec((kp, cb), lambda j, r: (0, j))],
        out_specs=[pl.BlockSpec((tr, cb), lambda j, r: (r, j)),
                   pl.BlockSpec((kp, cb), lambda j, r: (0, j)),
                   pl.BlockSpec((1, cb), lambda j, r: (0, j))],
        out_shape=[jax.ShapeDtypeStruct((n_rows, c), F32), jax.ShapeDtypeStruct((kp, c), F32),
                   jax.ShapeDtypeStruct((1, c), F32)], sem=("parallel", "arbitrary"), ride=ride)
    return du, dw[:taps], db[0]


def _attn_block(q, kprev, kcur, vprev, vcur, sink, n):
    qf = q.reshape(GROUP * BLOCK, HEAD_DIM).astype(BF16)
    kb = jnp.concatenate([kprev, kcur], axis=0).astype(BF16)
    vb = jnp.concatenate([vprev, vcur], axis=0).astype(BF16)
    s = lax.dot_general(qf, kb, (((1,), (1,)), ((), ())), preferred_element_type=F32) * (HEAD_DIM ** -0.5)
    s = s.reshape(GROUP, BLOCK, 2 * BLOCK)
    qi = lax.broadcasted_iota(jnp.int32, (BLOCK, 2 * BLOCK), 0)
    kj = lax.broadcasted_iota(jnp.int32, (BLOCK, 2 * BLOCK), 1)
    dist = qi + BLOCK - kj
    allowed = (dist >= 0) & (dist < BLOCK) & ((n - 1) * BLOCK + kj >= PAD_LEN)
    s = jnp.where(allowed[None], s, NEG_INF)
    m = lax.stop_gradient(jnp.maximum(jnp.max(s, axis=-1, keepdims=True), sink))
    e = jnp.exp(s - m)
    p = e / (jnp.sum(e, axis=-1, keepdims=True) + jnp.exp(sink - m))
    o = jnp.dot(p.reshape(GROUP * BLOCK, 2 * BLOCK).astype(BF16), vb, preferred_element_type=F32)
    return o.reshape(GROUP, BLOCK, HEAD_DIM)


def _attn_specs(nb):
    q_spec = pl.BlockSpec((GROUP, BLOCK, HEAD_DIM), lambda g, n: (g, n, 0))
    cur = pl.BlockSpec((1, BLOCK, HEAD_DIM), lambda g, n: (g, n, 0))
    prev = pl.BlockSpec((1, BLOCK, HEAD_DIM), lambda g, n: (g, jnp.maximum(n - 1, 0), 0))
    sink = pl.BlockSpec((1, GROUP, 1, 1), lambda g, n: (g, 0, 0, 0))
    return q_spec, cur, prev, sink


def _attn_fwd(q, k, v, sinks, name, ride=None):
    heads, n_rows, _ = q.shape
    nb = n_rows // BLOCK
    q_spec, cur, prev, sink = _attn_specs(nb)

    def body(q_ref, kp_ref, kc_ref, vp_ref, vc_ref, s_ref, o_ref):
        n = pl.program_id(1)
        o = _attn_block(q_ref[...].astype(F32), kp_ref[0].astype(F32), kc_ref[0].astype(F32), vp_ref[0].astype(F32),
                        vc_ref[0].astype(F32), s_ref[0], n)
        o_ref[...] = o.astype(o_ref.dtype)

    (out,) = _pcall(
        body, [q, k, k, v, v, sinks.reshape(heads // GROUP, GROUP, 1, 1)], name=name, grid=(heads // GROUP, nb),
        in_specs=[q_spec, prev, cur, prev, cur, sink], out_specs=[q_spec], out_shape=[jax.ShapeDtypeStruct(q.shape, BF16)],
        sem=("parallel", "parallel"), ride=ride)
    return out


def _attn_bwd(q, k, v, sinks, do, name):
    heads, n_rows, _ = q.shape
    kvh = heads // GROUP
    nb = n_rows // BLOCK
    q_spec, cur, prev, sink = _attn_specs(nb)
    part = pl.BlockSpec((1, 1, BLOCK, HEAD_DIM), lambda g, n: (g, n, 0, 0))
    part_shape = jax.ShapeDtypeStruct((kvh, nb, BLOCK, HEAD_DIM), F32)

    def body(q_ref, kp_ref, kc_ref, vp_ref, vc_ref, s_ref, do_ref, dq_ref, dkp_ref, dkc_ref, dvp_ref, dvc_ref, ds_ref):
        n = pl.program_id(1)
        f = functools.partial(_attn_block, n=n)
        _, vjp = jax.vjp(f, q_ref[...].astype(F32), kp_ref[0].astype(F32), kc_ref[0].astype(F32), vp_ref[0].astype(F32),
                         vc_ref[0].astype(F32), s_ref[0])
        dq, dkp, dkc, dvp, dvc, ds = vjp(do_ref[...].astype(F32))
        dq_ref[...] = dq.astype(dq_ref.dtype)
        dkp_ref[0, 0], dkc_ref[0, 0], dvp_ref[0, 0], dvc_ref[0, 0] = dkp, dkc, dvp, dvc

        @pl.when(n == 0)
        def _():
            ds_ref[...] = jnp.zeros_like(ds_ref)

        ds_ref[0] += ds

    return pl.pallas_call(
        body, name=name, grid=(kvh, nb), in_specs=[q_spec, prev, cur, prev, cur, sink, q_spec],
        out_specs=[q_spec, part, part, part, part, sink],
        out_shape=[jax.ShapeDtypeStruct(q.shape, BF16), part_shape, part_shape, part_shape, part_shape,
                   jax.ShapeDtypeStruct((kvh, GROUP, 1, 1), F32)],
        compiler_params=_cparams(("parallel", "arbitrary")),
    )(q, k, k, v, v, sinks.reshape(kvh, GROUP, 1, 1), do)


def _shift_add(own, to_prev, name):
    kvh, nb = own.shape[:2]
    blk = (1, 1, BLOCK, HEAD_DIM)

    def body(a_ref, b_ref, o_ref):
        n = pl.program_id(1)
        o_ref[...] = (a_ref[...] + jnp.where(n < nb - 1, b_ref[...], 0.0)).astype(o_ref.dtype)

    return pl.pallas_call(
        body, name=name, grid=(kvh, nb),
        in_specs=[pl.BlockSpec(blk, lambda g, n: (g, n, 0, 0)),
                  pl.BlockSpec(blk, lambda g, n: (g, jnp.minimum(n + 1, nb - 1), 0, 0))],
        out_specs=pl.BlockSpec(blk, lambda g, n: (g, n, 0, 0)),
        out_shape=jax.ShapeDtypeStruct(own.shape, BF16), compiler_params=_cparams(("parallel", "parallel")),
    )(own, to_prev)


def _gdn_gates(ba, alog, dt, r0, hv):
    lane = lax.broadcasted_iota(jnp.int32, ba.shape, 1)
    t = ba + dt
    softplus = jnp.maximum(t, 0.0) + jnp.log(1.0 + jnp.exp(-jnp.abs(t)))
    val = jnp.where(lane < hv, jax.nn.sigmoid(ba), jnp.where(lane < 2 * hv, -jnp.exp(alog) * softplus, 0.0))
    return jnp.where(_row_mask(r0, ba.shape[0], ba.shape[1]), val, 0.0)


def _l2n(x):
    return x * lax.rsqrt(jnp.sum(x * x, axis=-1, keepdims=True) + 1e-6)


_NN = (((2,), (1,)), ((0,), (0,)))
_NT = (((2,), (2,)), ((0,), (0,)))
_TN = (((1,), (1,)), ((0,), (0,)))


def _bdot(a, b, dims=_NN):
    return lax.dot_general(a.astype(BF16), b.astype(BF16), dims, preferred_element_type=F32)


def _dot3(a, b, dims):
    ah, bh = a.astype(BF16), b.astype(BF16)
    al, bl = (a - ah.astype(F32)).astype(BF16), (b - bh.astype(F32)).astype(BF16)

    def d(p, q):
        return lax.dot_general(p, q, dims, preferred_element_type=F32)

    return d(ah, bh) + (d(ah, bl) + d(al, bh))


@jax.custom_vjp
def _pdot(a, b):
    return _dot3(a, b, _NN)


def _pdot_fwd(a, b):
    return _dot3(a, b, _NN), (a, b)


def _pdot_bwd(res, ct):
    a, b = res
    return _dot3(ct, b, _NT), _dot3(a, ct, _TN)


_pdot.defvjp(_pdot_fwd, _pdot_bwd)


def _scan_chunks(x, reverse):
    n, c = x.shape[0], GDN_CHUNK
    row = lax.broadcasted_iota(jnp.int32, x.shape, 0) % c
    s = 1
    while s < c:
        if reverse:
            x = x + jnp.where(row < c - s, pltpu.roll(x, n - s, axis=0), 0.0)
        else:
            x = x + jnp.where(row >= s, pltpu.roll(x, s, axis=0), 0.0)
        s *= 2
    return x


@jax.custom_vjp
def _cumsum_chunks(x):
    return _scan_chunks(x, False)


_cumsum_chunks.defvjp(lambda x: (_scan_chunks(x, False), None), lambda _, ct: (_scan_chunks(ct, True),))


def _gdn_heads(states, qkv, z, gates, norm_w, hk0, hv_total):
    c, h = GDN_CHUNK, states.shape[0]
    g = h // 2

    def cols(src, starts):
        return jnp.stack([src[:, s:s + GDN_DIM] for s in starts])

    q = _l2n(jax.nn.silu(cols(qkv, [4 * GDN_DIM * t for t in range(g)]))) * (GDN_DIM ** -0.5)
    k = _l2n(jax.nn.silu(cols(qkv, [4 * GDN_DIM * t + GDN_DIM for t in range(g)])))
    q, k = jnp.repeat(q, 2, axis=0), jnp.repeat(k, 2, axis=0)
    v = jax.nn.silu(cols(qkv, [4 * GDN_DIM * (t // 2) + (2 + t % 2) * GDN_DIM for t in range(h)]))
    zz = cols(z, [GDN_DIM * t for t in range(h)])
    lane = lax.broadcasted_iota(jnp.int32, gates.shape, 1)

    def col_of(first):
        return jnp.stack([jnp.sum(jnp.where(lane == first + t, gates, 0.0), axis=1, keepdims=True) for t in range(h)])

    beta_col, g_col = col_of(2 * hk0), col_of(hv_total + 2 * hk0)
    i = lax.broadcasted_iota(jnp.int32, (c, c), 0)
    j = lax.broadcasted_iota(jnp.int32, (c, c), 1)
    causal, strict = (i >= j)[None], (i > j)[None]
    gc = _cumsum_chunks(jnp.broadcast_to(g_col, (h, c, GDN_DIM)).reshape(h * c, GDN_DIM)).reshape(h, c, GDN_DIM)
    gc_i = gc[:, :, :c]
    gc_j = jnp.swapaxes(gc_i, 1, 2)
    gc_last = jnp.broadcast_to(gc[:, c - 1:c, :], (h, GDN_DIM, GDN_DIM))
    decay = jnp.where(causal, jnp.exp(jnp.where(causal, gc_i - gc_j, 0.0)), 0.0)
    k_beta = k * beta_col
    lower = jnp.where(strict, _bdot(k_beta, k, _NT) * decay, 0.0)
    eye = (i == j).astype(F32)[None]
    neg = -lower
    inv = eye + neg
    power = neg
    for _ in range(5):
        power = _pdot(power, power)
        inv = _pdot(inv, eye + power)
    sol = _pdot(inv, jnp.concatenate([v * beta_col, k_beta * jnp.exp(gc)], axis=2))
    u, w = sol[:, :, :GDN_DIM], sol[:, :, GDN_DIM:]
    intra = jnp.where(causal, _bdot(q, k, _NT) * decay, 0.0)
    q_dec = q * jnp.exp(gc)
    k_dec = k * jnp.exp(gc_last[:, :c] - gc)
    v_new = u - _bdot(w, states)
    o = _bdot(q_dec, states) + _bdot(intra, v_new)
    new_states = states * jnp.exp(gc_last) + _bdot(k_dec, v_new, _TN)
    y = _rms(o, norm_w) * jax.nn.silu(zz)
    return jnp.concatenate([y[t] for t in range(h)], axis=1), new_states


GDN_KEY_HEADS_PER_STEP = 4


def _key_heads_per_step(hk_total):
    return math.gcd(hk_total, GDN_KEY_HEADS_PER_STEP)


def _gdn_fwd(cq, proj, gates, norm_w, hk_total, name):
    n_rows = cq.shape[0]
    nc, hv_total = n_rows // GDN_CHUNK, 2 * hk_total
    grp = _key_heads_per_step(hk_total)
    heads = 2 * grp
    zblk0 = cq.shape[1] // (heads * GDN_DIM)

    def body(cq_ref, z_ref, g_ref, w_ref, y_ref, save_ref, state):
        n, hg = pl.program_id(0), pl.program_id(1)

        @pl.when(n == 0)
        def _():
            state[pl.ds(heads * hg, heads)] = jnp.zeros((heads, GDN_DIM, GDN_DIM), F32)

        s_in = state[pl.ds(heads * hg, heads)]
        save_ref[0] = s_in
        y, s_out = _gdn_heads(s_in, cq_ref[...], z_ref[...], g_ref[...], w_ref[...], grp * hg, hv_total)
        y_ref[...] = y.astype(y_ref.dtype)
        state[pl.ds(heads * hg, heads)] = s_out

    return pl.pallas_call(
        body, name=name, grid=(nc, hk_total // grp),
        in_specs=[pl.BlockSpec((GDN_CHUNK, 2 * heads * GDN_DIM), lambda n, h: (n, h)),
                  pl.BlockSpec((GDN_CHUNK, heads * GDN_DIM), lambda n, h: (n, zblk0 + h)),
                  pl.BlockSpec((GDN_CHUNK, LANES), lambda n, h: (n, 0)),
                  pl.BlockSpec((1, GDN_DIM), lambda n, h: (0, 0))],
        out_specs=[pl.BlockSpec((GDN_CHUNK, heads * GDN_DIM), lambda n, h: (n, h)),
                   pl.BlockSpec((1, heads, GDN_DIM, GDN_DIM), lambda n, h: (n, h, 0, 0))],
        out_shape=[jax.ShapeDtypeStruct((n_rows, hv_total * GDN_DIM), BF16),
                   jax.ShapeDtypeStruct((nc, hv_total, GDN_DIM, GDN_DIM), F32)],
        scratch_shapes=[pltpu.VMEM((hv_total, GDN_DIM, GDN_DIM), F32)],
        compiler_params=_cparams(("arbitrary", "arbitrary")),
    )(cq, proj, gates, norm_w.reshape(1, GDN_DIM))


def _gdn_bwd(cq, proj, gates, norm_w, saved, dy, hk_total, name):
    n_rows = cq.shape[0]
    nc, hv_total = n_rows // GDN_CHUNK, 2 * hk_total
    grp = _key_heads_per_step(hk_total)
    heads = 2 * grp
    zblk0 = cq.shape[1] // (heads * GDN_DIM)

    def body(cq_ref, z_ref, g_ref, w_ref, save_ref, dy_ref, dcq_ref, dz_ref, dg_ref, dw_ref, dstate):
        n, hg = pl.program_id(0), pl.program_id(1)

        @pl.when(n == 0)
        def _():
            dstate[pl.ds(heads * hg, heads)] = jnp.zeros((heads, GDN_DIM, GDN_DIM), F32)

        @pl.when((n == 0) & (hg == 0))
        def _():
            dw_ref[...] = jnp.zeros_like(dw_ref)

        @pl.when(hg == 0)
        def _():
            dg_ref[...] = jnp.zeros_like(dg_ref)

        f = functools.partial(_gdn_heads, hk0=grp * hg, hv_total=hv_total)
        _, vjp = jax.vjp(f, save_ref[0], cq_ref[...], z_ref[...], g_ref[...], w_ref[...])
        ds, dcq, dz, dg, dw = vjp((dy_ref[...].astype(F32), dstate[pl.ds(heads * hg, heads)]))
        dstate[pl.ds(heads * hg, heads)] = ds
        dcq_ref[...] = dcq
        dz_ref[...] = dz.astype(dz_ref.dtype)
        dg_ref[...] += dg
        dw_ref[...] += dw

    rev = lambda n: nc - 1 - n
    return pl.pallas_call(
        body, name=name, grid=(nc, hk_total // grp),
        in_specs=[pl.BlockSpec((GDN_CHUNK, 2 * heads * GDN_DIM), lambda n, h: (rev(n), h)),
                  pl.BlockSpec((GDN_CHUNK, heads * GDN_DIM), lambda n, h: (rev(n), zblk0 + h)),
                  pl.BlockSpec((GDN_CHUNK, LANES), lambda n, h: (rev(n), 0)),
                  pl.BlockSpec((1, GDN_DIM), lambda n, h: (0, 0)),
                  pl.BlockSpec((1, heads, GDN_DIM, GDN_DIM), lambda n, h: (rev(n), h, 0, 0)),
                  pl.BlockSpec((GDN_CHUNK, heads * GDN_DIM), lambda n, h: (rev(n), h))],
        out_specs=[pl.BlockSpec((GDN_CHUNK, 2 * heads * GDN_DIM), lambda n, h: (rev(n), h)),
                   pl.BlockSpec((GDN_CHUNK, heads * GDN_DIM), lambda n, h: (rev(n), h)),
                   pl.BlockSpec((GDN_CHUNK, LANES), lambda n, h: (rev(n), 0)),
                   pl.BlockSpec((1, GDN_DIM), lambda n, h: (0, 0))],
        out_shape=[jax.ShapeDtypeStruct(cq.shape, F32), jax.ShapeDtypeStruct((n_rows, hv_total * GDN_DIM), BF16),
                   jax.ShapeDtypeStruct((n_rows, LANES), F32), jax.ShapeDtypeStruct((1, GDN_DIM), F32)],
        scratch_shapes=[pltpu.VMEM((hv_total, GDN_DIM, GDN_DIM), F32)],
        compiler_params=_cparams(("arbitrary", "arbitrary")),
    )(cq, proj, gates, norm_w.reshape(1, GDN_DIM), saved, dy)


def _final_loss(h, w, target, name):
    d = h.shape[1]

    def fn(r0, hv, tv, wv):
        def loss_of(hh, ww):
            err = jnp.where(_row_mask(r0, hh.shape[0], d) & (r0 + lax.broadcasted_iota(jnp.int32, hh.shape, 0) >= FRONT),
                            _rms(hh, ww) - tv, 0.0)
            return 0.5 * jnp.sum(jnp.sum(err * err, axis=1, keepdims=True) / d)

        loss, vjp = jax.vjp(loss_of, hv, wv)
        dh, dw = vjp(jnp.ones((), F32))
        return dh, jnp.zeros((1, LANES), F32) + loss, dw

    dh, loss, dw = _rowwise(fn, [h, target], [w.reshape(1, -1)], [(d, F32)], [(1, LANES), (1, d)], name)
    return loss[0, 0], dh, dw[0]


def _rope_tables(n_rows):
    pos = (jnp.arange(n_rows) - PAD_LEN).astype(F32)
    inv_freq = ROPE_THETA ** (-jnp.arange(0, HEAD_DIM, 2, dtype=F32) / HEAD_DIM)
    ang = pos[:, None] * inv_freq[None, :]
    reps = LANES // (HEAD_DIM // 2)
    return jnp.tile(jnp.cos(ang), (1, reps)), jnp.tile(jnp.sin(ang), (1, reps))


def _to_heads(t):
    n_rows, w = t.shape
    return t.reshape(n_rows, w // HEAD_DIM, HEAD_DIM).transpose(1, 0, 2)


def _from_heads(t):
    heads, n_rows, _ = t.shape
    return t.transpose(1, 0, 2).reshape(n_rows, heads * HEAD_DIM)


def _local_step(h0, target, p, tr):
    n_rows, d = h0.shape
    depth = p["norm_mix"].shape[0]
    cos, sin = _rope_tables(n_rows)
    hk_total = d // GDN_DIM
    hv_total = 2 * hk_total
    conv_dim = 4 * hk_total * GDN_DIM
    qw, kw = d, d // GROUP
    saved = []
    h = h0
    mm = functools.partial(tr.mm, True)
    for i in range(depth):
        kind, j = i % 3, i // 3
        s = {"h": h}
        hn = _rms_fwd(h, p["norm_mix"][i], "rms_fwd")
        s["hn"] = hn
        if kind == 0:
            pre = mm(hn, tr.weight("conv_w_pw1", j), "nn", F32, "mm_pw1")
            u1 = tr.call(True, _us_rows(pre), lambda r: _glu_fwd(pre, p["conv_b_pw1"][j], "glu_fwd", ride=r))
            c = tr.call(True, _us_dwconv(n_rows, d, CONV_KERNEL),
                        lambda r: _dwconv_fwd(u1, p["conv_w_dw"][j], p["conv_b_dw"][j], d, "dwconv31_fwd", ride=r))
            sv = _ln_silu_fwd(c, p["conv_ln_g"][j], p["conv_ln_b"][j], "ln_silu_fwd")
            h = mm(sv, tr.weight("conv_w_pw2", j), "nn", F32, "mm_d_d_res", bias=p["conv_b_pw2"][j], residual=h)
            s.update(pre=pre, u1=u1, c=c, sv=sv)
        elif kind == 1:
            pre = mm(hn, tr.weight("attn_w_qkv", j), "nn", F32, "mm_qkv")
            q, k, v = _qkv_post_fwd(pre, p["attn_b_qkv"][j], cos, sin, qw, kw, "qkv_post_fwd")
            qh, kh, vh = _to_heads(q), _to_heads(k), _to_heads(v)
            o = _from_heads(tr.call(True, _us_attn(qh), lambda r: _attn_fwd(qh, kh, vh, p["attn_sinks"][j], "attn_fwd", ride=r)))
            h = mm(o, tr.weight("attn_w_o", j), "nn", F32, "mm_d_d_res", bias=p["attn_b_o"][j], residual=h)
            s.update(qh=qh, kh=kh, vh=vh, o=o)
        else:
            proj = mm(hn, tr.weight("gdn_w_in", j), "nn", F32, "mm_gdn_in")
            cq = tr.call(True, _us_dwconv(n_rows, conv_dim, GDN_CONV),
                         lambda r: _dwconv_fwd(proj, p["gdn_conv_w"][j], None, conv_dim, "dwconv4_fwd", ride=r))
            vec = _gate_vectors(p["gdn_a_log"][j], p["gdn_dt_bias"][j], hv_total)
            ba_blk = (conv_dim + hv_total * GDN_DIM) // LANES
            (gates,) = _rowwise(lambda r0, bav, av, dv: (_gdn_gates(bav, av, dv, r0, hv_total),), [(proj, LANES, ba_blk)],
                                [vec[0], vec[1]], [(LANES, F32)], [], "gdn_gates_fwd")
            y, states = _gdn_fwd(cq, proj, gates, p["gdn_norm_w"][j], hk_total, "gdn_fwd")
            h = mm(y, tr.weight("gdn_w_out", j), "nn", F32, "mm_gdn_out_res", residual=h)
            s.update(proj=proj, cq=cq, gates=gates, y=y, states=states)
        s["h1"] = h
        hn2 = _rms_fwd(h, p["norm_ffn"][i], "rms_fwd")
        gu = mm(hn2, tr.weight("ffn_w_gu", i), "nn", BF16, "mm_ffn_gu")
        a = tr.call(True, _us_rows(gu), lambda r: _swiglu_fwd(gu, "swiglu_fwd", ride=r))
        h = mm(a, tr.weight("ffn_w_down", i), "nn", F32, "mm_ffn_down_res", residual=h)
        s.update(hn2=hn2, gu=gu, a=a)
        saved.append(s)

    loss, dh, g_final = _final_loss(h, p["norm_final"], target, "final_loss")
    g = {k: [None] * v.shape[0] for k, v in p.items() if k not in ("norm_final", "meta_tokens")}
    g["norm_final"] = g_final
    mm = functools.partial(tr.mm, False)
    for i in reversed(range(depth)):
        kind, j = i % 3, i // 3
        s = saved[i]
        tr.give("ffn_w_down", i, mm(s["a"], dh, "tn", BF16, "mm_dw_down"))
        da = mm(dh, tr.weight("ffn_w_down", i), "nt", F32, "mm_da")
        dgu = tr.call(False, 2 * _us_rows(s["gu"]), lambda r: _swiglu_bwd(s["gu"], da, "swiglu_bwd", ride=r))
        tr.give("ffn_w_gu", i, mm(s["hn2"], dgu, "tn", BF16, "mm_dw_gu"))
        dhn2 = mm(dgu, tr.weight("ffn_w_gu", i), "nt", F32, "mm_dhn2")
        dh, g["norm_ffn"][i] = _rms_bwd(s["h1"], p["norm_ffn"][i], dhn2, dh, "rms_bwd")
        if kind == 0:
            g["conv_b_pw2"][j] = _colsum_rows(dh, "colsum_d")
            tr.give("conv_w_pw2", j, mm(s["sv"], dh, "tn", BF16, "mm_dw_d_d"))
            dsv = mm(dh, tr.weight("conv_w_pw2", j), "nt", F32, "mm_dx_d_d")
            dc, g["conv_ln_g"][j], g["conv_ln_b"][j] = _ln_silu_bwd(s["c"], p["conv_ln_g"][j], p["conv_ln_b"][j], dsv, "ln_silu_bwd")
            du1, g["conv_w_dw"][j], g["conv_b_dw"][j] = tr.call(
                False, 2 * _us_dwconv(n_rows, d, CONV_KERNEL),
                lambda r: _dwconv_bwd(s["u1"], p["conv_w_dw"][j], dc, d, "dwconv31_bwd", ride=r))
            dpre, g["conv_b_pw1"][j] = _glu_bwd(s["pre"], p["conv_b_pw1"][j], du1, "glu_bwd")
            tr.give("conv_w_pw1", j, mm(s["hn"], dpre, "tn", BF16, "mm_dw_pw1"))
            dhn = mm(dpre, tr.weight("conv_w_pw1", j), "nt", F32, "mm_dx_pw1")
        elif kind == 1:
            g["attn_b_o"][j] = _colsum_rows(dh, "colsum_d")
            tr.give("attn_w_o", j, mm(s["o"], dh, "tn", BF16, "mm_dw_d_d"))
            do = _to_heads(mm(dh, tr.weight("attn_w_o", j), "nt", BF16, "mm_dx_d_d_bf16"))
            dq, dkp, dkc, dvp, dvc, dsink = _attn_bwd(s["qh"], s["kh"], s["vh"], p["attn_sinks"][j], do, "attn_bwd")
            g["attn_sinks"][j] = dsink.reshape(-1)
            kvh = kw // HEAD_DIM
            dk = _shift_add(dkc, dkp, "attn_shift_add").reshape(kvh, n_rows, HEAD_DIM)
            dv = _shift_add(dvc, dvp, "attn_shift_add").reshape(kvh, n_rows, HEAD_DIM)
            dpre, g["attn_b_qkv"][j] = _qkv_post_bwd(_from_heads(dq), _from_heads(dk), _from_heads(dv), cos, sin, "qkv_post_bwd")
            tr.give("attn_w_qkv", j, mm(s["hn"], dpre, "tn", BF16, "mm_dw_qkv"))
            dhn = mm(dpre, tr.weight("attn_w_qkv", j), "nt", F32, "mm_dx_qkv")
        else:
            tr.give("gdn_w_out", j, mm(s["y"], dh, "tn", BF16, "mm_dw_gdn_out"))
            dy = mm(dh, tr.weight("gdn_w_out", j), "nt", BF16, "mm_dx_gdn_out")
            dcq, dz, dgates, g_nw = _gdn_bwd(s["cq"], s["proj"], s["gates"], p["gdn_norm_w"][j], s["states"], dy, hk_total, "gdn_bwd")
            g["gdn_norm_w"][j] = g_nw[0]
            vec = _gate_vectors(p["gdn_a_log"][j], p["gdn_dt_bias"][j], hv_total)
            ba_blk = (conv_dim + hv_total * GDN_DIM) // LANES

            def gates_bwd(r0, bav, dgv, av, dv):
                _, vjp = jax.vjp(functools.partial(_gdn_gates, r0=r0, hv=hv_total), bav, av, dv)
                return vjp(dgv)

            dba, d_alog, d_dt = _rowwise(gates_bwd, [(s["proj"], LANES, ba_blk), dgates], [vec[0], vec[1]], [(LANES, BF16)],
                                         [(1, LANES), (1, LANES)], "gdn_gates_bwd")
            g["gdn_a_log"][j] = d_alog[0, hv_total:2 * hv_total]
            g["gdn_dt_bias"][j] = d_dt[0, hv_total:2 * hv_total]
            dconv_in, g["gdn_conv_w"][j], _ = tr.call(
                False, 2 * _us_dwconv(n_rows, conv_dim, GDN_CONV),
                lambda r: _dwconv_bwd(s["proj"], p["gdn_conv_w"][j], dcq, conv_dim, "dwconv4_bwd", ride=r))
            w_in = tr.weight("gdn_w_in", j)
            pad = jnp.zeros((n_rows, w_in.shape[1] - conv_dim - hv_total * GDN_DIM - LANES), BF16)
            dproj = jnp.concatenate([dconv_in.astype(BF16), dz, dba, pad], axis=1)
            tr.give("gdn_w_in", j, mm(s["hn"], dproj, "tn", BF16, "mm_dw_gdn_in"))
            dhn = mm(dproj, w_in, "nt", F32, "mm_dx_gdn_in")
        dh, g["norm_mix"][i] = _rms_bwd(s["h"], p["norm_mix"][i], dhn, dh, "rms_bwd")
    g = {k: (jnp.stack(v) if isinstance(v, list) else v) for k, v in g.items()}
    return loss, dh, g


def _us_mm(m, n, k):
    return 2.0 * m * n * k / 8.0e8


def _us_rows(t):
    return t.shape[0] * t.shape[1] / 8.0e5


def _us_dwconv(n_rows, c, taps):
    return n_rows * c * (taps + 8) / 2.0e6


def _us_attn(qh):
    return qh.shape[0] * qh.shape[1] / 500.0


def _gate_vectors(a_log, dt_bias, hv):
    def place(t):
        return jnp.concatenate([jnp.zeros((hv,), F32), t, jnp.zeros((LANES - 2 * hv,), F32)]).reshape(1, LANES)

    return place(a_log), place(dt_bias)


GDN_IN_ALIGN = 512


def _gdn_group(w, hk):
    lead, kw = w.shape[:-1], hk * GDN_DIM
    q = w[..., :kw].reshape(*lead, hk, 1, GDN_DIM)
    k = w[..., kw:2 * kw].reshape(*lead, hk, 1, GDN_DIM)
    v = w[..., 2 * kw:4 * kw].reshape(*lead, hk, 2, GDN_DIM)
    return jnp.concatenate([q, k, v], axis=-2).reshape(*lead, 4 * kw)


def _gdn_ungroup(w, hk):
    lead, kw = w.shape[:-1], hk * GDN_DIM
    t = w.reshape(*lead, hk, 4, GDN_DIM)
    return jnp.concatenate([t[..., 0, :].reshape(*lead, kw), t[..., 1, :].reshape(*lead, kw),
                            t[..., 2:, :].reshape(*lead, 2 * kw)], axis=-1)


def _gdn_in_layout(w, hk):
    conv_dim = 4 * hk * GDN_DIM
    width = -(-w.shape[-1] // GDN_IN_ALIGN) * GDN_IN_ALIGN
    pad = jnp.zeros(w.shape[:-1] + (width - w.shape[-1],), w.dtype)
    return jnp.concatenate([_gdn_group(w[..., :conv_dim], hk), w[..., conv_dim:], pad], axis=-1)


def _gdn_in_natural(w, hk, in_width):
    conv_dim = 4 * hk * GDN_DIM
    return jnp.concatenate([_gdn_ungroup(w[..., :conv_dim], hk), w[..., conv_dim:in_width]], axis=-1)


def _exchange(srcs, gather, name):
    n_src = len(srcs)

    def body(*refs):
        sems = refs[2 * n_src:]
        copies = [_exchange_copies(refs[t], refs[n_src + t], *sems[3 * t:3 * t + 3], gather=gather) for t in range(n_src)]
        for cps in copies:
            _exchange_start(cps)
        for cps in copies:
            _exchange_wait(cps)

    any_spec = pl.BlockSpec(memory_space=pl.ANY)
    return pl.pallas_call(
        body, name=name, out_shape=[_exchange_out(s, gather) for s in srcs], in_specs=[any_spec] * n_src,
        out_specs=[any_spec] * n_src, scratch_shapes=[s for src in srcs for s in _exchange_sems(src)],
    )(*srcs)


BIG_COL = ("conv_w_pw1", "attn_w_qkv", "gdn_w_in", "ffn_w_gate", "ffn_w_up")
BIG_ROW = ("conv_w_pw2", "attn_w_o", "gdn_w_out", "ffn_w_down")
EXCHANGE_US_PER_BYTE = 11.4e-6
RIDE_PART_US = 150.0


class _Traffic:
    def __init__(self, shards, hk):
        self.shards, self.hk = shards, hk
        self.in_width = N_DEV * shards["gdn_w_in"].shape[-1]
        self.parts_of = {}
        for k, s in shards.items():
            parts = 1
            while (N_DEV * s.shape[1] * s.shape[2] * 2 * EXCHANGE_US_PER_BYTE / parts > RIDE_PART_US
                   and s.shape[1] % (32 * parts) == 0):
                parts *= 2
            self.parts_of[k] = parts
        depth = shards["ffn_w_down"].shape[0]
        order = []
        for i in range(depth):
            j = i // 3
            order += [[("conv_w_pw1", j), ("conv_w_pw2", j)], [("attn_w_qkv", j), ("attn_w_o", j)],
                      [("gdn_w_in", j), ("gdn_w_out", j)]][i % 3]
            order += [("ffn_w_gate", i), ("ffn_w_up", i), ("ffn_w_down", i)]
        self.wanted = [(k, i, part) for k, i in order for part in range(self.parts_of[k])]
        self.arrived = {}
        self.ready = {}
        self.owed = []
        self.received = {}

    def _us(self, k):
        s = self.shards[k]
        return N_DEV * s.shape[1] * s.shape[2] * 2 * EXCHANGE_US_PER_BYTE / self.parts_of[k]

    def _shard_part(self, item):
        k, i, part = item
        rows = self.shards[k].shape[1] // self.parts_of[k]
        return self.shards[k][i:i + 1, part * rows:(part + 1) * rows]

    def _pick(self, queue, us_of, room):
        taken = []
        while queue and room >= 0.5 * us_of(queue[0]):
            room -= us_of(queue[0])
            taken.append(queue.pop(0))
        return taken

    def _run(self, forward, room, fn):
        if forward:
            taken = self._pick(self.wanted, lambda it: self._us(it[0]), room)
            items = [(self._shard_part(it), True) for it in taken]
        else:
            taken = self._pick(self.owed, lambda it: self._us(it[0][0]), room)
            items = [(pieces, False) for _, pieces in taken]
        if not taken:
            return fn(None)
        ride = _Ride(items)
        out = fn(ride)
        for it, got in zip(taken, ride.outs):
            if forward:
                self.arrived[it] = got[0]
            else:
                self.received[it[0]] = got
        return out

    def call(self, forward, room, fn):
        return self._run(forward, room, fn)

    def mm(self, forward, a, b, mode, out_dtype, name, **kw):
        m = a.shape[1] if mode == "tn" else a.shape[0]
        k = a.shape[0] if mode == "tn" else a.shape[1]
        n = b.shape[0] if mode == "nt" else b.shape[1]
        return self._run(forward, _us_mm(m, n, k), lambda ride: _mm(a, b, mode, out_dtype, name, ride=ride, **kw))

    def _natural(self, k, i):
        parts = self.parts_of[k]
        missing = [(k, i, part) for part in range(parts) if (k, i, part) not in self.arrived]
        if missing:
            for it in missing:
                self.wanted.remove(it)
            got = _exchange([self._shard_part(it) for it in missing], True, "gather_weights")
            for it, t in zip(missing, got):
                self.arrived[it] = t[0]
        got = [self.arrived[(k, i, part)] for part in range(parts)]
        rows, c = got[0].shape[1], got[0].shape[2]
        if k in BIG_COL:
            return jnp.concatenate([t.transpose(1, 0, 2).reshape(rows, N_DEV * c) for t in got], axis=0)
        return jnp.stack(got, axis=1).reshape(N_DEV * parts * rows, c)

    def weight(self, k, i):
        if (k, i) not in self.ready:
            if k == "ffn_w_gu":
                w = jnp.concatenate([self._natural("ffn_w_gate", i), self._natural("ffn_w_up", i)], axis=1)
            elif k == "gdn_w_in":
                w = _gdn_in_layout(self._natural(k, i), self.hk)
            else:
                w = self._natural(k, i)
            self.ready[(k, i)] = w
        return self.ready[(k, i)]

    def give(self, k, i, grad):
        if k == "ffn_w_gu":
            f = grad.shape[1] // 2
            self.give("ffn_w_gate", i, grad[:, :f])
            self.give("ffn_w_up", i, grad[:, f:])
            return
        if k == "gdn_w_in":
            grad = _gdn_in_natural(grad, self.hk, self.in_width)
        r, c = self.shards[k].shape[1:]
        pieces = grad.reshape(r, N_DEV, c).transpose(1, 0, 2) if k in BIG_COL else grad.reshape(N_DEV, r, c)
        rows = r // self.parts_of[k]
        for part in range(self.parts_of[k]):
            self.owed.append(((k, i, part), pieces[None, :, part * rows:(part + 1) * rows]))

    def gradient_parts(self, k):
        if self.owed:
            got = _exchange([pieces for _, pieces in self.owed], False, "scatter_grads")
            for (it, _), t in zip(self.owed, got):
                self.received[it] = t
            self.owed = []
        layers = self.shards[k].shape[0]
        return jnp.concatenate([jnp.concatenate([self.received[(k, i, part)] for part in range(self.parts_of[k])], axis=2)
                                for i in range(layers)], axis=0)


def _cast_bf16(w, name):
    n, r, c = w.shape
    tr = _tile(r, max(16, (1 << 20) // c), 16)

    def body(w_ref, o_ref):
        o_ref[...] = w_ref[...].astype(BF16)

    return pl.pallas_call(
        body, name=name, grid=(n, r // tr), in_specs=[pl.BlockSpec((1, tr, c), lambda l, i: (l, i, 0))],
        out_specs=pl.BlockSpec((1, tr, c), lambda l, i: (l, i, 0)), out_shape=jax.ShapeDtypeStruct(w.shape, BF16),
        compiler_params=_cparams(("parallel", "parallel")),
    )(w)


def _adamw(w, g, m, v):
    m = ADAM_B1 * m + (1.0 - ADAM_B1) * g
    v = ADAM_B2 * v + (1.0 - ADAM_B2) * jnp.square(g)
    m_hat = m / (1.0 - ADAM_B1 ** ADAM_STEP)
    v_hat = v / (1.0 - ADAM_B2 ** ADAM_STEP)
    delta = -ADAM_LR * (m_hat / (jnp.sqrt(v_hat) + ADAM_EPS) + ADAM_WD * w)
    return delta, m, v


def _sum8_adam(parts, w, m, v, name):
    n, _, r, c = parts.shape
    tr = _tile(r, max(16, (1 << 18) // c), 16)
    blk = pl.BlockSpec((1, tr, c), lambda l, i: (l, i, 0))

    def body(p_ref, w_ref, m_ref, v_ref, g_ref, d_ref, mo_ref, vo_ref):
        g = p_ref[0, 0].astype(F32)
        for s in range(1, N_DEV):
            g = g + p_ref[0, s].astype(F32)
        delta, m2, v2 = _adamw(w_ref[0], g, m_ref[0], v_ref[0])
        g_ref[0], d_ref[0], mo_ref[0], vo_ref[0] = g, delta, m2, v2

    shp = jax.ShapeDtypeStruct(w.shape, F32)
    return pl.pallas_call(
        body, name=name, grid=(n, r // tr),
        in_specs=[pl.BlockSpec((1, N_DEV, tr, c), lambda l, i: (l, 0, i, 0)), blk, blk, blk],
        out_specs=[blk, blk, blk, blk], out_shape=[shp, shp, shp, shp],
        compiler_params=_cparams(("parallel", "parallel")),
    )(parts, w, m, v)


PACK_ROWS = 8


def _pack(arrays):
    flat = jnp.concatenate([a.reshape(-1).astype(F32) for a in arrays])
    unit = PACK_ROWS * LANES
    total = -(-flat.shape[0] // unit) * unit
    return jnp.concatenate([flat, jnp.zeros((total - flat.shape[0],), F32)]).reshape(-1, LANES)


def _unpack(packed, shapes):
    flat, out, pos = packed.reshape(-1), [], 0
    for s in shapes:
        size = math.prod(s)
        out.append(flat[pos:pos + size].reshape(s))
        pos += size
    return out


SMALL_SHARDED =("meta_tokens", "conv_b_pw1", "conv_w_dw", "conv_b_dw", "conv_ln_g", "conv_ln_b", "conv_b_pw2", "gdn_conv_w")
REPLICATED = ("norm_mix", "norm_ffn", "norm_final", "attn_b_qkv", "attn_sinks", "attn_b_o", "gdn_a_log", "gdn_dt_bias", "gdn_norm_w")
WEIGHTS = ("meta_tokens", "norm_mix", "norm_ffn", "norm_final", "conv_w_pw1", "conv_b_pw1", "conv_w_dw", "conv_b_dw", "conv_ln_g",
           "conv_ln_b", "conv_w_pw2", "conv_b_pw2", "attn_w_qkv", "attn_b_qkv", "attn_sinks", "attn_w_o", "attn_b_o", "gdn_w_in",
           "gdn_conv_w", "gdn_a_log", "gdn_dt_bias", "gdn_norm_w", "gdn_w_out", "ffn_w_gate", "ffn_w_up", "ffn_w_down")


def kernel(x, meta_tokens, norm_mix, norm_ffn, norm_final, conv_w_pw1, conv_b_pw1, conv_w_dw, conv_b_dw, conv_ln_g, conv_ln_b, conv_w_pw2, conv_b_pw2, attn_w_qkv, attn_b_qkv, attn_sinks, attn_w_o, attn_b_o, gdn_w_in, gdn_conv_w, gdn_a_log, gdn_dt_bias, gdn_norm_w, gdn_w_out, ffn_w_gate, ffn_w_up, ffn_w_down, loss_target, m_meta_tokens, m_norm_mix, m_norm_ffn, m_norm_final, m_conv_w_pw1, m_conv_b_pw1, m_conv_w_dw, m_conv_b_dw, m_conv_ln_g, m_conv_ln_b, m_conv_w_pw2, m_conv_b_pw2, m_attn_w_qkv, m_attn_b_qkv, m_attn_sinks, m_attn_w_o, m_attn_b_o, m_gdn_w_in, m_gdn_conv_w, m_gdn_a_log, m_gdn_dt_bias, m_gdn_norm_w, m_gdn_w_out, m_ffn_w_gate, m_ffn_w_up, m_ffn_w_down, v_meta_tokens, v_norm_mix, v_norm_ffn, v_norm_final, v_conv_w_pw1, v_conv_b_pw1, v_conv_w_dw, v_conv_b_dw, v_conv_ln_g, v_conv_ln_b, v_conv_w_pw2, v_conv_b_pw2, v_attn_w_qkv, v_attn_b_qkv, v_attn_sinks, v_attn_w_o, v_attn_b_o, v_gdn_w_in, v_gdn_conv_w, v_gdn_a_log, v_gdn_dt_bias, v_gdn_norm_w, v_gdn_w_out, v_ffn_w_gate, v_ffn_w_up, v_ffn_w_down):
    a = dict(locals())
    me = 4 * lax.axis_index("x") + 2 * lax.axis_index("y") + lax.axis_index("c")
    d = x.shape[-1]

    hk = d // GDN_DIM
    full = {k: a[k] for k in REPLICATED}
    shard_shapes = [a[k].shape for k in SMALL_SHARDED]
    (got,) = _exchange([_pack([a[k] for k in SMALL_SHARDED])[None]], True, "gather_small")
    per_dev = [_unpack(got[0, s], shard_shapes) for s in range(N_DEV)]
    for i, k in enumerate(SMALL_SHARDED):
        st = jnp.stack([per_dev[s][i] for s in range(N_DEV)], axis=-2)
        full[k] = st.reshape(st.shape[:-2] + (N_DEV * st.shape[-1],))
    traffic = _Traffic({k: _cast_bf16(a[k], "cast_bf16") for k in BIG_COL + BIG_ROW}, hk)

    h0 = jnp.concatenate([jnp.zeros((PAD_LEN, d), F32), full["meta_tokens"], x[0]], axis=0)
    target = jnp.concatenate([jnp.zeros((FRONT, d), F32), loss_target[0]], axis=0)
    loss, dh0, g = _local_step(h0, target, {**full, "gdn_conv_w": _gdn_group(full["gdn_conv_w"], hk)}, traffic)
    g["gdn_conv_w"] = _gdn_ungroup(g["gdn_conv_w"], hk)
    g["meta_tokens"] = dh0[PAD_LEN:FRONT]
    loss = lax.psum(loss, AXES)
    grad_x = dh0[FRONT:][None]

    grads, deltas, new_m, new_v = {}, {}, {}, {}
    for k in BIG_COL + BIG_ROW:
        grads[k], deltas[k], new_m[k], new_v[k] = _sum8_adam(traffic.gradient_parts(k), a[k], a["m_" + k], a["v_" + k], "sum8_adamw")

    small = SMALL_SHARDED + REPLICATED
    full_shapes = [full[k].shape for k in small]
    (got,) = _exchange([_pack([g[k] for k in small])[None]], True, "gather_small_grads")
    got = got[0]
    (total,) = _rowwise(lambda r0, *t: (functools.reduce(lambda p, q: p + q, t),), [got[s] for s in range(N_DEV)], [],
                        [(LANES, F32)], [], "sum8_small", tm=got.shape[1])
    for k, t in zip(small, _unpack(total, full_shapes)):
        if k in SMALL_SHARDED:
            c = a[k].shape[-1]
            t = lax.dynamic_index_in_dim(t.reshape(t.shape[:-1] + (N_DEV, c)), me, axis=t.ndim - 1, keepdims=False)
        grads[k] = t
    shapes = [a[k].shape for k in small]
    packed = [_pack([src[k] for k in small]) for src in (grads, a, {k: a["m_" + k] for k in small}, {k: a["v_" + k] for k in small})]

    def small_adam(r0, gv, wv, mv, vv):
        return _adamw(wv, gv, mv, vv)

    outs = _rowwise(small_adam, packed, [], [(LANES, F32)] * 3, [], "adamw_small", tm=packed[0].shape[0])
    for dst, o in zip((deltas, new_m, new_v), outs):
        for k, t in zip(small, _unpack(o, shapes)):
            dst[k] = t

    return (loss, grad_x, *[grads[k] for k in WEIGHTS], *[deltas[k] for k in WEIGHTS], *[new_m[k] for k in WEIGHTS],
            *[new_v[k] for k in WEIGHTS])
```

```python
import functools
import math

import jax
import jax.numpy as jnp
from jax import lax
from jax.experimental import pallas as pl
from jax.experimental.pallas import tpu as pltpu

F32 = jnp.float32
BF16 = jnp.bfloat16

AXES = ("x", "y", "c")
N_DEV = 8

N_META = 16
FRONT = 128
PAD_LEN = FRONT - N_META
NORM_EPS = 1e-6
LN_EPS = 1e-5
NEG_INF = -1e30
CONV_KERNEL = 31
HEAD_DIM = 64
GROUP = 8
BLOCK = 128
ROPE_THETA = 10000.0
GDN_DIM = 128
GDN_CONV = 4
GDN_CHUNK = 64
ADAM_LR, ADAM_B1, ADAM_B2, ADAM_EPS, ADAM_WD, ADAM_STEP = 0.001, 0.9, 0.999, 1e-08, 0.01, 10

VMEM_LIMIT_BYTES = 52 * 1024 * 1024
LANES = 128
CONV_HALO = 32


def _tile(n, pref, align=128):
    best = None
    for t in range(align, min(n, pref) + 1, align):
        if n % t == 0:
            best = t
    return best if best is not None else n


def _cparams(sem):
    return pltpu.CompilerParams(dimension_semantics=sem, vmem_limit_bytes=VMEM_LIMIT_BYTES)


def _exchange_copies(src_ref, out_ref, send_sems, recv_sems, local_sems, gather):
    n = src_ref.shape[0]
    x, y, c = lax.axis_index("x"), lax.axis_index("y"), lax.axis_index("c")
    me = 4 * x + 2 * y + c
    remote, local = [], []
    for l in range(n):
        local.append(pltpu.make_async_copy(src_ref.at[l] if gather else src_ref.at[l, me], out_ref.at[l, me], local_sems.at[l]))
        for k in range(1, N_DEV):
            px = 1 - x if k & 4 else x
            py = 1 - y if k & 2 else y
            pc = 1 - c if k & 1 else c
            peer = 4 * px + 2 * py + pc
            remote.append(pltpu.make_async_remote_copy(
                src_ref=src_ref.at[l] if gather else src_ref.at[l, peer], dst_ref=out_ref.at[l, me],
                send_sem=send_sems.at[l, k - 1], recv_sem=recv_sems.at[l, k - 1],
                device_id=(px, py, pc), device_id_type=pl.DeviceIdType.MESH))
    return remote, local


def _exchange_start(copies):
    remote, local = copies
    for cp in local + remote:
        cp.start()


def _exchange_wait(copies):
    remote, local = copies
    for cp in remote:
        cp.wait_send()
    for cp in remote:
        cp.wait_recv()
    for cp in local:
        cp.wait()


def _exchange_out(src, gather):
    return jax.ShapeDtypeStruct((src.shape[0], N_DEV) + src.shape[-2:], src.dtype)


def _exchange_sems(src):
    n = src.shape[0]
    return [pltpu.SemaphoreType.DMA((n, N_DEV - 1)), pltpu.SemaphoreType.DMA((n, N_DEV - 1)), pltpu.SemaphoreType.DMA((n,))]


class _Ride:
    def __init__(self, items):
        self.items = items
        self.outs = None


def _pcall(body, args, *, name, grid, in_specs, out_specs, out_shape, sem, scratch_shapes=(), ride=None):
    if ride is None:
        return pl.pallas_call(body, name=name, grid=grid, in_specs=in_specs, out_specs=out_specs, out_shape=out_shape,
                              scratch_shapes=list(scratch_shapes), compiler_params=_cparams(sem))(*args)
    n_in, n_out, n_scr, n_ride = len(in_specs), len(out_specs), len(scratch_shapes), len(ride.items)
    any_spec = pl.BlockSpec(memory_space=pl.ANY)

    def with_ride(*refs):
        pos = 0
        ins, pos = refs[pos:pos + n_in], pos + n_in
        srcs, pos = refs[pos:pos + n_ride], pos + n_ride
        outs, pos = refs[pos:pos + n_out], pos + n_out
        dsts, pos = refs[pos:pos + n_ride], pos + n_ride
        scr, pos = refs[pos:pos + n_scr], pos + n_scr
        sems = refs[pos:]
        first = functools.reduce(lambda p, q: p & q, [pl.program_id(d) == 0 for d in range(len(grid))])
        last = functools.reduce(lambda p, q: p & q, [pl.program_id(d) == grid[d] - 1 for d in range(len(grid))])

        def copies():
            return [_exchange_copies(srcs[t], dsts[t], *sems[3 * t:3 * t + 3], gather=ride.items[t][1]) for t in range(n_ride)]

        @pl.when(first)
        def _():
            for cps in copies():
                _exchange_start(cps)

        body(*ins, *outs, *scr)

        @pl.when(last)
        def _():
            for cps in copies():
                _exchange_wait(cps)

    res = pl.pallas_call(
        with_ride, name=name, grid=grid, in_specs=list(in_specs) + [any_spec] * n_ride,
        out_specs=list(out_specs) + [any_spec] * n_ride,
        out_shape=list(out_shape) + [_exchange_out(s, g) for s, g in ride.items],
        scratch_shapes=list(scratch_shapes) + [s for src, _ in ride.items for s in _exchange_sems(src)],
        compiler_params=_cparams(("arbitrary",) * len(grid)),
    )(*args, *[s for s, _ in ride.items])
    ride.outs = list(res[n_out:])
    return list(res[:n_out])


def _mm_shape(a, b, mode):
    if mode == "nn":
        (m, k), (k2, n) = a.shape, b.shape
    elif mode == "nt":
        (m, k), (n, k2) = a.shape, b.shape
    else:
        (k, m), (k2, n) = a.shape, b.shape
    assert k == k2, (a.shape, b.shape, mode)
    return m, n, k


def _mmx(pairs, mode, name, out_dtypes, epilogue, extras=(), n_acc=1, ride=None):
    m, n, k = _mm_shape(pairs[0][0], pairs[0][1], mode)
    tm = _tile(m, 1024 if mode == "tn" else 640)
    tn = _tile(n, 1280)
    tk = _tile(k, 2048)
    nk = k // tk
    dims = {"nn": (((1,), (0,)), ((), ())), "nt": (((1,), (1,)), ((), ())), "tn": (((0,), (0,)), ((), ()))}[mode]
    a_spec = pl.BlockSpec((tk, tm), lambda i, j, kk: (kk, i)) if mode == "tn" else pl.BlockSpec((tm, tk), lambda i, j, kk: (i, kk))
    b_spec = pl.BlockSpec((tn, tk), lambda i, j, kk: (j, kk)) if mode == "nt" else pl.BlockSpec((tk, tn), lambda i, j, kk: (kk, j))
    ins, specs = [], []
    for a, b, _ in pairs:
        assert _mm_shape(a, b, mode) == (m, n, k)
        ins += [a, b]
        specs += [a_spec, b_spec]
    for e in extras:
        ins.append(e)
        specs.append(pl.BlockSpec((1, tn), lambda i, j, kk: (0, j)) if e.shape[0] == 1 and m != 1 else
                     pl.BlockSpec((tm, tn), lambda i, j, kk: (i, j)))
    n_pairs, n_ex, n_out = len(pairs), len(extras), len(out_dtypes)

    def body(*refs):
        ex_refs = refs[2 * n_pairs:2 * n_pairs + n_ex]
        o_refs = refs[2 * n_pairs + n_ex:2 * n_pairs + n_ex + n_out]
        accs = refs[2 * n_pairs + n_ex + n_out:]
        kk = pl.program_id(2)

        @pl.when(kk == 0)
        def _():
            for acc in accs:
                acc[...] = jnp.zeros_like(acc)

        for t, (_, _, which) in enumerate(pairs):
            accs[which][...] += lax.dot_general(refs[2 * t][...].astype(BF16), refs[2 * t + 1][...].astype(BF16), dims,
                                                preferred_element_type=F32)

        @pl.when(kk == nk - 1)
        def _():
            tiles = epilogue([acc[...] for acc in accs], *[e[...] for e in ex_refs])
            for o_ref, t in zip(o_refs, tiles):
                o_ref[...] = t.astype(o_ref.dtype)

    return _pcall(
        body, ins, name=name, grid=(m // tm, n // tn, nk), in_specs=specs,
        out_specs=[pl.BlockSpec((tm, tn), lambda i, j, kk: (i, j))] * n_out,
        out_shape=[jax.ShapeDtypeStruct((m, n), dt) for dt in out_dtypes],
        scratch_shapes=[pltpu.VMEM((tm, tn), F32)] * n_acc, sem=("parallel", "parallel", "arbitrary"), ride=ride)


def _mm(a, b, mode, out_dtype, name, bias=None, residual=None, ride=None):
    extras = ([] if bias is None else [bias.reshape(1, -1).astype(F32)]) + ([] if residual is None else [residual])
    (out,) = _mmx([(a, b, 0)], mode, name, [out_dtype], lambda accs, *ex: [functools.reduce(lambda p, q: p + q.astype(F32), ex, accs[0])],
                  extras=extras, ride=ride)
    return out


def _rowwise(fn, rows, consts, out_rows, out_accs, name, tm=None, ride=None):
    rows = [r if isinstance(r, tuple) else (r, r.shape[1], 0) for r in rows]
    n_rows = rows[0][0].shape[0]
    widest = max([w for _, w, _ in rows] + [w for w, _ in out_rows])
    if tm is None:
        tm = _tile(n_rows, 640 if widest <= 2560 else 128, 8)
    steps = n_rows // tm
    in_specs = [pl.BlockSpec((tm, w), functools.partial(lambda i, c: (i, c), c=cb)) for _, w, cb in rows]
    in_specs += [pl.BlockSpec(c.shape, lambda i: (0, 0)) for c in consts]
    out_specs = [pl.BlockSpec((tm, w), lambda i: (i, 0)) for w, _ in out_rows]
    out_specs += [pl.BlockSpec(s, lambda i: (0, 0)) for s in out_accs]
    out_shape = [jax.ShapeDtypeStruct((n_rows, w), d) for w, d in out_rows]
    out_shape += [jax.ShapeDtypeStruct(s, F32) for s in out_accs]
    n_in, n_or = len(rows) + len(consts), len(out_rows)

    def body(*refs):
        i = pl.program_id(0)
        vals = fn(i * tm, *[r[...] for r in refs[:n_in]])
        outs = refs[n_in:]
        for o_ref, v in zip(outs[:n_or], vals[:n_or]):
            o_ref[...] = v.astype(o_ref.dtype)
        if out_accs:
            @pl.when(i == 0)
            def _():
                for a_ref in outs[n_or:]:
                    a_ref[...] = jnp.zeros_like(a_ref)

            for a_ref, v in zip(outs[n_or:], vals[n_or:]):
                a_ref[...] += v

    return _pcall(body, [r[0] for r in rows] + list(consts), name=name, grid=(steps,), in_specs=in_specs, out_specs=out_specs,
                  out_shape=out_shape, sem=("arbitrary",) if out_accs else ("parallel",), ride=ride)


def _colsum(v):
    return jnp.sum(v, axis=0, keepdims=True)


def _rms(h, w):
    return h * lax.rsqrt(jnp.mean(h * h, axis=-1, keepdims=True) + NORM_EPS) * w


def _rms_fwd(h, w, name):
    (hn,) = _rowwise(lambda r0, hv, wv: (_rms(hv, wv),), [h], [w.reshape(1, -1)], [(h.shape[1], BF16)], [], name)
    return hn


def _rms_bwd(h, w, dhn, dh_in, name):
    d = h.shape[1]

    def fn(r0, hv, dv, rv, wv):
        _, vjp = jax.vjp(_rms, hv, wv)
        dh, dw = vjp(dv.astype(F32))
        return dh + rv, dw

    dh, dw = _rowwise(fn, [h, dhn, dh_in], [w.reshape(1, -1)], [(d, F32)], [(1, d)], name)
    return dh, dw[0]


def _glu(p, b):
    t = p + b
    d = t.shape[1] // 2
    return t[:, :d] * jax.nn.sigmoid(t[:, d:])


def _glu_fwd(p, b, name, ride=None):
    (u,) = _rowwise(lambda r0, v, bv: (_glu(v, bv),), [p], [b.reshape(1, -1)], [(p.shape[1] // 2, F32)], [], name, ride=ride)
    return u


def _glu_bwd(p, b, du, name):
    def fn(r0, v, dv, bv):
        _, vjp = jax.vjp(_glu, v, bv)
        dp, db = vjp(dv)
        return dp, db

    dp, db = _rowwise(fn, [p, du], [b.reshape(1, -1)], [(p.shape[1], BF16)], [(1, p.shape[1])], name)
    return dp, db[0]


def _ln_silu(c, g, b):
    mu = jnp.mean(c, axis=-1, keepdims=True)
    xc = c - mu
    var = jnp.mean(xc * xc, axis=-1, keepdims=True)
    return jax.nn.silu(xc * lax.rsqrt(var + LN_EPS) * g + b)


def _ln_silu_fwd(c, g, b, name):
    (s,) = _rowwise(lambda r0, v, gv, bv: (_ln_silu(v, gv, bv),), [c], [g.reshape(1, -1), b.reshape(1, -1)],
                    [(c.shape[1], BF16)], [], name)
    return s


def _ln_silu_bwd(c, g, b, ds, name):
    d = c.shape[1]

    def fn(r0, v, dv, gv, bv):
        _, vjp = jax.vjp(_ln_silu, v, gv, bv)
        return vjp(dv.astype(F32))

    dc, dg, db = _rowwise(fn, [c, ds], [g.reshape(1, -1), b.reshape(1, -1)], [(d, F32)], [(1, d), (1, d)], name)
    return dc, dg[0], db[0]


def _colsum_rows(v, name):
    (s,) = _rowwise(lambda r0, t: (_colsum(t.astype(F32)),), [v], [], [], [(1, v.shape[1])], name)
    return s[0]


def _rot_half(x):
    w = x.shape[1]
    lane = lax.broadcasted_iota(jnp.int32, x.shape, 1)
    lo = (lane % HEAD_DIM) < (HEAD_DIM // 2)
    return jnp.where(lo, -pltpu.roll(x, w - HEAD_DIM // 2, axis=1), pltpu.roll(x, HEAD_DIM // 2, axis=1))


def _qkv_post_fwd(pre, b, cos, sin, qw, kw, name):
    reps = (qw + kw) // LANES

    def fn(r0, pv, cv, sv, bv):
        t = pv + bv
        tq = t[:, :qw + kw]
        y = tq * jnp.tile(cv, (1, reps)) + _rot_half(tq) * jnp.tile(sv, (1, reps))
        return y[:, :qw], y[:, qw:], t[:, qw + kw:]

    return _rowwise(fn, [pre, cos, sin], [b.reshape(1, -1)], [(qw, BF16), (kw, BF16), (kw, BF16)], [], name)


def _qkv_post_bwd(dq, dk, dv, cos, sin, name):
    qw, kw = dq.shape[1], dk.shape[1]
    reps = (qw + kw) // LANES
    width = qw + 2 * kw

    def fn(r0, dqv, dkv, dvv, cv, sv):
        dy = jnp.concatenate([dqv.astype(F32), dkv.astype(F32)], axis=1)
        dt = dy * jnp.tile(cv, (1, reps)) - _rot_half(dy * jnp.tile(sv, (1, reps)))
        dpre = jnp.concatenate([dt, dvv.astype(F32)], axis=1)
        return dpre, _colsum(dpre)

    dpre, db = _rowwise(fn, [dq, dk, dv, cos, sin], [], [(width, BF16)], [(1, width)], name)
    return dpre, db[0]


def _dw_tiles(n_rows, c):
    return _tile(n_rows, 640, 8), _tile(c, 512)


class _RowWindow:
    SUBLANES = 8

    def __init__(self, value):
        self.copies = {0: value}

    def rows(self, off, n):
        q, s = divmod(off, self.SUBLANES)
        if s not in self.copies:
            base = self.copies[0]
            self.copies[s] = pltpu.roll(base, base.shape[0] - s, axis=0)
        return self.copies[s][self.SUBLANES * q:self.SUBLANES * q + n, :]


def _row_mask(r0, n, width):
    row = r0 + lax.broadcasted_iota(jnp.int32, (n, width), 0)
    return row >= PAD_LEN


def _dwconv_fwd(u, w, bias, c, name, ride=None):
    n_rows, taps = u.shape[0], w.shape[0]
    tr, cb = _dw_tiles(n_rows, c)
    kp = -(-taps // 8) * 8
    wp = jnp.concatenate([w.astype(F32), jnp.zeros((kp - taps, c), F32)], axis=0)
    bp = jnp.zeros((1, c), F32) if bias is None else bias.reshape(1, c).astype(F32)

    def body(cur_ref, prev_ref, w_ref, b_ref, o_ref):
        r = pl.program_id(1)
        cur = jnp.where(_row_mask(r * tr, tr, cb), cur_ref[...], 0.0)
        tail = prev_ref[tr - CONV_HALO:, :]
        tail = jnp.where(_row_mask(r * tr - CONV_HALO, CONV_HALO, cb) & (r > 0), tail, 0.0)
        win = _RowWindow(jnp.concatenate([tail, cur], axis=0))
        acc = jnp.zeros((tr, cb), F32) + b_ref[...]
        for k in range(taps):
            acc = acc + w_ref[k:k + 1, :] * win.rows(CONV_HALO - (taps - 1) + k, tr)
        o_ref[...] = acc

    (out,) = _pcall(
        body, [u, u, wp, bp], name=name, grid=(c // cb, n_rows // tr),
        in_specs=[pl.BlockSpec((tr, cb), lambda j, r: (r, j)),
                  pl.BlockSpec((tr, cb), lambda j, r: (jnp.maximum(r - 1, 0), j)),
                  pl.BlockSpec((kp, cb), lambda j, r: (0, j)),
                  pl.BlockSpec((1, cb), lambda j, r: (0, j))],
        out_specs=[pl.BlockSpec((tr, cb), lambda j, r: (r, j))],
        out_shape=[jax.ShapeDtypeStruct((n_rows, c), F32)], sem=("parallel", "parallel"), ride=ride)
    return out


def _dwconv_bwd(u, w, dc, c, name, ride=None):
    n_rows, taps = u.shape[0], w.shape[0]
    tr, cb = _dw_tiles(n_rows, c)
    nr = n_rows // tr
    kp = -(-taps // 8) * 8
    wp = jnp.concatenate([w.astype(F32), jnp.zeros((kp - taps, c), F32)], axis=0)

    def body(cur_ref, prev_ref, d_ref, dnext_ref, w_ref, du_ref, dw_ref, db_ref):
        r = pl.program_id(1)
        cur = jnp.where(_row_mask(r * tr, tr, cb), cur_ref[...], 0.0)
        tail = prev_ref[tr - CONV_HALO:, :]
        tail = jnp.where(_row_mask(r * tr - CONV_HALO, CONV_HALO, cb) & (r > 0), tail, 0.0)
        win_u = _RowWindow(jnp.concatenate([tail, cur], axis=0))
        d = d_ref[...]
        head = jnp.where(r < nr - 1, dnext_ref[:CONV_HALO, :], 0.0)
        win_d = _RowWindow(jnp.concatenate([d, head], axis=0))

        @pl.when(r == 0)
        def _():
            dw_ref[...] = jnp.zeros_like(dw_ref)
            db_ref[...] = jnp.zeros_like(db_ref)

        du = jnp.zeros((tr, cb), F32)
        for k in range(taps):
            du = du + w_ref[k:k + 1, :] * win_d.rows(taps - 1 - k, tr)
            dw_ref[k:k + 1, :] += _colsum(d * win_u.rows(CONV_HALO - (taps - 1) + k, tr))
        du_ref[...] = jnp.where(_row_mask(r * tr, tr, cb), du, 0.0)
        db_ref[...] += _colsum(d)

    du, dw, db = _pcall(
        body, [u, u, dc, dc, wp], name=name, grid=(c // cb, nr),
        in_specs=[pl.BlockSpec((tr, cb), lambda j, r: (r, j)),
                  pl.BlockSpec((tr, cb), lambda j, r: (jnp.maximum(r - 1, 0), j)),
                  pl.BlockSpec((tr, cb), lambda j, r: (r, j)),
                  pl.BlockSpec((tr, cb), lambda j, r: (jnp.minimum(r + 1, nr - 1), j)),
                  pl.BlockSpec((kp, cb), lambda j, r: (0, j))],
        out_specs=[pl.BlockSpec((tr, cb), lambda j, r: (r, j)),
                   pl.BlockSpec((kp, cb), lambda j, r: (0, j)),
                   pl.BlockSpec((1, cb), lambda j, r: (0, j))],
        out_shape=[jax.ShapeDtypeStruct((n_rows, c), F32), jax.ShapeDtypeStruct((kp, c), F32),
                   jax.ShapeDtypeStruct((1, c), F32)], sem=("parallel", "arbitrary"), ride=ride)
    return du, dw[:taps], db[0]


def _attn_block(q, kprev, kcur, vprev, vcur, sink, n):
    qf = q.reshape(GROUP * BLOCK, HEAD_DIM).astype(BF16)
    kb = jnp.concatenate([kprev, kcur], axis=0).astype(BF16)
    vb = jnp.concatenate([vprev, vcur], axis=0).astype(BF16)
    s = lax.dot_general(qf, kb, (((1,), (1,)), ((), ())), preferred_element_type=F32) * (HEAD_DIM ** -0.5)
    s = s.reshape(GROUP, BLOCK, 2 * BLOCK)
    qi = lax.broadcasted_iota(jnp.int32, (BLOCK, 2 * BLOCK), 0)
    kj = lax.broadcasted_iota(jnp.int32, (BLOCK, 2 * BLOCK), 1)
    dist = qi + BLOCK - kj
    allowed = (dist >= 0) & (dist < BLOCK) & ((n - 1) * BLOCK + kj >= PAD_LEN)
    s = jnp.where(allowed[None], s, NEG_INF)
    m = lax.stop_gradient(jnp.maximum(jnp.max(s, axis=-1, keepdims=True), sink))
    e = jnp.exp(s - m)
    p = e / (jnp.sum(e, axis=-1, keepdims=True) + jnp.exp(sink - m))
    o = jnp.dot(p.reshape(GROUP * BLOCK, 2 * BLOCK).astype(BF16), vb, preferred_element_type=F32)
    return o.reshape(GROUP, BLOCK, HEAD_DIM)


def _attn_specs(nb):
    q_spec = pl.BlockSpec((GROUP, BLOCK, HEAD_DIM), lambda g, n: (g, n, 0))
    cur = pl.BlockSpec((1, BLOCK, HEAD_DIM), lambda g, n: (g, n, 0))
    prev = pl.BlockSpec((1, BLOCK, HEAD_DIM), lambda g, n: (g, jnp.maximum(n - 1, 0), 0))
    sink = pl.BlockSpec((1, GROUP, 1, 1), lambda g, n: (g, 0, 0, 0))
    return q_spec, cur, prev, sink


def _attn_fwd(q, k, v, sinks, name, ride=None):
    heads, n_rows, _ = q.shape
    nb = n_rows // BLOCK
    q_spec, cur, prev, sink = _attn_specs(nb)

    def body(q_ref, kp_ref, kc_ref, vp_ref, vc_ref, s_ref, o_ref):
        n = pl.program_id(1)
        o = _attn_block(q_ref[...].astype(F32), kp_ref[0].astype(F32), kc_ref[0].astype(F32), vp_ref[0].astype(F32),
                        vc_ref[0].astype(F32), s_ref[0], n)
        o_ref[...] = o.astype(o_ref.dtype)

    (out,) = _pcall(
        body, [q, k, k, v, v, sinks.reshape(heads // GROUP, GROUP, 1, 1)], name=name, grid=(heads // GROUP, nb),
        in_specs=[q_spec, prev, cur, prev, cur, sink], out_specs=[q_spec], out_shape=[jax.ShapeDtypeStruct(q.shape, BF16)],
        sem=("parallel", "parallel"), ride=ride)
    return out


def _attn_bwd(q, k, v, sinks, do, name):
    heads, n_rows, _ = q.shape
    kvh = heads // GROUP
    nb = n_rows // BLOCK
    q_spec, cur, prev, sink = _attn_specs(nb)
    part = pl.BlockSpec((1, 1, BLOCK, HEAD_DIM), lambda g, n: (g, n, 0, 0))
    part_shape = jax.ShapeDtypeStruct((kvh, nb, BLOCK, HEAD_DIM), F32)

    def body(q_ref, kp_ref, kc_ref, vp_ref, vc_ref, s_ref, do_ref, dq_ref, dkp_ref, dkc_ref, dvp_ref, dvc_ref, ds_ref):
        n = pl.program_id(1)
        f = functools.partial(_attn_block, n=n)
        _, vjp = jax.vjp(f, q_ref[...].astype(F32), kp_ref[0].astype(F32), kc_ref[0].astype(F32), vp_ref[0].astype(F32),
                         vc_ref[0].astype(F32), s_ref[0])
        dq, dkp, dkc, dvp, dvc, ds = vjp(do_ref[...].astype(F32))
        dq_ref[...] = dq.astype(dq_ref.dtype)
        dkp_ref[0, 0], dkc_ref[0, 0], dvp_ref[0, 0], dvc_ref[0, 0] = dkp, dkc, dvp, dvc

        @pl.when(n == 0)
        def _():
            ds_ref[...] = jnp.zeros_like(ds_ref)

        ds_ref[0] += ds

    return pl.pallas_call(
        body, name=name, grid=(kvh, nb), in_specs=[q_spec, prev, cur, prev, cur, sink, q_spec],
        out_specs=[q_spec, part, part, part, part, sink],
        out_shape=[jax.ShapeDtypeStruct(q.shape, BF16), part_shape, part_shape, part_shape, part_shape,
                   jax.ShapeDtypeStruct((kvh, GROUP, 1, 1), F32)],
        compiler_params=_cparams(("parallel", "arbitrary")),
    )(q, k, k, v, v, sinks.reshape(kvh, GROUP, 1, 1), do)


def _shift_add(own, to_prev, name):
    kvh, nb = own.shape[:2]
    blk = (1, 1, BLOCK, HEAD_DIM)

    def body(a_ref, b_ref, o_ref):
        n = pl.program_id(1)
        o_ref[...] = (a_ref[...] + jnp.where(n < nb - 1, b_ref[...], 0.0)).astype(o_ref.dtype)

    return pl.pallas_call(
        body, name=name, grid=(kvh, nb),
        in_specs=[pl.BlockSpec(blk, lambda g, n: (g, n, 0, 0)),
                  pl.BlockSpec(blk, lambda g, n: (g, jnp.minimum(n + 1, nb - 1), 0, 0))],
        out_specs=pl.BlockSpec(blk, lambda g, n: (g, n, 0, 0)),
        out_shape=jax.ShapeDtypeStruct(own.shape, BF16), compiler_params=_cparams(("parallel", "parallel")),
    )(own, to_prev)


def _gdn_gates(ba, alog, dt, r0, hv):
    lane = lax.broadcasted_iota(jnp.int32, ba.shape, 1)
    t = ba + dt
    softplus = jnp.maximum(t, 0.0) + jnp.log(1.0 + jnp.exp(-jnp.abs(t)))
    val = jnp.where(lane < hv, jax.nn.sigmoid(ba), jnp.where(lane < 2 * hv, -jnp.exp(alog) * softplus, 0.0))
    return jnp.where(_row_mask(r0, ba.shape[0], ba.shape[1]), val, 0.0)


def _l2n(x):
    return x * lax.rsqrt(jnp.sum(x * x, axis=-1, keepdims=True) + 1e-6)


_NN = (((2,), (1,)), ((0,), (0,)))
_NT = (((2,), (2,)), ((0,), (0,)))
_TN = (((1,), (1,)), ((0,), (0,)))


def _bdot(a, b, dims=_NN):
    return lax.dot_general(a.astype(BF16), b.astype(BF16), dims, preferred_element_type=F32)


def _dot3(a, b, dims):
    ah, bh = a.astype(BF16), b.astype(BF16)
    al, bl = (a - ah.astype(F32)).astype(BF16), (b - bh.astype(F32)).astype(BF16)

    def d(p, q):
        return lax.dot_general(p, q, dims, preferred_element_type=F32)

    return d(ah, bh) + (d(ah, bl) + d(al, bh))


@jax.custom_vjp
def _pdot(a, b):
    return _dot3(a, b, _NN)


def _pdot_fwd(a, b):
    return _dot3(a, b, _NN), (a, b)


def _pdot_bwd(res, ct):
    a, b = res
    return _bdot(ct, b, _NT), _bdot(a, ct, _TN)


_pdot.defvjp(_pdot_fwd, _pdot_bwd)


def _scan_chunks(x, reverse):
    n, c = x.shape[0], GDN_CHUNK
    row = lax.broadcasted_iota(jnp.int32, x.shape, 0) % c
    s = 1
    while s < c:
        if reverse:
            x = x + jnp.where(row < c - s, pltpu.roll(x, n - s, axis=0), 0.0)
        else:
            x = x + jnp.where(row >= s, pltpu.roll(x, s, axis=0), 0.0)
        s *= 2
    return x


@jax.custom_vjp
def _cumsum_chunks(x):
    return _scan_chunks(x, False)


_cumsum_chunks.defvjp(lambda x: (_scan_chunks(x, False), None), lambda _, ct: (_scan_chunks(ct, True),))


def _gdn_heads(states, qkv, z, gates, norm_w, hk0, hv_total):
    c, h = GDN_CHUNK, states.shape[0]
    g = h // 2

    def cols(src, starts):
        return jnp.stack([src[:, s:s + GDN_DIM] for s in starts])

    q = _l2n(jax.nn.silu(cols(qkv, [4 * GDN_DIM * t for t in range(g)]))) * (GDN_DIM ** -0.5)
    k = _l2n(jax.nn.silu(cols(qkv, [4 * GDN_DIM * t + GDN_DIM for t in range(g)])))
    q, k = jnp.repeat(q, 2, axis=0), jnp.repeat(k, 2, axis=0)
    v = jax.nn.silu(cols(qkv, [4 * GDN_DIM * (t // 2) + (2 + t % 2) * GDN_DIM for t in range(h)]))
    zz = cols(z, [GDN_DIM * t for t in range(h)])
    lane = lax.broadcasted_iota(jnp.int32, gates.shape, 1)

    def col_of(first):
        return jnp.stack([jnp.sum(jnp.where(lane == first + t, gates, 0.0), axis=1, keepdims=True) for t in range(h)])

    beta_col, g_col = col_of(2 * hk0), col_of(hv_total + 2 * hk0)
    i = lax.broadcasted_iota(jnp.int32, (c, c), 0)
    j = lax.broadcasted_iota(jnp.int32, (c, c), 1)
    causal, strict = (i >= j)[None], (i > j)[None]
    gc = _cumsum_chunks(jnp.broadcast_to(g_col, (h, c, GDN_DIM)).reshape(h * c, GDN_DIM)).reshape(h, c, GDN_DIM)
    gc_i = gc[:, :, :c]
    gc_j = jnp.swapaxes(gc_i, 1, 2)
    gc_last = jnp.broadcast_to(gc[:, c - 1:c, :], (h, GDN_DIM, GDN_DIM))
    decay = jnp.where(causal, jnp.exp(jnp.where(causal, gc_i - gc_j, 0.0)), 0.0)
    k_beta = k * beta_col
    lower = jnp.where(strict, _bdot(k_beta, k, _NT) * decay, 0.0)
    eye = (i == j).astype(F32)[None]
    neg = -lower
    inv = eye + neg
    power = neg
    for _ in range(5):
        power = _pdot(power, power)
        inv = _pdot(inv, eye + power)
    sol = _pdot(inv, jnp.concatenate([v * beta_col, k_beta * jnp.exp(gc)], axis=2))
    u, w = sol[:, :, :GDN_DIM], sol[:, :, GDN_DIM:]
    intra = jnp.where(causal, _bdot(q, k, _NT) * decay, 0.0)
    q_dec = q * jnp.exp(gc)
    k_dec = k * jnp.exp(gc_last[:, :c] - gc)
    v_new = u - _bdot(w, states)
    o = _bdot(q_dec, states) + _bdot(intra, v_new)
    new_states = states * jnp.exp(gc_last) + _bdot(k_dec, v_new, _TN)
    y = _rms(o, norm_w) * jax.nn.silu(zz)
    return jnp.concatenate([y[t] for t in range(h)], axis=1), new_states


GDN_KEY_HEADS_PER_STEP = 4


def _key_heads_per_step(hk_total):
    return math.gcd(hk_total, GDN_KEY_HEADS_PER_STEP)


def _gdn_fwd(cq, proj, gates, norm_w, hk_total, name):
    n_rows = cq.shape[0]
    nc, hv_total = n_rows // GDN_CHUNK, 2 * hk_total
    grp = _key_heads_per_step(hk_total)
    heads = 2 * grp
    zblk0 = cq.shape[1] // (heads * GDN_DIM)

    def body(cq_ref, z_ref, g_ref, w_ref, y_ref, save_ref, state):
        n, hg = pl.program_id(0), pl.program_id(1)

        @pl.when(n == 0)
        def _():
            state[pl.ds(heads * hg, heads)] = jnp.zeros((heads, GDN_DIM, GDN_DIM), F32)

        s_in = state[pl.ds(heads * hg, heads)]
        save_ref[0] = s_in
        y, s_out = _gdn_heads(s_in, cq_ref[...], z_ref[...], g_ref[...], w_ref[...], grp * hg, hv_total)
        y_ref[...] = y.astype(y_ref.dtype)
        state[pl.ds(heads * hg, heads)] = s_out

    return pl.pallas_call(
        body, name=name, grid=(nc, hk_total // grp),
        in_specs=[pl.BlockSpec((GDN_CHUNK, 2 * heads * GDN_DIM), lambda n, h: (n, h)),
                  pl.BlockSpec((GDN_CHUNK, heads * GDN_DIM), lambda n, h: (n, zblk0 + h)),
                  pl.BlockSpec((GDN_CHUNK, LANES), lambda n, h: (n, 0)),
                  pl.BlockSpec((1, GDN_DIM), lambda n, h: (0, 0))],
        out_specs=[pl.BlockSpec((GDN_CHUNK, heads * GDN_DIM), lambda n, h: (n, h)),
                   pl.BlockSpec((1, heads, GDN_DIM, GDN_DIM), lambda n, h: (n, h, 0, 0))],
        out_shape=[jax.ShapeDtypeStruct((n_rows, hv_total * GDN_DIM), BF16),
                   jax.ShapeDtypeStruct((nc, hv_total, GDN_DIM, GDN_DIM), F32)],
        scratch_shapes=[pltpu.VMEM((hv_total, GDN_DIM, GDN_DIM), F32)],
        compiler_params=_cparams(("arbitrary", "arbitrary")),
    )(cq, proj, gates, norm_w.reshape(1, GDN_DIM))


def _gdn_bwd(cq, proj, gates, norm_w, saved, dy, hk_total, name):
    n_rows = cq.shape[0]
    nc, hv_total = n_rows // GDN_CHUNK, 2 * hk_total
    grp = _key_heads_per_step(hk_total)
    heads = 2 * grp
    zblk0 = cq.shape[1] // (heads * GDN_DIM)

    def body(cq_ref, z_ref, g_ref, w_ref, save_ref, dy_ref, dcq_ref, dz_ref, dg_ref, dw_ref, dstate):
        n, hg = pl.program_id(0), pl.program_id(1)

        @pl.when(n == 0)
        def _():
            dstate[pl.ds(heads * hg, heads)] = jnp.zeros((heads, GDN_DIM, GDN_DIM), F32)

        @pl.when((n == 0) & (hg == 0))
        def _():
            dw_ref[...] = jnp.zeros_like(dw_ref)

        @pl.when(hg == 0)
        def _():
            dg_ref[...] = jnp.zeros_like(dg_ref)

        f = functools.partial(_gdn_heads, hk0=grp * hg, hv_total=hv_total)
        _, vjp = jax.vjp(f, save_ref[0], cq_ref[...], z_ref[...], g_ref[...], w_ref[...])
        ds, dcq, dz, dg, dw = vjp((dy_ref[...].astype(F32), dstate[pl.ds(heads * hg, heads)]))
        dstate[pl.ds(heads * hg, heads)] = ds
        dcq_ref[...] = dcq
        dz_ref[...] = dz.astype(dz_ref.dtype)
        dg_ref[...] += dg
        dw_ref[...] += dw

    rev = lambda n: nc - 1 - n
    return pl.pallas_call(
        body, name=name, grid=(nc, hk_total // grp),
        in_specs=[pl.BlockSpec((GDN_CHUNK, 2 * heads * GDN_DIM), lambda n, h: (rev(n), h)),
                  pl.BlockSpec((GDN_CHUNK, heads * GDN_DIM), lambda n, h: (rev(n), zblk0 + h)),
                  pl.BlockSpec((GDN_CHUNK, LANES), lambda n, h: (rev(n), 0)),
                  pl.BlockSpec((1, GDN_DIM), lambda n, h: (0, 0)),
                  pl.BlockSpec((1, heads, GDN_DIM, GDN_DIM), lambda n, h: (rev(n), h, 0, 0)),
                  pl.BlockSpec((GDN_CHUNK, heads * GDN_DIM), lambda n, h: (rev(n), h))],
        out_specs=[pl.BlockSpec((GDN_CHUNK, 2 * heads * GDN_DIM), lambda n, h: (rev(n), h)),
                   pl.BlockSpec((GDN_CHUNK, heads * GDN_DIM), lambda n, h: (rev(n), h)),
                   pl.BlockSpec((GDN_CHUNK, LANES), lambda n, h: (rev(n), 0)),
                   pl.BlockSpec((1, GDN_DIM), lambda n, h: (0, 0))],
        out_shape=[jax.ShapeDtypeStruct(cq.shape, F32), jax.ShapeDtypeStruct((n_rows, hv_total * GDN_DIM), BF16),
                   jax.ShapeDtypeStruct((n_rows, LANES), F32), jax.ShapeDtypeStruct((1, GDN_DIM), F32)],
        scratch_shapes=[pltpu.VMEM((hv_total, GDN_DIM, GDN_DIM), F32)],
        compiler_params=_cparams(("arbitrary", "arbitrary")),
    )(cq, proj, gates, norm_w.reshape(1, GDN_DIM), saved, dy)


def _final_loss(h, w, target, name):
    d = h.shape[1]

    def fn(r0, hv, tv, wv):
        def loss_of(hh, ww):
            err = jnp.where(_row_mask(r0, hh.shape[0], d) & (r0 + lax.broadcasted_iota(jnp.int32, hh.shape, 0) >= FRONT),
                            _rms(hh, ww) - tv, 0.0)
            return 0.5 * jnp.sum(jnp.sum(err * err, axis=1, keepdims=True) / d)

        loss, vjp = jax.vjp(loss_of, hv, wv)
        dh, dw = vjp(jnp.ones((), F32))
        return dh, jnp.zeros((1, LANES), F32) + loss, dw

    dh, loss, dw = _rowwise(fn, [h, target], [w.reshape(1, -1)], [(d, F32)], [(1, LANES), (1, d)], name)
    return loss[0, 0], dh, dw[0]


def _rope_tables(n_rows):
    pos = (jnp.arange(n_rows) - PAD_LEN).astype(F32)
    inv_freq = ROPE_THETA ** (-jnp.arange(0, HEAD_DIM, 2, dtype=F32) / HEAD_DIM)
    ang = pos[:, None] * inv_freq[None, :]
    reps = LANES // (HEAD_DIM // 2)
    return jnp.tile(jnp.cos(ang), (1, reps)), jnp.tile(jnp.sin(ang), (1, reps))


def _to_heads(t):
    n_rows, w = t.shape
    return t.reshape(n_rows, w // HEAD_DIM, HEAD_DIM).transpose(1, 0, 2)


def _from_heads(t):
    heads, n_rows, _ = t.shape
    return t.transpose(1, 0, 2).reshape(n_rows, heads * HEAD_DIM)


def _local_step(h0, target, p, tr):
    n_rows, d = h0.shape
    depth = p["norm_mix"].shape[0]
    cos, sin = _rope_tables(n_rows)
    hk_total = d // GDN_DIM
    hv_total = 2 * hk_total
    conv_dim = 4 * hk_total * GDN_DIM
    qw, kw = d, d // GROUP
    saved = []
    h = h0
    mm = functools.partial(tr.mm, True)
    for i in range(depth):
        kind, j = i % 3, i // 3
        s = {"h": h}
        hn = _rms_fwd(h, p["norm_mix"][i], "rms_fwd")
        s["hn"] = hn
        if kind == 0:
            pre = mm(hn, tr.weight("conv_w_pw1", j), "nn", F32, "mm_pw1")
            u1 = tr.call(True, _us_rows(pre), lambda r: _glu_fwd(pre, p["conv_b_pw1"][j], "glu_fwd", ride=r))
            c = tr.call(True, _us_dwconv(n_rows, d, CONV_KERNEL),
                        lambda r: _dwconv_fwd(u1, p["conv_w_dw"][j], p["conv_b_dw"][j], d, "dwconv31_fwd", ride=r))
            sv = _ln_silu_fwd(c, p["conv_ln_g"][j], p["conv_ln_b"][j], "ln_silu_fwd")
            h = mm(sv, tr.weight("conv_w_pw2", j), "nn", F32, "mm_d_d_res", bias=p["conv_b_pw2"][j], residual=h)
            s.update(pre=pre, u1=u1, c=c, sv=sv)
        elif kind == 1:
            pre = mm(hn, tr.weight("attn_w_qkv", j), "nn", F32, "mm_qkv")
            q, k, v = _qkv_post_fwd(pre, p["attn_b_qkv"][j], cos, sin, qw, kw, "qkv_post_fwd")
            qh, kh, vh = _to_heads(q), _to_heads(k), _to_heads(v)
            o = _from_heads(tr.call(True, _us_attn(qh), lambda r: _attn_fwd(qh, kh, vh, p["attn_sinks"][j], "attn_fwd", ride=r)))
            h = mm(o, tr.weight("attn_w_o", j), "nn", F32, "mm_d_d_res", bias=p["attn_b_o"][j], residual=h)
            s.update(qh=qh, kh=kh, vh=vh, o=o)
        else:
            proj = mm(hn, tr.weight("gdn_w_in", j), "nn", F32, "mm_gdn_in")
            cq = tr.call(True, _us_dwconv(n_rows, conv_dim, GDN_CONV),
                         lambda r: _dwconv_fwd(proj, p["gdn_conv_w"][j], None, conv_dim, "dwconv4_fwd", ride=r))
            vec = _gate_vectors(p["gdn_a_log"][j], p["gdn_dt_bias"][j], hv_total)
            ba_blk = (conv_dim + hv_total * GDN_DIM) // LANES
            (gates,) = _rowwise(lambda r0, bav, av, dv: (_gdn_gates(bav, av, dv, r0, hv_total),), [(proj, LANES, ba_blk)],
                                [vec[0], vec[1]], [(LANES, F32)], [], "gdn_gates_fwd")
            y, states = _gdn_fwd(cq, proj, gates, p["gdn_norm_w"][j], hk_total, "gdn_fwd")
            h = mm(y, tr.weight("gdn_w_out", j), "nn", F32, "mm_gdn_out_res", residual=h)
            s.update(proj=proj, cq=cq, gates=gates, y=y, states=states)
        s["h1"] = h
        hn2 = _rms_fwd(h, p["norm_ffn"][i], "rms_fwd")
        w_gate, w_up = tr.weight("ffn_w_gate", i), tr.weight("ffn_w_up", i)
        gate, up, a = tr.call(True, 2 * _us_mm(n_rows, w_gate.shape[1], d), lambda r: _mmx(
            [(hn2, w_gate, 0), (hn2, w_up, 1)], "nn", "mm_ffn_swiglu", [BF16, BF16, BF16],
            lambda accs: [accs[0], accs[1], jax.nn.silu(accs[0]) * accs[1]], n_acc=2, ride=r))
        h = mm(a, tr.weight("ffn_w_down", i), "nn", F32, "mm_ffn_down_res", residual=h)
        s.update(hn2=hn2, gate=gate, up=up, a=a)
        saved.append(s)

    loss, dh, g_final = _final_loss(h, p["norm_final"], target, "final_loss")
    g = {k: [None] * v.shape[0] for k, v in p.items() if k not in ("norm_final", "meta_tokens")}
    g["norm_final"] = g_final
    mm = functools.partial(tr.mm, False)
    for i in reversed(range(depth)):
        kind, j = i % 3, i // 3
        s = saved[i]
        tr.give("ffn_w_down", i, mm(s["a"], dh, "tn", BF16, "mm_dw_down"))
        w_gate, w_up, w_down = tr.weight("ffn_w_gate", i), tr.weight("ffn_w_up", i), tr.weight("ffn_w_down", i)

        def swiglu_bwd(accs, gate, up):
            _, vjp = jax.vjp(lambda gv, uv: jax.nn.silu(gv) * uv, gate.astype(F32), up.astype(F32))
            return list(vjp(accs[0]))

        dgate, dup = tr.call(False, _us_mm(n_rows, w_gate.shape[1], d), lambda r: _mmx(
            [(dh, w_down, 0)], "nt", "mm_da_swiglu", [BF16, BF16], swiglu_bwd, extras=[s["gate"], s["up"]], ride=r))
        tr.give("ffn_w_gate", i, mm(s["hn2"], dgate, "tn", BF16, "mm_dw_gate"))
        tr.give("ffn_w_up", i, mm(s["hn2"], dup, "tn", BF16, "mm_dw_gate"))
        (dhn2,) = tr.call(False, 2 * _us_mm(n_rows, d, w_gate.shape[1]), lambda r: _mmx(
            [(dgate, w_gate, 0), (dup, w_up, 0)], "nt", "mm_dhn2", [F32], lambda accs: accs, ride=r))
        dh, g["norm_ffn"][i] = _rms_bwd(s["h1"], p["norm_ffn"][i], dhn2, dh, "rms_bwd")
        if kind == 0:
            g["conv_b_pw2"][j] = _colsum_rows(dh, "colsum_d")
            tr.give("conv_w_pw2", j, mm(s["sv"], dh, "tn", BF16, "mm_dw_d_d"))
            dsv = mm(dh, tr.weight("conv_w_pw2", j), "nt", F32, "mm_dx_d_d")
            dc, g["conv_ln_g"][j], g["conv_ln_b"][j] = _ln_silu_bwd(s["c"], p["conv_ln_g"][j], p["conv_ln_b"][j], dsv, "ln_silu_bwd")
            du1, g["conv_w_dw"][j], g["conv_b_dw"][j] = tr.call(
                False, 2 * _us_dwconv(n_rows, d, CONV_KERNEL),
                lambda r: _dwconv_bwd(s["u1"], p["conv_w_dw"][j], dc, d, "dwconv31_bwd", ride=r))
            dpre, g["conv_b_pw1"][j] = _glu_bwd(s["pre"], p["conv_b_pw1"][j], du1, "glu_bwd")
            tr.give("conv_w_pw1", j, mm(s["hn"], dpre, "tn", BF16, "mm_dw_pw1"))
            dhn = mm(dpre, tr.weight("conv_w_pw1", j), "nt", F32, "mm_dx_pw1")
        elif kind == 1:
            g["attn_b_o"][j] = _colsum_rows(dh, "colsum_d")
            tr.give("attn_w_o", j, mm(s["o"], dh, "tn", BF16, "mm_dw_d_d"))
            do = _to_heads(mm(dh, tr.weight("attn_w_o", j), "nt", BF16, "mm_dx_d_d_bf16"))
            dq, dkp, dkc, dvp, dvc, dsink = _attn_bwd(s["qh"], s["kh"], s["vh"], p["attn_sinks"][j], do, "attn_bwd")
            g["attn_sinks"][j] = dsink.reshape(-1)
            kvh = kw // HEAD_DIM
            dk = _shift_add(dkc, dkp, "attn_shift_add").reshape(kvh, n_rows, HEAD_DIM)
            dv = _shift_add(dvc, dvp, "attn_shift_add").reshape(kvh, n_rows, HEAD_DIM)
            dpre, g["attn_b_qkv"][j] = _qkv_post_bwd(_from_heads(dq), _from_heads(dk), _from_heads(dv), cos, sin, "qkv_post_bwd")
            tr.give("attn_w_qkv", j, mm(s["hn"], dpre, "tn", BF16, "mm_dw_qkv"))
            dhn = mm(dpre, tr.weight("attn_w_qkv", j), "nt", F32, "mm_dx_qkv")
        else:
            tr.give("gdn_w_out", j, mm(s["y"], dh, "tn", BF16, "mm_dw_gdn_out"))
            dy = mm(dh, tr.weight("gdn_w_out", j), "nt", BF16, "mm_dx_gdn_out")
            dcq, dz, dgates, g_nw = _gdn_bwd(s["cq"], s["proj"], s["gates"], p["gdn_norm_w"][j], s["states"], dy, hk_total, "gdn_bwd")
            g["gdn_norm_w"][j] = g_nw[0]
            vec = _gate_vectors(p["gdn_a_log"][j], p["gdn_dt_bias"][j], hv_total)
            ba_blk = (conv_dim + hv_total * GDN_DIM) // LANES

            def gates_bwd(r0, bav, dgv, av, dv):
                _, vjp = jax.vjp(functools.partial(_gdn_gates, r0=r0, hv=hv_total), bav, av, dv)
                return vjp(dgv)

            dba, d_alog, d_dt = _rowwise(gates_bwd, [(s["proj"], LANES, ba_blk), dgates], [vec[0], vec[1]], [(LANES, BF16)],
                                         [(1, LANES), (1, LANES)], "gdn_gates_bwd")
            g["gdn_a_log"][j] = d_alog[0, hv_total:2 * hv_total]
            g["gdn_dt_bias"][j] = d_dt[0, hv_total:2 * hv_total]
            dconv_in, g["gdn_conv_w"][j], _ = tr.call(
                False, 2 * _us_dwconv(n_rows, conv_dim, GDN_CONV),
                lambda r: _dwconv_bwd(s["proj"], p["gdn_conv_w"][j], dcq, conv_dim, "dwconv4_bwd", ride=r))
            w_in = tr.weight("gdn_w_in", j)
            pad = jnp.zeros((n_rows, w_in.shape[1] - conv_dim - hv_total * GDN_DIM - LANES), BF16)
            dproj = jnp.concatenate([dconv_in.astype(BF16), dz, dba, pad], axis=1)
            tr.give("gdn_w_in", j, mm(s["hn"], dproj, "tn", BF16, "mm_dw_gdn_in"))
            dhn = mm(dproj, w_in, "nt", F32, "mm_dx_gdn_in")
        dh, g["norm_mix"][i] = _rms_bwd(s["h"], p["norm_mix"][i], dhn, dh, "rms_bwd")
    g = {k: (jnp.stack(v) if isinstance(v, list) else v) for k, v in g.items()}
    return loss, dh, g


def _us_mm(m, n, k):
    return 2.0 * m * n * k / 8.0e8


def _us_rows(t):
    return t.shape[0] * t.shape[1] / 8.0e5


def _us_dwconv(n_rows, c, taps):
    return n_rows * c * (taps + 8) / 2.0e6


def _us_attn(qh):
    return qh.shape[0] * qh.shape[1] / 500.0


def _gate_vectors(a_log, dt_bias, hv):
    def place(t):
        return jnp.concatenate([jnp.zeros((hv,), F32), t, jnp.zeros((LANES - 2 * hv,), F32)]).reshape(1, LANES)

    return place(a_log), place(dt_bias)


GDN_IN_ALIGN = 512


def _gdn_group(w, hk):
    lead, kw = w.shape[:-1], hk * GDN_DIM
    q = w[..., :kw].reshape(*lead, hk, 1, GDN_DIM)
    k = w[..., kw:2 * kw].reshape(*lead, hk, 1, GDN_DIM)
    v = w[..., 2 * kw:4 * kw].reshape(*lead, hk, 2, GDN_DIM)
    return jnp.concatenate([q, k, v], axis=-2).reshape(*lead, 4 * kw)


def _gdn_ungroup(w, hk):
    lead, kw = w.shape[:-1], hk * GDN_DIM
    t = w.reshape(*lead, hk, 4, GDN_DIM)
    return jnp.concatenate([t[..., 0, :].reshape(*lead, kw), t[..., 1, :].reshape(*lead, kw),
                            t[..., 2:, :].reshape(*lead, 2 * kw)], axis=-1)


def _gdn_in_layout(w, hk):
    conv_dim = 4 * hk * GDN_DIM
    width = -(-w.shape[-1] // GDN_IN_ALIGN) * GDN_IN_ALIGN
    pad = jnp.zeros(w.shape[:-1] + (width - w.shape[-1],), w.dtype)
    return jnp.concatenate([_gdn_group(w[..., :conv_dim], hk), w[..., conv_dim:], pad], axis=-1)


def _gdn_in_natural(w, hk, in_width):
    conv_dim = 4 * hk * GDN_DIM
    return jnp.concatenate([_gdn_ungroup(w[..., :conv_dim], hk), w[..., conv_dim:in_width]], axis=-1)


def _exchange(srcs, gather, name):
    n_src = len(srcs)

    def body(*refs):
        sems = refs[2 * n_src:]
        copies = [_exchange_copies(refs[t], refs[n_src + t], *sems[3 * t:3 * t + 3], gather=gather) for t in range(n_src)]
        for cps in copies:
            _exchange_start(cps)
        for cps in copies:
            _exchange_wait(cps)

    any_spec = pl.BlockSpec(memory_space=pl.ANY)
    return pl.pallas_call(
        body, name=name, out_shape=[_exchange_out(s, gather) for s in srcs], in_specs=[any_spec] * n_src,
        out_specs=[any_spec] * n_src, scratch_shapes=[s for src in srcs for s in _exchange_sems(src)],
    )(*srcs)


BIG_COL = ("conv_w_pw1", "attn_w_qkv", "gdn_w_in", "ffn_w_gate", "ffn_w_up")
BIG_ROW = ("conv_w_pw2", "attn_w_o", "gdn_w_out", "ffn_w_down")
EXCHANGE_US_PER_BYTE = 11.4e-6
RIDE_PART_US = 150.0


class _Traffic:
    def __init__(self, shards, hk):
        self.shards, self.hk = shards, hk
        self.in_width = N_DEV * shards["gdn_w_in"].shape[-1]
        self.parts_of = {}
        for k, s in shards.items():
            parts = 1
            while (N_DEV * s.shape[1] * s.shape[2] * 2 * EXCHANGE_US_PER_BYTE / parts > RIDE_PART_US
                   and s.shape[1] % (32 * parts) == 0):
                parts *= 2
            self.parts_of[k] = parts
        depth = shards["ffn_w_down"].shape[0]
        order = []
        for i in range(depth):
            j = i // 3
            order += [[("conv_w_pw1", j), ("conv_w_pw2", j)], [("attn_w_qkv", j), ("attn_w_o", j)],
                      [("gdn_w_in", j), ("gdn_w_out", j)]][i % 3]
            order += [("ffn_w_gate", i), ("ffn_w_up", i), ("ffn_w_down", i)]
        self.wanted = [(k, i, part) for k, i in order for part in range(self.parts_of[k])]
        self.arrived = {}
        self.ready = {}
        self.owed = []
        self.received = {}

    def _us(self, k):
        s = self.shards[k]
        return N_DEV * s.shape[1] * s.shape[2] * 2 * EXCHANGE_US_PER_BYTE / self.parts_of[k]

    def _shard_part(self, item):
        k, i, part = item
        rows = self.shards[k].shape[1] // self.parts_of[k]
        return self.shards[k][i:i + 1, part * rows:(part + 1) * rows]

    def _pick(self, queue, us_of, room):
        taken = []
        while queue and room >= 0.5 * us_of(queue[0]):
            room -= us_of(queue[0])
            taken.append(queue.pop(0))
        return taken

    def _run(self, forward, room, fn):
        if forward:
            taken = self._pick(self.wanted, lambda it: self._us(it[0]), room)
            items = [(self._shard_part(it), True) for it in taken]
        else:
            taken = self._pick(self.owed, lambda it: self._us(it[0][0]), room)
            items = [(pieces, False) for _, pieces in taken]
        if not taken:
            return fn(None)
        ride = _Ride(items)
        out = fn(ride)
        for it, got in zip(taken, ride.outs):
            if forward:
                self.arrived[it] = got[0]
            else:
                self.received[it[0]] = got
        return out

    def call(self, forward, room, fn):
        return self._run(forward, room, fn)

    def mm(self, forward, a, b, mode, out_dtype, name, **kw):
        m = a.shape[1] if mode == "tn" else a.shape[0]
        k = a.shape[0] if mode == "tn" else a.shape[1]
        n = b.shape[0] if mode == "nt" else b.shape[1]
        return self._run(forward, _us_mm(m, n, k), lambda ride: _mm(a, b, mode, out_dtype, name, ride=ride, **kw))

    def _natural(self, k, i):
        parts = self.parts_of[k]
        missing = [(k, i, part) for part in range(parts) if (k, i, part) not in self.arrived]
        if missing:
            for it in missing:
                self.wanted.remove(it)
            got = _exchange([self._shard_part(it) for it in missing], True, "gather_weights")
            for it, t in zip(missing, got):
                self.arrived[it] = t[0]
        got = [self.arrived[(k, i, part)] for part in range(parts)]
        rows, c = got[0].shape[1], got[0].shape[2]
        if k in BIG_COL:
            return jnp.concatenate([t.transpose(1, 0, 2).reshape(rows, N_DEV * c) for t in got], axis=0)
        return jnp.stack(got, axis=1).reshape(N_DEV * parts * rows, c)

    def weight(self, k, i):
        if (k, i) not in self.ready:
            w = self._natural(k, i)
            self.ready[(k, i)] = _gdn_in_layout(w, self.hk) if k == "gdn_w_in" else w
        return self.ready[(k, i)]

    def give(self, k, i, grad):
        if k == "gdn_w_in":
            grad = _gdn_in_natural(grad, self.hk, self.in_width)
        r, c = self.shards[k].shape[1:]
        pieces = grad.reshape(r, N_DEV, c).transpose(1, 0, 2) if k in BIG_COL else grad.reshape(N_DEV, r, c)
        rows = r // self.parts_of[k]
        for part in range(self.parts_of[k]):
            self.owed.append(((k, i, part), pieces[None, :, part * rows:(part + 1) * rows]))

    def gradient_parts(self, k):
        if self.owed:
            got = _exchange([pieces for _, pieces in self.owed], False, "scatter_grads")
            for (it, _), t in zip(self.owed, got):
                self.received[it] = t
            self.owed = []
        layers = self.shards[k].shape[0]
        return jnp.concatenate([jnp.concatenate([self.received[(k, i, part)] for part in range(self.parts_of[k])], axis=2)
                                for i in range(layers)], axis=0)


def _cast_bf16(w, name):
    n, r, c = w.shape
    tr = _tile(r, max(16, (1 << 20) // c), 16)

    def body(w_ref, o_ref):
        o_ref[...] = w_ref[...].astype(BF16)

    return pl.pallas_call(
        body, name=name, grid=(n, r // tr), in_specs=[pl.BlockSpec((1, tr, c), lambda l, i: (l, i, 0))],
        out_specs=pl.BlockSpec((1, tr, c), lambda l, i: (l, i, 0)), out_shape=jax.ShapeDtypeStruct(w.shape, BF16),
        compiler_params=_cparams(("parallel", "parallel")),
    )(w)


def _adamw(w, g, m, v):
    m = ADAM_B1 * m + (1.0 - ADAM_B1) * g
    v = ADAM_B2 * v + (1.0 - ADAM_B2) * jnp.square(g)
    m_hat = m / (1.0 - ADAM_B1 ** ADAM_STEP)
    v_hat = v / (1.0 - ADAM_B2 ** ADAM_STEP)
    delta = -ADAM_LR * (m_hat / (jnp.sqrt(v_hat) + ADAM_EPS) + ADAM_WD * w)
    return delta, m, v


def _sum8_adam(parts, w, m, v, name):
    n, _, r, c = parts.shape
    tr = _tile(r, max(16, (1 << 18) // c), 16)
    blk = pl.BlockSpec((1, tr, c), lambda l, i: (l, i, 0))

    def body(p_ref, w_ref, m_ref, v_ref, g_ref, d_ref, mo_ref, vo_ref):
        g = p_ref[0, 0].astype(F32)
        for s in range(1, N_DEV):
            g = g + p_ref[0, s].astype(F32)
        delta, m2, v2 = _adamw(w_ref[0], g, m_ref[0], v_ref[0])
        g_ref[0], d_ref[0], mo_ref[0], vo_ref[0] = g, delta, m2, v2

    shp = jax.ShapeDtypeStruct(w.shape, F32)
    return pl.pallas_call(
        body, name=name, grid=(n, r // tr),
        in_specs=[pl.BlockSpec((1, N_DEV, tr, c), lambda l, i: (l, 0, i, 0)), blk, blk, blk],
        out_specs=[blk, blk, blk, blk], out_shape=[shp, shp, shp, shp],
        compiler_params=_cparams(("parallel", "parallel")),
    )(parts, w, m, v)


PACK_ROWS = 8


def _pack(arrays):
    flat = jnp.concatenate([a.reshape(-1).astype(F32) for a in arrays])
    unit = PACK_ROWS * LANES
    total = -(-flat.shape[0] // unit) * unit
    return jnp.concatenate([flat, jnp.zeros((total - flat.shape[0],), F32)]).reshape(-1, LANES)


def _unpack(packed, shapes):
    flat, out, pos = packed.reshape(-1), [], 0
    for s in shapes:
        size = math.prod(s)
        out.append(flat[pos:pos + size].reshape(s))
        pos += size
    return out


SMALL_SHARDED =("meta_tokens", "conv_b_pw1", "conv_w_dw", "conv_b_dw", "conv_ln_g", "conv_ln_b", "conv_b_pw2", "gdn_conv_w")
REPLICATED = ("norm_mix", "norm_ffn", "norm_final", "attn_b_qkv", "attn_sinks", "attn_b_o", "gdn_a_log", "gdn_dt_bias", "gdn_norm_w")
WEIGHTS = ("meta_tokens", "norm_mix", "norm_ffn", "norm_final", "conv_w_pw1", "conv_b_pw1", "conv_w_dw", "conv_b_dw", "conv_ln_g",
           "conv_ln_b", "conv_w_pw2", "conv_b_pw2", "attn_w_qkv", "attn_b_qkv", "attn_sinks", "attn_w_o", "attn_b_o", "gdn_w_in",
           "gdn_conv_w", "gdn_a_log", "gdn_dt_bias", "gdn_norm_w", "gdn_w_out", "ffn_w_gate", "ffn_w_up", "ffn_w_down")


def kernel(x, meta_tokens, norm_mix, norm_ffn, norm_final, conv_w_pw1, conv_b_pw1, conv_w_dw, conv_b_dw, conv_ln_g, conv_ln_b, conv_w_pw2, conv_b_pw2, attn_w_qkv, attn_b_qkv, attn_sinks, attn_w_o, attn_b_o, gdn_w_in, gdn_conv_w, gdn_a_log, gdn_dt_bias, gdn_norm_w, gdn_w_out, ffn_w_gate, ffn_w_up, ffn_w_down, loss_target, m_meta_tokens, m_norm_mix, m_norm_ffn, m_norm_final, m_conv_w_pw1, m_conv_b_pw1, m_conv_w_dw, m_conv_b_dw, m_conv_ln_g, m_conv_ln_b, m_conv_w_pw2, m_conv_b_pw2, m_attn_w_qkv, m_attn_b_qkv, m_attn_sinks, m_attn_w_o, m_attn_b_o, m_gdn_w_in, m_gdn_conv_w, m_gdn_a_log, m_gdn_dt_bias, m_gdn_norm_w, m_gdn_w_out, m_ffn_w_gate, m_ffn_w_up, m_ffn_w_down, v_meta_tokens, v_norm_mix, v_norm_ffn, v_norm_final, v_conv_w_pw1, v_conv_b_pw1, v_conv_w_dw, v_conv_b_dw, v_conv_ln_g, v_conv_ln_b, v_conv_w_pw2, v_conv_b_pw2, v_attn_w_qkv, v_attn_b_qkv, v_attn_sinks, v_attn_w_o, v_attn_b_o, v_gdn_w_in, v_gdn_conv_w, v_gdn_a_log, v_gdn_dt_bias, v_gdn_norm_w, v_gdn_w_out, v_ffn_w_gate, v_ffn_w_up, v_ffn_w_down):
    a = dict(locals())
    me = 4 * lax.axis_index("x") + 2 * lax.axis_index("y") + lax.axis_index("c")
    d = x.shape[-1]

    hk = d // GDN_DIM
    full = {k: a[k] for k in REPLICATED}
    shard_shapes = [a[k].shape for k in SMALL_SHARDED]
    (got,) = _exchange([_pack([a[k] for k in SMALL_SHARDED])[None]], True, "gather_small")
    per_dev = [_unpack(got[0, s], shard_shapes) for s in range(N_DEV)]
    for i, k in enumerate(SMALL_SHARDED):
        st = jnp.stack([per_dev[s][i] for s in range(N_DEV)], axis=-2)
        full[k] = st.reshape(st.shape[:-2] + (N_DEV * st.shape[-1],))
    traffic = _Traffic({k: _cast_bf16(a[k], "cast_bf16") for k in BIG_COL + BIG_ROW}, hk)

    h0 = jnp.concatenate([jnp.zeros((PAD_LEN, d), F32), full["meta_tokens"], x[0]], axis=0)
    target = jnp.concatenate([jnp.zeros((FRONT, d), F32), loss_target[0]], axis=0)
    loss, dh0, g = _local_step(h0, target, {**full, "gdn_conv_w": _gdn_group(full["gdn_conv_w"], hk)}, traffic)
    g["gdn_conv_w"] = _gdn_ungroup(g["gdn_conv_w"], hk)
    g["meta_tokens"] = dh0[PAD_LEN:FRONT]
    loss = lax.psum(loss, AXES)
    grad_x = dh0[FRONT:][None]

    grads, deltas, new_m, new_v = {}, {}, {}, {}
    for k in BIG_COL + BIG_ROW:
        grads[k], deltas[k], new_m[k], new_v[k] = _sum8_adam(traffic.gradient_parts(k), a[k], a["m_" + k], a["v_" + k], "sum8_adamw")

    small = SMALL_SHARDED + REPLICATED
    full_shapes = [full[k].shape for k in small]
    (got,) = _exchange([_pack([g[k] for k in small])[None]], True, "gather_small_grads")
    got = got[0]
    (total,) = _rowwise(lambda r0, *t: (functools.reduce(lambda p, q: p + q, t),), [got[s] for s in range(N_DEV)], [],
                        [(LANES, F32)], [], "sum8_small", tm=got.shape[1])
    for k, t in zip(small, _unpack(total, full_shapes)):
        if k in SMALL_SHARDED:
            c = a[k].shape[-1]
            t = lax.dynamic_index_in_dim(t.reshape(t.shape[:-1] + (N_DEV, c)), me, axis=t.ndim - 1, keepdims=False)
        grads[k] = t
    shapes = [a[k].shape for k in small]
    packed = [_pack([src[k] for k in small]) for src in (grads, a, {k: a["m_" + k] for k in small}, {k: a["v_" + k] for k in small})]

    def small_adam(r0, gv, wv, mv, vv):
        return _adamw(wv, gv, mv, vv)

    outs = _rowwise(small_adam, packed, [], [(LANES, F32)] * 3, [], "adamw_small", tm=packed[0].shape[0])
    for dst, o in zip((deltas, new_m, new_v), outs):
        for k, t in zip(small, _unpack(o, shapes)):
            dst[k] = t

    return (loss, grad_x, *[grads[k] for k in WEIGHTS], *[deltas[k] for k in WEIGHTS], *[new_m[k] for k in WEIGHTS],
            *[new_v[k] for k in WEIGHTS])
```

```python
import functools
import math

import jax
import jax.numpy as jnp
from jax import lax
from jax.experimental import pallas as pl
from jax.experimental.pallas import tpu as pltpu

F32 = jnp.float32
BF16 = jnp.bfloat16

AXES = ("x", "y", "c")
N_DEV = 8

N_META = 16
FRONT = 128
PAD_LEN = FRONT - N_META
NORM_EPS = 1e-6
LN_EPS = 1e-5
NEG_INF = -1e30
CONV_KERNEL = 31
HEAD_DIM = 64
GROUP = 8
BLOCK = 128
ROPE_THETA = 10000.0
GDN_DIM = 128
GDN_CONV = 4
GDN_CHUNK = 64
ADAM_LR, ADAM_B1, ADAM_B2, ADAM_EPS, ADAM_WD, ADAM_STEP = 0.001, 0.9, 0.999, 1e-08, 0.01, 10

VMEM_LIMIT_BYTES = 52 * 1024 * 1024
LANES = 128
MM_TILE_PREF = 1408
MM_VMEM_BUDGET_BYTES = 40 * 1024 * 1024
ROW_TILE_BUDGET_BYTES = 16 * 1024 * 1024
CONV_HALO = 32


def _tile(n, pref, align=128):
    best = None
    for t in range(align, min(n, pref) + 1, align):
        if n % t == 0:
            best = t
    return best if best is not None else n


def _cparams(sem):
    return pltpu.CompilerParams(dimension_semantics=sem, vmem_limit_bytes=VMEM_LIMIT_BYTES)


def _exchange_copies(src_ref, out_ref, send_sems, recv_sems, local_sems, gather):
    n = src_ref.shape[0]
    x, y, c = lax.axis_index("x"), lax.axis_index("y"), lax.axis_index("c")
    me = 4 * x + 2 * y + c
    remote, local = [], []
    for l in range(n):
        local.append(pltpu.make_async_copy(src_ref.at[l] if gather else src_ref.at[l, me], out_ref.at[l, me], local_sems.at[l]))
        for k in range(1, N_DEV):
            px = 1 - x if k & 4 else x
            py = 1 - y if k & 2 else y
            pc = 1 - c if k & 1 else c
            peer = 4 * px + 2 * py + pc
            remote.append(pltpu.make_async_remote_copy(
                src_ref=src_ref.at[l] if gather else src_ref.at[l, peer], dst_ref=out_ref.at[l, me],
                send_sem=send_sems.at[l, k - 1], recv_sem=recv_sems.at[l, k - 1],
                device_id=(px, py, pc), device_id_type=pl.DeviceIdType.MESH))
    return remote, local


def _exchange_start(copies):
    remote, local = copies
    for cp in local + remote:
        cp.start()


def _exchange_wait(copies):
    remote, local = copies
    for cp in remote:
        cp.wait_send()
    for cp in remote:
        cp.wait_recv()
    for cp in local:
        cp.wait()


def _exchange_out(src, gather):
    return jax.ShapeDtypeStruct((src.shape[0], N_DEV) + src.shape[-2:], src.dtype)


def _exchange_sems(src):
    n = src.shape[0]
    return [pltpu.SemaphoreType.DMA((n, N_DEV - 1)), pltpu.SemaphoreType.DMA((n, N_DEV - 1)), pltpu.SemaphoreType.DMA((n,))]


class _Ride:
    def __init__(self, items):
        self.items = items
        self.outs = None


def _pcall(body, args, *, name, grid, in_specs, out_specs, out_shape, sem, scratch_shapes=(), ride=None):
    if ride is None:
        return pl.pallas_call(body, name=name, grid=grid, in_specs=in_specs, out_specs=out_specs, out_shape=out_shape,
                              scratch_shapes=list(scratch_shapes), compiler_params=_cparams(sem))(*args)
    n_in, n_out, n_scr, n_ride = len(in_specs), len(out_specs), len(scratch_shapes), len(ride.items)
    any_spec = pl.BlockSpec(memory_space=pl.ANY)

    def with_ride(*refs):
        pos = 0
        ins, pos = refs[pos:pos + n_in], pos + n_in
        srcs, pos = refs[pos:pos + n_ride], pos + n_ride
        outs, pos = refs[pos:pos + n_out], pos + n_out
        dsts, pos = refs[pos:pos + n_ride], pos + n_ride
        scr, pos = refs[pos:pos + n_scr], pos + n_scr
        sems = refs[pos:]
        first = functools.reduce(lambda p, q: p & q, [pl.program_id(d) == 0 for d in range(len(grid))])
        last = functools.reduce(lambda p, q: p & q, [pl.program_id(d) == grid[d] - 1 for d in range(len(grid))])

        def copies():
            return [_exchange_copies(srcs[t], dsts[t], *sems[3 * t:3 * t + 3], gather=ride.items[t][1]) for t in range(n_ride)]

        @pl.when(first)
        def _():
            for cps in copies():
                _exchange_start(cps)

        body(*ins, *outs, *scr)

        @pl.when(last)
        def _():
            for cps in copies():
                _exchange_wait(cps)

    res = pl.pallas_call(
        with_ride, name=name, grid=grid, in_specs=list(in_specs) + [any_spec] * n_ride,
        out_specs=list(out_specs) + [any_spec] * n_ride,
        out_shape=list(out_shape) + [_exchange_out(s, g) for s, g in ride.items],
        scratch_shapes=list(scratch_shapes) + [s for src, _ in ride.items for s in _exchange_sems(src)],
        compiler_params=_cparams(("arbitrary",) * len(grid)),
    )(*args, *[s for s, _ in ride.items])
    ride.outs = list(res[n_out:])
    return list(res[:n_out])


def _mm_shape(a, b, mode):
    if mode == "nn":
        (m, k), (k2, n) = a.shape, b.shape
    elif mode == "nt":
        (m, k), (n, k2) = a.shape, b.shape
    else:
        (k, m), (k2, n) = a.shape, b.shape
    assert k == k2, (a.shape, b.shape, mode)
    return m, n, k


def _mmx(pairs, mode, name, out_dtypes, epilogue, extras=(), n_acc=1, ride=None):
    m, n, k = _mm_shape(pairs[0][0], pairs[0][1], mode)
    tm = _tile(m, MM_TILE_PREF if mode == "tn" else 640)
    tk = _tile(k, 2048)
    nk = k // tk

    def vmem_bytes(tn):
        total = n_acc * tm * tn * 4
        for a, b, _ in pairs:
            total += 2 * (tm * tk * a.dtype.itemsize + tk * tn * b.dtype.itemsize)
        total += 2 * sum((1 if e.shape[0] == 1 and m != 1 else tm) * tn * e.dtype.itemsize for e in extras)
        return total + 2 * sum(tm * tn * jnp.dtype(dt).itemsize for dt in out_dtypes)

    tn = _tile(n, MM_TILE_PREF)
    while vmem_bytes(tn) > MM_VMEM_BUDGET_BYTES and tn > LANES:
        tn = _tile(n, tn - LANES)
    dims = {"nn": (((1,), (0,)), ((), ())), "nt": (((1,), (1,)), ((), ())), "tn": (((0,), (0,)), ((), ()))}[mode]
    a_spec = pl.BlockSpec((tk, tm), lambda i, j, kk: (kk, i)) if mode == "tn" else pl.BlockSpec((tm, tk), lambda i, j, kk: (i, kk))
    b_spec = pl.BlockSpec((tn, tk), lambda i, j, kk: (j, kk)) if mode == "nt" else pl.BlockSpec((tk, tn), lambda i, j, kk: (kk, j))
    ins, specs, where = [], [], []
    for a, b, _ in pairs:
        assert _mm_shape(a, b, mode) == (m, n, k)
        ia = next((t for t, x in enumerate(ins) if x is a), None)
        if ia is None:
            ins.append(a)
            specs.append(a_spec)
            ia = len(ins) - 1
        ins.append(b)
        specs.append(b_spec)
        where.append((ia, len(ins) - 1))
    n_ops = len(ins)
    for e in extras:
        ins.append(e)
        specs.append(pl.BlockSpec((1, tn), lambda i, j, kk: (0, j)) if e.shape[0] == 1 and m != 1 else
                     pl.BlockSpec((tm, tn), lambda i, j, kk: (i, j)))
    n_ex, n_out = len(extras), len(out_dtypes)

    def body(*refs):
        ex_refs = refs[n_ops:n_ops + n_ex]
        o_refs = refs[n_ops + n_ex:n_ops + n_ex + n_out]
        accs = refs[n_ops + n_ex + n_out:]
        kk = pl.program_id(2)

        @pl.when(kk == 0)
        def _():
            for acc in accs:
                acc[...] = jnp.zeros_like(acc)

        for (ia, ib), (_, _, which) in zip(where, pairs):
            accs[which][...] += lax.dot_general(refs[ia][...].astype(BF16), refs[ib][...].astype(BF16), dims,
                                                preferred_element_type=F32)

        @pl.when(kk == nk - 1)
        def _():
            tiles = epilogue([acc[...] for acc in accs], *[e[...] for e in ex_refs])
            for o_ref, t in zip(o_refs, tiles):
                o_ref[...] = t.astype(o_ref.dtype)

    return _pcall(
        body, ins, name=name, grid=(m // tm, n // tn, nk), in_specs=specs,
        out_specs=[pl.BlockSpec((tm, tn), lambda i, j, kk: (i, j))] * n_out,
        out_shape=[jax.ShapeDtypeStruct((m, n), dt) for dt in out_dtypes],
        scratch_shapes=[pltpu.VMEM((tm, tn), F32)] * n_acc, sem=("parallel", "parallel", "arbitrary"), ride=ride)


def _mm(a, b, mode, out_dtype, name, bias=None, residual=None, ride=None):
    extras = ([] if bias is None else [bias.reshape(1, -1).astype(F32)]) + ([] if residual is None else [residual])
    (out,) = _mmx([(a, b, 0)], mode, name, [out_dtype], lambda accs, *ex: [functools.reduce(lambda p, q: p + q.astype(F32), ex, accs[0])],
                  extras=extras, ride=ride)
    return out


def _rowwise(fn, rows, consts, out_rows, out_accs, name, tm=None, ride=None):
    rows = [r if isinstance(r, tuple) else (r, r.shape[1], 0) for r in rows]
    n_rows = rows[0][0].shape[0]
    if tm is None:
        row_bytes = sum(w * r.dtype.itemsize for r, w, _ in rows) + sum(w * jnp.dtype(dt).itemsize for w, dt in out_rows)
        tm = _tile(n_rows, max(8, min(640, ROW_TILE_BUDGET_BYTES // (2 * row_bytes))), 8)
    steps = n_rows // tm
    in_specs = [pl.BlockSpec((tm, w), functools.partial(lambda i, c: (i, c), c=cb)) for _, w, cb in rows]
    in_specs += [pl.BlockSpec(c.shape, lambda i: (0, 0)) for c in consts]
    out_specs = [pl.BlockSpec((tm, w), lambda i: (i, 0)) for w, _ in out_rows]
    out_specs += [pl.BlockSpec(s, lambda i: (0, 0)) for s in out_accs]
    out_shape = [jax.ShapeDtypeStruct((n_rows, w), d) for w, d in out_rows]
    out_shape += [jax.ShapeDtypeStruct(s, F32) for s in out_accs]
    n_in, n_or = len(rows) + len(consts), len(out_rows)

    def body(*refs):
        i = pl.program_id(0)
        vals = fn(i * tm, *[r[...] for r in refs[:n_in]])
        outs = refs[n_in:]
        for o_ref, v in zip(outs[:n_or], vals[:n_or]):
            o_ref[...] = v.astype(o_ref.dtype)
        if out_accs:
            @pl.when(i == 0)
            def _():
                for a_ref in outs[n_or:]:
                    a_ref[...] = jnp.zeros_like(a_ref)

            for a_ref, v in zip(outs[n_or:], vals[n_or:]):
                a_ref[...] += v

    return _pcall(body, [r[0] for r in rows] + list(consts), name=name, grid=(steps,), in_specs=in_specs, out_specs=out_specs,
                  out_shape=out_shape, sem=("arbitrary",) if out_accs else ("parallel",), ride=ride)


def _colsum(v):
    return jnp.sum(v, axis=0, keepdims=True)


def _rms(h, w):
    return h * lax.rsqrt(jnp.mean(h * h, axis=-1, keepdims=True) + NORM_EPS) * w


def _rms_fwd(h, w, name):
    (hn,) = _rowwise(lambda r0, hv, wv: (_rms(hv, wv),), [h], [w.reshape(1, -1)], [(h.shape[1], BF16)], [], name)
    return hn


def _rms_bwd(h, w, dhn, dh_in, name):
    d = h.shape[1]

    def fn(r0, hv, dv, rv, wv):
        _, vjp = jax.vjp(_rms, hv, wv)
        dh, dw = vjp(dv.astype(F32))
        return dh + rv, dh + rv, dw

    dh, dh_bf16, dw = _rowwise(fn, [h, dhn, dh_in], [w.reshape(1, -1)], [(d, F32), (d, BF16)], [(1, d)], name)
    return dh, dh_bf16, dw[0]


def _glu(p, b):
    t = p + b
    d = t.shape[1] // 2
    return t[:, :d] * jax.nn.sigmoid(t[:, d:])


def _glu_fwd(p, b, name, ride=None):
    (u,) = _rowwise(lambda r0, v, bv: (_glu(v, bv),), [p], [b.reshape(1, -1)], [(p.shape[1] // 2, F32)], [], name, ride=ride)
    return u


def _glu_bwd(p, b, du, name):
    def fn(r0, v, dv, bv):
        _, vjp = jax.vjp(_glu, v, bv)
        dp, db = vjp(dv)
        return dp, db

    dp, db = _rowwise(fn, [p, du], [b.reshape(1, -1)], [(p.shape[1], BF16)], [(1, p.shape[1])], name)
    return dp, db[0]


def _ln_silu(c, g, b):
    mu = jnp.mean(c, axis=-1, keepdims=True)
    xc = c - mu
    var = jnp.mean(xc * xc, axis=-1, keepdims=True)
    return jax.nn.silu(xc * lax.rsqrt(var + LN_EPS) * g + b)


def _ln_silu_fwd(c, g, b, name):
    (s,) = _rowwise(lambda r0, v, gv, bv: (_ln_silu(v, gv, bv),), [c], [g.reshape(1, -1), b.reshape(1, -1)],
                    [(c.shape[1], BF16)], [], name)
    return s


def _ln_silu_bwd(c, g, b, ds, name):
    d = c.shape[1]

    def fn(r0, v, dv, gv, bv):
        _, vjp = jax.vjp(_ln_silu, v, gv, bv)
        return vjp(dv.astype(F32))

    dc, dg, db = _rowwise(fn, [c, ds], [g.reshape(1, -1), b.reshape(1, -1)], [(d, F32)], [(1, d), (1, d)], name)
    return dc, dg[0], db[0]


def _colsum_rows(v, name):
    (s,) = _rowwise(lambda r0, t: (_colsum(t.astype(F32)),), [v], [], [], [(1, v.shape[1])], name)
    return s[0]


def _rot_half(x):
    w = x.shape[1]
    lane = lax.broadcasted_iota(jnp.int32, x.shape, 1)
    lo = (lane % HEAD_DIM) < (HEAD_DIM // 2)
    return jnp.where(lo, -pltpu.roll(x, w - HEAD_DIM // 2, axis=1), pltpu.roll(x, HEAD_DIM // 2, axis=1))


def _qkv_post_fwd(pre, b, cos, sin, qw, kw, name):
    reps = (qw + kw) // LANES

    def fn(r0, pv, cv, sv, bv):
        t = pv + bv
        tq = t[:, :qw + kw]
        y = tq * jnp.tile(cv, (1, reps)) + _rot_half(tq) * jnp.tile(sv, (1, reps))
        return y[:, :qw], y[:, qw:], t[:, qw + kw:]

    return _rowwise(fn, [pre, cos, sin], [b.reshape(1, -1)], [(qw, BF16), (kw, BF16), (kw, BF16)], [], name)


def _qkv_post_bwd(dq, dk, dv, cos, sin, name):
    qw, kw = dq.shape[1], dk.shape[1]
    reps = (qw + kw) // LANES
    width = qw + 2 * kw

    def fn(r0, dqv, dkv, dvv, cv, sv):
        dy = jnp.concatenate([dqv.astype(F32), dkv.astype(F32)], axis=1)
        dt = dy * jnp.tile(cv, (1, reps)) - _rot_half(dy * jnp.tile(sv, (1, reps)))
        dpre = jnp.concatenate([dt, dvv.astype(F32)], axis=1)
        return dpre, _colsum(dpre)

    dpre, db = _rowwise(fn, [dq, dk, dv, cos, sin], [], [(width, BF16)], [(1, width)], name)
    return dpre, db[0]


def _dw_tiles(n_rows, c):
    return _tile(n_rows, 640, 8), _tile(c, 512)


class _RowWindow:
    SUBLANES = 8

    def __init__(self, value):
        self.copies = {0: value}

    def rows(self, off, n):
        q, s = divmod(off, self.SUBLANES)
        if s not in self.copies:
            base = self.copies[0]
            self.copies[s] = pltpu.roll(base, base.shape[0] - s, axis=0)
        return self.copies[s][self.SUBLANES * q:self.SUBLANES * q + n, :]


def _row_mask(r0, n, width):
    row = r0 + lax.broadcasted_iota(jnp.int32, (n, width), 0)
    return row >= PAD_LEN


def _dwconv_fwd(u, w, bias, c, name, ride=None):
    n_rows, taps = u.shape[0], w.shape[0]
    tr, cb = _dw_tiles(n_rows, c)
    kp = -(-taps // 8) * 8
    wp = jnp.concatenate([w.astype(F32), jnp.zeros((kp - taps, c), F32)], axis=0)
    bp = jnp.zeros((1, c), F32) if bias is None else bias.reshape(1, c).astype(F32)

    def body(cur_ref, prev_ref, w_ref, b_ref, o_ref):
        r = pl.program_id(1)
        cur = jnp.where(_row_mask(r * tr, tr, cb), cur_ref[...], 0.0)
        tail = prev_ref[tr - CONV_HALO:, :]
        tail = jnp.where(_row_mask(r * tr - CONV_HALO, CONV_HALO, cb) & (r > 0), tail, 0.0)
        win = _RowWindow(jnp.concatenate([tail, cur], axis=0))
        acc = jnp.zeros((tr, cb), F32) + b_ref[...]
        for k in range(taps):
            acc = acc + w_ref[k:k + 1, :] * win.rows(CONV_HALO - (taps - 1) + k, tr)
        o_ref[...] = acc

    (out,) = _pcall(
        body, [u, u, wp, bp], name=name, grid=(c // cb, n_rows // tr),
        in_specs=[pl.BlockSpec((tr, cb), lambda j, r: (r, j)),
                  pl.BlockSpec((tr, cb), lambda j, r: (jnp.maximum(r - 1, 0), j)),
                  pl.BlockSpec((kp, cb), lambda j, r: (0, j)),
                  pl.BlockSpec((1, cb), lambda j, r: (0, j))],
        out_specs=[pl.BlockSpec((tr, cb), lambda j, r: (r, j))],
        out_shape=[jax.ShapeDtypeStruct((n_rows, c), F32)], sem=("parallel", "parallel"), ride=ride)
    return out


def _dwconv_bwd(u, w, dc, c, name, ride=None):
    n_rows, taps = u.shape[0], w.shape[0]
    tr, cb = _dw_tiles(n_rows, c)
    nr = n_rows // tr
    kp = -(-taps // 8) * 8
    wp = jnp.concatenate([w.astype(F32), jnp.zeros((kp - taps, c), F32)], axis=0)

    def body(cur_ref, prev_ref, d_ref, dnext_ref, w_ref, du_ref, dw_ref, db_ref):
        r = pl.program_id(1)
        cur = jnp.where(_row_mask(r * tr, tr, cb), cur_ref[...], 0.0)
        tail = prev_ref[tr - CONV_HALO:, :]
        tail = jnp.where(_row_mask(r * tr - CONV_HALO, CONV_HALO, cb) & (r > 0), tail, 0.0)
        win_u = _RowWindow(jnp.concatenate([tail, cur], axis=0))
        d = d_ref[...]
        head = jnp.where(r < nr - 1, dnext_ref[:CONV_HALO, :], 0.0)
        win_d = _RowWindow(jnp.concatenate([d, head], axis=0))

        @pl.when(r == 0)
        def _():
            dw_ref[...] = jnp.zeros_like(dw_ref)
            db_ref[...] = jnp.zeros_like(db_ref)

        du = jnp.zeros((tr, cb), F32)
        for k in range(taps):
            du = du + w_ref[k:k + 1, :] * win_d.rows(taps - 1 - k, tr)
            dw_ref[k:k + 1, :] += _colsum(d * win_u.rows(CONV_HALO - (taps - 1) + k, tr))
        du_ref[...] = jnp.where(_row_mask(r * tr, tr, cb), du, 0.0)
        db_ref[...] += _colsum(d)

    du, dw, db = _pcall(
        body, [u, u, dc, dc, wp], name=name, grid=(c // cb, nr),
        in_specs=[pl.BlockSpec((tr, cb), lambda j, r: (r, j)),
                  pl.BlockSpec((tr, cb), lambda j, r: (jnp.maximum(r - 1, 0), j)),
                  pl.BlockSpec((tr, cb), lambda j, r: (r, j)),
                  pl.BlockSpec((tr, cb), lambda j, r: (jnp.minimum(r + 1, nr - 1), j)),
                  pl.BlockSpec((kp, cb), lambda j, r: (0, j))],
        out_specs=[pl.BlockSpec((tr, cb), lambda j, r: (r, j)),
                   pl.BlockSpec((kp, cb), lambda j, r: (0, j)),
                   pl.BlockSpec((1, cb), lambda j, r: (0, j))],
        out_shape=[jax.ShapeDtypeStruct((n_rows, c), F32), jax.ShapeDtypeStruct((kp, c), F32),
                   jax.ShapeDtypeStruct((1, c), F32)], sem=("parallel", "arbitrary"), ride=ride)
    return du, dw[:taps], db[0]


def _attn_block(q, kprev, kcur, vprev, vcur, sink, n):
    qf = q.reshape(GROUP * BLOCK, HEAD_DIM).astype(BF16)
    kb = jnp.concatenate([kprev, kcur], axis=0).astype(BF16)
    vb = jnp.concatenate([vprev, vcur], axis=0).astype(BF16)
    s = lax.dot_general(qf, kb, (((1,), (1,)), ((), ())), preferred_element_type=F32) * (HEAD_DIM ** -0.5)
    s = s.reshape(GROUP, BLOCK, 2 * BLOCK)
    qi = lax.broadcasted_iota(jnp.int32, (BLOCK, 2 * BLOCK), 0)
    kj = lax.broadcasted_iota(jnp.int32, (BLOCK, 2 * BLOCK), 1)
    dist = qi + BLOCK - kj
    allowed = (dist >= 0) & (dist < BLOCK) & ((n - 1) * BLOCK + kj >= PAD_LEN)
    s = jnp.where(allowed[None], s, NEG_INF)
    m = lax.stop_gradient(jnp.maximum(jnp.max(s, axis=-1, keepdims=True), sink))
    e = jnp.exp(s - m)
    p = e / (jnp.sum(e, axis=-1, keepdims=True) + jnp.exp(sink - m))
    o = jnp.dot(p.reshape(GROUP * BLOCK, 2 * BLOCK).astype(BF16), vb, preferred_element_type=F32)
    return o.reshape(GROUP, BLOCK, HEAD_DIM)


def _attn_specs(nb):
    q_spec = pl.BlockSpec((GROUP, BLOCK, HEAD_DIM), lambda g, n: (g, n, 0))
    cur = pl.BlockSpec((1, BLOCK, HEAD_DIM), lambda g, n: (g, n, 0))
    prev = pl.BlockSpec((1, BLOCK, HEAD_DIM), lambda g, n: (g, jnp.maximum(n - 1, 0), 0))
    sink = pl.BlockSpec((1, GROUP, 1, 1), lambda g, n: (g, 0, 0, 0))
    return q_spec, cur, prev, sink


def _attn_fwd(q, k, v, sinks, name, ride=None):
    heads, n_rows, _ = q.shape
    nb = n_rows // BLOCK
    q_spec, cur, prev, sink = _attn_specs(nb)

    def body(q_ref, kp_ref, kc_ref, vp_ref, vc_ref, s_ref, o_ref):
        n = pl.program_id(1)
        o = _attn_block(q_ref[...].astype(F32), kp_ref[0].astype(F32), kc_ref[0].astype(F32), vp_ref[0].astype(F32),
                        vc_ref[0].astype(F32), s_ref[0], n)
        o_ref[...] = o.astype(o_ref.dtype)

    (out,) = _pcall(
        body, [q, k, k, v, v, sinks.reshape(heads // GROUP, GROUP, 1, 1)], name=name, grid=(heads // GROUP, nb),
        in_specs=[q_spec, prev, cur, prev, cur, sink], out_specs=[q_spec], out_shape=[jax.ShapeDtypeStruct(q.shape, BF16)],
        sem=("parallel", "parallel"), ride=ride)
    return out


def _attn_bwd(q, k, v, sinks, do, name):
    heads, n_rows, _ = q.shape
    kvh = heads // GROUP
    nb = n_rows // BLOCK
    q_spec, cur, prev, sink = _attn_specs(nb)
    part = pl.BlockSpec((1, 1, BLOCK, HEAD_DIM), lambda g, n: (g, n, 0, 0))
    part_shape = jax.ShapeDtypeStruct((kvh, nb, BLOCK, HEAD_DIM), F32)

    def body(q_ref, kp_ref, kc_ref, vp_ref, vc_ref, s_ref, do_ref, dq_ref, dkp_ref, dkc_ref, dvp_ref, dvc_ref, ds_ref):
        n = pl.program_id(1)
        f = functools.partial(_attn_block, n=n)
        _, vjp = jax.vjp(f, q_ref[...].astype(F32), kp_ref[0].astype(F32), kc_ref[0].astype(F32), vp_ref[0].astype(F32),
                         vc_ref[0].astype(F32), s_ref[0])
        dq, dkp, dkc, dvp, dvc, ds = vjp(do_ref[...].astype(F32))
        dq_ref[...] = dq.astype(dq_ref.dtype)
        dkp_ref[0, 0], dkc_ref[0, 0], dvp_ref[0, 0], dvc_ref[0, 0] = dkp, dkc, dvp, dvc

        @pl.when(n == 0)
        def _():
            ds_ref[...] = jnp.zeros_like(ds_ref)

        ds_ref[0] += ds

    return pl.pallas_call(
        body, name=name, grid=(kvh, nb), in_specs=[q_spec, prev, cur, prev, cur, sink, q_spec],
        out_specs=[q_spec, part, part, part, part, sink],
        out_shape=[jax.ShapeDtypeStruct(q.shape, BF16), part_shape, part_shape, part_shape, part_shape,
                   jax.ShapeDtypeStruct((kvh, GROUP, 1, 1), F32)],
        compiler_params=_cparams(("parallel", "arbitrary")),
    )(q, k, k, v, v, sinks.reshape(kvh, GROUP, 1, 1), do)


def _shift_add(own, to_prev, name):
    kvh, nb = own.shape[:2]
    blk = (1, 1, BLOCK, HEAD_DIM)

    def body(a_ref, b_ref, o_ref):
        n = pl.program_id(1)
        o_ref[...] = (a_ref[...] + jnp.where(n < nb - 1, b_ref[...], 0.0)).astype(o_ref.dtype)

    return pl.pallas_call(
        body, name=name, grid=(kvh, nb),
        in_specs=[pl.BlockSpec(blk, lambda g, n: (g, n, 0, 0)),
                  pl.BlockSpec(blk, lambda g, n: (g, jnp.minimum(n + 1, nb - 1), 0, 0))],
        out_specs=pl.BlockSpec(blk, lambda g, n: (g, n, 0, 0)),
        out_shape=jax.ShapeDtypeStruct(own.shape, BF16), compiler_params=_cparams(("parallel", "parallel")),
    )(own, to_prev)


def _gdn_gates(ba, alog, dt, r0, hv):
    lane = lax.broadcasted_iota(jnp.int32, ba.shape, 1)
    t = ba + dt
    softplus = jnp.maximum(t, 0.0) + jnp.log(1.0 + jnp.exp(-jnp.abs(t)))
    val = jnp.where(lane < hv, jax.nn.sigmoid(ba), jnp.where(lane < 2 * hv, -jnp.exp(alog) * softplus, 0.0))
    return jnp.where(_row_mask(r0, ba.shape[0], ba.shape[1]), val, 0.0)


def _l2n(x):
    return x * lax.rsqrt(jnp.sum(x * x, axis=-1, keepdims=True) + 1e-6)


_NN = (((2,), (1,)), ((0,), (0,)))
_NT = (((2,), (2,)), ((0,), (0,)))
_TN = (((1,), (1,)), ((0,), (0,)))


def _bdot(a, b, dims=_NN):
    return lax.dot_general(a.astype(BF16), b.astype(BF16), dims, preferred_element_type=F32)


def _dot3(a, b, dims):
    ah, bh = a.astype(BF16), b.astype(BF16)
    al, bl = (a - ah.astype(F32)).astype(BF16), (b - bh.astype(F32)).astype(BF16)

    def d(p, q):
        return lax.dot_general(p, q, dims, preferred_element_type=F32)

    return d(ah, bh) + (d(ah, bl) + d(al, bh))


@jax.custom_vjp
def _pdot(a, b):
    return _dot3(a, b, _NN)


def _pdot_fwd(a, b):
    return _dot3(a, b, _NN), (a, b)


def _pdot_bwd(res, ct):
    a, b = res
    return _bdot(ct, b, _NT), _bdot(a, ct, _TN)


_pdot.defvjp(_pdot_fwd, _pdot_bwd)


def _scan_chunks(x, reverse):
    n, c = x.shape[0], GDN_CHUNK
    row = lax.broadcasted_iota(jnp.int32, x.shape, 0) % c
    s = 1
    while s < c:
        if reverse:
            x = x + jnp.where(row < c - s, pltpu.roll(x, n - s, axis=0), 0.0)
        else:
            x = x + jnp.where(row >= s, pltpu.roll(x, s, axis=0), 0.0)
        s *= 2
    return x


@jax.custom_vjp
def _cumsum_chunks(x):
    return _scan_chunks(x, False)


_cumsum_chunks.defvjp(lambda x: (_scan_chunks(x, False), None), lambda _, ct: (_scan_chunks(ct, True),))


def _gdn_heads(states, qkv, z, gates, norm_w, hk0, hv_total):
    c, h = GDN_CHUNK, states.shape[0]
    g = h // 2

    def cols(src, starts):
        return jnp.stack([src[:, s:s + GDN_DIM] for s in starts])

    q = _l2n(jax.nn.silu(cols(qkv, [4 * GDN_DIM * t for t in range(g)]))) * (GDN_DIM ** -0.5)
    k = _l2n(jax.nn.silu(cols(qkv, [4 * GDN_DIM * t + GDN_DIM for t in range(g)])))
    q, k = jnp.repeat(q, 2, axis=0), jnp.repeat(k, 2, axis=0)
    v = jax.nn.silu(cols(qkv, [4 * GDN_DIM * (t // 2) + (2 + t % 2) * GDN_DIM for t in range(h)]))
    zz = cols(z, [GDN_DIM * t for t in range(h)])
    lane = lax.broadcasted_iota(jnp.int32, gates.shape, 1)

    def col_of(first):
        return jnp.stack([jnp.sum(jnp.where(lane == first + t, gates, 0.0), axis=1, keepdims=True) for t in range(h)])

    beta_col, g_col = col_of(2 * hk0), col_of(hv_total + 2 * hk0)
    i = lax.broadcasted_iota(jnp.int32, (c, c), 0)
    j = lax.broadcasted_iota(jnp.int32, (c, c), 1)
    causal, strict = (i >= j)[None], (i > j)[None]
    gc = _cumsum_chunks(jnp.broadcast_to(g_col, (h, c, GDN_DIM)).reshape(h * c, GDN_DIM)).reshape(h, c, GDN_DIM)
    gc_i = gc[:, :, :c]
    gc_j = jnp.swapaxes(gc_i, 1, 2)
    gc_last = jnp.broadcast_to(gc[:, c - 1:c, :], (h, GDN_DIM, GDN_DIM))
    decay = jnp.where(causal, jnp.exp(jnp.where(causal, gc_i - gc_j, 0.0)), 0.0)
    k_beta = k * beta_col
    lower = jnp.where(strict, _bdot(k_beta, k, _NT) * decay, 0.0)
    eye = (i == j).astype(F32)[None]
    neg = -lower
    inv = eye + neg
    power = neg
    for _ in range(5):
        power = _pdot(power, power)
        inv = _pdot(inv, eye + power)
    sol = _pdot(inv, jnp.concatenate([v * beta_col, k_beta * jnp.exp(gc)], axis=2))
    u, w = sol[:, :, :GDN_DIM], sol[:, :, GDN_DIM:]
    intra = jnp.where(causal, _bdot(q, k, _NT) * decay, 0.0)
    q_dec = q * jnp.exp(gc)
    k_dec = k * jnp.exp(gc_last[:, :c] - gc)
    v_new = u - _bdot(w, states)
    o = _bdot(q_dec, states) + _bdot(intra, v_new)
    new_states = states * jnp.exp(gc_last) + _bdot(k_dec, v_new, _TN)
    y = _rms(o, norm_w) * jax.nn.silu(zz)
    return jnp.concatenate([y[t] for t in range(h)], axis=1), new_states


GDN_KEY_HEADS_PER_STEP = 8


def _key_heads_per_step(hk_total):
    return math.gcd(hk_total, GDN_KEY_HEADS_PER_STEP)


def _gdn_fwd(cq, proj, gates, norm_w, hk_total, name):
    n_rows = cq.shape[0]
    nc, hv_total = n_rows // GDN_CHUNK, 2 * hk_total
    grp = _key_heads_per_step(hk_total)
    heads = 2 * grp
    zblk0 = cq.shape[1] // (heads * GDN_DIM)

    def body(cq_ref, z_ref, g_ref, w_ref, y_ref, save_ref, state):
        n, hg = pl.program_id(0), pl.program_id(1)

        @pl.when(n == 0)
        def _():
            state[pl.ds(heads * hg, heads)] = jnp.zeros((heads, GDN_DIM, GDN_DIM), F32)

        s_in = state[pl.ds(heads * hg, heads)]
        save_ref[0] = s_in
        y, s_out = _gdn_heads(s_in, cq_ref[...], z_ref[...], g_ref[...], w_ref[...], grp * hg, hv_total)
        y_ref[...] = y.astype(y_ref.dtype)
        state[pl.ds(heads * hg, heads)] = s_out

    return pl.pallas_call(
        body, name=name, grid=(nc, hk_total // grp),
        in_specs=[pl.BlockSpec((GDN_CHUNK, 2 * heads * GDN_DIM), lambda n, h: (n, h)),
                  pl.BlockSpec((GDN_CHUNK, heads * GDN_DIM), lambda n, h: (n, zblk0 + h)),
                  pl.BlockSpec((GDN_CHUNK, LANES), lambda n, h: (n, 0)),
                  pl.BlockSpec((1, GDN_DIM), lambda n, h: (0, 0))],
        out_specs=[pl.BlockSpec((GDN_CHUNK, heads * GDN_DIM), lambda n, h: (n, h)),
                   pl.BlockSpec((1, heads, GDN_DIM, GDN_DIM), lambda n, h: (n, h, 0, 0))],
        out_shape=[jax.ShapeDtypeStruct((n_rows, hv_total * GDN_DIM), BF16),
                   jax.ShapeDtypeStruct((nc, hv_total, GDN_DIM, GDN_DIM), F32)],
        scratch_shapes=[pltpu.VMEM((hv_total, GDN_DIM, GDN_DIM), F32)],
        compiler_params=_cparams(("arbitrary", "arbitrary")),
    )(cq, proj, gates, norm_w.reshape(1, GDN_DIM))


def _gdn_bwd(cq, proj, gates, norm_w, saved, dy, hk_total, name):
    n_rows = cq.shape[0]
    nc, hv_total = n_rows // GDN_CHUNK, 2 * hk_total
    grp = _key_heads_per_step(hk_total)
    heads = 2 * grp
    zblk0 = cq.shape[1] // (heads * GDN_DIM)

    def body(cq_ref, z_ref, g_ref, w_ref, save_ref, dy_ref, dcq_ref, dz_ref, dg_ref, dw_ref, dstate):
        n, hg = pl.program_id(0), pl.program_id(1)

        @pl.when(n == 0)
        def _():
            dstate[pl.ds(heads * hg, heads)] = jnp.zeros((heads, GDN_DIM, GDN_DIM), F32)

        @pl.when((n == 0) & (hg == 0))
        def _():
            dw_ref[...] = jnp.zeros_like(dw_ref)

        @pl.when(hg == 0)
        def _():
            dg_ref[...] = jnp.zeros_like(dg_ref)

        f = functools.partial(_gdn_heads, hk0=grp * hg, hv_total=hv_total)
        _, vjp = jax.vjp(f, save_ref[0], cq_ref[...], z_ref[...], g_ref[...], w_ref[...])
        ds, dcq, dz, dg, dw = vjp((dy_ref[...].astype(F32), dstate[pl.ds(heads * hg, heads)]))
        dstate[pl.ds(heads * hg, heads)] = ds
        dcq_ref[...] = dcq
        dz_ref[...] = dz.astype(dz_ref.dtype)
        dg_ref[...] += dg
        dw_ref[...] += dw

    rev = lambda n: nc - 1 - n
    return pl.pallas_call(
        body, name=name, grid=(nc, hk_total // grp),
        in_specs=[pl.BlockSpec((GDN_CHUNK, 2 * heads * GDN_DIM), lambda n, h: (rev(n), h)),
                  pl.BlockSpec((GDN_CHUNK, heads * GDN_DIM), lambda n, h: (rev(n), zblk0 + h)),
                  pl.BlockSpec((GDN_CHUNK, LANES), lambda n, h: (rev(n), 0)),
                  pl.BlockSpec((1, GDN_DIM), lambda n, h: (0, 0)),
                  pl.BlockSpec((1, heads, GDN_DIM, GDN_DIM), lambda n, h: (rev(n), h, 0, 0)),
                  pl.BlockSpec((GDN_CHUNK, heads * GDN_DIM), lambda n, h: (rev(n), h))],
        out_specs=[pl.BlockSpec((GDN_CHUNK, 2 * heads * GDN_DIM), lambda n, h: (rev(n), h)),
                   pl.BlockSpec((GDN_CHUNK, heads * GDN_DIM), lambda n, h: (rev(n), h)),
                   pl.BlockSpec((GDN_CHUNK, LANES), lambda n, h: (rev(n), 0)),
                   pl.BlockSpec((1, GDN_DIM), lambda n, h: (0, 0))],
        out_shape=[jax.ShapeDtypeStruct(cq.shape, F32), jax.ShapeDtypeStruct((n_rows, hv_total * GDN_DIM), BF16),
                   jax.ShapeDtypeStruct((n_rows, LANES), F32), jax.ShapeDtypeStruct((1, GDN_DIM), F32)],
        scratch_shapes=[pltpu.VMEM((hv_total, GDN_DIM, GDN_DIM), F32)],
        compiler_params=_cparams(("arbitrary", "arbitrary")),
    )(cq, proj, gates, norm_w.reshape(1, GDN_DIM), saved, dy)


def _final_loss(h, w, target, name):
    d = h.shape[1]

    def fn(r0, hv, tv, wv):
        def loss_of(hh, ww):
            err = jnp.where(_row_mask(r0, hh.shape[0], d) & (r0 + lax.broadcasted_iota(jnp.int32, hh.shape, 0) >= FRONT),
                            _rms(hh, ww) - tv, 0.0)
            return 0.5 * jnp.sum(jnp.sum(err * err, axis=1, keepdims=True) / d)

        loss, vjp = jax.vjp(loss_of, hv, wv)
        dh, dw = vjp(jnp.ones((), F32))
        return dh, dh, jnp.zeros((1, LANES), F32) + loss, dw

    dh, dh_bf16, loss, dw = _rowwise(fn, [h, target], [w.reshape(1, -1)], [(d, F32), (d, BF16)], [(1, LANES), (1, d)], name)
    return loss[0, 0], dh, dh_bf16, dw[0]


def _rope_tables(n_rows):
    pos = (jnp.arange(n_rows) - PAD_LEN).astype(F32)
    inv_freq = ROPE_THETA ** (-jnp.arange(0, HEAD_DIM, 2, dtype=F32) / HEAD_DIM)
    ang = pos[:, None] * inv_freq[None, :]
    reps = LANES // (HEAD_DIM // 2)
    return jnp.tile(jnp.cos(ang), (1, reps)), jnp.tile(jnp.sin(ang), (1, reps))


def _to_heads(t):
    n_rows, w = t.shape
    return t.reshape(n_rows, w // HEAD_DIM, HEAD_DIM).transpose(1, 0, 2)


def _from_heads(t):
    heads, n_rows, _ = t.shape
    return t.transpose(1, 0, 2).reshape(n_rows, heads * HEAD_DIM)


def _local_step(h0, target, p, tr):
    n_rows, d = h0.shape
    depth = p["norm_mix"].shape[0]
    cos, sin = _rope_tables(n_rows)
    hk_total = d // GDN_DIM
    hv_total = 2 * hk_total
    conv_dim = 4 * hk_total * GDN_DIM
    qw, kw = d, d // GROUP
    saved = []
    h = h0
    mm = functools.partial(tr.mm, True)
    for i in range(depth):
        kind, j = i % 3, i // 3
        s = {"h": h}
        hn = _rms_fwd(h, p["norm_mix"][i], "rms_fwd")
        s["hn"] = hn
        if kind == 0:
            pre = mm(hn, tr.weight("conv_w_pw1", j), "nn", F32, "mm_pw1")
            u1 = tr.call(True, _us_rows(pre), lambda r: _glu_fwd(pre, p["conv_b_pw1"][j], "glu_fwd", ride=r))
            c = tr.call(True, _us_dwconv(n_rows, d, CONV_KERNEL),
                        lambda r: _dwconv_fwd(u1, p["conv_w_dw"][j], p["conv_b_dw"][j], d, "dwconv31_fwd", ride=r))
            sv = _ln_silu_fwd(c, p["conv_ln_g"][j], p["conv_ln_b"][j], "ln_silu_fwd")
            h = mm(sv, tr.weight("conv_w_pw2", j), "nn", F32, "mm_d_d_res", bias=p["conv_b_pw2"][j], residual=h)
            s.update(pre=pre, u1=u1, c=c, sv=sv)
        elif kind == 1:
            pre = mm(hn, tr.weight("attn_w_qkv", j), "nn", F32, "mm_qkv")
            q, k, v = _qkv_post_fwd(pre, p["attn_b_qkv"][j], cos, sin, qw, kw, "qkv_post_fwd")
            qh, kh, vh = _to_heads(q), _to_heads(k), _to_heads(v)
            o = _from_heads(tr.call(True, _us_attn(qh), lambda r: _attn_fwd(qh, kh, vh, p["attn_sinks"][j], "attn_fwd", ride=r)))
            h = mm(o, tr.weight("attn_w_o", j), "nn", F32, "mm_d_d_res", bias=p["attn_b_o"][j], residual=h)
            s.update(qh=qh, kh=kh, vh=vh, o=o)
        else:
            proj = mm(hn, tr.weight("gdn_w_in", j), "nn", F32, "mm_gdn_in")
            cq = tr.call(True, _us_dwconv(n_rows, conv_dim, GDN_CONV),
                         lambda r: _dwconv_fwd(proj, p["gdn_conv_w"][j], None, conv_dim, "dwconv4_fwd", ride=r))
            vec = _gate_vectors(p["gdn_a_log"][j], p["gdn_dt_bias"][j], hv_total)
            ba_blk = (conv_dim + hv_total * GDN_DIM) // LANES
            (gates,) = _rowwise(lambda r0, bav, av, dv: (_gdn_gates(bav, av, dv, r0, hv_total),), [(proj, LANES, ba_blk)],
                                [vec[0], vec[1]], [(LANES, F32)], [], "gdn_gates_fwd")
            y, states = _gdn_fwd(cq, proj, gates, p["gdn_norm_w"][j], hk_total, "gdn_fwd")
            h = mm(y, tr.weight("gdn_w_out", j), "nn", F32, "mm_gdn_out_res", residual=h)
            s.update(proj=proj, cq=cq, gates=gates, y=y, states=states)
        s["h1"] = h
        hn2 = _rms_fwd(h, p["norm_ffn"][i], "rms_fwd")
        w_gate, w_up = tr.weight("ffn_w_gate", i), tr.weight("ffn_w_up", i)
        gate, up, a = tr.call(True, 2 * _us_mm(n_rows, w_gate.shape[1], d), lambda r: _mmx(
            [(hn2, w_gate, 0), (hn2, w_up, 1)], "nn", "mm_ffn_swiglu", [BF16, BF16, BF16],
            lambda accs: [accs[0], accs[1], jax.nn.silu(accs[0]) * accs[1]], n_acc=2, ride=r))
        h = mm(a, tr.weight("ffn_w_down", i), "nn", F32, "mm_ffn_down_res", residual=h)
        s.update(hn2=hn2, gate=gate, up=up, a=a)
        saved.append(s)

    loss, dh, dhb, g_final = _final_loss(h, p["norm_final"], target, "final_loss")
    g = {k: [None] * v.shape[0] for k, v in p.items() if k not in ("norm_final", "meta_tokens")}
    g["norm_final"] = g_final
    mm = functools.partial(tr.mm, False)
    for i in reversed(range(depth)):
        kind, j = i % 3, i // 3
        s = saved[i]
        tr.give("ffn_w_down", i, mm(s["a"], dhb, "tn", BF16, "mm_dw_down"))
        w_gate, w_up, w_down = tr.weight("ffn_w_gate", i), tr.weight("ffn_w_up", i), tr.weight("ffn_w_down", i)

        def swiglu_bwd(accs, gate, up):
            _, vjp = jax.vjp(lambda gv, uv: jax.nn.silu(gv) * uv, gate.astype(F32), up.astype(F32))
            return list(vjp(accs[0]))

        dgate, dup = tr.call(False, _us_mm(n_rows, w_gate.shape[1], d), lambda r: _mmx(
            [(dhb, w_down, 0)], "nt", "mm_da_swiglu", [BF16, BF16], swiglu_bwd, extras=[s["gate"], s["up"]], ride=r))
        g_gate, g_up = tr.call(False, 2 * _us_mm(d, w_gate.shape[1], n_rows), lambda r: _mmx(
            [(s["hn2"], dgate, 0), (s["hn2"], dup, 1)], "tn", "mm_dw_gate_up", [BF16, BF16], lambda accs: accs, n_acc=2, ride=r))
        tr.give("ffn_w_gate", i, g_gate)
        tr.give("ffn_w_up", i, g_up)
        (dhn2,) = tr.call(False, 2 * _us_mm(n_rows, d, w_gate.shape[1]), lambda r: _mmx(
            [(dgate, w_gate, 0), (dup, w_up, 0)], "nt", "mm_dhn2", [F32], lambda accs: accs, ride=r))
        dh, dhb, g["norm_ffn"][i] = _rms_bwd(s["h1"], p["norm_ffn"][i], dhn2, dh, "rms_bwd")
        if kind == 0:
            g["conv_b_pw2"][j] = _colsum_rows(dh, "colsum_d")
            tr.give("conv_w_pw2", j, mm(s["sv"], dhb, "tn", BF16, "mm_dw_d_d"))
            dsv = mm(dhb, tr.weight("conv_w_pw2", j), "nt", F32, "mm_dx_d_d")
            dc, g["conv_ln_g"][j], g["conv_ln_b"][j] = _ln_silu_bwd(s["c"], p["conv_ln_g"][j], p["conv_ln_b"][j], dsv, "ln_silu_bwd")
            du1, g["conv_w_dw"][j], g["conv_b_dw"][j] = tr.call(
                False, 2 * _us_dwconv(n_rows, d, CONV_KERNEL),
                lambda r: _dwconv_bwd(s["u1"], p["conv_w_dw"][j], dc, d, "dwconv31_bwd", ride=r))
            dpre, g["conv_b_pw1"][j] = _glu_bwd(s["pre"], p["conv_b_pw1"][j], du1, "glu_bwd")
            tr.give("conv_w_pw1", j, mm(s["hn"], dpre, "tn", BF16, "mm_dw_pw1"))
            dhn = mm(dpre, tr.weight("conv_w_pw1", j), "nt", F32, "mm_dx_pw1")
        elif kind == 1:
            g["attn_b_o"][j] = _colsum_rows(dh, "colsum_d")
            tr.give("attn_w_o", j, mm(s["o"], dhb, "tn", BF16, "mm_dw_d_d"))
            do = _to_heads(mm(dhb, tr.weight("attn_w_o", j), "nt", BF16, "mm_dx_d_d_bf16"))
            dq, dkp, dkc, dvp, dvc, dsink = _attn_bwd(s["qh"], s["kh"], s["vh"], p["attn_sinks"][j], do, "attn_bwd")
            g["attn_sinks"][j] = dsink.reshape(-1)
            kvh = kw // HEAD_DIM
            dk = _shift_add(dkc, dkp, "attn_shift_add").reshape(kvh, n_rows, HEAD_DIM)
            dv = _shift_add(dvc, dvp, "attn_shift_add").reshape(kvh, n_rows, HEAD_DIM)
            dpre, g["attn_b_qkv"][j] = _qkv_post_bwd(_from_heads(dq), _from_heads(dk), _from_heads(dv), cos, sin, "qkv_post_bwd")
            tr.give("attn_w_qkv", j, mm(s["hn"], dpre, "tn", BF16, "mm_dw_qkv"))
            dhn = mm(dpre, tr.weight("attn_w_qkv", j), "nt", F32, "mm_dx_qkv")
        else:
            tr.give("gdn_w_out", j, mm(s["y"], dhb, "tn", BF16, "mm_dw_gdn_out"))
            dy = mm(dhb, tr.weight("gdn_w_out", j), "nt", BF16, "mm_dx_gdn_out")
            dcq, dz, dgates, g_nw = _gdn_bwd(s["cq"], s["proj"], s["gates"], p["gdn_norm_w"][j], s["states"], dy, hk_total, "gdn_bwd")
            g["gdn_norm_w"][j] = g_nw[0]
            vec = _gate_vectors(p["gdn_a_log"][j], p["gdn_dt_bias"][j], hv_total)
            ba_blk = (conv_dim + hv_total * GDN_DIM) // LANES

            def gates_bwd(r0, bav, dgv, av, dv):
                _, vjp = jax.vjp(functools.partial(_gdn_gates, r0=r0, hv=hv_total), bav, av, dv)
                return vjp(dgv)

            dba, d_alog, d_dt = _rowwise(gates_bwd, [(s["proj"], LANES, ba_blk), dgates], [vec[0], vec[1]], [(LANES, BF16)],
                                         [(1, LANES), (1, LANES)], "gdn_gates_bwd")
            g["gdn_a_log"][j] = d_alog[0, hv_total:2 * hv_total]
            g["gdn_dt_bias"][j] = d_dt[0, hv_total:2 * hv_total]
            dconv_in, g["gdn_conv_w"][j], _ = tr.call(
                False, 2 * _us_dwconv(n_rows, conv_dim, GDN_CONV),
                lambda r: _dwconv_bwd(s["proj"], p["gdn_conv_w"][j], dcq, conv_dim, "dwconv4_bwd", ride=r))
            w_in = tr.weight("gdn_w_in", j)
            pad = jnp.zeros((n_rows, w_in.shape[1] - conv_dim - hv_total * GDN_DIM - LANES), BF16)
            dproj = jnp.concatenate([dconv_in.astype(BF16), dz, dba, pad], axis=1)
            tr.give("gdn_w_in", j, mm(s["hn"], dproj, "tn", BF16, "mm_dw_gdn_in"))
            dhn = mm(dproj, w_in, "nt", F32, "mm_dx_gdn_in")
        dh, dhb, g["norm_mix"][i] = _rms_bwd(s["h"], p["norm_mix"][i], dhn, dh, "rms_bwd")
    g = {k: (jnp.stack(v) if isinstance(v, list) else v) for k, v in g.items()}
    return loss, dh, g


def _us_mm(m, n, k):
    return 2.0 * m * n * k / 8.0e8


def _us_rows(t):
    return t.shape[0] * t.shape[1] / 8.0e5


def _us_dwconv(n_rows, c, taps):
    return n_rows * c * (taps + 8) / 2.0e6


def _us_attn(qh):
    return qh.shape[0] * qh.shape[1] / 500.0


def _gate_vectors(a_log, dt_bias, hv):
    def place(t):
        return jnp.concatenate([jnp.zeros((hv,), F32), t, jnp.zeros((LANES - 2 * hv,), F32)]).reshape(1, LANES)

    return place(a_log), place(dt_bias)


GDN_IN_ALIGN = 512


def _gdn_group(w, hk):
    lead, kw = w.shape[:-1], hk * GDN_DIM
    q = w[..., :kw].reshape(*lead, hk, 1, GDN_DIM)
    k = w[..., kw:2 * kw].reshape(*lead, hk, 1, GDN_DIM)
    v = w[..., 2 * kw:4 * kw].reshape(*lead, hk, 2, GDN_DIM)
    return jnp.concatenate([q, k, v], axis=-2).reshape(*lead, 4 * kw)


def _gdn_ungroup(w, hk):
    lead, kw = w.shape[:-1], hk * GDN_DIM
    t = w.reshape(*lead, hk, 4, GDN_DIM)
    return jnp.concatenate([t[..., 0, :].reshape(*lead, kw), t[..., 1, :].reshape(*lead, kw),
                            t[..., 2:, :].reshape(*lead, 2 * kw)], axis=-1)


def _gdn_in_layout(w, hk):
    conv_dim = 4 * hk * GDN_DIM
    width = -(-w.shape[-1] // GDN_IN_ALIGN) * GDN_IN_ALIGN
    pad = jnp.zeros(w.shape[:-1] + (width - w.shape[-1],), w.dtype)
    return jnp.concatenate([_gdn_group(w[..., :conv_dim], hk), w[..., conv_dim:], pad], axis=-1)


def _gdn_in_natural(w, hk, in_width):
    conv_dim = 4 * hk * GDN_DIM
    return jnp.concatenate([_gdn_ungroup(w[..., :conv_dim], hk), w[..., conv_dim:in_width]], axis=-1)


def _exchange(srcs, gather, name):
    n_src = len(srcs)

    def body(*refs):
        sems = refs[2 * n_src:]
        copies = [_exchange_copies(refs[t], refs[n_src + t], *sems[3 * t:3 * t + 3], gather=gather) for t in range(n_src)]
        for cps in copies:
            _exchange_start(cps)
        for cps in copies:
            _exchange_wait(cps)

    any_spec = pl.BlockSpec(memory_space=pl.ANY)
    return pl.pallas_call(
        body, name=name, out_shape=[_exchange_out(s, gather) for s in srcs], in_specs=[any_spec] * n_src,
        out_specs=[any_spec] * n_src, scratch_shapes=[s for src in srcs for s in _exchange_sems(src)],
    )(*srcs)


BIG_COL = ("conv_w_pw1", "attn_w_qkv", "gdn_w_in", "ffn_w_gate", "ffn_w_up")
BIG_ROW = ("conv_w_pw2", "attn_w_o", "gdn_w_out", "ffn_w_down")
EXCHANGE_US_PER_BYTE = 11.4e-6
RIDE_PART_US = 150.0


class _Traffic:
    def __init__(self, shards, hk):
        self.shards, self.hk = shards, hk
        self.in_width = N_DEV * shards["gdn_w_in"].shape[-1]
        self.parts_of = {}
        for k, s in shards.items():
            parts = 1
            while (N_DEV * s.shape[1] * s.shape[2] * 2 * EXCHANGE_US_PER_BYTE / parts > RIDE_PART_US
                   and s.shape[1] % (32 * parts) == 0):
                parts *= 2
            self.parts_of[k] = parts
        depth = shards["ffn_w_down"].shape[0]
        order = []
        for i in range(depth):
            j = i // 3
            order += [[("conv_w_pw1", j), ("conv_w_pw2", j)], [("attn_w_qkv", j), ("attn_w_o", j)],
                      [("gdn_w_in", j), ("gdn_w_out", j)]][i % 3]
            order += [("ffn_w_gate", i), ("ffn_w_up", i), ("ffn_w_down", i)]
        self.wanted = [(k, i, part) for k, i in order for part in range(self.parts_of[k])]
        self.arrived = {}
        self.ready = {}
        self.owed = []
        self.received = {}

    def _us(self, k):
        s = self.shards[k]
        return N_DEV * s.shape[1] * s.shape[2] * 2 * EXCHANGE_US_PER_BYTE / self.parts_of[k]

    def _shard_part(self, item):
        k, i, part = item
        rows = self.shards[k].shape[1] // self.parts_of[k]
        return self.shards[k][i:i + 1, part * rows:(part + 1) * rows]

    def _pick(self, queue, us_of, room):
        taken = []
        while queue and room >= 0.5 * us_of(queue[0]):
            room -= us_of(queue[0])
            taken.append(queue.pop(0))
        return taken

    def _run(self, forward, room, fn):
        if forward:
            taken = self._pick(self.wanted, lambda it: self._us(it[0]), room)
            items = [(self._shard_part(it), True) for it in taken]
        else:
            taken = self._pick(self.owed, lambda it: self._us(it[0][0]), room)
            items = [(pieces, False) for _, pieces in taken]
        if not taken:
            return fn(None)
        ride = _Ride(items)
        out = fn(ride)
        for it, got in zip(taken, ride.outs):
            if forward:
                self.arrived[it] = got[0]
            else:
                self.received[it[0]] = got
        return out

    def call(self, forward, room, fn):
        return self._run(forward, room, fn)

    def mm(self, forward, a, b, mode, out_dtype, name, **kw):
        m = a.shape[1] if mode == "tn" else a.shape[0]
        k = a.shape[0] if mode == "tn" else a.shape[1]
        n = b.shape[0] if mode == "nt" else b.shape[1]
        return self._run(forward, _us_mm(m, n, k), lambda ride: _mm(a, b, mode, out_dtype, name, ride=ride, **kw))

    def _natural(self, k, i):
        parts = self.parts_of[k]
        missing = [(k, i, part) for part in range(parts) if (k, i, part) not in self.arrived]
        if missing:
            for it in missing:
                self.wanted.remove(it)
            got = _exchange([self._shard_part(it) for it in missing], True, "gather_weights")
            for it, t in zip(missing, got):
                self.arrived[it] = t[0]
        got = [self.arrived[(k, i, part)] for part in range(parts)]
        rows, c = got[0].shape[1], got[0].shape[2]
        if k in BIG_COL:
            return jnp.concatenate([t.transpose(1, 0, 2).reshape(rows, N_DEV * c) for t in got], axis=0)
        return jnp.stack(got, axis=1).reshape(N_DEV * parts * rows, c)

    def weight(self, k, i):
        if (k, i) not in self.ready:
            w = self._natural(k, i)
            self.ready[(k, i)] = _gdn_in_layout(w, self.hk) if k == "gdn_w_in" else w
        return self.ready[(k, i)]

    def give(self, k, i, grad):
        if k == "gdn_w_in":
            grad = _gdn_in_natural(grad, self.hk, self.in_width)
        r, c = self.shards[k].shape[1:]
        pieces = grad.reshape(r, N_DEV, c).transpose(1, 0, 2) if k in BIG_COL else grad.reshape(N_DEV, r, c)
        rows = r // self.parts_of[k]
        for part in range(self.parts_of[k]):
            self.owed.append(((k, i, part), pieces[None, :, part * rows:(part + 1) * rows]))

    def gradient_parts(self, k):
        if self.owed:
            got = _exchange([pieces for _, pieces in self.owed], False, "scatter_grads")
            for (it, _), t in zip(self.owed, got):
                self.received[it] = t
            self.owed = []
        layers = self.shards[k].shape[0]
        return jnp.concatenate([jnp.concatenate([self.received[(k, i, part)] for part in range(self.parts_of[k])], axis=2)
                                for i in range(layers)], axis=0)


def _cast_bf16(w, name):
    n, r, c = w.shape
    tr = _tile(r, max(16, (1 << 20) // c), 16)

    def body(w_ref, o_ref):
        o_ref[...] = w_ref[...].astype(BF16)

    return pl.pallas_call(
        body, name=name, grid=(n, r // tr), in_specs=[pl.BlockSpec((1, tr, c), lambda l, i: (l, i, 0))],
        out_specs=pl.BlockSpec((1, tr, c), lambda l, i: (l, i, 0)), out_shape=jax.ShapeDtypeStruct(w.shape, BF16),
        compiler_params=_cparams(("parallel", "parallel")),
    )(w)


def _adamw(w, g, m, v):
    m = ADAM_B1 * m + (1.0 - ADAM_B1) * g
    v = ADAM_B2 * v + (1.0 - ADAM_B2) * jnp.square(g)
    m_hat = m / (1.0 - ADAM_B1 ** ADAM_STEP)
    v_hat = v / (1.0 - ADAM_B2 ** ADAM_STEP)
    delta = -ADAM_LR * (m_hat / (jnp.sqrt(v_hat) + ADAM_EPS) + ADAM_WD * w)
    return delta, m, v


def _sum8_adam(parts, w, m, v, name):
    n, _, r, c = parts.shape
    tr = _tile(r, max(16, (1 << 18) // c), 16)
    blk = pl.BlockSpec((1, tr, c), lambda l, i: (l, i, 0))

    def body(p_ref, w_ref, m_ref, v_ref, g_ref, d_ref, mo_ref, vo_ref):
        g = p_ref[0, 0].astype(F32)
        for s in range(1, N_DEV):
            g = g + p_ref[0, s].astype(F32)
        delta, m2, v2 = _adamw(w_ref[0], g, m_ref[0], v_ref[0])
        g_ref[0], d_ref[0], mo_ref[0], vo_ref[0] = g, delta, m2, v2

    shp = jax.ShapeDtypeStruct(w.shape, F32)
    return pl.pallas_call(
        body, name=name, grid=(n, r // tr),
        in_specs=[pl.BlockSpec((1, N_DEV, tr, c), lambda l, i: (l, 0, i, 0)), blk, blk, blk],
        out_specs=[blk, blk, blk, blk], out_shape=[shp, shp, shp, shp],
        compiler_params=_cparams(("parallel", "parallel")),
    )(parts, w, m, v)


PACK_ROWS = 8


def _pack(arrays):
    flat = jnp.concatenate([a.reshape(-1).astype(F32) for a in arrays])
    unit = PACK_ROWS * LANES
    total = -(-flat.shape[0] // unit) * unit
    return jnp.concatenate([flat, jnp.zeros((total - flat.shape[0],), F32)]).reshape(-1, LANES)


def _unpack(packed, shapes):
    flat, out, pos = packed.reshape(-1), [], 0
    for s in shapes:
        size = math.prod(s)
        out.append(flat[pos:pos + size].reshape(s))
        pos += size
    return out


SMALL_SHARDED =("meta_tokens", "conv_b_pw1", "conv_w_dw", "conv_b_dw", "conv_ln_g", "conv_ln_b", "conv_b_pw2", "gdn_conv_w")
REPLICATED = ("norm_mix", "norm_ffn", "norm_final", "attn_b_qkv", "attn_sinks", "attn_b_o", "gdn_a_log", "gdn_dt_bias", "gdn_norm_w")
WEIGHTS = ("meta_tokens", "norm_mix", "norm_ffn", "norm_final", "conv_w_pw1", "conv_b_pw1", "conv_w_dw", "conv_b_dw", "conv_ln_g",
           "conv_ln_b", "conv_w_pw2", "conv_b_pw2", "attn_w_qkv", "attn_b_qkv", "attn_sinks", "attn_w_o", "attn_b_o", "gdn_w_in",
           "gdn_conv_w", "gdn_a_log", "gdn_dt_bias", "gdn_norm_w", "gdn_w_out", "ffn_w_gate", "ffn_w_up", "ffn_w_down")


def kernel(x, meta_tokens, norm_mix, norm_ffn, norm_final, conv_w_pw1, conv_b_pw1, conv_w_dw, conv_b_dw, conv_ln_g, conv_ln_b, conv_w_pw2, conv_b_pw2, attn_w_qkv, attn_b_qkv, attn_sinks, attn_w_o, attn_b_o, gdn_w_in, gdn_conv_w, gdn_a_log, gdn_dt_bias, gdn_norm_w, gdn_w_out, ffn_w_gate, ffn_w_up, ffn_w_down, loss_target, m_meta_tokens, m_norm_mix, m_norm_ffn, m_norm_final, m_conv_w_pw1, m_conv_b_pw1, m_conv_w_dw, m_conv_b_dw, m_conv_ln_g, m_conv_ln_b, m_conv_w_pw2, m_conv_b_pw2, m_attn_w_qkv, m_attn_b_qkv, m_attn_sinks, m_attn_w_o, m_attn_b_o, m_gdn_w_in, m_gdn_conv_w, m_gdn_a_log, m_gdn_dt_bias, m_gdn_norm_w, m_gdn_w_out, m_ffn_w_gate, m_ffn_w_up, m_ffn_w_down, v_meta_tokens, v_norm_mix, v_norm_ffn, v_norm_final, v_conv_w_pw1, v_conv_b_pw1, v_conv_w_dw, v_conv_b_dw, v_conv_ln_g, v_conv_ln_b, v_conv_w_pw2, v_conv_b_pw2, v_attn_w_qkv, v_attn_b_qkv, v_attn_sinks, v_attn_w_o, v_attn_b_o, v_gdn_w_in, v_gdn_conv_w, v_gdn_a_log, v_gdn_dt_bias, v_gdn_norm_w, v_gdn_w_out, v_ffn_w_gate, v_ffn_w_up, v_ffn_w_down):
    a = dict(locals())
    me = 4 * lax.axis_index("x") + 2 * lax.axis_index("y") + lax.axis_index("c")
    d = x.shape[-1]

    hk = d // GDN_DIM
    full = {k: a[k] for k in REPLICATED}
    shard_shapes = [a[k].shape for k in SMALL_SHARDED]
    (got,) = _exchange([_pack([a[k] for k in SMALL_SHARDED])[None]], True, "gather_small")
    per_dev = [_unpack(got[0, s], shard_shapes) for s in range(N_DEV)]
    for i, k in enumerate(SMALL_SHARDED):
        st = jnp.stack([per_dev[s][i] for s in range(N_DEV)], axis=-2)
        full[k] = st.reshape(st.shape[:-2] + (N_DEV * st.shape[-1],))
    traffic = _Traffic({k: _cast_bf16(a[k], "cast_bf16") for k in BIG_COL + BIG_ROW}, hk)

    h0 = jnp.concatenate([jnp.zeros((PAD_LEN, d), F32), full["meta_tokens"], x[0]], axis=0)
    target = jnp.concatenate([jnp.zeros((FRONT, d), F32), loss_target[0]], axis=0)
    loss, dh0, g = _local_step(h0, target, {**full, "gdn_conv_w": _gdn_group(full["gdn_conv_w"], hk)}, traffic)
    g["gdn_conv_w"] = _gdn_ungroup(g["gdn_conv_w"], hk)
    g["meta_tokens"] = dh0[PAD_LEN:FRONT]
    loss = lax.psum(loss, AXES)
    grad_x = dh0[FRONT:][None]

    grads, deltas, new_m, new_v = {}, {}, {}, {}
    for k in BIG_COL + BIG_ROW:
        grads[k], deltas[k], new_m[k], new_v[k] = _sum8_adam(traffic.gradient_parts(k), a[k], a["m_" + k], a["v_" + k], "sum8_adamw")

    small = SMALL_SHARDED + REPLICATED
    full_shapes = [full[k].shape for k in small]
    (got,) = _exchange([_pack([g[k] for k in small])[None]], True, "gather_small_grads")
    got = got[0]
    (total,) = _rowwise(lambda r0, *t: (functools.reduce(lambda p, q: p + q, t),), [got[s] for s in range(N_DEV)], [],
                        [(LANES, F32)], [], "sum8_small", tm=got.shape[1])
    for k, t in zip(small, _unpack(total, full_shapes)):
        if k in SMALL_SHARDED:
            c = a[k].shape[-1]
            t = lax.dynamic_index_in_dim(t.reshape(t.shape[:-1] + (N_DEV, c)), me, axis=t.ndim - 1, keepdims=False)
        grads[k] = t
    shapes = [a[k].shape for k in small]
    packed = [_pack([src[k] for k in small]) for src in (grads, a, {k: a["m_" + k] for k in small}, {k: a["v_" + k] for k in small})]

    def small_adam(r0, gv, wv, mv, vv):
        return _adamw(wv, gv, mv, vv)

    outs = _rowwise(small_adam, packed, [], [(LANES, F32)] * 3, [], "adamw_small", tm=packed[0].shape[0])
    for dst, o in zip((deltas, new_m, new_v), outs):
        for k, t in zip(small, _unpack(o, shapes)):
            dst[k] = t

    return (loss, grad_x, *[grads[k] for k in WEIGHTS], *[deltas[k] for k in WEIGHTS], *[new_m[k] for k in WEIGHTS],
            *[new_v[k] for k in WEIGHTS])
```

```python
import functools
import math

import jax
import jax.numpy as jnp
from jax import lax
from jax.experimental import pallas as pl
from jax.experimental.pallas import tpu as pltpu

F32 = jnp.float32
BF16 = jnp.bfloat16

AXES = ("x", "y", "c")
N_DEV = 8

N_META = 16
FRONT = 128
PAD_LEN = FRONT - N_META
NORM_EPS = 1e-6
LN_EPS = 1e-5
NEG_INF = -1e30
CONV_KERNEL = 31
HEAD_DIM = 64
GROUP = 8
BLOCK = 128
ROPE_THETA = 10000.0
GDN_DIM = 128
GDN_CONV = 4
GDN_CHUNK = 64
ADAM_LR, ADAM_B1, ADAM_B2, ADAM_EPS, ADAM_WD, ADAM_STEP = 0.001, 0.9, 0.999, 1e-08, 0.01, 10

VMEM_LIMIT_BYTES = 52 * 1024 * 1024
LANES = 128
MM_TILE_PREF = 1408
MM_VMEM_BUDGET_BYTES = 40 * 1024 * 1024
MM_CHUNK = 256
ROW_TILE_BUDGET_BYTES = 24 * 1024 * 1024
CONV_HALO = 32


def _tile(n, pref, align=128):
    best = None
    for t in range(align, min(n, pref) + 1, align):
        if n % t == 0:
            best = t
    return best if best is not None else n


def _cparams(sem):
    return pltpu.CompilerParams(dimension_semantics=sem, vmem_limit_bytes=VMEM_LIMIT_BYTES)


def _exchange_copies(src_ref, out_ref, send_sems, recv_sems, local_sems, gather):
    n = src_ref.shape[0]
    x, y, c = lax.axis_index("x"), lax.axis_index("y"), lax.axis_index("c")
    me = 4 * x + 2 * y + c
    remote, local = [], []
    for l in range(n):
        local.append(pltpu.make_async_copy(src_ref.at[l] if gather else src_ref.at[l, me], out_ref.at[l, me], local_sems.at[l]))
        for k in range(1, N_DEV):
            px = 1 - x if k & 4 else x
            py = 1 - y if k & 2 else y
            pc = 1 - c if k & 1 else c
            peer = 4 * px + 2 * py + pc
            remote.append(pltpu.make_async_remote_copy(
                src_ref=src_ref.at[l] if gather else src_ref.at[l, peer], dst_ref=out_ref.at[l, me],
                send_sem=send_sems.at[l, k - 1], recv_sem=recv_sems.at[l, k - 1],
                device_id=(px, py, pc), device_id_type=pl.DeviceIdType.MESH))
    return remote, local


def _exchange_start(copies):
    remote, local = copies
    for cp in local + remote:
        cp.start()


def _exchange_wait(copies):
    remote, local = copies
    for cp in remote:
        cp.wait_send()
    for cp in remote:
        cp.wait_recv()
    for cp in local:
        cp.wait()


def _exchange_out(src, gather):
    return jax.ShapeDtypeStruct((src.shape[0], N_DEV) + src.shape[-2:], src.dtype)


def _exchange_sems(src):
    n = src.shape[0]
    return [pltpu.SemaphoreType.DMA((n, N_DEV - 1)), pltpu.SemaphoreType.DMA((n, N_DEV - 1)), pltpu.SemaphoreType.DMA((n,))]


class _Ride:
    def __init__(self, items):
        self.items = items
        self.outs = None


def _pcall(body, args, *, name, grid, in_specs, out_specs, out_shape, sem, scratch_shapes=(), ride=None):
    if ride is None:
        return pl.pallas_call(body, name=name, grid=grid, in_specs=in_specs, out_specs=out_specs, out_shape=out_shape,
                              scratch_shapes=list(scratch_shapes), compiler_params=_cparams(sem))(*args)
    n_in, n_out, n_scr, n_ride = len(in_specs), len(out_specs), len(scratch_shapes), len(ride.items)
    any_spec = pl.BlockSpec(memory_space=pl.ANY)

    def with_ride(*refs):
        pos = 0
        ins, pos = refs[pos:pos + n_in], pos + n_in
        srcs, pos = refs[pos:pos + n_ride], pos + n_ride
        outs, pos = refs[pos:pos + n_out], pos + n_out
        dsts, pos = refs[pos:pos + n_ride], pos + n_ride
        scr, pos = refs[pos:pos + n_scr], pos + n_scr
        sems = refs[pos:]
        first = functools.reduce(lambda p, q: p & q, [pl.program_id(d) == 0 for d in range(len(grid))])
        last = functools.reduce(lambda p, q: p & q, [pl.program_id(d) == grid[d] - 1 for d in range(len(grid))])

        def copies():
            return [_exchange_copies(srcs[t], dsts[t], *sems[3 * t:3 * t + 3], gather=ride.items[t][1]) for t in range(n_ride)]

        @pl.when(first)
        def _():
            for cps in copies():
                _exchange_start(cps)

        body(*ins, *outs, *scr)

        @pl.when(last)
        def _():
            for cps in copies():
                _exchange_wait(cps)

    res = pl.pallas_call(
        with_ride, name=name, grid=grid, in_specs=list(in_specs) + [any_spec] * n_ride,
        out_specs=list(out_specs) + [any_spec] * n_ride,
        out_shape=list(out_shape) + [_exchange_out(s, g) for s, g in ride.items],
        scratch_shapes=list(scratch_shapes) + [s for src, _ in ride.items for s in _exchange_sems(src)],
        compiler_params=_cparams(("arbitrary",) * len(grid)),
    )(*args, *[s for s, _ in ride.items])
    ride.outs = list(res[n_out:])
    return list(res[:n_out])


def _mm_shape(a, b, mode):
    if mode == "nn":
        (m, k), (k2, n) = a.shape, b.shape
    elif mode == "nt":
        (m, k), (n, k2) = a.shape, b.shape
    else:
        (k, m), (k2, n) = a.shape, b.shape
    assert k == k2, (a.shape, b.shape, mode)
    return m, n, k


def _mmx(pairs, mode, name, out_dtypes, epilogue, extras=(), n_acc=1, ride=None):
    m, n, k = _mm_shape(pairs[0][0], pairs[0][1], mode)
    tm = _tile(m, MM_TILE_PREF if mode == "tn" else 640)
    tk = _tile(k, 2048)
    nk = k // tk

    def vmem_bytes(tn):
        total = n_acc * tm * tn * 4
        for a, b, _ in pairs:
            total += 2 * (tm * tk * a.dtype.itemsize + tk * tn * b.dtype.itemsize)
        total += 2 * sum((1 if e.shape[0] == 1 and m != 1 else tm) * tn * e.dtype.itemsize for e in extras)
        return total + 2 * sum(tm * tn * jnp.dtype(dt).itemsize for dt in out_dtypes)

    tn = _tile(n, MM_TILE_PREF)
    while vmem_bytes(tn) > MM_VMEM_BUDGET_BYTES and tn > LANES:
        tn = _tile(n, tn - LANES)
    dims = {"nn": (((1,), (0,)), ((), ())), "nt": (((1,), (1,)), ((), ())), "tn": (((0,), (0,)), ((), ()))}[mode]
    a_spec = pl.BlockSpec((tk, tm), lambda i, j, kk: (kk, i)) if mode == "tn" else pl.BlockSpec((tm, tk), lambda i, j, kk: (i, kk))
    b_spec = pl.BlockSpec((tn, tk), lambda i, j, kk: (j, kk)) if mode == "nt" else pl.BlockSpec((tk, tn), lambda i, j, kk: (kk, j))
    ins, specs, where = [], [], []
    for a, b, _ in pairs:
        assert _mm_shape(a, b, mode) == (m, n, k)
        ia = next((t for t, x in enumerate(ins) if x is a), None)
        if ia is None:
            ins.append(a)
            specs.append(a_spec)
            ia = len(ins) - 1
        ins.append(b)
        specs.append(b_spec)
        where.append((ia, len(ins) - 1))
    n_ops = len(ins)
    for e in extras:
        ins.append(e)
        specs.append(pl.BlockSpec((1, tn), lambda i, j, kk: (0, j)) if e.shape[0] == 1 and m != 1 else
                     pl.BlockSpec((tm, tn), lambda i, j, kk: (i, j)))
    n_ex, n_out = len(extras), len(out_dtypes)

    def body(*refs):
        ex_refs = refs[n_ops:n_ops + n_ex]
        o_refs = refs[n_ops + n_ex:n_ops + n_ex + n_out]
        accs = refs[n_ops + n_ex + n_out:]
        kk = pl.program_id(2)

        @pl.when(kk == 0)
        def _():
            for acc in accs:
                acc[...] = jnp.zeros_like(acc)

        for (ia, ib), (_, _, which) in zip(where, pairs):
            accs[which][...] += lax.dot_general(refs[ia][...].astype(BF16), refs[ib][...].astype(BF16), dims,
                                                preferred_element_type=F32)

        @pl.when(kk == nk - 1)
        def _():
            tiles = epilogue([acc[...] for acc in accs], *[e[...] for e in ex_refs])
            for o_ref, t in zip(o_refs, tiles):
                o_ref[...] = t.astype(o_ref.dtype)

    def body_one_pass(*refs):
        ex_refs = refs[n_ops:n_ops + n_ex]
        o_refs = refs[n_ops + n_ex:n_ops + n_ex + n_out]
        lefts = {ia: refs[ia][...].astype(BF16) for ia, _ in where}
        for c0 in range(0, tn, MM_CHUNK):
            cw = min(MM_CHUNK, tn - c0)
            accs = [None] * n_acc
            for (ia, ib), (_, _, which) in zip(where, pairs):
                right = refs[ib][c0:c0 + cw, :] if mode == "nt" else refs[ib][:, c0:c0 + cw]
                part = lax.dot_general(lefts[ia], right.astype(BF16), dims, preferred_element_type=F32)
                accs[which] = part if accs[which] is None else accs[which] + part
            tiles = epilogue(accs, *[e[:, c0:c0 + cw] for e in ex_refs])
            for o_ref, t in zip(o_refs, tiles):
                o_ref[:, c0:c0 + cw] = t.astype(o_ref.dtype)

    return _pcall(
        body_one_pass if nk == 1 else body, ins, name=name, grid=(m // tm, n // tn, nk), in_specs=specs,
        out_specs=[pl.BlockSpec((tm, tn), lambda i, j, kk: (i, j))] * n_out,
        out_shape=[jax.ShapeDtypeStruct((m, n), dt) for dt in out_dtypes],
        scratch_shapes=[] if nk == 1 else [pltpu.VMEM((tm, tn), F32)] * n_acc,
        sem=("parallel", "parallel", "arbitrary"), ride=ride)


def _mm(a, b, mode, out_dtype, name, bias=None, residual=None, ride=None):
    extras = ([] if bias is None else [bias.reshape(1, -1).astype(F32)]) + ([] if residual is None else [residual])
    (out,) = _mmx([(a, b, 0)], mode, name, [out_dtype], lambda accs, *ex: [functools.reduce(lambda p, q: p + q.astype(F32), ex, accs[0])],
                  extras=extras, ride=ride)
    return out


def _rowwise(fn, rows, consts, out_rows, out_accs, name, tm=None, ride=None):
    rows = [r if isinstance(r, tuple) else (r, r.shape[1], 0) for r in rows]
    n_rows = rows[0][0].shape[0]
    if tm is None:
        row_bytes = sum(w * r.dtype.itemsize for r, w, _ in rows) + sum(w * jnp.dtype(dt).itemsize for w, dt in out_rows)
        tm = _tile(n_rows, max(8, min(640, ROW_TILE_BUDGET_BYTES // (2 * row_bytes))), 8)
    steps = n_rows // tm
    in_specs = [pl.BlockSpec((tm, w), functools.partial(lambda i, c: (i, c), c=cb)) for _, w, cb in rows]
    in_specs += [pl.BlockSpec(c.shape, lambda i: (0, 0)) for c in consts]
    out_specs = [pl.BlockSpec((tm, w), lambda i: (i, 0)) for w, _ in out_rows]
    out_specs += [pl.BlockSpec(s, lambda i: (0, 0)) for s in out_accs]
    out_shape = [jax.ShapeDtypeStruct((n_rows, w), d) for w, d in out_rows]
    out_shape += [jax.ShapeDtypeStruct(s, F32) for s in out_accs]
    n_in, n_or = len(rows) + len(consts), len(out_rows)

    def body(*refs):
        i = pl.program_id(0)
        vals = fn(i * tm, *[r[...] for r in refs[:n_in]])
        outs = refs[n_in:]
        for o_ref, v in zip(outs[:n_or], vals[:n_or]):
            o_ref[...] = v.astype(o_ref.dtype)
        if out_accs:
            @pl.when(i == 0)
            def _():
                for a_ref in outs[n_or:]:
                    a_ref[...] = jnp.zeros_like(a_ref)

            for a_ref, v in zip(outs[n_or:], vals[n_or:]):
                a_ref[...] += v

    return _pcall(body, [r[0] for r in rows] + list(consts), name=name, grid=(steps,), in_specs=in_specs, out_specs=out_specs,
                  out_shape=out_shape, sem=("arbitrary",) if out_accs else ("parallel",), ride=ride)


def _colsum(v):
    return jnp.sum(v, axis=0, keepdims=True)


def _rms(h, w):
    return h * lax.rsqrt(jnp.mean(h * h, axis=-1, keepdims=True) + NORM_EPS) * w


def _rms_fwd(h, w, name):
    (hn,) = _rowwise(lambda r0, hv, wv: (_rms(hv, wv),), [h], [w.reshape(1, -1)], [(h.shape[1], BF16)], [], name)
    return hn


def _rms_bwd(h, w, dhn, dh_in, name):
    d = h.shape[1]

    def fn(r0, hv, dv, rv, wv):
        _, vjp = jax.vjp(_rms, hv, wv)
        dh, dw = vjp(dv.astype(F32))
        return dh + rv, dh + rv, dw

    dh, dh_bf16, dw = _rowwise(fn, [h, dhn, dh_in], [w.reshape(1, -1)], [(d, F32), (d, BF16)], [(1, d)], name)
    return dh, dh_bf16, dw[0]


def _glu(p, b):
    t = p + b
    d = t.shape[1] // 2
    return t[:, :d] * jax.nn.sigmoid(t[:, d:])


def _glu_fwd(p, b, name, ride=None):
    (u,) = _rowwise(lambda r0, v, bv: (_glu(v, bv),), [p], [b.reshape(1, -1)], [(p.shape[1] // 2, F32)], [], name, ride=ride)
    return u


def _glu_bwd(p, b, du, name):
    def fn(r0, v, dv, bv):
        _, vjp = jax.vjp(_glu, v, bv)
        dp, db = vjp(dv)
        return dp, db

    dp, db = _rowwise(fn, [p, du], [b.reshape(1, -1)], [(p.shape[1], BF16)], [(1, p.shape[1])], name)
    return dp, db[0]


def _ln_silu(c, g, b):
    mu = jnp.mean(c, axis=-1, keepdims=True)
    xc = c - mu
    var = jnp.mean(xc * xc, axis=-1, keepdims=True)
    return jax.nn.silu(xc * lax.rsqrt(var + LN_EPS) * g + b)


def _ln_silu_fwd(c, g, b, name):
    (s,) = _rowwise(lambda r0, v, gv, bv: (_ln_silu(v, gv, bv),), [c], [g.reshape(1, -1), b.reshape(1, -1)],
                    [(c.shape[1], BF16)], [], name)
    return s


def _ln_silu_bwd(c, g, b, ds, name):
    d = c.shape[1]

    def fn(r0, v, dv, gv, bv):
        _, vjp = jax.vjp(_ln_silu, v, gv, bv)
        return vjp(dv.astype(F32))

    dc, dg, db = _rowwise(fn, [c, ds], [g.reshape(1, -1), b.reshape(1, -1)], [(d, F32)], [(1, d), (1, d)], name)
    return dc, dg[0], db[0]


def _colsum_rows(v, name):
    (s,) = _rowwise(lambda r0, t: (_colsum(t.astype(F32)),), [v], [], [], [(1, v.shape[1])], name)
    return s[0]


def _rot_half(x):
    w = x.shape[1]
    lane = lax.broadcasted_iota(jnp.int32, x.shape, 1)
    lo = (lane % HEAD_DIM) < (HEAD_DIM // 2)
    return jnp.where(lo, -pltpu.roll(x, w - HEAD_DIM // 2, axis=1), pltpu.roll(x, HEAD_DIM // 2, axis=1))


def _qkv_post_fwd(pre, b, cos, sin, qw, kw, name):
    reps = (qw + kw) // LANES

    def fn(r0, pv, cv, sv, bv):
        t = pv + bv
        tq = t[:, :qw + kw]
        y = tq * jnp.tile(cv, (1, reps)) + _rot_half(tq) * jnp.tile(sv, (1, reps))
        return y[:, :qw], y[:, qw:], t[:, qw + kw:]

    return _rowwise(fn, [pre, cos, sin], [b.reshape(1, -1)], [(qw, BF16), (kw, BF16), (kw, BF16)], [], name)


def _qkv_post_bwd(dq, dk, dv, cos, sin, name):
    qw, kw = dq.shape[1], dk.shape[1]
    reps = (qw + kw) // LANES
    width = qw + 2 * kw

    def fn(r0, dqv, dkv, dvv, cv, sv):
        dy = jnp.concatenate([dqv.astype(F32), dkv.astype(F32)], axis=1)
        dt = dy * jnp.tile(cv, (1, reps)) - _rot_half(dy * jnp.tile(sv, (1, reps)))
        dpre = jnp.concatenate([dt, dvv.astype(F32)], axis=1)
        return dpre, _colsum(dpre)

    dpre, db = _rowwise(fn, [dq, dk, dv, cos, sin], [], [(width, BF16)], [(1, width)], name)
    return dpre, db[0]


def _dw_tiles(n_rows, c):
    return _tile(n_rows, 640, 8), _tile(c, 512)


class _RowWindow:
    SUBLANES = 8

    def __init__(self, value):
        self.copies = {0: value}

    def rows(self, off, n):
        q, s = divmod(off, self.SUBLANES)
        if s not in self.copies:
            base = self.copies[0]
            self.copies[s] = pltpu.roll(base, base.shape[0] - s, axis=0)
        return self.copies[s][self.SUBLANES * q:self.SUBLANES * q + n, :]


def _row_mask(r0, n, width):
    row = r0 + lax.broadcasted_iota(jnp.int32, (n, width), 0)
    return row >= PAD_LEN


def _dwconv_fwd(u, w, bias, c, name, ride=None):
    n_rows, taps = u.shape[0], w.shape[0]
    tr, cb = _dw_tiles(n_rows, c)
    kp = -(-taps // 8) * 8
    wp = jnp.concatenate([w.astype(F32), jnp.zeros((kp - taps, c), F32)], axis=0)
    bp = jnp.zeros((1, c), F32) if bias is None else bias.reshape(1, c).astype(F32)

    def body(cur_ref, prev_ref, w_ref, b_ref, o_ref):
        r = pl.program_id(1)
        cur = jnp.where(_row_mask(r * tr, tr, cb), cur_ref[...], 0.0)
        tail = prev_ref[tr - CONV_HALO:, :]
        tail = jnp.where(_row_mask(r * tr - CONV_HALO, CONV_HALO, cb) & (r > 0), tail, 0.0)
        win = _RowWindow(jnp.concatenate([tail, cur], axis=0))
        acc = jnp.zeros((tr, cb), F32) + b_ref[...]
        for k in range(taps):
            acc = acc + w_ref[k:k + 1, :] * win.rows(CONV_HALO - (taps - 1) + k, tr)
        o_ref[...] = acc

    (out,) = _pcall(
        body, [u, u, wp, bp], name=name, grid=(c // cb, n_rows // tr),
        in_specs=[pl.BlockSpec((tr, cb), lambda j, r: (r, j)),
                  pl.BlockSpec((tr, cb), lambda j, r: (jnp.maximum(r - 1, 0), j)),
                  pl.BlockSpec((kp, cb), lambda j, r: (0, j)),
                  pl.BlockSpec((1, cb), lambda j, r: (0, j))],
        out_specs=[pl.BlockSpec((tr, cb), lambda j, r: (r, j))],
        out_shape=[jax.ShapeDtypeStruct((n_rows, c), F32)], sem=("parallel", "parallel"), ride=ride)
    return out


def _dwconv_bwd(u, w, dc, c, name, ride=None):
    n_rows, taps = u.shape[0], w.shape[0]
    tr, cb = _dw_tiles(n_rows, c)
    nr = n_rows // tr
    kp = -(-taps // 8) * 8
    wp = jnp.concatenate([w.astype(F32), jnp.zeros((kp - taps, c), F32)], axis=0)

    def body(cur_ref, prev_ref, d_ref, dnext_ref, w_ref, du_ref, dw_ref, db_ref):
        r = pl.program_id(1)
        cur = jnp.where(_row_mask(r * tr, tr, cb), cur_ref[...], 0.0)
        tail = prev_ref[tr - CONV_HALO:, :]
        tail = jnp.where(_row_mask(r * tr - CONV_HALO, CONV_HALO, cb) & (r > 0), tail, 0.0)
        win_u = _RowWindow(jnp.concatenate([tail, cur], axis=0))
        d = d_ref[...]
        head = jnp.where(r < nr - 1, dnext_ref[:CONV_HALO, :], 0.0)
        win_d = _RowWindow(jnp.concatenate([d, head], axis=0))

        @pl.when(r == 0)
        def _():
            dw_ref[...] = jnp.zeros_like(dw_ref)
            db_ref[...] = jnp.zeros_like(db_ref)

        du = jnp.zeros((tr, cb), F32)
        for k in range(taps):
            du = du + w_ref[k:k + 1, :] * win_d.rows(taps - 1 - k, tr)
            dw_ref[k:k + 1, :] += _colsum(d * win_u.rows(CONV_HALO - (taps - 1) + k, tr))
        du_ref[...] = jnp.where(_row_mask(r * tr, tr, cb), du, 0.0)
        db_ref[...] += _colsum(d)

    du, dw, db = _pcall(
        body, [u, u, dc, dc, wp], name=name, grid=(c // cb, nr),
        in_specs=[pl.BlockSpec((tr, cb), lambda j, r: (r, j)),
                  pl.BlockSpec((tr, cb), lambda j, r: (jnp.maximum(r - 1, 0), j)),
                  pl.BlockSpec((tr, cb), lambda j, r: (r, j)),
                  pl.BlockSpec((tr, cb), lambda j, r: (jnp.minimum(r + 1, nr - 1), j)),
                  pl.BlockSpec((kp, cb), lambda j, r: (0, j))],
        out_specs=[pl.BlockSpec((tr, cb), lambda j, r: (r, j)),
                   pl.BlockSpec((kp, cb), lambda j, r: (0, j)),
                   pl.BlockSpec((1, cb), lambda j, r: (0, j))],
        out_shape=[jax.ShapeDtypeStruct((n_rows, c), F32), jax.ShapeDtypeStruct((kp, c), F32),
                   jax.ShapeDtypeStruct((1, c), F32)], sem=("parallel", "arbitrary"), ride=ride)
    return du, dw[:taps], db[0]


def _attn_block(q, kprev, kcur, vprev, vcur, sink, n):
    qf = q.reshape(GROUP * BLOCK, HEAD_DIM).astype(BF16)
    kb = jnp.concatenate([kprev, kcur], axis=0).astype(BF16)
    vb = jnp.concatenate([vprev, vcur], axis=0).astype(BF16)
    s = lax.dot_general(qf, kb, (((1,), (1,)), ((), ())), preferred_element_type=F32) * (HEAD_DIM ** -0.5)
    s = s.reshape(GROUP, BLOCK, 2 * BLOCK)
    qi = lax.broadcasted_iota(jnp.int32, (BLOCK, 2 * BLOCK), 0)
    kj = lax.broadcasted_iota(jnp.int32, (BLOCK, 2 * BLOCK), 1)
    dist = qi + BLOCK - kj
    allowed = (dist >= 0) & (dist < BLOCK) & ((n - 1) * BLOCK + kj >= PAD_LEN)
    s = jnp.where(allowed[None], s, NEG_INF)
    m = lax.stop_gradient(jnp.maximum(jnp.max(s, axis=-1, keepdims=True), sink))
    e = jnp.exp(s - m)
    p = e / (jnp.sum(e, axis=-1, keepdims=True) + jnp.exp(sink - m))
    o = jnp.dot(p.reshape(GROUP * BLOCK, 2 * BLOCK).astype(BF16), vb, preferred_element_type=F32)
    return o.reshape(GROUP, BLOCK, HEAD_DIM)


def _attn_specs(nb):
    q_spec = pl.BlockSpec((GROUP, BLOCK, HEAD_DIM), lambda g, n: (g, n, 0))
    cur = pl.BlockSpec((1, BLOCK, HEAD_DIM), lambda g, n: (g, n, 0))
    prev = pl.BlockSpec((1, BLOCK, HEAD_DIM), lambda g, n: (g, jnp.maximum(n - 1, 0), 0))
    sink = pl.BlockSpec((1, GROUP, 1, 1), lambda g, n: (g, 0, 0, 0))
    return q_spec, cur, prev, sink


def _attn_fwd(q, k, v, sinks, name, ride=None):
    heads, n_rows, _ = q.shape
    nb = n_rows // BLOCK
    q_spec, cur, prev, sink = _attn_specs(nb)

    def body(q_ref, kp_ref, kc_ref, vp_ref, vc_ref, s_ref, o_ref):
        n = pl.program_id(1)
        o = _attn_block(q_ref[...].astype(F32), kp_ref[0].astype(F32), kc_ref[0].astype(F32), vp_ref[0].astype(F32),
                        vc_ref[0].astype(F32), s_ref[0], n)
        o_ref[...] = o.astype(o_ref.dtype)

    (out,) = _pcall(
        body, [q, k, k, v, v, sinks.reshape(heads // GROUP, GROUP, 1, 1)], name=name, grid=(heads // GROUP, nb),
        in_specs=[q_spec, prev, cur, prev, cur, sink], out_specs=[q_spec], out_shape=[jax.ShapeDtypeStruct(q.shape, BF16)],
        sem=("parallel", "parallel"), ride=ride)
    return out


def _attn_bwd(q, k, v, sinks, do, name):
    heads, n_rows, _ = q.shape
    kvh = heads // GROUP
    nb = n_rows // BLOCK
    q_spec, cur, prev, sink = _attn_specs(nb)
    part = pl.BlockSpec((1, 1, BLOCK, HEAD_DIM), lambda g, n: (g, n, 0, 0))
    part_shape = jax.ShapeDtypeStruct((kvh, nb, BLOCK, HEAD_DIM), F32)

    def body(q_ref, kp_ref, kc_ref, vp_ref, vc_ref, s_ref, do_ref, dq_ref, dkp_ref, dkc_ref, dvp_ref, dvc_ref, ds_ref):
        n = pl.program_id(1)
        f = functools.partial(_attn_block, n=n)
        _, vjp = jax.vjp(f, q_ref[...].astype(F32), kp_ref[0].astype(F32), kc_ref[0].astype(F32), vp_ref[0].astype(F32),
                         vc_ref[0].astype(F32), s_ref[0])
        dq, dkp, dkc, dvp, dvc, ds = vjp(do_ref[...].astype(F32))
        dq_ref[...] = dq.astype(dq_ref.dtype)
        dkp_ref[0, 0], dkc_ref[0, 0], dvp_ref[0, 0], dvc_ref[0, 0] = dkp, dkc, dvp, dvc

        @pl.when(n == 0)
        def _():
            ds_ref[...] = jnp.zeros_like(ds_ref)

        ds_ref[0] += ds

    return pl.pallas_call(
        body, name=name, grid=(kvh, nb), in_specs=[q_spec, prev, cur, prev, cur, sink, q_spec],
        out_specs=[q_spec, part, part, part, part, sink],
        out_shape=[jax.ShapeDtypeStruct(q.shape, BF16), part_shape, part_shape, part_shape, part_shape,
                   jax.ShapeDtypeStruct((kvh, GROUP, 1, 1), F32)],
        compiler_params=_cparams(("parallel", "arbitrary")),
    )(q, k, k, v, v, sinks.reshape(kvh, GROUP, 1, 1), do)


def _shift_add(own, to_prev, name):
    kvh, nb = own.shape[:2]
    blk = (1, 1, BLOCK, HEAD_DIM)

    def body(a_ref, b_ref, o_ref):
        n = pl.program_id(1)
        o_ref[...] = (a_ref[...] + jnp.where(n < nb - 1, b_ref[...], 0.0)).astype(o_ref.dtype)

    return pl.pallas_call(
        body, name=name, grid=(kvh, nb),
        in_specs=[pl.BlockSpec(blk, lambda g, n: (g, n, 0, 0)),
                  pl.BlockSpec(blk, lambda g, n: (g, jnp.minimum(n + 1, nb - 1), 0, 0))],
        out_specs=pl.BlockSpec(blk, lambda g, n: (g, n, 0, 0)),
        out_shape=jax.ShapeDtypeStruct(own.shape, BF16), compiler_params=_cparams(("parallel", "parallel")),
    )(own, to_prev)


def _gdn_gates(ba, alog, dt, r0, hv):
    lane = lax.broadcasted_iota(jnp.int32, ba.shape, 1)
    t = ba + dt
    softplus = jnp.maximum(t, 0.0) + jnp.log(1.0 + jnp.exp(-jnp.abs(t)))
    val = jnp.where(lane < hv, jax.nn.sigmoid(ba), jnp.where(lane < 2 * hv, -jnp.exp(alog) * softplus, 0.0))
    return jnp.where(_row_mask(r0, ba.shape[0], ba.shape[1]), val, 0.0)


def _l2n(x):
    return x * lax.rsqrt(jnp.sum(x * x, axis=-1, keepdims=True) + 1e-6)


_NN = (((2,), (1,)), ((0,), (0,)))
_NT = (((2,), (2,)), ((0,), (0,)))
_TN = (((1,), (1,)), ((0,), (0,)))


def _bdot(a, b, dims=_NN):
    return lax.dot_general(a.astype(BF16), b.astype(BF16), dims, preferred_element_type=F32)


def _dot3(a, b, dims):
    ah, bh = a.astype(BF16), b.astype(BF16)
    al, bl = (a - ah.astype(F32)).astype(BF16), (b - bh.astype(F32)).astype(BF16)

    def d(p, q):
        return lax.dot_general(p, q, dims, preferred_element_type=F32)

    return d(ah, bh) + (d(ah, bl) + d(al, bh))


@jax.custom_vjp
def _pdot(a, b):
    return _dot3(a, b, _NN)


def _pdot_fwd(a, b):
    return _dot3(a, b, _NN), (a, b)


def _pdot_bwd(res, ct):
    a, b = res
    return _bdot(ct, b, _NT), _bdot(a, ct, _TN)


_pdot.defvjp(_pdot_fwd, _pdot_bwd)


def _scan_chunks(x, reverse):
    n, c = x.shape[0], GDN_CHUNK
    row = lax.broadcasted_iota(jnp.int32, x.shape, 0) % c
    s = 1
    while s < c:
        if reverse:
            x = x + jnp.where(row < c - s, pltpu.roll(x, n - s, axis=0), 0.0)
        else:
            x = x + jnp.where(row >= s, pltpu.roll(x, s, axis=0), 0.0)
        s *= 2
    return x


@jax.custom_vjp
def _cumsum_chunks(x):
    return _scan_chunks(x, False)


_cumsum_chunks.defvjp(lambda x: (_scan_chunks(x, False), None), lambda _, ct: (_scan_chunks(ct, True),))


def _gdn_heads(states, qkv, z, gates, norm_w, hk0, hv_total):
    c, h = GDN_CHUNK, states.shape[0]
    g = h // 2

    def cols(src, starts):
        return jnp.stack([src[:, s:s + GDN_DIM] for s in starts])

    q = _l2n(jax.nn.silu(cols(qkv, [4 * GDN_DIM * t for t in range(g)]))) * (GDN_DIM ** -0.5)
    k = _l2n(jax.nn.silu(cols(qkv, [4 * GDN_DIM * t + GDN_DIM for t in range(g)])))
    q, k = jnp.repeat(q, 2, axis=0), jnp.repeat(k, 2, axis=0)
    v = jax.nn.silu(cols(qkv, [4 * GDN_DIM * (t // 2) + (2 + t % 2) * GDN_DIM for t in range(h)]))
    zz = cols(z, [GDN_DIM * t for t in range(h)])
    lane = lax.broadcasted_iota(jnp.int32, gates.shape, 1)

    def col_of(first):
        return jnp.stack([jnp.sum(jnp.where(lane == first + t, gates, 0.0), axis=1, keepdims=True) for t in range(h)])

    beta_col, g_col = col_of(2 * hk0), col_of(hv_total + 2 * hk0)
    i = lax.broadcasted_iota(jnp.int32, (c, c), 0)
    j = lax.broadcasted_iota(jnp.int32, (c, c), 1)
    causal, strict = (i >= j)[None], (i > j)[None]
    gc = _cumsum_chunks(jnp.broadcast_to(g_col, (h, c, GDN_DIM)).reshape(h * c, GDN_DIM)).reshape(h, c, GDN_DIM)
    gc_i = gc[:, :, :c]
    gc_j = jnp.swapaxes(gc_i, 1, 2)
    gc_last = jnp.broadcast_to(gc[:, c - 1:c, :], (h, GDN_DIM, GDN_DIM))
    decay = jnp.where(causal, jnp.exp(jnp.where(causal, gc_i - gc_j, 0.0)), 0.0)
    k_beta = k * beta_col
    lower = jnp.where(strict, _bdot(k_beta, k, _NT) * decay, 0.0)
    eye = (i == j).astype(F32)[None]
    neg = -lower
    inv = eye + neg
    power = neg
    for _ in range(5):
        power = _pdot(power, power)
        inv = _pdot(inv, eye + power)
    sol = _pdot(inv, jnp.concatenate([v * beta_col, k_beta * jnp.exp(gc)], axis=2))
    u, w = sol[:, :, :GDN_DIM], sol[:, :, GDN_DIM:]
    intra = jnp.where(causal, _bdot(q, k, _NT) * decay, 0.0)
    q_dec = q * jnp.exp(gc)
    k_dec = k * jnp.exp(gc_last[:, :c] - gc)
    v_new = u - _bdot(w, states)
    o = _bdot(q_dec, states) + _bdot(intra, v_new)
    new_states = states * jnp.exp(gc_last) + _bdot(k_dec, v_new, _TN)
    y = _rms(o, norm_w) * jax.nn.silu(zz)
    return jnp.concatenate([y[t] for t in range(h)], axis=1), new_states


GDN_KEY_HEADS_PER_STEP = 8


def _key_heads_per_step(hk_total):
    return math.gcd(hk_total, GDN_KEY_HEADS_PER_STEP)


def _gdn_fwd(cq, proj, gates, norm_w, hk_total, name):
    n_rows = cq.shape[0]
    nc, hv_total = n_rows // GDN_CHUNK, 2 * hk_total
    grp = _key_heads_per_step(hk_total)
    heads = 2 * grp
    zblk0 = cq.shape[1] // (heads * GDN_DIM)

    def body(cq_ref, z_ref, g_ref, w_ref, y_ref, save_ref, state):
        n, hg = pl.program_id(0), pl.program_id(1)

        @pl.when(n == 0)
        def _():
            state[pl.ds(heads * hg, heads)] = jnp.zeros((heads, GDN_DIM, GDN_DIM), F32)

        s_in = state[pl.ds(heads * hg, heads)]
        save_ref[0] = s_in
        y, s_out = _gdn_heads(s_in, cq_ref[...], z_ref[...], g_ref[...], w_ref[...], grp * hg, hv_total)
        y_ref[...] = y.astype(y_ref.dtype)
        state[pl.ds(heads * hg, heads)] = s_out

    return pl.pallas_call(
        body, name=name, grid=(nc, hk_total // grp),
        in_specs=[pl.BlockSpec((GDN_CHUNK, 2 * heads * GDN_DIM), lambda n, h: (n, h)),
                  pl.BlockSpec((GDN_CHUNK, heads * GDN_DIM), lambda n, h: (n, zblk0 + h)),
                  pl.BlockSpec((GDN_CHUNK, LANES), lambda n, h: (n, 0)),
                  pl.BlockSpec((1, GDN_DIM), lambda n, h: (0, 0))],
        out_specs=[pl.BlockSpec((GDN_CHUNK, heads * GDN_DIM), lambda n, h: (n, h)),
                   pl.BlockSpec((1, heads, GDN_DIM, GDN_DIM), lambda n, h: (n, h, 0, 0))],
        out_shape=[jax.ShapeDtypeStruct((n_rows, hv_total * GDN_DIM), BF16),
                   jax.ShapeDtypeStruct((nc, hv_total, GDN_DIM, GDN_DIM), F32)],
        scratch_shapes=[pltpu.VMEM((hv_total, GDN_DIM, GDN_DIM), F32)],
        compiler_params=_cparams(("arbitrary", "arbitrary")),
    )(cq, proj, gates, norm_w.reshape(1, GDN_DIM))


def _gdn_bwd(cq, proj, gates, norm_w, saved, dy, hk_total, name):
    n_rows = cq.shape[0]
    nc, hv_total = n_rows // GDN_CHUNK, 2 * hk_total
    grp = _key_heads_per_step(hk_total)
    heads = 2 * grp
    zblk0 = cq.shape[1] // (heads * GDN_DIM)

    def body(cq_ref, z_ref, g_ref, w_ref, save_ref, dy_ref, dcq_ref, dz_ref, dg_ref, dw_ref, dstate):
        n, hg = pl.program_id(0), pl.program_id(1)

        @pl.when(n == 0)
        def _():
            dstate[pl.ds(heads * hg, heads)] = jnp.zeros((heads, GDN_DIM, GDN_DIM), F32)

        @pl.when((n == 0) & (hg == 0))
        def _():
            dw_ref[...] = jnp.zeros_like(dw_ref)

        @pl.when(hg == 0)
        def _():
            dg_ref[...] = jnp.zeros_like(dg_ref)

        f = functools.partial(_gdn_heads, hk0=grp * hg, hv_total=hv_total)
        _, vjp = jax.vjp(f, save_ref[0], cq_ref[...], z_ref[...], g_ref[...], w_ref[...])
        ds, dcq, dz, dg, dw = vjp((dy_ref[...].astype(F32), dstate[pl.ds(heads * hg, heads)]))
        dstate[pl.ds(heads * hg, heads)] = ds
        dcq_ref[...] = dcq
        dz_ref[...] = dz.astype(dz_ref.dtype)
        dg_ref[...] += dg
        dw_ref[...] += dw

    rev = lambda n: nc - 1 - n
    return pl.pallas_call(
        body, name=name, grid=(nc, hk_total // grp),
        in_specs=[pl.BlockSpec((GDN_CHUNK, 2 * heads * GDN_DIM), lambda n, h: (rev(n), h)),
                  pl.BlockSpec((GDN_CHUNK, heads * GDN_DIM), lambda n, h: (rev(n), zblk0 + h)),
                  pl.BlockSpec((GDN_CHUNK, LANES), lambda n, h: (rev(n), 0)),
                  pl.BlockSpec((1, GDN_DIM), lambda n, h: (0, 0)),
                  pl.BlockSpec((1, heads, GDN_DIM, GDN_DIM), lambda n, h: (rev(n), h, 0, 0)),
                  pl.BlockSpec((GDN_CHUNK, heads * GDN_DIM), lambda n, h: (rev(n), h))],
        out_specs=[pl.BlockSpec((GDN_CHUNK, 2 * heads * GDN_DIM), lambda n, h: (rev(n), h)),
                   pl.BlockSpec((GDN_CHUNK, heads * GDN_DIM), lambda n, h: (rev(n), h)),
                   pl.BlockSpec((GDN_CHUNK, LANES), lambda n, h: (rev(n), 0)),
                   pl.BlockSpec((1, GDN_DIM), lambda n, h: (0, 0))],
        out_shape=[jax.ShapeDtypeStruct(cq.shape, F32), jax.ShapeDtypeStruct((n_rows, hv_total * GDN_DIM), BF16),
                   jax.ShapeDtypeStruct((n_rows, LANES), F32), jax.ShapeDtypeStruct((1, GDN_DIM), F32)],
        scratch_shapes=[pltpu.VMEM((hv_total, GDN_DIM, GDN_DIM), F32)],
        compiler_params=_cparams(("arbitrary", "arbitrary")),
    )(cq, proj, gates, norm_w.reshape(1, GDN_DIM), saved, dy)


def _final_loss(h, w, target, name):
    d = h.shape[1]

    def fn(r0, hv, tv, wv):
        def loss_of(hh, ww):
            err = jnp.where(_row_mask(r0, hh.shape[0], d) & (r0 + lax.broadcasted_iota(jnp.int32, hh.shape, 0) >= FRONT),
                            _rms(hh, ww) - tv, 0.0)
            return 0.5 * jnp.sum(jnp.sum(err * err, axis=1, keepdims=True) / d)

        loss, vjp = jax.vjp(loss_of, hv, wv)
        dh, dw = vjp(jnp.ones((), F32))
        return dh, dh, jnp.zeros((1, LANES), F32) + loss, dw

    dh, dh_bf16, loss, dw = _rowwise(fn, [h, target], [w.reshape(1, -1)], [(d, F32), (d, BF16)], [(1, LANES), (1, d)], name)
    return loss[0, 0], dh, dh_bf16, dw[0]


def _rope_tables(n_rows):
    pos = (jnp.arange(n_rows) - PAD_LEN).astype(F32)
    inv_freq = ROPE_THETA ** (-jnp.arange(0, HEAD_DIM, 2, dtype=F32) / HEAD_DIM)
    ang = pos[:, None] * inv_freq[None, :]
    reps = LANES // (HEAD_DIM // 2)
    return jnp.tile(jnp.cos(ang), (1, reps)), jnp.tile(jnp.sin(ang), (1, reps))


def _to_heads(t):
    n_rows, w = t.shape
    return t.reshape(n_rows, w // HEAD_DIM, HEAD_DIM).transpose(1, 0, 2)


def _from_heads(t):
    heads, n_rows, _ = t.shape
    return t.transpose(1, 0, 2).reshape(n_rows, heads * HEAD_DIM)


def _local_step(h0, target, p, tr):
    n_rows, d = h0.shape
    depth = p["norm_mix"].shape[0]
    cos, sin = _rope_tables(n_rows)
    hk_total = d // GDN_DIM
    hv_total = 2 * hk_total
    conv_dim = 4 * hk_total * GDN_DIM
    qw, kw = d, d // GROUP
    saved = []
    h = h0
    mm = functools.partial(tr.mm, True)
    for i in range(depth):
        kind, j = i % 3, i // 3
        s = {"h": h}
        hn = _rms_fwd(h, p["norm_mix"][i], "rms_fwd")
        s["hn"] = hn
        if kind == 0:
            pre = mm(hn, tr.weight("conv_w_pw1", j), "nn", F32, "mm_pw1")
            u1 = tr.call(True, _us_rows(pre), lambda r: _glu_fwd(pre, p["conv_b_pw1"][j], "glu_fwd", ride=r))
            c = tr.call(True, _us_dwconv(n_rows, d, CONV_KERNEL),
                        lambda r: _dwconv_fwd(u1, p["conv_w_dw"][j], p["conv_b_dw"][j], d, "dwconv31_fwd", ride=r))
            sv = _ln_silu_fwd(c, p["conv_ln_g"][j], p["conv_ln_b"][j], "ln_silu_fwd")
            h = mm(sv, tr.weight("conv_w_pw2", j), "nn", F32, "mm_d_d_res", bias=p["conv_b_pw2"][j], residual=h)
            s.update(pre=pre, u1=u1, c=c, sv=sv)
        elif kind == 1:
            pre = mm(hn, tr.weight("attn_w_qkv", j), "nn", F32, "mm_qkv")
            q, k, v = _qkv_post_fwd(pre, p["attn_b_qkv"][j], cos, sin, qw, kw, "qkv_post_fwd")
            qh, kh, vh = _to_heads(q), _to_heads(k), _to_heads(v)
            o = _from_heads(tr.call(True, _us_attn(qh), lambda r: _attn_fwd(qh, kh, vh, p["attn_sinks"][j], "attn_fwd", ride=r)))
            h = mm(o, tr.weight("attn_w_o", j), "nn", F32, "mm_d_d_res", bias=p["attn_b_o"][j], residual=h)
            s.update(qh=qh, kh=kh, vh=vh, o=o)
        else:
            proj = mm(hn, tr.weight("gdn_w_in", j), "nn", F32, "mm_gdn_in")
            cq = tr.call(True, _us_dwconv(n_rows, conv_dim, GDN_CONV),
                         lambda r: _dwconv_fwd(proj, p["gdn_conv_w"][j], None, conv_dim, "dwconv4_fwd", ride=r))
            vec = _gate_vectors(p["gdn_a_log"][j], p["gdn_dt_bias"][j], hv_total)
            ba_blk = (conv_dim + hv_total * GDN_DIM) // LANES
            (gates,) = _rowwise(lambda r0, bav, av, dv: (_gdn_gates(bav, av, dv, r0, hv_total),), [(proj, LANES, ba_blk)],
                                [vec[0], vec[1]], [(LANES, F32)], [], "gdn_gates_fwd")
            y, states = _gdn_fwd(cq, proj, gates, p["gdn_norm_w"][j], hk_total, "gdn_fwd")
            h = mm(y, tr.weight("gdn_w_out", j), "nn", F32, "mm_gdn_out_res", residual=h)
            s.update(proj=proj, cq=cq, gates=gates, y=y, states=states)
        s["h1"] = h
        hn2 = _rms_fwd(h, p["norm_ffn"][i], "rms_fwd")
        w_gate, w_up = tr.weight("ffn_w_gate", i), tr.weight("ffn_w_up", i)
        gate, up, a = tr.call(True, 2 * _us_mm(n_rows, w_gate.shape[1], d), lambda r: _mmx(
            [(hn2, w_gate, 0), (hn2, w_up, 1)], "nn", "mm_ffn_swiglu", [BF16, BF16, BF16],
            lambda accs: [accs[0], accs[1], jax.nn.silu(accs[0]) * accs[1]], n_acc=2, ride=r))
        h = mm(a, tr.weight("ffn_w_down", i), "nn", F32, "mm_ffn_down_res", residual=h)
        s.update(hn2=hn2, gate=gate, up=up, a=a)
        saved.append(s)

    loss, dh, dhb, g_final = _final_loss(h, p["norm_final"], target, "final_loss")
    g = {k: [None] * v.shape[0] for k, v in p.items() if k not in ("norm_final", "meta_tokens")}
    g["norm_final"] = g_final
    mm = functools.partial(tr.mm, False)
    for i in reversed(range(depth)):
        kind, j = i % 3, i // 3
        s = saved[i]
        tr.give("ffn_w_down", i, mm(s["a"], dhb, "tn", BF16, "mm_dw_down"))
        w_gate, w_up, w_down = tr.weight("ffn_w_gate", i), tr.weight("ffn_w_up", i), tr.weight("ffn_w_down", i)

        def swiglu_bwd(accs, gate, up):
            _, vjp = jax.vjp(lambda gv, uv: jax.nn.silu(gv) * uv, gate.astype(F32), up.astype(F32))
            return list(vjp(accs[0]))

        dgate, dup = tr.call(False, _us_mm(n_rows, w_gate.shape[1], d), lambda r: _mmx(
            [(dhb, w_down, 0)], "nt", "mm_da_swiglu", [BF16, BF16], swiglu_bwd, extras=[s["gate"], s["up"]], ride=r))
        g_gate, g_up = tr.call(False, 2 * _us_mm(d, w_gate.shape[1], n_rows), lambda r: _mmx(
            [(s["hn2"], dgate, 0), (s["hn2"], dup, 1)], "tn", "mm_dw_gate_up", [BF16, BF16], lambda accs: accs, n_acc=2, ride=r))
        tr.give("ffn_w_gate", i, g_gate)
        tr.give("ffn_w_up", i, g_up)
        (dhn2,) = tr.call(False, 2 * _us_mm(n_rows, d, w_gate.shape[1]), lambda r: _mmx(
            [(dgate, w_gate, 0), (dup, w_up, 0)], "nt", "mm_dhn2", [F32], lambda accs: accs, ride=r))
        dh, dhb, g["norm_ffn"][i] = _rms_bwd(s["h1"], p["norm_ffn"][i], dhn2, dh, "rms_bwd")
        if kind == 0:
            g["conv_b_pw2"][j] = _colsum_rows(dh, "colsum_d")
            tr.give("conv_w_pw2", j, mm(s["sv"], dhb, "tn", BF16, "mm_dw_d_d"))
            dsv = mm(dhb, tr.weight("conv_w_pw2", j), "nt", F32, "mm_dx_d_d")
            dc, g["conv_ln_g"][j], g["conv_ln_b"][j] = _ln_silu_bwd(s["c"], p["conv_ln_g"][j], p["conv_ln_b"][j], dsv, "ln_silu_bwd")
            du1, g["conv_w_dw"][j], g["conv_b_dw"][j] = tr.call(
                False, 2 * _us_dwconv(n_rows, d, CONV_KERNEL),
                lambda r: _dwconv_bwd(s["u1"], p["conv_w_dw"][j], dc, d, "dwconv31_bwd", ride=r))
            dpre, g["conv_b_pw1"][j] = _glu_bwd(s["pre"], p["conv_b_pw1"][j], du1, "glu_bwd")
            tr.give("conv_w_pw1", j, mm(s["hn"], dpre, "tn", BF16, "mm_dw_pw1"))
            dhn = mm(dpre, tr.weight("conv_w_pw1", j), "nt", F32, "mm_dx_pw1")
        elif kind == 1:
            g["attn_b_o"][j] = _colsum_rows(dh, "colsum_d")
            tr.give("attn_w_o", j, mm(s["o"], dhb, "tn", BF16, "mm_dw_d_d"))
            do = _to_heads(mm(dhb, tr.weight("attn_w_o", j), "nt", BF16, "mm_dx_d_d_bf16"))
            dq, dkp, dkc, dvp, dvc, dsink = _attn_bwd(s["qh"], s["kh"], s["vh"], p["attn_sinks"][j], do, "attn_bwd")
            g["attn_sinks"][j] = dsink.reshape(-1)
            kvh = kw // HEAD_DIM
            dk = _shift_add(dkc, dkp, "attn_shift_add").reshape(kvh, n_rows, HEAD_DIM)
            dv = _shift_add(dvc, dvp, "attn_shift_add").reshape(kvh, n_rows, HEAD_DIM)
            dpre, g["attn_b_qkv"][j] = _qkv_post_bwd(_from_heads(dq), _from_heads(dk), _from_heads(dv), cos, sin, "qkv_post_bwd")
            tr.give("attn_w_qkv", j, mm(s["hn"], dpre, "tn", BF16, "mm_dw_qkv"))
            dhn = mm(dpre, tr.weight("attn_w_qkv", j), "nt", F32, "mm_dx_qkv")
        else:
            tr.give("gdn_w_out", j, mm(s["y"], dhb, "tn", BF16, "mm_dw_gdn_out"))
            dy = mm(dhb, tr.weight("gdn_w_out", j), "nt", BF16, "mm_dx_gdn_out")
            dcq, dz, dgates, g_nw = _gdn_bwd(s["cq"], s["proj"], s["gates"], p["gdn_norm_w"][j], s["states"], dy, hk_total, "gdn_bwd")
            g["gdn_norm_w"][j] = g_nw[0]
            vec = _gate_vectors(p["gdn_a_log"][j], p["gdn_dt_bias"][j], hv_total)
            ba_blk = (conv_dim + hv_total * GDN_DIM) // LANES

            def gates_bwd(r0, bav, dgv, av, dv):
                _, vjp = jax.vjp(functools.partial(_gdn_gates, r0=r0, hv=hv_total), bav, av, dv)
                return vjp(dgv)

            dba, d_alog, d_dt = _rowwise(gates_bwd, [(s["proj"], LANES, ba_blk), dgates], [vec[0], vec[1]], [(LANES, BF16)],
                                         [(1, LANES), (1, LANES)], "gdn_gates_bwd")
            g["gdn_a_log"][j] = d_alog[0, hv_total:2 * hv_total]
            g["gdn_dt_bias"][j] = d_dt[0, hv_total:2 * hv_total]
            dconv_in, g["gdn_conv_w"][j], _ = tr.call(
                False, 2 * _us_dwconv(n_rows, conv_dim, GDN_CONV),
                lambda r: _dwconv_bwd(s["proj"], p["gdn_conv_w"][j], dcq, conv_dim, "dwconv4_bwd", ride=r))
            w_in = tr.weight("gdn_w_in", j)
            pad = jnp.zeros((n_rows, w_in.shape[1] - conv_dim - hv_total * GDN_DIM - LANES), BF16)
            dproj = jnp.concatenate([dconv_in.astype(BF16), dz, dba, pad], axis=1)
            tr.give("gdn_w_in", j, mm(s["hn"], dproj, "tn", BF16, "mm_dw_gdn_in"))
            dhn = mm(dproj, w_in, "nt", F32, "mm_dx_gdn_in")
        dh, dhb, g["norm_mix"][i] = _rms_bwd(s["h"], p["norm_mix"][i], dhn, dh, "rms_bwd")
    g = {k: (jnp.stack(v) if isinstance(v, list) else v) for k, v in g.items()}
    return loss, dh, g


def _us_mm(m, n, k):
    return 2.0 * m * n * k / 8.0e8


def _us_rows(t):
    return t.shape[0] * t.shape[1] / 8.0e5


def _us_dwconv(n_rows, c, taps):
    return n_rows * c * (taps + 8) / 2.0e6


def _us_attn(qh):
    return qh.shape[0] * qh.shape[1] / 500.0


def _gate_vectors(a_log, dt_bias, hv):
    def place(t):
        return jnp.concatenate([jnp.zeros((hv,), F32), t, jnp.zeros((LANES - 2 * hv,), F32)]).reshape(1, LANES)

    return place(a_log), place(dt_bias)


GDN_IN_ALIGN = 512


def _gdn_group(w, hk):
    lead, kw = w.shape[:-1], hk * GDN_DIM
    q = w[..., :kw].reshape(*lead, hk, 1, GDN_DIM)
    k = w[..., kw:2 * kw].reshape(*lead, hk, 1, GDN_DIM)
    v = w[..., 2 * kw:4 * kw].reshape(*lead, hk, 2, GDN_DIM)
    return jnp.concatenate([q, k, v], axis=-2).reshape(*lead, 4 * kw)


def _gdn_ungroup(w, hk):
    lead, kw = w.shape[:-1], hk * GDN_DIM
    t = w.reshape(*lead, hk, 4, GDN_DIM)
    return jnp.concatenate([t[..., 0, :].reshape(*lead, kw), t[..., 1, :].reshape(*lead, kw),
                            t[..., 2:, :].reshape(*lead, 2 * kw)], axis=-1)


def _gdn_in_layout(w, hk):
    conv_dim = 4 * hk * GDN_DIM
    width = -(-w.shape[-1] // GDN_IN_ALIGN) * GDN_IN_ALIGN
    pad = jnp.zeros(w.shape[:-1] + (width - w.shape[-1],), w.dtype)
    return jnp.concatenate([_gdn_group(w[..., :conv_dim], hk), w[..., conv_dim:], pad], axis=-1)


def _gdn_in_natural(w, hk, in_width):
    conv_dim = 4 * hk * GDN_DIM
    return jnp.concatenate([_gdn_ungroup(w[..., :conv_dim], hk), w[..., conv_dim:in_width]], axis=-1)


def _exchange(srcs, gather, name):
    n_src = len(srcs)

    def body(*refs):
        sems = refs[2 * n_src:]
        copies = [_exchange_copies(refs[t], refs[n_src + t], *sems[3 * t:3 * t + 3], gather=gather) for t in range(n_src)]
        for cps in copies:
            _exchange_start(cps)
        for cps in copies:
            _exchange_wait(cps)

    any_spec = pl.BlockSpec(memory_space=pl.ANY)
    return pl.pallas_call(
        body, name=name, out_shape=[_exchange_out(s, gather) for s in srcs], in_specs=[any_spec] * n_src,
        out_specs=[any_spec] * n_src, scratch_shapes=[s for src in srcs for s in _exchange_sems(src)],
    )(*srcs)


BIG_COL = ("conv_w_pw1", "attn_w_qkv", "gdn_w_in", "ffn_w_gate", "ffn_w_up")
BIG_ROW = ("conv_w_pw2", "attn_w_o", "gdn_w_out", "ffn_w_down")
EXCHANGE_US_PER_BYTE = 11.4e-6
RIDE_PART_US = 150.0


class _Traffic:
    def __init__(self, shards, hk):
        self.shards, self.hk = shards, hk
        self.in_width = N_DEV * shards["gdn_w_in"].shape[-1]
        self.parts_of = {}
        for k, s in shards.items():
            parts = 1
            while (N_DEV * s.shape[1] * s.shape[2] * 2 * EXCHANGE_US_PER_BYTE / parts > RIDE_PART_US
                   and s.shape[1] % (32 * parts) == 0):
                parts *= 2
            self.parts_of[k] = parts
        depth = shards["ffn_w_down"].shape[0]
        order = []
        for i in range(depth):
            j = i // 3
            order += [[("conv_w_pw1", j), ("conv_w_pw2", j)], [("attn_w_qkv", j), ("attn_w_o", j)],
                      [("gdn_w_in", j), ("gdn_w_out", j)]][i % 3]
            order += [("ffn_w_gate", i), ("ffn_w_up", i), ("ffn_w_down", i)]
        self.wanted = [(k, i, part) for k, i in order for part in range(self.parts_of[k])]
        self.arrived = {}
        self.ready = {}
        self.owed = []
        self.received = {}

    def _us(self, k):
        s = self.shards[k]
        return N_DEV * s.shape[1] * s.shape[2] * 2 * EXCHANGE_US_PER_BYTE / self.parts_of[k]

    def _shard_part(self, item):
        k, i, part = item
        rows = self.shards[k].shape[1] // self.parts_of[k]
        return self.shards[k][i:i + 1, part * rows:(part + 1) * rows]

    def _pick(self, queue, us_of, room):
        taken = []
        while queue and room >= 0.5 * us_of(queue[0]):
            room -= us_of(queue[0])
            taken.append(queue.pop(0))
        return taken

    def _run(self, forward, room, fn):
        if forward:
            taken = self._pick(self.wanted, lambda it: self._us(it[0]), room)
            items = [(self._shard_part(it), True) for it in taken]
        else:
            taken = self._pick(self.owed, lambda it: self._us(it[0][0]), room)
            items = [(pieces, False) for _, pieces in taken]
        if not taken:
            return fn(None)
        ride = _Ride(items)
        out = fn(ride)
        for it, got in zip(taken, ride.outs):
            if forward:
                self.arrived[it] = got[0]
            else:
                self.received[it[0]] = got
        return out

    def call(self, forward, room, fn):
        return self._run(forward, room, fn)

    def mm(self, forward, a, b, mode, out_dtype, name, **kw):
        m = a.shape[1] if mode == "tn" else a.shape[0]
        k = a.shape[0] if mode == "tn" else a.shape[1]
        n = b.shape[0] if mode == "nt" else b.shape[1]
        return self._run(forward, _us_mm(m, n, k), lambda ride: _mm(a, b, mode, out_dtype, name, ride=ride, **kw))

    def _natural(self, k, i):
        parts = self.parts_of[k]
        missing = [(k, i, part) for part in range(parts) if (k, i, part) not in self.arrived]
        if missing:
            for it in missing:
                self.wanted.remove(it)
            got = _exchange([self._shard_part(it) for it in missing], True, "gather_weights")
            for it, t in zip(missing, got):
                self.arrived[it] = t[0]
        got = [self.arrived[(k, i, part)] for part in range(parts)]
        rows, c = got[0].shape[1], got[0].shape[2]
        if k in BIG_COL:
            return jnp.concatenate([t.transpose(1, 0, 2).reshape(rows, N_DEV * c) for t in got], axis=0)
        return jnp.stack(got, axis=1).reshape(N_DEV * parts * rows, c)

    def weight(self, k, i):
        if (k, i) not in self.ready:
            w = self._natural(k, i)
            self.ready[(k, i)] = _gdn_in_layout(w, self.hk) if k == "gdn_w_in" else w
        return self.ready[(k, i)]

    def give(self, k, i, grad):
        if k == "gdn_w_in":
            grad = _gdn_in_natural(grad, self.hk, self.in_width)
        r, c = self.shards[k].shape[1:]
        pieces = grad.reshape(r, N_DEV, c).transpose(1, 0, 2) if k in BIG_COL else grad.reshape(N_DEV, r, c)
        rows = r // self.parts_of[k]
        for part in range(self.parts_of[k]):
            self.owed.append(((k, i, part), pieces[None, :, part * rows:(part + 1) * rows]))

    def gradient_parts(self, k):
        if self.owed:
            got = _exchange([pieces for _, pieces in self.owed], False, "scatter_grads")
            for (it, _), t in zip(self.owed, got):
                self.received[it] = t
            self.owed = []
        layers = self.shards[k].shape[0]
        return jnp.concatenate([jnp.concatenate([self.received[(k, i, part)] for part in range(self.parts_of[k])], axis=2)
                                for i in range(layers)], axis=0)


def _cast_bf16(w, name):
    n, r, c = w.shape
    tr = _tile(r, max(16, (1 << 20) // c), 16)

    def body(w_ref, o_ref):
        o_ref[...] = w_ref[...].astype(BF16)

    return pl.pallas_call(
        body, name=name, grid=(n, r // tr), in_specs=[pl.BlockSpec((1, tr, c), lambda l, i: (l, i, 0))],
        out_specs=pl.BlockSpec((1, tr, c), lambda l, i: (l, i, 0)), out_shape=jax.ShapeDtypeStruct(w.shape, BF16),
        compiler_params=_cparams(("parallel", "parallel")),
    )(w)


def _adamw(w, g, m, v):
    m = ADAM_B1 * m + (1.0 - ADAM_B1) * g
    v = ADAM_B2 * v + (1.0 - ADAM_B2) * jnp.square(g)
    m_hat = m / (1.0 - ADAM_B1 ** ADAM_STEP)
    v_hat = v / (1.0 - ADAM_B2 ** ADAM_STEP)
    delta = -ADAM_LR * (m_hat / (jnp.sqrt(v_hat) + ADAM_EPS) + ADAM_WD * w)
    return delta, m, v


def _sum8_adam(parts, w, m, v, name):
    n, _, r, c = parts.shape
    tr = _tile(r, max(16, (1 << 18) // c), 16)
    blk = pl.BlockSpec((1, tr, c), lambda l, i: (l, i, 0))

    def body(p_ref, w_ref, m_ref, v_ref, g_ref, d_ref, mo_ref, vo_ref):
        g = p_ref[0, 0].astype(F32)
        for s in range(1, N_DEV):
            g = g + p_ref[0, s].astype(F32)
        delta, m2, v2 = _adamw(w_ref[0], g, m_ref[0], v_ref[0])
        g_ref[0], d_ref[0], mo_ref[0], vo_ref[0] = g, delta, m2, v2

    shp = jax.ShapeDtypeStruct(w.shape, F32)
    return pl.pallas_call(
        body, name=name, grid=(n, r // tr),
        in_specs=[pl.BlockSpec((1, N_DEV, tr, c), lambda l, i: (l, 0, i, 0)), blk, blk, blk],
        out_specs=[blk, blk, blk, blk], out_shape=[shp, shp, shp, shp],
        compiler_params=_cparams(("parallel", "parallel")),
    )(parts, w, m, v)


PACK_ROWS = 8


def _pack(arrays):
    flat = jnp.concatenate([a.reshape(-1).astype(F32) for a in arrays])
    unit = PACK_ROWS * LANES
    total = -(-flat.shape[0] // unit) * unit
    return jnp.concatenate([flat, jnp.zeros((total - flat.shape[0],), F32)]).reshape(-1, LANES)


def _unpack(packed, shapes):
    flat, out, pos = packed.reshape(-1), [], 0
    for s in shapes:
        size = math.prod(s)
        out.append(flat[pos:pos + size].reshape(s))
        pos += size
    return out


SMALL_SHARDED =("meta_tokens", "conv_b_pw1", "conv_w_dw", "conv_b_dw", "conv_ln_g", "conv_ln_b", "conv_b_pw2", "gdn_conv_w")
REPLICATED = ("norm_mix", "norm_ffn", "norm_final", "attn_b_qkv", "attn_sinks", "attn_b_o", "gdn_a_log", "gdn_dt_bias", "gdn_norm_w")
WEIGHTS = ("meta_tokens", "norm_mix", "norm_ffn", "norm_final", "conv_w_pw1", "conv_b_pw1", "conv_w_dw", "conv_b_dw", "conv_ln_g",
           "conv_ln_b", "conv_w_pw2", "conv_b_pw2", "attn_w_qkv", "attn_b_qkv", "attn_sinks", "attn_w_o", "attn_b_o", "gdn_w_in",
           "gdn_conv_w", "gdn_a_log", "gdn_dt_bias", "gdn_norm_w", "gdn_w_out", "ffn_w_gate", "ffn_w_up", "ffn_w_down")


def kernel(x, meta_tokens, norm_mix, norm_ffn, norm_final, conv_w_pw1, conv_b_pw1, conv_w_dw, conv_b_dw, conv_ln_g, conv_ln_b, conv_w_pw2, conv_b_pw2, attn_w_qkv, attn_b_qkv, attn_sinks, attn_w_o, attn_b_o, gdn_w_in, gdn_conv_w, gdn_a_log, gdn_dt_bias, gdn_norm_w, gdn_w_out, ffn_w_gate, ffn_w_up, ffn_w_down, loss_target, m_meta_tokens, m_norm_mix, m_norm_ffn, m_norm_final, m_conv_w_pw1, m_conv_b_pw1, m_conv_w_dw, m_conv_b_dw, m_conv_ln_g, m_conv_ln_b, m_conv_w_pw2, m_conv_b_pw2, m_attn_w_qkv, m_attn_b_qkv, m_attn_sinks, m_attn_w_o, m_attn_b_o, m_gdn_w_in, m_gdn_conv_w, m_gdn_a_log, m_gdn_dt_bias, m_gdn_norm_w, m_gdn_w_out, m_ffn_w_gate, m_ffn_w_up, m_ffn_w_down, v_meta_tokens, v_norm_mix, v_norm_ffn, v_norm_final, v_conv_w_pw1, v_conv_b_pw1, v_conv_w_dw, v_conv_b_dw, v_conv_ln_g, v_conv_ln_b, v_conv_w_pw2, v_conv_b_pw2, v_attn_w_qkv, v_attn_b_qkv, v_attn_sinks, v_attn_w_o, v_attn_b_o, v_gdn_w_in, v_gdn_conv_w, v_gdn_a_log, v_gdn_dt_bias, v_gdn_norm_w, v_gdn_w_out, v_ffn_w_gate, v_ffn_w_up, v_ffn_w_down):
    a = dict(locals())
    me = 4 * lax.axis_index("x") + 2 * lax.axis_index("y") + lax.axis_index("c")
    d = x.shape[-1]

    hk = d // GDN_DIM
    full = {k: a[k] for k in REPLICATED}
    shard_shapes = [a[k].shape for k in SMALL_SHARDED]
    (got,) = _exchange([_pack([a[k] for k in SMALL_SHARDED])[None]], True, "gather_small")
    per_dev = [_unpack(got[0, s], shard_shapes) for s in range(N_DEV)]
    for i, k in enumerate(SMALL_SHARDED):
        st = jnp.stack([per_dev[s][i] for s in range(N_DEV)], axis=-2)
        full[k] = st.reshape(st.shape[:-2] + (N_DEV * st.shape[-1],))
    traffic = _Traffic({k: _cast_bf16(a[k], "cast_bf16") for k in BIG_COL + BIG_ROW}, hk)

    h0 = jnp.concatenate([jnp.zeros((PAD_LEN, d), F32), full["meta_tokens"], x[0]], axis=0)
    target = jnp.concatenate([jnp.zeros((FRONT, d), F32), loss_target[0]], axis=0)
    loss, dh0, g = _local_step(h0, target, {**full, "gdn_conv_w": _gdn_group(full["gdn_conv_w"], hk)}, traffic)
    g["gdn_conv_w"] = _gdn_ungroup(g["gdn_conv_w"], hk)
    g["meta_tokens"] = dh0[PAD_LEN:FRONT]
    loss = lax.psum(loss, AXES)
    grad_x = dh0[FRONT:][None]

    grads, deltas, new_m, new_v = {}, {}, {}, {}
    for k in BIG_COL + BIG_ROW:
        grads[k], deltas[k], new_m[k], new_v[k] = _sum8_adam(traffic.gradient_parts(k), a[k], a["m_" + k], a["v_" + k], "sum8_adamw")

    small = SMALL_SHARDED + REPLICATED
    full_shapes = [full[k].shape for k in small]
    (got,) = _exchange([_pack([g[k] for k in small])[None]], True, "gather_small_grads")
    got = got[0]
    (total,) = _rowwise(lambda r0, *t: (functools.reduce(lambda p, q: p + q, t),), [got[s] for s in range(N_DEV)], [],
                        [(LANES, F32)], [], "sum8_small", tm=got.shape[1])
    for k, t in zip(small, _unpack(total, full_shapes)):
        if k in SMALL_SHARDED:
            c = a[k].shape[-1]
            t = lax.dynamic_index_in_dim(t.reshape(t.shape[:-1] + (N_DEV, c)), me, axis=t.ndim - 1, keepdims=False)
        grads[k] = t
    shapes = [a[k].shape for k in small]
    packed = [_pack([src[k] for k in small]) for src in (grads, a, {k: a["m_" + k] for k in small}, {k: a["v_" + k] for k in small})]

    def small_adam(r0, gv, wv, mv, vv):
        return _adamw(wv, gv, mv, vv)

    outs = _rowwise(small_adam, packed, [], [(LANES, F32)] * 3, [], "adamw_small", tm=packed[0].shape[0])
    for dst, o in zip((deltas, new_m, new_v), outs):
        for k, t in zip(small, _unpack(o, shapes)):
            dst[k] = t

    return (loss, grad_x, *[grads[k] for k in WEIGHTS], *[deltas[k] for k in WEIGHTS], *[new_m[k] for k in WEIGHTS],
            *[new_v[k] for k in WEIGHTS])
```

```python
import functools
import math

import jax
import jax.numpy as jnp
from jax import lax
from jax.experimental import pallas as pl
from jax.experimental.pallas import tpu as pltpu

F32 = jnp.float32
BF16 = jnp.bfloat16

AXES = ("x", "y", "c")
N_DEV = 8

N_META = 16
FRONT = 128
PAD_LEN = FRONT - N_META
NORM_EPS = 1e-6
LN_EPS = 1e-5
NEG_INF = -1e30
CONV_KERNEL = 31
HEAD_DIM = 64
GROUP = 8
BLOCK = 128
ROPE_THETA = 10000.0
GDN_DIM = 128
GDN_CONV = 4
GDN_CHUNK = 64
ADAM_LR, ADAM_B1, ADAM_B2, ADAM_EPS, ADAM_WD, ADAM_STEP = 0.001, 0.9, 0.999, 1e-08, 0.01, 10

VMEM_LIMIT_BYTES = 52 * 1024 * 1024
LANES = 128
MM_TILE_PREF = 1408
MM_VMEM_BUDGET_BYTES = 40 * 1024 * 1024
MM_CHUNK = 256
ROW_TILE_BUDGET_BYTES = 24 * 1024 * 1024
CONV_HALO = 32


def _tile(n, pref, align=128):
    best = None
    for t in range(align, min(n, pref) + 1, align):
        if n % t == 0:
            best = t
    return best if best is not None else n


def _cparams(sem):
    return pltpu.CompilerParams(dimension_semantics=sem, vmem_limit_bytes=VMEM_LIMIT_BYTES)


class _Slab:
    def __init__(self, array, gather, index=0, row0=0, rows=None):
        self.array, self.gather, self.index, self.row0 = array, gather, index, row0
        self.rows = array.shape[1] if rows is None else rows

    def out_shape(self):
        return jax.ShapeDtypeStruct((N_DEV, self.rows, self.array.shape[2]), self.array.dtype)


def _exchange_copies(src_ref, out_ref, send_sems, recv_sems, local_sem, slab):
    x, y, c = lax.axis_index("x"), lax.axis_index("y"), lax.axis_index("c")
    me = 4 * x + 2 * y + c
    rows = pl.ds(slab.row0, slab.rows)

    def src_for(dev):
        return src_ref.at[slab.index, rows] if slab.gather else src_ref.at[dev, rows]

    local = [pltpu.make_async_copy(src_for(me), out_ref.at[me], local_sem.at[0])]
    remote = []
    for k in range(1, N_DEV):
        px = 1 - x if k & 4 else x
        py = 1 - y if k & 2 else y
        pc = 1 - c if k & 1 else c
        remote.append(pltpu.make_async_remote_copy(
            src_ref=src_for(4 * px + 2 * py + pc), dst_ref=out_ref.at[me], send_sem=send_sems.at[k - 1], recv_sem=recv_sems.at[k - 1],
            device_id=(px, py, pc), device_id_type=pl.DeviceIdType.MESH))
    return remote, local


def _exchange_start(copies):
    remote, local = copies
    for cp in local + remote:
        cp.start()


def _exchange_wait(copies):
    remote, local = copies
    for cp in remote:
        cp.wait_send()
    for cp in remote:
        cp.wait_recv()
    for cp in local:
        cp.wait()


def _exchange_sems():
    return [pltpu.SemaphoreType.DMA((N_DEV - 1,)), pltpu.SemaphoreType.DMA((N_DEV - 1,)), pltpu.SemaphoreType.DMA((1,))]


class _Ride:
    def __init__(self, items):
        self.items = items
        self.outs = None


def _pcall(body, args, *, name, grid, in_specs, out_specs, out_shape, sem, scratch_shapes=(), ride=None):
    if ride is None:
        return pl.pallas_call(body, name=name, grid=grid, in_specs=in_specs, out_specs=out_specs, out_shape=out_shape,
                              scratch_shapes=list(scratch_shapes), compiler_params=_cparams(sem))(*args)
    n_in, n_out, n_scr, n_ride = len(in_specs), len(out_specs), len(scratch_shapes), len(ride.items)
    any_spec = pl.BlockSpec(memory_space=pl.ANY)

    def with_ride(*refs):
        pos = 0
        ins, pos = refs[pos:pos + n_in], pos + n_in
        srcs, pos = refs[pos:pos + n_ride], pos + n_ride
        outs, pos = refs[pos:pos + n_out], pos + n_out
        dsts, pos = refs[pos:pos + n_ride], pos + n_ride
        scr, pos = refs[pos:pos + n_scr], pos + n_scr
        sems = refs[pos:]
        first = functools.reduce(lambda p, q: p & q, [pl.program_id(d) == 0 for d in range(len(grid))])
        last = functools.reduce(lambda p, q: p & q, [pl.program_id(d) == grid[d] - 1 for d in range(len(grid))])

        def copies():
            return [_exchange_copies(srcs[t], dsts[t], *sems[3 * t:3 * t + 3], slab=ride.items[t]) for t in range(n_ride)]

        @pl.when(first)
        def _():
            for cps in copies():
                _exchange_start(cps)

        body(*ins, *outs, *scr)

        @pl.when(last)
        def _():
            for cps in copies():
                _exchange_wait(cps)

    res = pl.pallas_call(
        with_ride, name=name, grid=grid, in_specs=list(in_specs) + [any_spec] * n_ride,
        out_specs=list(out_specs) + [any_spec] * n_ride,
        out_shape=list(out_shape) + [slab.out_shape() for slab in ride.items],
        scratch_shapes=list(scratch_shapes) + [s for _ in ride.items for s in _exchange_sems()],
        compiler_params=_cparams(("arbitrary",) * len(grid)),
    )(*args, *[slab.array for slab in ride.items])
    ride.outs = list(res[n_out:])
    return list(res[:n_out])


def _mm_shape(a, b, mode):
    if mode == "nn":
        (m, k), (k2, n) = a.shape, b.shape
    elif mode == "nt":
        (m, k), (n, k2) = a.shape, b.shape
    else:
        (k, m), (k2, n) = a.shape, b.shape
    assert k == k2, (a.shape, b.shape, mode)
    return m, n, k


def _mmx(pairs, mode, name, out_dtypes, epilogue, extras=(), n_acc=1, ride=None):
    m, n, k = _mm_shape(pairs[0][0], pairs[0][1], mode)
    tm = _tile(m, MM_TILE_PREF if mode == "tn" else 640)
    tk = _tile(k, 2048)
    nk = k // tk

    def vmem_bytes(tn):
        total = n_acc * tm * tn * 4
        for a, b, _ in pairs:
            total += 2 * (tm * tk * a.dtype.itemsize + tk * tn * b.dtype.itemsize)
        total += 2 * sum((1 if e.shape[0] == 1 and m != 1 else tm) * tn * e.dtype.itemsize for e in extras)
        return total + 2 * sum(tm * tn * jnp.dtype(dt).itemsize for dt in out_dtypes)

    tn = _tile(n, MM_TILE_PREF)
    while vmem_bytes(tn) > MM_VMEM_BUDGET_BYTES and tn > LANES:
        tn = _tile(n, tn - LANES)
    dims = {"nn": (((1,), (0,)), ((), ())), "nt": (((1,), (1,)), ((), ())), "tn": (((0,), (0,)), ((), ()))}[mode]
    a_spec = pl.BlockSpec((tk, tm), lambda i, j, kk: (kk, i)) if mode == "tn" else pl.BlockSpec((tm, tk), lambda i, j, kk: (i, kk))
    b_spec = pl.BlockSpec((tn, tk), lambda i, j, kk: (j, kk)) if mode == "nt" else pl.BlockSpec((tk, tn), lambda i, j, kk: (kk, j))
    ins, specs, where = [], [], []
    for a, b, _ in pairs:
        assert _mm_shape(a, b, mode) == (m, n, k)
        ia = next((t for t, x in enumerate(ins) if x is a), None)
        if ia is None:
            ins.append(a)
            specs.append(a_spec)
            ia = len(ins) - 1
        ins.append(b)
        specs.append(b_spec)
        where.append((ia, len(ins) - 1))
    n_ops = len(ins)
    for e in extras:
        ins.append(e)
        specs.append(pl.BlockSpec((1, tn), lambda i, j, kk: (0, j)) if e.shape[0] == 1 and m != 1 else
                     pl.BlockSpec((tm, tn), lambda i, j, kk: (i, j)))
    n_ex, n_out = len(extras), len(out_dtypes)

    def body(*refs):
        ex_refs = refs[n_ops:n_ops + n_ex]
        o_refs = refs[n_ops + n_ex:n_ops + n_ex + n_out]
        accs = refs[n_ops + n_ex + n_out:]
        kk = pl.program_id(2)

        @pl.when(kk == 0)
        def _():
            for acc in accs:
                acc[...] = jnp.zeros_like(acc)

        for (ia, ib), (_, _, which) in zip(where, pairs):
            accs[which][...] += lax.dot_general(refs[ia][...].astype(BF16), refs[ib][...].astype(BF16), dims,
                                                preferred_element_type=F32)

        @pl.when(kk == nk - 1)
        def _():
            tiles = epilogue([acc[...] for acc in accs], *[e[...] for e in ex_refs])
            for o_ref, t in zip(o_refs, tiles):
                o_ref[...] = t.astype(o_ref.dtype)

    def body_one_pass(*refs):
        ex_refs = refs[n_ops:n_ops + n_ex]
        o_refs = refs[n_ops + n_ex:n_ops + n_ex + n_out]
        lefts = {ia: refs[ia][...].astype(BF16) for ia, _ in where}
        for c0 in range(0, tn, MM_CHUNK):
            cw = min(MM_CHUNK, tn - c0)
            accs = [None] * n_acc
            for (ia, ib), (_, _, which) in zip(where, pairs):
                right = refs[ib][c0:c0 + cw, :] if mode == "nt" else refs[ib][:, c0:c0 + cw]
                part = lax.dot_general(lefts[ia], right.astype(BF16), dims, preferred_element_type=F32)
                accs[which] = part if accs[which] is None else accs[which] + part
            tiles = epilogue(accs, *[e[:, c0:c0 + cw] for e in ex_refs])
            for o_ref, t in zip(o_refs, tiles):
                o_ref[:, c0:c0 + cw] = t.astype(o_ref.dtype)

    return _pcall(
        body_one_pass if nk == 1 else body, ins, name=name, grid=(m // tm, n // tn, nk), in_specs=specs,
        out_specs=[pl.BlockSpec((tm, tn), lambda i, j, kk: (i, j))] * n_out,
        out_shape=[jax.ShapeDtypeStruct((m, n), dt) for dt in out_dtypes],
        scratch_shapes=[] if nk == 1 else [pltpu.VMEM((tm, tn), F32)] * n_acc,
        sem=("parallel", "parallel", "arbitrary"), ride=ride)


def _mm(a, b, mode, out_dtype, name, bias=None, residual=None, ride=None):
    extras = ([] if bias is None else [bias.reshape(1, -1).astype(F32)]) + ([] if residual is None else [residual])
    (out,) = _mmx([(a, b, 0)], mode, name, [out_dtype], lambda accs, *ex: [functools.reduce(lambda p, q: p + q.astype(F32), ex, accs[0])],
                  extras=extras, ride=ride)
    return out


def _rowwise(fn, rows, consts, out_rows, out_accs, name, tm=None, ride=None):
    rows = [r if isinstance(r, tuple) else (r, r.shape[1], 0) for r in rows]
    n_rows = rows[0][0].shape[0]
    if tm is None:
        row_bytes = sum(w * r.dtype.itemsize for r, w, _ in rows) + sum(w * jnp.dtype(dt).itemsize for w, dt in out_rows)
        tm = _tile(n_rows, max(8, min(640, ROW_TILE_BUDGET_BYTES // (2 * row_bytes))), 8)
    steps = n_rows // tm
    in_specs = [pl.BlockSpec((tm, w), functools.partial(lambda i, c: (i, c), c=cb)) for _, w, cb in rows]
    in_specs += [pl.BlockSpec(c.shape, lambda i: (0, 0)) for c in consts]
    out_specs = [pl.BlockSpec((tm, w), lambda i: (i, 0)) for w, _ in out_rows]
    out_specs += [pl.BlockSpec(s, lambda i: (0, 0)) for s in out_accs]
    out_shape = [jax.ShapeDtypeStruct((n_rows, w), d) for w, d in out_rows]
    out_shape += [jax.ShapeDtypeStruct(s, F32) for s in out_accs]
    n_in, n_or = len(rows) + len(consts), len(out_rows)

    def body(*refs):
        i = pl.program_id(0)
        vals = fn(i * tm, *[r[...] for r in refs[:n_in]])
        outs = refs[n_in:]
        for o_ref, v in zip(outs[:n_or], vals[:n_or]):
            o_ref[...] = v.astype(o_ref.dtype)
        if out_accs:
            @pl.when(i == 0)
            def _():
                for a_ref in outs[n_or:]:
                    a_ref[...] = jnp.zeros_like(a_ref)

            for a_ref, v in zip(outs[n_or:], vals[n_or:]):
                a_ref[...] += v

    return _pcall(body, [r[0] for r in rows] + list(consts), name=name, grid=(steps,), in_specs=in_specs, out_specs=out_specs,
                  out_shape=out_shape, sem=("arbitrary",) if out_accs else ("parallel",), ride=ride)


def _colsum(v):
    return jnp.sum(v, axis=0, keepdims=True)


def _rms(h, w):
    return h * lax.rsqrt(jnp.mean(h * h, axis=-1, keepdims=True) + NORM_EPS) * w


def _rms_fwd(h, w, name):
    (hn,) = _rowwise(lambda r0, hv, wv: (_rms(hv, wv),), [h], [w.reshape(1, -1)], [(h.shape[1], BF16)], [], name)
    return hn


def _rms_bwd(h, w, dhn, dh_in, name):
    d = h.shape[1]

    def fn(r0, hv, dv, rv, wv):
        _, vjp = jax.vjp(_rms, hv, wv)
        dh, dw = vjp(dv.astype(F32))
        return dh + rv, dh + rv, dw

    dh, dh_bf16, dw = _rowwise(fn, [h, dhn, dh_in], [w.reshape(1, -1)], [(d, F32), (d, BF16)], [(1, d)], name)
    return dh, dh_bf16, dw[0]


def _glu(p, b):
    t = p + b
    d = t.shape[1] // 2
    return t[:, :d] * jax.nn.sigmoid(t[:, d:])


def _glu_fwd(p, b, name, ride=None):
    (u,) = _rowwise(lambda r0, v, bv: (_glu(v, bv),), [p], [b.reshape(1, -1)], [(p.shape[1] // 2, F32)], [], name, ride=ride)
    return u


def _glu_bwd(p, b, du, name):
    def fn(r0, v, dv, bv):
        _, vjp = jax.vjp(_glu, v, bv)
        dp, db = vjp(dv)
        return dp, db

    dp, db = _rowwise(fn, [p, du], [b.reshape(1, -1)], [(p.shape[1], BF16)], [(1, p.shape[1])], name)
    return dp, db[0]


def _ln_silu(c, g, b):
    mu = jnp.mean(c, axis=-1, keepdims=True)
    xc = c - mu
    var = jnp.mean(xc * xc, axis=-1, keepdims=True)
    return jax.nn.silu(xc * lax.rsqrt(var + LN_EPS) * g + b)


def _ln_silu_fwd(c, g, b, name):
    (s,) = _rowwise(lambda r0, v, gv, bv: (_ln_silu(v, gv, bv),), [c], [g.reshape(1, -1), b.reshape(1, -1)],
                    [(c.shape[1], BF16)], [], name)
    return s


def _ln_silu_bwd(c, g, b, ds, name):
    d = c.shape[1]

    def fn(r0, v, dv, gv, bv):
        _, vjp = jax.vjp(_ln_silu, v, gv, bv)
        return vjp(dv.astype(F32))

    dc, dg, db = _rowwise(fn, [c, ds], [g.reshape(1, -1), b.reshape(1, -1)], [(d, F32)], [(1, d), (1, d)], name)
    return dc, dg[0], db[0]


def _colsum_rows(v, name):
    (s,) = _rowwise(lambda r0, t: (_colsum(t.astype(F32)),), [v], [], [], [(1, v.shape[1])], name)
    return s[0]


def _rot_half(x):
    w = x.shape[1]
    lane = lax.broadcasted_iota(jnp.int32, x.shape, 1)
    lo = (lane % HEAD_DIM) < (HEAD_DIM // 2)
    return jnp.where(lo, -pltpu.roll(x, w - HEAD_DIM // 2, axis=1), pltpu.roll(x, HEAD_DIM // 2, axis=1))


def _qkv_post_fwd(pre, b, cos, sin, qw, kw, name):
    reps = (qw + kw) // LANES

    def fn(r0, pv, cv, sv, bv):
        t = pv + bv
        tq = t[:, :qw + kw]
        y = tq * jnp.tile(cv, (1, reps)) + _rot_half(tq) * jnp.tile(sv, (1, reps))
        return y[:, :qw], y[:, qw:], t[:, qw + kw:]

    return _rowwise(fn, [pre, cos, sin], [b.reshape(1, -1)], [(qw, BF16), (kw, BF16), (kw, BF16)], [], name)


def _qkv_post_bwd(dq, dk, dv, cos, sin, name):
    qw, kw = dq.shape[1], dk.shape[1]
    reps = (qw + kw) // LANES
    width = qw + 2 * kw

    def fn(r0, dqv, dkv, dvv, cv, sv):
        dy = jnp.concatenate([dqv.astype(F32), dkv.astype(F32)], axis=1)
        dt = dy * jnp.tile(cv, (1, reps)) - _rot_half(dy * jnp.tile(sv, (1, reps)))
        dpre = jnp.concatenate([dt, dvv.astype(F32)], axis=1)
        return dpre, _colsum(dpre)

    dpre, db = _rowwise(fn, [dq, dk, dv, cos, sin], [], [(width, BF16)], [(1, width)], name)
    return dpre, db[0]


def _dw_tiles(n_rows, c):
    return _tile(n_rows, 640, 8), _tile(c, 512)


class _RowWindow:
    SUBLANES = 8

    def __init__(self, value):
        self.copies = {0: value}

    def rows(self, off, n):
        q, s = divmod(off, self.SUBLANES)
        if s not in self.copies:
            base = self.copies[0]
            self.copies[s] = pltpu.roll(base, base.shape[0] - s, axis=0)
        return self.copies[s][self.SUBLANES * q:self.SUBLANES * q + n, :]


def _row_mask(r0, n, width):
    row = r0 + lax.broadcasted_iota(jnp.int32, (n, width), 0)
    return row >= PAD_LEN


def _dwconv_fwd(u, w, bias, c, name, ride=None):
    n_rows, taps = u.shape[0], w.shape[0]
    tr, cb = _dw_tiles(n_rows, c)
    kp = -(-taps // 8) * 8
    wp = jnp.concatenate([w.astype(F32), jnp.zeros((kp - taps, c), F32)], axis=0)
    bp = jnp.zeros((1, c), F32) if bias is None else bias.reshape(1, c).astype(F32)

    def body(cur_ref, prev_ref, w_ref, b_ref, o_ref):
        r = pl.program_id(1)
        cur = jnp.where(_row_mask(r * tr, tr, cb), cur_ref[...], 0.0)
        tail = jnp.where(_row_mask(r * tr - CONV_HALO, CONV_HALO, cb) & (r > 0), prev_ref[...], 0.0)
        win = _RowWindow(jnp.concatenate([tail, cur], axis=0))
        acc = jnp.zeros((tr, cb), F32) + b_ref[...]
        for k in range(taps):
            acc = acc + w_ref[k:k + 1, :] * win.rows(CONV_HALO - (taps - 1) + k, tr)
        o_ref[...] = acc

    per = tr // CONV_HALO
    (out,) = _pcall(
        body, [u, u, wp, bp], name=name, grid=(c // cb, n_rows // tr),
        in_specs=[pl.BlockSpec((tr, cb), lambda j, r: (r, j)),
                  pl.BlockSpec((CONV_HALO, cb), lambda j, r: (jnp.maximum(r * per - 1, 0), j)),
                  pl.BlockSpec((kp, cb), lambda j, r: (0, j)),
                  pl.BlockSpec((1, cb), lambda j, r: (0, j))],
        out_specs=[pl.BlockSpec((tr, cb), lambda j, r: (r, j))],
        out_shape=[jax.ShapeDtypeStruct((n_rows, c), F32)], sem=("parallel", "parallel"), ride=ride)
    return out


def _dwconv_bwd(u, w, dc, c, name, du_dtype=F32, ride=None):
    n_rows, taps = u.shape[0], w.shape[0]
    tr, cb = _dw_tiles(n_rows, c)
    nr = n_rows // tr
    per = tr // CONV_HALO
    kp = -(-taps // 8) * 8
    wp = jnp.concatenate([w.astype(F32), jnp.zeros((kp - taps, c), F32)], axis=0)

    def body(cur_ref, prev_ref, d_ref, dnext_ref, w_ref, du_ref, dw_ref, db_ref):
        r = pl.program_id(1)
        cur = jnp.where(_row_mask(r * tr, tr, cb), cur_ref[...], 0.0)
        tail = jnp.where(_row_mask(r * tr - CONV_HALO, CONV_HALO, cb) & (r > 0), prev_ref[...], 0.0)
        win_u = _RowWindow(jnp.concatenate([tail, cur], axis=0))
        d = d_ref[...]
        head = jnp.where(r < nr - 1, dnext_ref[...], 0.0)
        win_d = _RowWindow(jnp.concatenate([d, head], axis=0))

        @pl.when(r == 0)
        def _():
            dw_ref[...] = jnp.zeros_like(dw_ref)
            db_ref[...] = jnp.zeros_like(db_ref)

        du = jnp.zeros((tr, cb), F32)
        for k in range(taps):
            du = du + w_ref[k:k + 1, :] * win_d.rows(taps - 1 - k, tr)
            dw_ref[k:k + 1, :] += _colsum(d * win_u.rows(CONV_HALO - (taps - 1) + k, tr))
        du_ref[...] = jnp.where(_row_mask(r * tr, tr, cb), du, 0.0).astype(du_ref.dtype)
        db_ref[...] += _colsum(d)

    du, dw, db = _pcall(
        body, [u, u, dc, dc, wp], name=name, grid=(c // cb, nr),
        in_specs=[pl.BlockSpec((tr, cb), lambda j, r: (r, j)),
                  pl.BlockSpec((CONV_HALO, cb), lambda j, r: (jnp.maximum(r * per - 1, 0), j)),
                  pl.BlockSpec((tr, cb), lambda j, r: (r, j)),
                  pl.BlockSpec((CONV_HALO, cb), lambda j, r: (jnp.minimum((r + 1) * per, nr * per - 1), j)),
                  pl.BlockSpec((kp, cb), lambda j, r: (0, j))],
        out_specs=[pl.BlockSpec((tr, cb), lambda j, r: (r, j)),
                   pl.BlockSpec((kp, cb), lambda j, r: (0, j)),
                   pl.BlockSpec((1, cb), lambda j, r: (0, j))],
        out_shape=[jax.ShapeDtypeStruct((n_rows, c), du_dtype), jax.ShapeDtypeStruct((kp, c), F32),
                   jax.ShapeDtypeStruct((1, c), F32)], sem=("parallel", "arbitrary"), ride=ride)
    return du, dw[:taps], db[0]


def _attn_block(q, kprev, kcur, vprev, vcur, sink, n):
    qf = q.reshape(GROUP * BLOCK, HEAD_DIM).astype(BF16)
    kb = jnp.concatenate([kprev, kcur], axis=0).astype(BF16)
    vb = jnp.concatenate([vprev, vcur], axis=0).astype(BF16)
    s = lax.dot_general(qf, kb, (((1,), (1,)), ((), ())), preferred_element_type=F32) * (HEAD_DIM ** -0.5)
    s = s.reshape(GROUP, BLOCK, 2 * BLOCK)
    qi = lax.broadcasted_iota(jnp.int32, (BLOCK, 2 * BLOCK), 0)
    kj = lax.broadcasted_iota(jnp.int32, (BLOCK, 2 * BLOCK), 1)
    dist = qi + BLOCK - kj
    allowed = (dist >= 0) & (dist < BLOCK) & ((n - 1) * BLOCK + kj >= PAD_LEN)
    s = jnp.where(allowed[None], s, NEG_INF)
    m = lax.stop_gradient(jnp.maximum(jnp.max(s, axis=-1, keepdims=True), sink))
    e = jnp.exp(s - m)
    p = e / (jnp.sum(e, axis=-1, keepdims=True) + jnp.exp(sink - m))
    o = jnp.dot(p.reshape(GROUP * BLOCK, 2 * BLOCK).astype(BF16), vb, preferred_element_type=F32)
    return o.reshape(GROUP, BLOCK, HEAD_DIM)


def _attn_specs(nb):
    q_spec = pl.BlockSpec((GROUP, BLOCK, HEAD_DIM), lambda g, n: (g, n, 0))
    cur = pl.BlockSpec((1, BLOCK, HEAD_DIM), lambda g, n: (g, n, 0))
    prev = pl.BlockSpec((1, BLOCK, HEAD_DIM), lambda g, n: (g, jnp.maximum(n - 1, 0), 0))
    sink = pl.BlockSpec((1, GROUP, 1, 1), lambda g, n: (g, 0, 0, 0))
    return q_spec, cur, prev, sink


def _attn_fwd(q, k, v, sinks, name, ride=None):
    heads, n_rows, _ = q.shape
    nb = n_rows // BLOCK
    q_spec, cur, prev, sink = _attn_specs(nb)

    def body(q_ref, kp_ref, kc_ref, vp_ref, vc_ref, s_ref, o_ref):
        n = pl.program_id(1)
        o = _attn_block(q_ref[...].astype(F32), kp_ref[0].astype(F32), kc_ref[0].astype(F32), vp_ref[0].astype(F32),
                        vc_ref[0].astype(F32), s_ref[0], n)
        o_ref[...] = o.astype(o_ref.dtype)

    (out,) = _pcall(
        body, [q, k, k, v, v, sinks.reshape(heads // GROUP, GROUP, 1, 1)], name=name, grid=(heads // GROUP, nb),
        in_specs=[q_spec, prev, cur, prev, cur, sink], out_specs=[q_spec], out_shape=[jax.ShapeDtypeStruct(q.shape, BF16)],
        sem=("parallel", "parallel"), ride=ride)
    return out


def _attn_bwd(q, k, v, sinks, do, name):
    heads, n_rows, _ = q.shape
    kvh = heads // GROUP
    nb = n_rows // BLOCK
    q_spec, cur, prev, sink = _attn_specs(nb)
    part = pl.BlockSpec((1, 1, BLOCK, HEAD_DIM), lambda g, n: (g, n, 0, 0))
    part_shape = jax.ShapeDtypeStruct((kvh, nb, BLOCK, HEAD_DIM), F32)

    def body(q_ref, kp_ref, kc_ref, vp_ref, vc_ref, s_ref, do_ref, dq_ref, dkp_ref, dkc_ref, dvp_ref, dvc_ref, ds_ref):
        n = pl.program_id(1)
        f = functools.partial(_attn_block, n=n)
        _, vjp = jax.vjp(f, q_ref[...].astype(F32), kp_ref[0].astype(F32), kc_ref[0].astype(F32), vp_ref[0].astype(F32),
                         vc_ref[0].astype(F32), s_ref[0])
        dq, dkp, dkc, dvp, dvc, ds = vjp(do_ref[...].astype(F32))
        dq_ref[...] = dq.astype(dq_ref.dtype)
        dkp_ref[0, 0], dkc_ref[0, 0], dvp_ref[0, 0], dvc_ref[0, 0] = dkp, dkc, dvp, dvc

        @pl.when(n == 0)
        def _():
            ds_ref[...] = jnp.zeros_like(ds_ref)

        ds_ref[0] += ds

    return pl.pallas_call(
        body, name=name, grid=(kvh, nb), in_specs=[q_spec, prev, cur, prev, cur, sink, q_spec],
        out_specs=[q_spec, part, part, part, part, sink],
        out_shape=[jax.ShapeDtypeStruct(q.shape, BF16), part_shape, part_shape, part_shape, part_shape,
                   jax.ShapeDtypeStruct((kvh, GROUP, 1, 1), F32)],
        compiler_params=_cparams(("parallel", "arbitrary")),
    )(q, k, k, v, v, sinks.reshape(kvh, GROUP, 1, 1), do)


def _shift_add(own, to_prev, name):
    kvh, nb = own.shape[:2]
    blk = (1, 1, BLOCK, HEAD_DIM)

    def body(a_ref, b_ref, o_ref):
        n = pl.program_id(1)
        o_ref[...] = (a_ref[...] + jnp.where(n < nb - 1, b_ref[...], 0.0)).astype(o_ref.dtype)

    return pl.pallas_call(
        body, name=name, grid=(kvh, nb),
        in_specs=[pl.BlockSpec(blk, lambda g, n: (g, n, 0, 0)),
                  pl.BlockSpec(blk, lambda g, n: (g, jnp.minimum(n + 1, nb - 1), 0, 0))],
        out_specs=pl.BlockSpec(blk, lambda g, n: (g, n, 0, 0)),
        out_shape=jax.ShapeDtypeStruct(own.shape, BF16), compiler_params=_cparams(("parallel", "parallel")),
    )(own, to_prev)


def _gdn_gates(ba, alog, dt, r0, hv):
    lane = lax.broadcasted_iota(jnp.int32, ba.shape, 1)
    t = ba + dt
    softplus = jnp.maximum(t, 0.0) + jnp.log(1.0 + jnp.exp(-jnp.abs(t)))
    val = jnp.where(lane < hv, jax.nn.sigmoid(ba), jnp.where(lane < 2 * hv, -jnp.exp(alog) * softplus, 0.0))
    return jnp.where(_row_mask(r0, ba.shape[0], ba.shape[1]), val, 0.0)


def _l2n(x):
    return x * lax.rsqrt(jnp.sum(x * x, axis=-1, keepdims=True) + 1e-6)


_NN = (((2,), (1,)), ((0,), (0,)))
_NT = (((2,), (2,)), ((0,), (0,)))
_TN = (((1,), (1,)), ((0,), (0,)))


def _bdot(a, b, dims=_NN):
    return lax.dot_general(a.astype(BF16), b.astype(BF16), dims, preferred_element_type=F32)


def _dot3(a, b, dims):
    ah, bh = a.astype(BF16), b.astype(BF16)
    al, bl = (a - ah.astype(F32)).astype(BF16), (b - bh.astype(F32)).astype(BF16)

    def d(p, q):
        return lax.dot_general(p, q, dims, preferred_element_type=F32)

    return d(ah, bh) + (d(ah, bl) + d(al, bh))


@jax.custom_vjp
def _pdot(a, b):
    return _dot3(a, b, _NN)


def _pdot_fwd(a, b):
    return _dot3(a, b, _NN), (a, b)


def _pdot_bwd(res, ct):
    a, b = res
    return _bdot(ct, b, _NT), _bdot(a, ct, _TN)


_pdot.defvjp(_pdot_fwd, _pdot_bwd)


def _scan_chunks(x, reverse):
    n, c = x.shape[0], GDN_CHUNK
    row = lax.broadcasted_iota(jnp.int32, x.shape, 0) % c
    s = 1
    while s < c:
        if reverse:
            x = x + jnp.where(row < c - s, pltpu.roll(x, n - s, axis=0), 0.0)
        else:
            x = x + jnp.where(row >= s, pltpu.roll(x, s, axis=0), 0.0)
        s *= 2
    return x


@jax.custom_vjp
def _cumsum_chunks(x):
    return _scan_chunks(x, False)


_cumsum_chunks.defvjp(lambda x: (_scan_chunks(x, False), None), lambda _, ct: (_scan_chunks(ct, True),))


def _gdn_heads(states, qkv, z, gates, norm_w, hk0, hv_total):
    c, h = GDN_CHUNK, states.shape[0]
    g = h // 2

    def cols(src, starts):
        return jnp.stack([src[:, s:s + GDN_DIM] for s in starts])

    q = _l2n(jax.nn.silu(cols(qkv, [4 * GDN_DIM * t for t in range(g)]))) * (GDN_DIM ** -0.5)
    k = _l2n(jax.nn.silu(cols(qkv, [4 * GDN_DIM * t + GDN_DIM for t in range(g)])))
    q, k = jnp.repeat(q, 2, axis=0), jnp.repeat(k, 2, axis=0)
    v = jax.nn.silu(cols(qkv, [4 * GDN_DIM * (t // 2) + (2 + t % 2) * GDN_DIM for t in range(h)]))
    zz = cols(z, [GDN_DIM * t for t in range(h)])
    lane = lax.broadcasted_iota(jnp.int32, gates.shape, 1)

    def col_of(first):
        return jnp.stack([jnp.sum(jnp.where(lane == first + t, gates, 0.0), axis=1, keepdims=True) for t in range(h)])

    beta_col, g_col = col_of(2 * hk0), col_of(hv_total + 2 * hk0)
    i = lax.broadcasted_iota(jnp.int32, (c, c), 0)
    j = lax.broadcasted_iota(jnp.int32, (c, c), 1)
    causal, strict = (i >= j)[None], (i > j)[None]
    gc = _cumsum_chunks(jnp.broadcast_to(g_col, (h, c, GDN_DIM)).reshape(h * c, GDN_DIM)).reshape(h, c, GDN_DIM)
    gc_i = gc[:, :, :c]
    gc_j = jnp.swapaxes(gc_i, 1, 2)
    gc_last = jnp.broadcast_to(gc[:, c - 1:c, :], (h, GDN_DIM, GDN_DIM))
    decay = jnp.where(causal, jnp.exp(jnp.where(causal, gc_i - gc_j, 0.0)), 0.0)
    k_beta = k * beta_col
    lower = jnp.where(strict, _bdot(k_beta, k, _NT) * decay, 0.0)
    eye = (i == j).astype(F32)[None]
    neg = -lower
    inv = eye + neg
    power = neg
    for _ in range(5):
        power = _pdot(power, power)
        inv = _pdot(inv, eye + power)
    sol = _pdot(inv, jnp.concatenate([v * beta_col, k_beta * jnp.exp(gc)], axis=2))
    u, w = sol[:, :, :GDN_DIM], sol[:, :, GDN_DIM:]
    intra = jnp.where(causal, _bdot(q, k, _NT) * decay, 0.0)
    q_dec = q * jnp.exp(gc)
    k_dec = k * jnp.exp(gc_last[:, :c] - gc)
    v_new = u - _bdot(w, states)
    o = _bdot(q_dec, states) + _bdot(intra, v_new)
    new_states = states * jnp.exp(gc_last) + _bdot(k_dec, v_new, _TN)
    y = _rms(o, norm_w) * jax.nn.silu(zz)
    return jnp.concatenate([y[t] for t in range(h)], axis=1), new_states


GDN_KEY_HEADS_PER_STEP = 8


def _key_heads_per_step(hk_total):
    return math.gcd(hk_total, GDN_KEY_HEADS_PER_STEP)


def _gdn_fwd(cq, proj, gates, norm_w, hk_total, name):
    n_rows = cq.shape[0]
    nc, hv_total = n_rows // GDN_CHUNK, 2 * hk_total
    grp = _key_heads_per_step(hk_total)
    heads = 2 * grp
    zblk0 = cq.shape[1] // (heads * GDN_DIM)

    def body(cq_ref, z_ref, g_ref, w_ref, y_ref, save_ref, state):
        n, hg = pl.program_id(0), pl.program_id(1)

        @pl.when(n == 0)
        def _():
            state[pl.ds(heads * hg, heads)] = jnp.zeros((heads, GDN_DIM, GDN_DIM), F32)

        s_in = state[pl.ds(heads * hg, heads)]
        save_ref[0] = s_in
        y, s_out = _gdn_heads(s_in, cq_ref[...], z_ref[...], g_ref[...], w_ref[...], grp * hg, hv_total)
        y_ref[...] = y.astype(y_ref.dtype)
        state[pl.ds(heads * hg, heads)] = s_out

    return pl.pallas_call(
        body, name=name, grid=(nc, hk_total // grp),
        in_specs=[pl.BlockSpec((GDN_CHUNK, 2 * heads * GDN_DIM), lambda n, h: (n, h)),
                  pl.BlockSpec((GDN_CHUNK, heads * GDN_DIM), lambda n, h: (n, zblk0 + h)),
                  pl.BlockSpec((GDN_CHUNK, LANES), lambda n, h: (n, 0)),
                  pl.BlockSpec((1, GDN_DIM), lambda n, h: (0, 0))],
        out_specs=[pl.BlockSpec((GDN_CHUNK, heads * GDN_DIM), lambda n, h: (n, h)),
                   pl.BlockSpec((1, heads, GDN_DIM, GDN_DIM), lambda n, h: (n, h, 0, 0))],
        out_shape=[jax.ShapeDtypeStruct((n_rows, hv_total * GDN_DIM), BF16),
                   jax.ShapeDtypeStruct((nc, hv_total, GDN_DIM, GDN_DIM), F32)],
        scratch_shapes=[pltpu.VMEM((hv_total, GDN_DIM, GDN_DIM), F32)],
        compiler_params=_cparams(("arbitrary", "arbitrary")),
    )(cq, proj, gates, norm_w.reshape(1, GDN_DIM))


def _gdn_bwd(cq, proj, gates, norm_w, saved, dy, hk_total, name):
    n_rows = cq.shape[0]
    nc, hv_total = n_rows // GDN_CHUNK, 2 * hk_total
    grp = _key_heads_per_step(hk_total)
    heads = 2 * grp
    zblk0 = cq.shape[1] // (heads * GDN_DIM)

    def body(cq_ref, z_ref, g_ref, w_ref, save_ref, dy_ref, dcq_ref, dz_ref, dg_ref, dw_ref, dstate):
        n, hg = pl.program_id(0), pl.program_id(1)

        @pl.when(n == 0)
        def _():
            dstate[pl.ds(heads * hg, heads)] = jnp.zeros((heads, GDN_DIM, GDN_DIM), F32)

        @pl.when((n == 0) & (hg == 0))
        def _():
            dw_ref[...] = jnp.zeros_like(dw_ref)

        @pl.when(hg == 0)
        def _():
            dg_ref[...] = jnp.zeros_like(dg_ref)

        f = functools.partial(_gdn_heads, hk0=grp * hg, hv_total=hv_total)
        _, vjp = jax.vjp(f, save_ref[0], cq_ref[...], z_ref[...], g_ref[...], w_ref[...])
        ds, dcq, dz, dg, dw = vjp((dy_ref[...].astype(F32), dstate[pl.ds(heads * hg, heads)]))
        dstate[pl.ds(heads * hg, heads)] = ds
        dcq_ref[...] = dcq
        dz_ref[...] = dz.astype(dz_ref.dtype)
        dg_ref[...] += dg
        dw_ref[...] += dw

    rev = lambda n: nc - 1 - n
    return pl.pallas_call(
        body, name=name, grid=(nc, hk_total // grp),
        in_specs=[pl.BlockSpec((GDN_CHUNK, 2 * heads * GDN_DIM), lambda n, h: (rev(n), h)),
                  pl.BlockSpec((GDN_CHUNK, heads * GDN_DIM), lambda n, h: (rev(n), zblk0 + h)),
                  pl.BlockSpec((GDN_CHUNK, LANES), lambda n, h: (rev(n), 0)),
                  pl.BlockSpec((1, GDN_DIM), lambda n, h: (0, 0)),
                  pl.BlockSpec((1, heads, GDN_DIM, GDN_DIM), lambda n, h: (rev(n), h, 0, 0)),
                  pl.BlockSpec((GDN_CHUNK, heads * GDN_DIM), lambda n, h: (rev(n), h))],
        out_specs=[pl.BlockSpec((GDN_CHUNK, 2 * heads * GDN_DIM), lambda n, h: (rev(n), h)),
                   pl.BlockSpec((GDN_CHUNK, heads * GDN_DIM), lambda n, h: (rev(n), h)),
                   pl.BlockSpec((GDN_CHUNK, LANES), lambda n, h: (rev(n), 0)),
                   pl.BlockSpec((1, GDN_DIM), lambda n, h: (0, 0))],
        out_shape=[jax.ShapeDtypeStruct(cq.shape, F32), jax.ShapeDtypeStruct((n_rows, hv_total * GDN_DIM), BF16),
                   jax.ShapeDtypeStruct((n_rows, LANES), F32), jax.ShapeDtypeStruct((1, GDN_DIM), F32)],
        scratch_shapes=[pltpu.VMEM((hv_total, GDN_DIM, GDN_DIM), F32)],
        compiler_params=_cparams(("arbitrary", "arbitrary")),
    )(cq, proj, gates, norm_w.reshape(1, GDN_DIM), saved, dy)


def _final_loss(h, w, target, name):
    d = h.shape[1]

    def fn(r0, hv, tv, wv):
        def loss_of(hh, ww):
            err = jnp.where(_row_mask(r0, hh.shape[0], d) & (r0 + lax.broadcasted_iota(jnp.int32, hh.shape, 0) >= FRONT),
                            _rms(hh, ww) - tv, 0.0)
            return 0.5 * jnp.sum(jnp.sum(err * err, axis=1, keepdims=True) / d)

        loss, vjp = jax.vjp(loss_of, hv, wv)
        dh, dw = vjp(jnp.ones((), F32))
        return dh, dh, jnp.zeros((1, LANES), F32) + loss, dw

    dh, dh_bf16, loss, dw = _rowwise(fn, [h, target], [w.reshape(1, -1)], [(d, F32), (d, BF16)], [(1, LANES), (1, d)], name)
    return loss[0, 0], dh, dh_bf16, dw[0]


def _rope_tables(n_rows):
    pos = (jnp.arange(n_rows) - PAD_LEN).astype(F32)
    inv_freq = ROPE_THETA ** (-jnp.arange(0, HEAD_DIM, 2, dtype=F32) / HEAD_DIM)
    ang = pos[:, None] * inv_freq[None, :]
    reps = LANES // (HEAD_DIM // 2)
    return jnp.tile(jnp.cos(ang), (1, reps)), jnp.tile(jnp.sin(ang), (1, reps))


def _to_heads(t):
    n_rows, w = t.shape
    return t.reshape(n_rows, w // HEAD_DIM, HEAD_DIM).transpose(1, 0, 2)


def _from_heads(t):
    heads, n_rows, _ = t.shape
    return t.transpose(1, 0, 2).reshape(n_rows, heads * HEAD_DIM)


def _local_step(h0, target, p, tr):
    n_rows, d = h0.shape
    depth = p["norm_mix"].shape[0]
    cos, sin = _rope_tables(n_rows)
    hk_total = d // GDN_DIM
    hv_total = 2 * hk_total
    conv_dim = 4 * hk_total * GDN_DIM
    qw, kw = d, d // GROUP
    saved = []
    h = h0
    mm = functools.partial(tr.mm, True)
    for i in range(depth):
        kind, j = i % 3, i // 3
        s = {"h": h}
        hn = _rms_fwd(h, p["norm_mix"][i], "rms_fwd")
        s["hn"] = hn
        if kind == 0:
            pre = mm(hn, tr.weight("conv_w_pw1", j), "nn", F32, "mm_pw1")
            u1 = tr.call(True, _us_rows(pre), lambda r: _glu_fwd(pre, p["conv_b_pw1"][j], "glu_fwd", ride=r))
            c = tr.call(True, _us_dwconv(n_rows, d, CONV_KERNEL),
                        lambda r: _dwconv_fwd(u1, p["conv_w_dw"][j], p["conv_b_dw"][j], d, "dwconv31_fwd", ride=r))
            sv = _ln_silu_fwd(c, p["conv_ln_g"][j], p["conv_ln_b"][j], "ln_silu_fwd")
            h = mm(sv, tr.weight("conv_w_pw2", j), "nn", F32, "mm_d_d_res", bias=p["conv_b_pw2"][j], residual=h)
            s.update(pre=pre, u1=u1, c=c, sv=sv)
        elif kind == 1:
            pre = mm(hn, tr.weight("attn_w_qkv", j), "nn", F32, "mm_qkv")
            q, k, v = _qkv_post_fwd(pre, p["attn_b_qkv"][j], cos, sin, qw, kw, "qkv_post_fwd")
            qh, kh, vh = _to_heads(q), _to_heads(k), _to_heads(v)
            o = _from_heads(tr.call(True, _us_attn(qh), lambda r: _attn_fwd(qh, kh, vh, p["attn_sinks"][j], "attn_fwd", ride=r)))
            h = mm(o, tr.weight("attn_w_o", j), "nn", F32, "mm_d_d_res", bias=p["attn_b_o"][j], residual=h)
            s.update(qh=qh, kh=kh, vh=vh, o=o)
        else:
            proj = mm(hn, tr.weight("gdn_w_in", j), "nn", F32, "mm_gdn_in")
            cq = tr.call(True, _us_dwconv(n_rows, conv_dim, GDN_CONV),
                         lambda r: _dwconv_fwd(proj, p["gdn_conv_w"][j], None, conv_dim, "dwconv4_fwd", ride=r))
            vec = _gate_vectors(p["gdn_a_log"][j], p["gdn_dt_bias"][j], hv_total)
            ba_blk = (conv_dim + hv_total * GDN_DIM) // LANES
            (gates,) = _rowwise(lambda r0, bav, av, dv: (_gdn_gates(bav, av, dv, r0, hv_total),), [(proj, LANES, ba_blk)],
                                [vec[0], vec[1]], [(LANES, F32)], [], "gdn_gates_fwd")
            y, states = _gdn_fwd(cq, proj, gates, p["gdn_norm_w"][j], hk_total, "gdn_fwd")
            h = mm(y, tr.weight("gdn_w_out", j), "nn", F32, "mm_gdn_out_res", residual=h)
            s.update(proj=proj, cq=cq, gates=gates, y=y, states=states)
        s["h1"] = h
        hn2 = _rms_fwd(h, p["norm_ffn"][i], "rms_fwd")
        w_gate, w_up = tr.weight("ffn_w_gate", i), tr.weight("ffn_w_up", i)
        gate, up, a = tr.call(True, 2 * _us_mm(n_rows, w_gate.shape[1], d), lambda r: _mmx(
            [(hn2, w_gate, 0), (hn2, w_up, 1)], "nn", "mm_ffn_swiglu", [BF16, BF16, BF16],
            lambda accs: [accs[0], accs[1], jax.nn.silu(accs[0]) * accs[1]], n_acc=2, ride=r))
        h = mm(a, tr.weight("ffn_w_down", i), "nn", F32, "mm_ffn_down_res", residual=h)
        s.update(hn2=hn2, gate=gate, up=up, a=a)
        saved.append(s)

    loss, dh, dhb, g_final = _final_loss(h, p["norm_final"], target, "final_loss")
    g = {k: [None] * v.shape[0] for k, v in p.items() if k not in ("norm_final", "meta_tokens")}
    g["norm_final"] = g_final
    mm = functools.partial(tr.mm, False)
    for i in reversed(range(depth)):
        kind, j = i % 3, i // 3
        s = saved[i]
        tr.give("ffn_w_down", i, mm(s["a"], dhb, "tn", BF16, "mm_dw_down"))
        w_gate, w_up, w_down = tr.weight("ffn_w_gate", i), tr.weight("ffn_w_up", i), tr.weight("ffn_w_down", i)

        def swiglu_bwd(accs, gate, up):
            _, vjp = jax.vjp(lambda gv, uv: jax.nn.silu(gv) * uv, gate.astype(F32), up.astype(F32))
            return list(vjp(accs[0]))

        dgate, dup = tr.call(False, _us_mm(n_rows, w_gate.shape[1], d), lambda r: _mmx(
            [(dhb, w_down, 0)], "nt", "mm_da_swiglu", [BF16, BF16], swiglu_bwd, extras=[s["gate"], s["up"]], ride=r))
        g_gate, g_up = tr.call(False, 2 * _us_mm(d, w_gate.shape[1], n_rows), lambda r: _mmx(
            [(s["hn2"], dgate, 0), (s["hn2"], dup, 1)], "tn", "mm_dw_gate_up", [BF16, BF16], lambda accs: accs, n_acc=2, ride=r))
        tr.give("ffn_w_gate", i, g_gate)
        tr.give("ffn_w_up", i, g_up)
        (dhn2,) = tr.call(False, 2 * _us_mm(n_rows, d, w_gate.shape[1]), lambda r: _mmx(
            [(dgate, w_gate, 0), (dup, w_up, 0)], "nt", "mm_dhn2", [F32], lambda accs: accs, ride=r))
        dh, dhb, g["norm_ffn"][i] = _rms_bwd(s["h1"], p["norm_ffn"][i], dhn2, dh, "rms_bwd")
        if kind == 0:
            g["conv_b_pw2"][j] = _colsum_rows(dh, "colsum_d")
            tr.give("conv_w_pw2", j, mm(s["sv"], dhb, "tn", BF16, "mm_dw_d_d"))
            dsv = mm(dhb, tr.weight("conv_w_pw2", j), "nt", F32, "mm_dx_d_d")
            dc, g["conv_ln_g"][j], g["conv_ln_b"][j] = _ln_silu_bwd(s["c"], p["conv_ln_g"][j], p["conv_ln_b"][j], dsv, "ln_silu_bwd")
            du1, g["conv_w_dw"][j], g["conv_b_dw"][j] = tr.call(
                False, 2 * _us_dwconv(n_rows, d, CONV_KERNEL),
                lambda r: _dwconv_bwd(s["u1"], p["conv_w_dw"][j], dc, d, "dwconv31_bwd", ride=r))
            dpre, g["conv_b_pw1"][j] = _glu_bwd(s["pre"], p["conv_b_pw1"][j], du1, "glu_bwd")
            tr.give("conv_w_pw1", j, mm(s["hn"], dpre, "tn", BF16, "mm_dw_pw1"))
            dhn = mm(dpre, tr.weight("conv_w_pw1", j), "nt", F32, "mm_dx_pw1")
        elif kind == 1:
            g["attn_b_o"][j] = _colsum_rows(dh, "colsum_d")
            tr.give("attn_w_o", j, mm(s["o"], dhb, "tn", BF16, "mm_dw_d_d"))
            do = _to_heads(mm(dhb, tr.weight("attn_w_o", j), "nt", BF16, "mm_dx_d_d_bf16"))
            dq, dkp, dkc, dvp, dvc, dsink = _attn_bwd(s["qh"], s["kh"], s["vh"], p["attn_sinks"][j], do, "attn_bwd")
            g["attn_sinks"][j] = dsink.reshape(-1)
            kvh = kw // HEAD_DIM
            dk = _shift_add(dkc, dkp, "attn_shift_add").reshape(kvh, n_rows, HEAD_DIM)
            dv = _shift_add(dvc, dvp, "attn_shift_add").reshape(kvh, n_rows, HEAD_DIM)
            dpre, g["attn_b_qkv"][j] = _qkv_post_bwd(_from_heads(dq), _from_heads(dk), _from_heads(dv), cos, sin, "qkv_post_bwd")
            tr.give("attn_w_qkv", j, mm(s["hn"], dpre, "tn", BF16, "mm_dw_qkv"))
            dhn = mm(dpre, tr.weight("attn_w_qkv", j), "nt", F32, "mm_dx_qkv")
        else:
            tr.give("gdn_w_out", j, mm(s["y"], dhb, "tn", BF16, "mm_dw_gdn_out"))
            dy = mm(dhb, tr.weight("gdn_w_out", j), "nt", BF16, "mm_dx_gdn_out")
            dcq, dz, dgates, g_nw = _gdn_bwd(s["cq"], s["proj"], s["gates"], p["gdn_norm_w"][j], s["states"], dy, hk_total, "gdn_bwd")
            g["gdn_norm_w"][j] = g_nw[0]
            vec = _gate_vectors(p["gdn_a_log"][j], p["gdn_dt_bias"][j], hv_total)
            ba_blk = (conv_dim + hv_total * GDN_DIM) // LANES

            def gates_bwd(r0, bav, dgv, av, dv):
                _, vjp = jax.vjp(functools.partial(_gdn_gates, r0=r0, hv=hv_total), bav, av, dv)
                return vjp(dgv)

            dba, d_alog, d_dt = _rowwise(gates_bwd, [(s["proj"], LANES, ba_blk), dgates], [vec[0], vec[1]], [(LANES, BF16)],
                                         [(1, LANES), (1, LANES)], "gdn_gates_bwd")
            g["gdn_a_log"][j] = d_alog[0, hv_total:2 * hv_total]
            g["gdn_dt_bias"][j] = d_dt[0, hv_total:2 * hv_total]
            dconv_in, g["gdn_conv_w"][j], _ = tr.call(
                False, 2 * _us_dwconv(n_rows, conv_dim, GDN_CONV),
                lambda r: _dwconv_bwd(s["proj"], p["gdn_conv_w"][j], dcq, conv_dim, "dwconv4_bwd", du_dtype=BF16, ride=r))
            w_in = tr.weight("gdn_w_in", j)
            pad = jnp.zeros((n_rows, w_in.shape[1] - conv_dim - hv_total * GDN_DIM - LANES), BF16)
            dproj = jnp.concatenate([dconv_in, dz, dba, pad], axis=1)
            tr.give("gdn_w_in", j, mm(s["hn"], dproj, "tn", BF16, "mm_dw_gdn_in"))
            dhn = mm(dproj, w_in, "nt", F32, "mm_dx_gdn_in")
        dh, dhb, g["norm_mix"][i] = _rms_bwd(s["h"], p["norm_mix"][i], dhn, dh, "rms_bwd")
    g = {k: (jnp.stack(v) if isinstance(v, list) else v) for k, v in g.items()}
    return loss, dh, g


def _us_mm(m, n, k):
    return 2.0 * m * n * k / 8.0e8


def _us_rows(t):
    return t.shape[0] * t.shape[1] / 8.0e5


def _us_dwconv(n_rows, c, taps):
    return n_rows * c * (taps + 8) / 2.0e6


def _us_attn(qh):
    return qh.shape[0] * qh.shape[1] / 500.0


def _gate_vectors(a_log, dt_bias, hv):
    def place(t):
        return jnp.concatenate([jnp.zeros((hv,), F32), t, jnp.zeros((LANES - 2 * hv,), F32)]).reshape(1, LANES)

    return place(a_log), place(dt_bias)


GDN_IN_ALIGN = 512


def _gdn_group(w, hk):
    lead, kw = w.shape[:-1], hk * GDN_DIM
    q = w[..., :kw].reshape(*lead, hk, 1, GDN_DIM)
    k = w[..., kw:2 * kw].reshape(*lead, hk, 1, GDN_DIM)
    v = w[..., 2 * kw:4 * kw].reshape(*lead, hk, 2, GDN_DIM)
    return jnp.concatenate([q, k, v], axis=-2).reshape(*lead, 4 * kw)


def _gdn_ungroup(w, hk):
    lead, kw = w.shape[:-1], hk * GDN_DIM
    t = w.reshape(*lead, hk, 4, GDN_DIM)
    return jnp.concatenate([t[..., 0, :].reshape(*lead, kw), t[..., 1, :].reshape(*lead, kw),
                            t[..., 2:, :].reshape(*lead, 2 * kw)], axis=-1)


def _gdn_in_layout(w, hk):
    conv_dim = 4 * hk * GDN_DIM
    width = -(-w.shape[-1] // GDN_IN_ALIGN) * GDN_IN_ALIGN
    pad = jnp.zeros(w.shape[:-1] + (width - w.shape[-1],), w.dtype)
    return jnp.concatenate([_gdn_group(w[..., :conv_dim], hk), w[..., conv_dim:], pad], axis=-1)


def _gdn_in_natural(w, hk, in_width):
    conv_dim = 4 * hk * GDN_DIM
    return jnp.concatenate([_gdn_ungroup(w[..., :conv_dim], hk), w[..., conv_dim:in_width]], axis=-1)


def _exchange(slabs, name):
    n_src = len(slabs)

    def body(*refs):
        sems = refs[2 * n_src:]
        copies = [_exchange_copies(refs[t], refs[n_src + t], *sems[3 * t:3 * t + 3], slab=slabs[t]) for t in range(n_src)]
        for cps in copies:
            _exchange_start(cps)
        for cps in copies:
            _exchange_wait(cps)

    any_spec = pl.BlockSpec(memory_space=pl.ANY)
    return pl.pallas_call(
        body, name=name, out_shape=[slab.out_shape() for slab in slabs], in_specs=[any_spec] * n_src,
        out_specs=[any_spec] * n_src, scratch_shapes=[s for _ in slabs for s in _exchange_sems()],
    )(*[slab.array for slab in slabs])


BIG_COL = ("conv_w_pw1", "attn_w_qkv", "gdn_w_in", "ffn_w_gate", "ffn_w_up")
BIG_ROW = ("conv_w_pw2", "attn_w_o", "gdn_w_out", "ffn_w_down")
EXCHANGE_US_PER_BYTE = 11.4e-6
RIDE_PART_US = 150.0


class _Traffic:
    def __init__(self, shards, hk):
        self.shards, self.hk = shards, hk
        self.in_width = N_DEV * shards["gdn_w_in"].shape[-1]
        self.parts_of = {}
        for k, s in shards.items():
            parts = 1
            while (N_DEV * s.shape[1] * s.shape[2] * 2 * EXCHANGE_US_PER_BYTE / parts > RIDE_PART_US
                   and s.shape[1] % (32 * parts) == 0):
                parts *= 2
            self.parts_of[k] = parts
        depth = shards["ffn_w_down"].shape[0]
        order = []
        for i in range(depth):
            j = i // 3
            order += [[("conv_w_pw1", j), ("conv_w_pw2", j)], [("attn_w_qkv", j), ("attn_w_o", j)],
                      [("gdn_w_in", j), ("gdn_w_out", j)]][i % 3]
            order += [("ffn_w_gate", i), ("ffn_w_up", i), ("ffn_w_down", i)]
        self.wanted = [(k, i, part) for k, i in order for part in range(self.parts_of[k])]
        self.arrived = {}
        self.ready = {}
        self.owed = []
        self.received = {}

    def _us(self, k):
        s = self.shards[k]
        return N_DEV * s.shape[1] * s.shape[2] * 2 * EXCHANGE_US_PER_BYTE / self.parts_of[k]

    def _shard_part(self, item):
        k, i, part = item
        rows = self.shards[k].shape[1] // self.parts_of[k]
        return _Slab(self.shards[k], True, i, part * rows, rows)

    def _pick(self, queue, us_of, room):
        taken = []
        while queue and room >= 0.5 * us_of(queue[0]):
            room -= us_of(queue[0])
            taken.append(queue.pop(0))
        return taken

    def _run(self, forward, room, fn):
        if forward:
            taken = self._pick(self.wanted, lambda it: self._us(it[0]), room)
            items = [self._shard_part(it) for it in taken]
        else:
            taken = self._pick(self.owed, lambda it: self._us(it[0][0]), room)
            items = [slab for _, slab in taken]
        if not taken:
            return fn(None)
        ride = _Ride(items)
        out = fn(ride)
        for it, got in zip(taken, ride.outs):
            if forward:
                self.arrived[it] = got
            else:
                self.received[it[0]] = got
        return out

    def call(self, forward, room, fn):
        return self._run(forward, room, fn)

    def mm(self, forward, a, b, mode, out_dtype, name, **kw):
        m = a.shape[1] if mode == "tn" else a.shape[0]
        k = a.shape[0] if mode == "tn" else a.shape[1]
        n = b.shape[0] if mode == "nt" else b.shape[1]
        return self._run(forward, _us_mm(m, n, k), lambda ride: _mm(a, b, mode, out_dtype, name, ride=ride, **kw))

    def _natural(self, k, i):
        parts = self.parts_of[k]
        missing = [(k, i, part) for part in range(parts) if (k, i, part) not in self.arrived]
        if missing:
            for it in missing:
                self.wanted.remove(it)
            got = _exchange([self._shard_part(it) for it in missing], "gather_weights")
            for it, t in zip(missing, got):
                self.arrived[it] = t
        got = [self.arrived[(k, i, part)] for part in range(parts)]
        rows, c = got[0].shape[1], got[0].shape[2]
        if k in BIG_COL:
            return jnp.concatenate([t.transpose(1, 0, 2).reshape(rows, N_DEV * c) for t in got], axis=0)
        return jnp.stack(got, axis=1).reshape(N_DEV * parts * rows, c)

    def weight(self, k, i):
        if (k, i) not in self.ready:
            w = self._natural(k, i)
            self.ready[(k, i)] = _gdn_in_layout(w, self.hk) if k == "gdn_w_in" else w
        return self.ready[(k, i)]

    def give(self, k, i, grad):
        if k == "gdn_w_in":
            grad = _gdn_in_natural(grad, self.hk, self.in_width)
        r, c = self.shards[k].shape[1:]
        pieces = grad.reshape(r, N_DEV, c).transpose(1, 0, 2) if k in BIG_COL else grad.reshape(N_DEV, r, c)
        rows = r // self.parts_of[k]
        for part in range(self.parts_of[k]):
            self.owed.append(((k, i, part), _Slab(pieces, False, 0, part * rows, rows)))

    def gradient_parts(self, k):
        if self.owed:
            got = _exchange([slab for _, slab in self.owed], "scatter_grads")
            for (it, _), t in zip(self.owed, got):
                self.received[it] = t
            self.owed = []
        layers = self.shards[k].shape[0]
        return jnp.stack([jnp.concatenate([self.received[(k, i, part)] for part in range(self.parts_of[k])], axis=1)
                          for i in range(layers)])


def _cast_bf16(w, name):
    n, r, c = w.shape
    tr = _tile(r, max(16, (1 << 20) // c), 16)

    def body(w_ref, o_ref):
        o_ref[...] = w_ref[...].astype(BF16)

    return pl.pallas_call(
        body, name=name, grid=(n, r // tr), in_specs=[pl.BlockSpec((1, tr, c), lambda l, i: (l, i, 0))],
        out_specs=pl.BlockSpec((1, tr, c), lambda l, i: (l, i, 0)), out_shape=jax.ShapeDtypeStruct(w.shape, BF16),
        compiler_params=_cparams(("parallel", "parallel")),
    )(w)


def _adamw(w, g, m, v):
    m = ADAM_B1 * m + (1.0 - ADAM_B1) * g
    v = ADAM_B2 * v + (1.0 - ADAM_B2) * jnp.square(g)
    m_hat = m / (1.0 - ADAM_B1 ** ADAM_STEP)
    v_hat = v / (1.0 - ADAM_B2 ** ADAM_STEP)
    delta = -ADAM_LR * (m_hat / (jnp.sqrt(v_hat) + ADAM_EPS) + ADAM_WD * w)
    return delta, m, v


def _sum8_adam(parts, w, m, v, name):
    n, _, r, c = parts.shape
    tr = _tile(r, max(16, (1 << 18) // c), 16)
    blk = pl.BlockSpec((1, tr, c), lambda l, i: (l, i, 0))

    def body(p_ref, w_ref, m_ref, v_ref, g_ref, d_ref, mo_ref, vo_ref):
        g = p_ref[0, 0].astype(F32)
        for s in range(1, N_DEV):
            g = g + p_ref[0, s].astype(F32)
        delta, m2, v2 = _adamw(w_ref[0], g, m_ref[0], v_ref[0])
        g_ref[0], d_ref[0], mo_ref[0], vo_ref[0] = g, delta, m2, v2

    shp = jax.ShapeDtypeStruct(w.shape, F32)
    return pl.pallas_call(
        body, name=name, grid=(n, r // tr),
        in_specs=[pl.BlockSpec((1, N_DEV, tr, c), lambda l, i: (l, 0, i, 0)), blk, blk, blk],
        out_specs=[blk, blk, blk, blk], out_shape=[shp, shp, shp, shp],
        compiler_params=_cparams(("parallel", "parallel")),
    )(parts, w, m, v)


PACK_ROWS = 8


def _pack(arrays):
    flat = jnp.concatenate([a.reshape(-1).astype(F32) for a in arrays])
    unit = PACK_ROWS * LANES
    total = -(-flat.shape[0] // unit) * unit
    return jnp.concatenate([flat, jnp.zeros((total - flat.shape[0],), F32)]).reshape(-1, LANES)


def _unpack(packed, shapes):
    flat, out, pos = packed.reshape(-1), [], 0
    for s in shapes:
        size = math.prod(s)
        out.append(flat[pos:pos + size].reshape(s))
        pos += size
    return out


SMALL_SHARDED =("meta_tokens", "conv_b_pw1", "conv_w_dw", "conv_b_dw", "conv_ln_g", "conv_ln_b", "conv_b_pw2", "gdn_conv_w")
REPLICATED = ("norm_mix", "norm_ffn", "norm_final", "attn_b_qkv", "attn_sinks", "attn_b_o", "gdn_a_log", "gdn_dt_bias", "gdn_norm_w")
WEIGHTS = ("meta_tokens", "norm_mix", "norm_ffn", "norm_final", "conv_w_pw1", "conv_b_pw1", "conv_w_dw", "conv_b_dw", "conv_ln_g",
           "conv_ln_b", "conv_w_pw2", "conv_b_pw2", "attn_w_qkv", "attn_b_qkv", "attn_sinks", "attn_w_o", "attn_b_o", "gdn_w_in",
           "gdn_conv_w", "gdn_a_log", "gdn_dt_bias", "gdn_norm_w", "gdn_w_out", "ffn_w_gate", "ffn_w_up", "ffn_w_down")


def kernel(x, meta_tokens, norm_mix, norm_ffn, norm_final, conv_w_pw1, conv_b_pw1, conv_w_dw, conv_b_dw, conv_ln_g, conv_ln_b, conv_w_pw2, conv_b_pw2, attn_w_qkv, attn_b_qkv, attn_sinks, attn_w_o, attn_b_o, gdn_w_in, gdn_conv_w, gdn_a_log, gdn_dt_bias, gdn_norm_w, gdn_w_out, ffn_w_gate, ffn_w_up, ffn_w_down, loss_target, m_meta_tokens, m_norm_mix, m_norm_ffn, m_norm_final, m_conv_w_pw1, m_conv_b_pw1, m_conv_w_dw, m_conv_b_dw, m_conv_ln_g, m_conv_ln_b, m_conv_w_pw2, m_conv_b_pw2, m_attn_w_qkv, m_attn_b_qkv, m_attn_sinks, m_attn_w_o, m_attn_b_o, m_gdn_w_in, m_gdn_conv_w, m_gdn_a_log, m_gdn_dt_bias, m_gdn_norm_w, m_gdn_w_out, m_ffn_w_gate, m_ffn_w_up, m_ffn_w_down, v_meta_tokens, v_norm_mix, v_norm_ffn, v_norm_final, v_conv_w_pw1, v_conv_b_pw1, v_conv_w_dw, v_conv_b_dw, v_conv_ln_g, v_conv_ln_b, v_conv_w_pw2, v_conv_b_pw2, v_attn_w_qkv, v_attn_b_qkv, v_attn_sinks, v_attn_w_o, v_attn_b_o, v_gdn_w_in, v_gdn_conv_w, v_gdn_a_log, v_gdn_dt_bias, v_gdn_norm_w, v_gdn_w_out, v_ffn_w_gate, v_ffn_w_up, v_ffn_w_down):
    a = dict(locals())
    me = 4 * lax.axis_index("x") + 2 * lax.axis_index("y") + lax.axis_index("c")
    d = x.shape[-1]

    hk = d // GDN_DIM
    full = {k: a[k] for k in REPLICATED}
    shard_shapes = [a[k].shape for k in SMALL_SHARDED]
    (got,) = _exchange([_Slab(_pack([a[k] for k in SMALL_SHARDED])[None], True)], "gather_small")
    per_dev = [_unpack(got[s], shard_shapes) for s in range(N_DEV)]
    for i, k in enumerate(SMALL_SHARDED):
        st = jnp.stack([per_dev[s][i] for s in range(N_DEV)], axis=-2)
        full[k] = st.reshape(st.shape[:-2] + (N_DEV * st.shape[-1],))
    traffic = _Traffic({k: _cast_bf16(a[k], "cast_bf16") for k in BIG_COL + BIG_ROW}, hk)

    h0 = jnp.concatenate([jnp.zeros((PAD_LEN, d), F32), full["meta_tokens"], x[0]], axis=0)
    target = jnp.concatenate([jnp.zeros((FRONT, d), F32), loss_target[0]], axis=0)
    loss, dh0, g = _local_step(h0, target, {**full, "gdn_conv_w": _gdn_group(full["gdn_conv_w"], hk)}, traffic)
    g["gdn_conv_w"] = _gdn_ungroup(g["gdn_conv_w"], hk)
    g["meta_tokens"] = dh0[PAD_LEN:FRONT]
    loss = lax.psum(loss, AXES)
    grad_x = dh0[FRONT:][None]

    grads, deltas, new_m, new_v = {}, {}, {}, {}
    for k in BIG_COL + BIG_ROW:
        grads[k], deltas[k], new_m[k], new_v[k] = _sum8_adam(traffic.gradient_parts(k), a[k], a["m_" + k], a["v_" + k], "sum8_adamw")

    small = SMALL_SHARDED + REPLICATED
    full_shapes = [full[k].shape for k in small]
    (got,) = _exchange([_Slab(_pack([g[k] for k in small])[None], True)], "gather_small_grads")
    (total,) = _rowwise(lambda r0, *t: (functools.reduce(lambda p, q: p + q, t),), [got[s] for s in range(N_DEV)], [],
                        [(LANES, F32)], [], "sum8_small", tm=got.shape[1])
    for k, t in zip(small, _unpack(total, full_shapes)):
        if k in SMALL_SHARDED:
            c = a[k].shape[-1]
            t = lax.dynamic_index_in_dim(t.reshape(t.shape[:-1] + (N_DEV, c)), me, axis=t.ndim - 1, keepdims=False)
        grads[k] = t
    shapes = [a[k].shape for k in small]
    packed = [_pack([src[k] for k in small]) for src in (grads, a, {k: a["m_" + k] for k in small}, {k: a["v_" + k] for k in small})]

    def small_adam(r0, gv, wv, mv, vv):
        return _adamw(wv, gv, mv, vv)

    outs = _rowwise(small_adam, packed, [], [(LANES, F32)] * 3, [], "adamw_small", tm=packed[0].shape[0])
    for dst, o in zip((deltas, new_m, new_v), outs):
        for k, t in zip(small, _unpack(o, shapes)):
            dst[k] = t

    return (loss, grad_x, *[grads[k] for k in WEIGHTS], *[deltas[k] for k in WEIGHTS], *[new_m[k] for k in WEIGHTS],
            *[new_v[k] for k in WEIGHTS])
```

```python
import functools
import math

import jax
import jax.numpy as jnp
from jax import lax
from jax.experimental import pallas as pl
from jax.experimental.pallas import tpu as pltpu

F32 = jnp.float32
BF16 = jnp.bfloat16

AXES = ("x", "y", "c")
N_DEV = 8

N_META = 16
FRONT = 128
PAD_LEN = FRONT - N_META
NORM_EPS = 1e-6
LN_EPS = 1e-5
NEG_INF = -1e30
CONV_KERNEL = 31
HEAD_DIM = 64
GROUP = 8
BLOCK = 128
ROPE_THETA = 10000.0
GDN_DIM = 128
GDN_CONV = 4
GDN_CHUNK = 64
ADAM_LR, ADAM_B1, ADAM_B2, ADAM_EPS, ADAM_WD, ADAM_STEP = 0.001, 0.9, 0.999, 1e-08, 0.01, 10

VMEM_LIMIT_BYTES = 52 * 1024 * 1024
LANES = 128
MM_VMEM_BUDGET_BYTES = 40 * 1024 * 1024
MM_ROWS_MAX, MM_COLS_MAX, MM_DEPTH_MAX = 1664, 2048, 2048
MXU_FLOPS = 9.0e14
HBM_BYTES_PER_S = 3.0e12
GRID_STEP_S = 0.4e-6
MM_CHUNK = 256
ROW_TILE_BUDGET_BYTES = 24 * 1024 * 1024
CONV_HALO = 32


def _tile(n, pref, align=128):
    best = None
    for t in range(align, min(n, pref) + 1, align):
        if n % t == 0:
            best = t
    return best if best is not None else n


def _cparams(sem):
    return pltpu.CompilerParams(dimension_semantics=sem, vmem_limit_bytes=VMEM_LIMIT_BYTES)


class _Slab:
    def __init__(self, array, gather, index=0, row0=0, rows=None):
        self.array, self.gather, self.index, self.row0 = array, gather, index, row0
        self.rows = array.shape[1] if rows is None else rows

    def out_shape(self):
        return jax.ShapeDtypeStruct((N_DEV, self.rows, self.array.shape[2]), self.array.dtype)


def _exchange_copies(src_ref, out_ref, send_sems, recv_sems, local_sem, slab):
    x, y, c = lax.axis_index("x"), lax.axis_index("y"), lax.axis_index("c")
    me = 4 * x + 2 * y + c
    rows = pl.ds(slab.row0, slab.rows)

    def src_for(dev):
        return src_ref.at[slab.index, rows] if slab.gather else src_ref.at[dev, rows]

    local = [pltpu.make_async_copy(src_for(me), out_ref.at[me], local_sem.at[0])]
    remote = []
    for k in range(1, N_DEV):
        px = 1 - x if k & 4 else x
        py = 1 - y if k & 2 else y
        pc = 1 - c if k & 1 else c
        remote.append(pltpu.make_async_remote_copy(
            src_ref=src_for(4 * px + 2 * py + pc), dst_ref=out_ref.at[me], send_sem=send_sems.at[k - 1], recv_sem=recv_sems.at[k - 1],
            device_id=(px, py, pc), device_id_type=pl.DeviceIdType.MESH))
    return remote, local


def _exchange_start(copies):
    remote, local = copies
    for cp in local + remote:
        cp.start()


def _exchange_wait(copies):
    remote, local = copies
    for cp in remote:
        cp.wait_send()
    for cp in remote:
        cp.wait_recv()
    for cp in local:
        cp.wait()


def _exchange_sems():
    return [pltpu.SemaphoreType.DMA((N_DEV - 1,)), pltpu.SemaphoreType.DMA((N_DEV - 1,)), pltpu.SemaphoreType.DMA((1,))]


class _Ride:
    def __init__(self, items):
        self.items = items
        self.outs = None


def _pcall(body, args, *, name, grid, in_specs, out_specs, out_shape, sem, scratch_shapes=(), ride=None):
    if ride is None:
        return pl.pallas_call(body, name=name, grid=grid, in_specs=in_specs, out_specs=out_specs, out_shape=out_shape,
                              scratch_shapes=list(scratch_shapes), compiler_params=_cparams(sem))(*args)
    n_in, n_out, n_scr, n_ride = len(in_specs), len(out_specs), len(scratch_shapes), len(ride.items)
    any_spec = pl.BlockSpec(memory_space=pl.ANY)

    def with_ride(*refs):
        pos = 0
        ins, pos = refs[pos:pos + n_in], pos + n_in
        srcs, pos = refs[pos:pos + n_ride], pos + n_ride
        outs, pos = refs[pos:pos + n_out], pos + n_out
        dsts, pos = refs[pos:pos + n_ride], pos + n_ride
        scr, pos = refs[pos:pos + n_scr], pos + n_scr
        sems = refs[pos:]
        first = functools.reduce(lambda p, q: p & q, [pl.program_id(d) == 0 for d in range(len(grid))])
        last = functools.reduce(lambda p, q: p & q, [pl.program_id(d) == grid[d] - 1 for d in range(len(grid))])

        def copies():
            return [_exchange_copies(srcs[t], dsts[t], *sems[3 * t:3 * t + 3], slab=ride.items[t]) for t in range(n_ride)]

        @pl.when(first)
        def _():
            for cps in copies():
                _exchange_start(cps)

        body(*ins, *outs, *scr)

        @pl.when(last)
        def _():
            for cps in copies():
                _exchange_wait(cps)

    res = pl.pallas_call(
        with_ride, name=name, grid=grid, in_specs=list(in_specs) + [any_spec] * n_ride,
        out_specs=list(out_specs) + [any_spec] * n_ride,
        out_shape=list(out_shape) + [slab.out_shape() for slab in ride.items],
        scratch_shapes=list(scratch_shapes) + [s for _ in ride.items for s in _exchange_sems()],
        compiler_params=_cparams(("arbitrary",) * len(grid)),
    )(*args, *[slab.array for slab in ride.items])
    ride.outs = list(res[n_out:])
    return list(res[:n_out])


def _mm_shape(a, b, mode):
    if mode == "nn":
        (m, k), (k2, n) = a.shape, b.shape
    elif mode == "nt":
        (m, k), (n, k2) = a.shape, b.shape
    else:
        (k, m), (k2, n) = a.shape, b.shape
    assert k == k2, (a.shape, b.shape, mode)
    return m, n, k


def _mmx(pairs, mode, name, out_dtypes, epilogue, extras=(), n_acc=1, ride=None):
    m, n, k = _mm_shape(pairs[0][0], pairs[0][1], mode)
    lefts = {id(a): a for a, _, _ in pairs}.values()
    tile_bytes = sum((1 if e.shape[0] == 1 and m != 1 else 0) * e.dtype.itemsize for e in extras)
    full_bytes = sum(e.dtype.itemsize for e in extras if not (e.shape[0] == 1 and m != 1))
    full_bytes += sum(jnp.dtype(dt).itemsize for dt in out_dtypes)

    def plan(tm, tn, tk):
        ni, nj, steps_k = m // tm, n // tn, k // tk
        a_bytes = sum(tm * tk * a.dtype.itemsize for a in lefts)
        b_bytes = sum(tk * tn * b.dtype.itemsize for _, b, _ in pairs)
        io_bytes = tm * tn * full_bytes + tn * tile_bytes
        vmem = 2 * (a_bytes + b_bytes + io_bytes) + (n_acc * tm * tn * 4 if steps_k > 1 else tm * MM_CHUNK * 4 * n_acc)
        stream = (a_bytes if steps_k > 1 else a_bytes / nj) + (b_bytes if nj * steps_k > 1 else b_bytes / ni) + io_bytes / steps_k
        step = max(2.0 * tm * tn * tk * len(pairs) / MXU_FLOPS, stream / HBM_BYTES_PER_S) + GRID_STEP_S
        return ni * nj * steps_k * step, vmem

    def divisors(size, cap):
        return [t for t in range(LANES, min(size, cap) + 1, LANES) if size % t == 0] or [size]

    options = [(tm, tn, tk) for tm in divisors(m, MM_ROWS_MAX) for tn in divisors(n, MM_COLS_MAX) for tk in divisors(k, MM_DEPTH_MAX)
               if tk >= min(k, 512)]
    fitting = [o for o in options if plan(*o)[1] <= MM_VMEM_BUDGET_BYTES]
    tm, tn, tk = min(fitting or options, key=lambda o: (plan(*o)[0] if fitting else plan(*o)[1]))
    nk = k // tk
    dims = {"nn": (((1,), (0,)), ((), ())), "nt": (((1,), (1,)), ((), ())), "tn": (((0,), (0,)), ((), ()))}[mode]
    a_spec = pl.BlockSpec((tk, tm), lambda i, j, kk: (kk, i)) if mode == "tn" else pl.BlockSpec((tm, tk), lambda i, j, kk: (i, kk))
    b_spec = pl.BlockSpec((tn, tk), lambda i, j, kk: (j, kk)) if mode == "nt" else pl.BlockSpec((tk, tn), lambda i, j, kk: (kk, j))
    ins, specs, where = [], [], []
    for a, b, _ in pairs:
        assert _mm_shape(a, b, mode) == (m, n, k)
        ia = next((t for t, x in enumerate(ins) if x is a), None)
        if ia is None:
            ins.append(a)
            specs.append(a_spec)
            ia = len(ins) - 1
        ins.append(b)
        specs.append(b_spec)
        where.append((ia, len(ins) - 1))
    n_ops = len(ins)
    for e in extras:
        ins.append(e)
        specs.append(pl.BlockSpec((1, tn), lambda i, j, kk: (0, j)) if e.shape[0] == 1 and m != 1 else
                     pl.BlockSpec((tm, tn), lambda i, j, kk: (i, j)))
    n_ex, n_out = len(extras), len(out_dtypes)

    def body(*refs):
        ex_refs = refs[n_ops:n_ops + n_ex]
        o_refs = refs[n_ops + n_ex:n_ops + n_ex + n_out]
        accs = refs[n_ops + n_ex + n_out:]
        kk = pl.program_id(2)

        @pl.when(kk == 0)
        def _():
            for acc in accs:
                acc[...] = jnp.zeros_like(acc)

        for (ia, ib), (_, _, which) in zip(where, pairs):
            accs[which][...] += lax.dot_general(refs[ia][...].astype(BF16), refs[ib][...].astype(BF16), dims,
                                                preferred_element_type=F32)

        @pl.when(kk == nk - 1)
        def _():
            tiles = epilogue([acc[...] for acc in accs], *[e[...] for e in ex_refs])
            for o_ref, t in zip(o_refs, tiles):
                o_ref[...] = t.astype(o_ref.dtype)

    def body_one_pass(*refs):
        ex_refs = refs[n_ops:n_ops + n_ex]
        o_refs = refs[n_ops + n_ex:n_ops + n_ex + n_out]
        lefts = {ia: refs[ia][...].astype(BF16) for ia, _ in where}
        for c0 in range(0, tn, MM_CHUNK):
            cw = min(MM_CHUNK, tn - c0)
            accs = [None] * n_acc
            for (ia, ib), (_, _, which) in zip(where, pairs):
                right = refs[ib][c0:c0 + cw, :] if mode == "nt" else refs[ib][:, c0:c0 + cw]
                part = lax.dot_general(lefts[ia], right.astype(BF16), dims, preferred_element_type=F32)
                accs[which] = part if accs[which] is None else accs[which] + part
            tiles = epilogue(accs, *[e[:, c0:c0 + cw] for e in ex_refs])
            for o_ref, t in zip(o_refs, tiles):
                o_ref[:, c0:c0 + cw] = t.astype(o_ref.dtype)

    return _pcall(
        body_one_pass if nk == 1 else body, ins, name=name, grid=(m // tm, n // tn, nk), in_specs=specs,
        out_specs=[pl.BlockSpec((tm, tn), lambda i, j, kk: (i, j))] * n_out,
        out_shape=[jax.ShapeDtypeStruct((m, n), dt) for dt in out_dtypes],
        scratch_shapes=[] if nk == 1 else [pltpu.VMEM((tm, tn), F32)] * n_acc,
        sem=("parallel", "parallel", "arbitrary"), ride=ride)


def _mm(a, b, mode, out_dtype, name, bias=None, residual=None, ride=None):
    extras = ([] if bias is None else [bias.reshape(1, -1).astype(F32)]) + ([] if residual is None else [residual])
    (out,) = _mmx([(a, b, 0)], mode, name, [out_dtype], lambda accs, *ex: [functools.reduce(lambda p, q: p + q.astype(F32), ex, accs[0])],
                  extras=extras, ride=ride)
    return out


def _rowwise(fn, rows, consts, out_rows, out_accs, name, tm=None, ride=None):
    rows = [r if isinstance(r, tuple) else (r, r.shape[1], 0) for r in rows]
    n_rows = rows[0][0].shape[0]
    if tm is None:
        row_bytes = sum(w * r.dtype.itemsize for r, w, _ in rows) + sum(w * jnp.dtype(dt).itemsize for w, dt in out_rows)
        tm = _tile(n_rows, max(8, min(640, ROW_TILE_BUDGET_BYTES // (2 * row_bytes))), 8)
    steps = n_rows // tm
    in_specs = [pl.BlockSpec((tm, w), functools.partial(lambda i, c: (i, c), c=cb)) for _, w, cb in rows]
    in_specs += [pl.BlockSpec(c.shape, lambda i: (0, 0)) for c in consts]
    out_specs = [pl.BlockSpec((tm, w), lambda i: (i, 0)) for w, _ in out_rows]
    out_specs += [pl.BlockSpec(s, lambda i: (0, 0)) for s in out_accs]
    out_shape = [jax.ShapeDtypeStruct((n_rows, w), d) for w, d in out_rows]
    out_shape += [jax.ShapeDtypeStruct(s, F32) for s in out_accs]
    n_in, n_or = len(rows) + len(consts), len(out_rows)

    def body(*refs):
        i = pl.program_id(0)
        vals = fn(i * tm, *[r[...] for r in refs[:n_in]])
        outs = refs[n_in:]
        for o_ref, v in zip(outs[:n_or], vals[:n_or]):
            o_ref[...] = v.astype(o_ref.dtype)
        if out_accs:
            @pl.when(i == 0)
            def _():
                for a_ref in outs[n_or:]:
                    a_ref[...] = jnp.zeros_like(a_ref)

            for a_ref, v in zip(outs[n_or:], vals[n_or:]):
                a_ref[...] += v

    return _pcall(body, [r[0] for r in rows] + list(consts), name=name, grid=(steps,), in_specs=in_specs, out_specs=out_specs,
                  out_shape=out_shape, sem=("arbitrary",) if out_accs else ("parallel",), ride=ride)


def _colsum(v):
    return jnp.sum(v, axis=0, keepdims=True)


def _rms(h, w):
    return h * lax.rsqrt(jnp.mean(h * h, axis=-1, keepdims=True) + NORM_EPS) * w


def _rms_fwd(h, w, name):
    (hn,) = _rowwise(lambda r0, hv, wv: (_rms(hv, wv),), [h], [w.reshape(1, -1)], [(h.shape[1], BF16)], [], name)
    return hn


def _rms_bwd(h, w, dhn, dh_in, name):
    d = h.shape[1]

    def fn(r0, hv, dv, rv, wv):
        _, vjp = jax.vjp(_rms, hv, wv)
        dh, dw = vjp(dv.astype(F32))
        return dh + rv, dh + rv, dw, _colsum(dh + rv)

    dh, dh_bf16, dw, sums = _rowwise(fn, [h, dhn, dh_in], [w.reshape(1, -1)], [(d, F32), (d, BF16)], [(1, d), (1, d)], name)
    return dh, dh_bf16, dw[0], sums[0]


def _glu(p, b):
    t = p + b
    d = t.shape[1] // 2
    return t[:, :d] * jax.nn.sigmoid(t[:, d:])


def _glu_fwd(p, b, name, ride=None):
    (u,) = _rowwise(lambda r0, v, bv: (_glu(v, bv),), [p], [b.reshape(1, -1)], [(p.shape[1] // 2, F32)], [], name, ride=ride)
    return u


def _glu_bwd(p, b, du, name):
    def fn(r0, v, dv, bv):
        _, vjp = jax.vjp(_glu, v, bv)
        dp, db = vjp(dv)
        return dp, db

    dp, db = _rowwise(fn, [p, du], [b.reshape(1, -1)], [(p.shape[1], BF16)], [(1, p.shape[1])], name)
    return dp, db[0]


def _ln_silu(c, g, b):
    mu = jnp.mean(c, axis=-1, keepdims=True)
    xc = c - mu
    var = jnp.mean(xc * xc, axis=-1, keepdims=True)
    return jax.nn.silu(xc * lax.rsqrt(var + LN_EPS) * g + b)


def _ln_silu_fwd(c, g, b, name):
    (s,) = _rowwise(lambda r0, v, gv, bv: (_ln_silu(v, gv, bv),), [c], [g.reshape(1, -1), b.reshape(1, -1)],
                    [(c.shape[1], BF16)], [], name)
    return s


def _ln_silu_bwd(c, g, b, ds, name):
    d = c.shape[1]

    def fn(r0, v, dv, gv, bv):
        _, vjp = jax.vjp(_ln_silu, v, gv, bv)
        return vjp(dv.astype(F32))

    dc, dg, db = _rowwise(fn, [c, ds], [g.reshape(1, -1), b.reshape(1, -1)], [(d, F32)], [(1, d), (1, d)], name)
    return dc, dg[0], db[0]


def _rot_half(x):
    w = x.shape[1]
    lane = lax.broadcasted_iota(jnp.int32, x.shape, 1)
    lo = (lane % HEAD_DIM) < (HEAD_DIM // 2)
    return jnp.where(lo, -pltpu.roll(x, w - HEAD_DIM // 2, axis=1), pltpu.roll(x, HEAD_DIM // 2, axis=1))


def _qkv_post_fwd(pre, b, cos, sin, qw, kw, name):
    reps = (qw + kw) // LANES

    def fn(r0, pv, cv, sv, bv):
        t = pv + bv
        tq = t[:, :qw + kw]
        y = tq * jnp.tile(cv, (1, reps)) + _rot_half(tq) * jnp.tile(sv, (1, reps))
        return y[:, :qw], y[:, qw:], t[:, qw + kw:]

    return _rowwise(fn, [pre, cos, sin], [b.reshape(1, -1)], [(qw, BF16), (kw, BF16), (kw, BF16)], [], name)


def _qkv_post_bwd(dq, dk, dv, cos, sin, name):
    qw, kw = dq.shape[1], dk.shape[1]
    reps = (qw + kw) // LANES
    width = qw + 2 * kw

    def fn(r0, dqv, dkv, dvv, cv, sv):
        dy = jnp.concatenate([dqv.astype(F32), dkv.astype(F32)], axis=1)
        dt = dy * jnp.tile(cv, (1, reps)) - _rot_half(dy * jnp.tile(sv, (1, reps)))
        dpre = jnp.concatenate([dt, dvv.astype(F32)], axis=1)
        return dpre, _colsum(dpre)

    dpre, db = _rowwise(fn, [dq, dk, dv, cos, sin], [], [(width, BF16)], [(1, width)], name)
    return dpre, db[0]


def _dw_tiles(n_rows, c):
    return _tile(n_rows, 640, 8), _tile(c, 512)


class _RowWindow:
    SUBLANES = 8

    def __init__(self, value):
        self.copies = {0: value}

    def rows(self, off, n):
        q, s = divmod(off, self.SUBLANES)
        if s not in self.copies:
            base = self.copies[0]
            self.copies[s] = pltpu.roll(base, base.shape[0] - s, axis=0)
        return self.copies[s][self.SUBLANES * q:self.SUBLANES * q + n, :]


def _row_mask(r0, n, width):
    row = r0 + lax.broadcasted_iota(jnp.int32, (n, width), 0)
    return row >= PAD_LEN


def _dwconv_fwd(u, w, bias, c, name, ride=None):
    n_rows, taps = u.shape[0], w.shape[0]
    tr, cb = _dw_tiles(n_rows, c)
    kp = -(-taps // 8) * 8
    wp = jnp.concatenate([w.astype(F32), jnp.zeros((kp - taps, c), F32)], axis=0)
    bp = jnp.zeros((1, c), F32) if bias is None else bias.reshape(1, c).astype(F32)

    def body(cur_ref, prev_ref, w_ref, b_ref, o_ref):
        r = pl.program_id(1)
        cur = jnp.where(_row_mask(r * tr, tr, cb), cur_ref[...], 0.0)
        tail = jnp.where(_row_mask(r * tr - CONV_HALO, CONV_HALO, cb) & (r > 0), prev_ref[...], 0.0)
        win = _RowWindow(jnp.concatenate([tail, cur], axis=0))
        acc = jnp.zeros((tr, cb), F32) + b_ref[...]
        for k in range(taps):
            acc = acc + w_ref[k:k + 1, :] * win.rows(CONV_HALO - (taps - 1) + k, tr)
        o_ref[...] = acc

    per = tr // CONV_HALO
    (out,) = _pcall(
        body, [u, u, wp, bp], name=name, grid=(c // cb, n_rows // tr),
        in_specs=[pl.BlockSpec((tr, cb), lambda j, r: (r, j)),
                  pl.BlockSpec((CONV_HALO, cb), lambda j, r: (jnp.maximum(r * per - 1, 0), j)),
                  pl.BlockSpec((kp, cb), lambda j, r: (0, j)),
                  pl.BlockSpec((1, cb), lambda j, r: (0, j))],
        out_specs=[pl.BlockSpec((tr, cb), lambda j, r: (r, j))],
        out_shape=[jax.ShapeDtypeStruct((n_rows, c), F32)], sem=("parallel", "parallel"), ride=ride)
    return out


def _dwconv_bwd(u, w, dc, c, name, du_dtype=F32, ride=None):
    n_rows, taps = u.shape[0], w.shape[0]
    tr, cb = _dw_tiles(n_rows, c)
    nr = n_rows // tr
    per = tr // CONV_HALO
    kp = -(-taps // 8) * 8
    wp = jnp.concatenate([w.astype(F32), jnp.zeros((kp - taps, c), F32)], axis=0)

    def body(cur_ref, prev_ref, d_ref, dnext_ref, w_ref, du_ref, dw_ref, db_ref):
        r = pl.program_id(1)
        cur = jnp.where(_row_mask(r * tr, tr, cb), cur_ref[...], 0.0)
        tail = jnp.where(_row_mask(r * tr - CONV_HALO, CONV_HALO, cb) & (r > 0), prev_ref[...], 0.0)
        win_u = _RowWindow(jnp.concatenate([tail, cur], axis=0))
        d = d_ref[...]
        head = jnp.where(r < nr - 1, dnext_ref[...], 0.0)
        win_d = _RowWindow(jnp.concatenate([d, head], axis=0))

        @pl.when(r == 0)
        def _():
            dw_ref[...] = jnp.zeros_like(dw_ref)
            db_ref[...] = jnp.zeros_like(db_ref)

        du = jnp.zeros((tr, cb), F32)
        for k in range(taps):
            du = du + w_ref[k:k + 1, :] * win_d.rows(taps - 1 - k, tr)
            dw_ref[k:k + 1, :] += _colsum(d * win_u.rows(CONV_HALO - (taps - 1) + k, tr))
        du_ref[...] = jnp.where(_row_mask(r * tr, tr, cb), du, 0.0).astype(du_ref.dtype)
        db_ref[...] += _colsum(d)

    du, dw, db = _pcall(
        body, [u, u, dc, dc, wp], name=name, grid=(c // cb, nr),
        in_specs=[pl.BlockSpec((tr, cb), lambda j, r: (r, j)),
                  pl.BlockSpec((CONV_HALO, cb), lambda j, r: (jnp.maximum(r * per - 1, 0), j)),
                  pl.BlockSpec((tr, cb), lambda j, r: (r, j)),
                  pl.BlockSpec((CONV_HALO, cb), lambda j, r: (jnp.minimum((r + 1) * per, nr * per - 1), j)),
                  pl.BlockSpec((kp, cb), lambda j, r: (0, j))],
        out_specs=[pl.BlockSpec((tr, cb), lambda j, r: (r, j)),
                   pl.BlockSpec((kp, cb), lambda j, r: (0, j)),
                   pl.BlockSpec((1, cb), lambda j, r: (0, j))],
        out_shape=[jax.ShapeDtypeStruct((n_rows, c), du_dtype), jax.ShapeDtypeStruct((kp, c), F32),
                   jax.ShapeDtypeStruct((1, c), F32)], sem=("parallel", "arbitrary"), ride=ride)
    return du, dw[:taps], db[0]


def _attn_block(q, kprev, kcur, vprev, vcur, sink, n):
    qf = q.reshape(GROUP * BLOCK, HEAD_DIM).astype(BF16)
    kb = jnp.concatenate([kprev, kcur], axis=0).astype(BF16)
    vb = jnp.concatenate([vprev, vcur], axis=0).astype(BF16)
    s = lax.dot_general(qf, kb, (((1,), (1,)), ((), ())), preferred_element_type=F32) * (HEAD_DIM ** -0.5)
    s = s.reshape(GROUP, BLOCK, 2 * BLOCK)
    qi = lax.broadcasted_iota(jnp.int32, (BLOCK, 2 * BLOCK), 0)
    kj = lax.broadcasted_iota(jnp.int32, (BLOCK, 2 * BLOCK), 1)
    dist = qi + BLOCK - kj
    allowed = (dist >= 0) & (dist < BLOCK) & ((n - 1) * BLOCK + kj >= PAD_LEN)
    s = jnp.where(allowed[None], s, NEG_INF)
    m = lax.stop_gradient(jnp.maximum(jnp.max(s, axis=-1, keepdims=True), sink))
    e = jnp.exp(s - m)
    p = e / (jnp.sum(e, axis=-1, keepdims=True) + jnp.exp(sink - m))
    o = jnp.dot(p.reshape(GROUP * BLOCK, 2 * BLOCK).astype(BF16), vb, preferred_element_type=F32)
    return o.reshape(GROUP, BLOCK, HEAD_DIM)


def _attn_specs(nb):
    q_spec = pl.BlockSpec((GROUP, BLOCK, HEAD_DIM), lambda g, n: (g, n, 0))
    cur = pl.BlockSpec((1, BLOCK, HEAD_DIM), lambda g, n: (g, n, 0))
    prev = pl.BlockSpec((1, BLOCK, HEAD_DIM), lambda g, n: (g, jnp.maximum(n - 1, 0), 0))
    sink = pl.BlockSpec((1, GROUP, 1, 1), lambda g, n: (g, 0, 0, 0))
    return q_spec, cur, prev, sink


def _attn_fwd(q, k, v, sinks, name, ride=None):
    heads, n_rows, _ = q.shape
    nb = n_rows // BLOCK
    q_spec, cur, prev, sink = _attn_specs(nb)

    def body(q_ref, kp_ref, kc_ref, vp_ref, vc_ref, s_ref, o_ref):
        n = pl.program_id(1)
        o = _attn_block(q_ref[...].astype(F32), kp_ref[0].astype(F32), kc_ref[0].astype(F32), vp_ref[0].astype(F32),
                        vc_ref[0].astype(F32), s_ref[0], n)
        o_ref[...] = o.astype(o_ref.dtype)

    (out,) = _pcall(
        body, [q, k, k, v, v, sinks.reshape(heads // GROUP, GROUP, 1, 1)], name=name, grid=(heads // GROUP, nb),
        in_specs=[q_spec, prev, cur, prev, cur, sink], out_specs=[q_spec], out_shape=[jax.ShapeDtypeStruct(q.shape, BF16)],
        sem=("parallel", "parallel"), ride=ride)
    return out


def _attn_bwd(q, k, v, sinks, do, name):
    heads, n_rows, _ = q.shape
    kvh = heads // GROUP
    nb = n_rows // BLOCK
    q_spec, cur, prev, sink = _attn_specs(nb)
    part = pl.BlockSpec((1, 1, BLOCK, HEAD_DIM), lambda g, n: (g, n, 0, 0))
    part_shape = jax.ShapeDtypeStruct((kvh, nb, BLOCK, HEAD_DIM), F32)

    def body(q_ref, kp_ref, kc_ref, vp_ref, vc_ref, s_ref, do_ref, dq_ref, dkp_ref, dkc_ref, dvp_ref, dvc_ref, ds_ref):
        n = pl.program_id(1)
        f = functools.partial(_attn_block, n=n)
        _, vjp = jax.vjp(f, q_ref[...].astype(F32), kp_ref[0].astype(F32), kc_ref[0].astype(F32), vp_ref[0].astype(F32),
                         vc_ref[0].astype(F32), s_ref[0])
        dq, dkp, dkc, dvp, dvc, ds = vjp(do_ref[...].astype(F32))
        dq_ref[...] = dq.astype(dq_ref.dtype)
        dkp_ref[0, 0], dkc_ref[0, 0], dvp_ref[0, 0], dvc_ref[0, 0] = dkp, dkc, dvp, dvc

        @pl.when(n == 0)
        def _():
            ds_ref[...] = jnp.zeros_like(ds_ref)

        ds_ref[0] += ds

    return pl.pallas_call(
        body, name=name, grid=(kvh, nb), in_specs=[q_spec, prev, cur, prev, cur, sink, q_spec],
        out_specs=[q_spec, part, part, part, part, sink],
        out_shape=[jax.ShapeDtypeStruct(q.shape, BF16), part_shape, part_shape, part_shape, part_shape,
                   jax.ShapeDtypeStruct((kvh, GROUP, 1, 1), F32)],
        compiler_params=_cparams(("parallel", "arbitrary")),
    )(q, k, k, v, v, sinks.reshape(kvh, GROUP, 1, 1), do)


def _shift_add(own, to_prev, name):
    kvh, nb = own.shape[:2]
    blk = (1, 1, BLOCK, HEAD_DIM)

    def body(a_ref, b_ref, o_ref):
        n = pl.program_id(1)
        o_ref[...] = (a_ref[...] + jnp.where(n < nb - 1, b_ref[...], 0.0)).astype(o_ref.dtype)

    return pl.pallas_call(
        body, name=name, grid=(kvh, nb),
        in_specs=[pl.BlockSpec(blk, lambda g, n: (g, n, 0, 0)),
                  pl.BlockSpec(blk, lambda g, n: (g, jnp.minimum(n + 1, nb - 1), 0, 0))],
        out_specs=pl.BlockSpec(blk, lambda g, n: (g, n, 0, 0)),
        out_shape=jax.ShapeDtypeStruct(own.shape, BF16), compiler_params=_cparams(("parallel", "parallel")),
    )(own, to_prev)


def _gdn_gates(ba, alog, dt, r0, hv):
    lane = lax.broadcasted_iota(jnp.int32, ba.shape, 1)
    t = ba + dt
    softplus = jnp.maximum(t, 0.0) + jnp.log(1.0 + jnp.exp(-jnp.abs(t)))
    val = jnp.where(lane < hv, jax.nn.sigmoid(ba), jnp.where(lane < 2 * hv, -jnp.exp(alog) * softplus, 0.0))
    return jnp.where(_row_mask(r0, ba.shape[0], ba.shape[1]), val, 0.0)


def _l2n(x):
    return x * lax.rsqrt(jnp.sum(x * x, axis=-1, keepdims=True) + 1e-6)


_NN = (((2,), (1,)), ((0,), (0,)))
_NT = (((2,), (2,)), ((0,), (0,)))
_TN = (((1,), (1,)), ((0,), (0,)))


def _bdot(a, b, dims=_NN):
    return lax.dot_general(a.astype(BF16), b.astype(BF16), dims, preferred_element_type=F32)


def _dot3(a, b, dims):
    ah, bh = a.astype(BF16), b.astype(BF16)
    al, bl = (a - ah.astype(F32)).astype(BF16), (b - bh.astype(F32)).astype(BF16)

    def d(p, q):
        return lax.dot_general(p, q, dims, preferred_element_type=F32)

    return d(ah, bh) + (d(ah, bl) + d(al, bh))


@jax.custom_vjp
def _pdot(a, b):
    return _dot3(a, b, _NN)


def _pdot_fwd(a, b):
    return _dot3(a, b, _NN), (a, b)


def _pdot_bwd(res, ct):
    a, b = res
    return _bdot(ct, b, _NT), _bdot(a, ct, _TN)


_pdot.defvjp(_pdot_fwd, _pdot_bwd)


def _scan_chunks(x, reverse):
    n, c = x.shape[0], GDN_CHUNK
    row = lax.broadcasted_iota(jnp.int32, x.shape, 0) % c
    s = 1
    while s < c:
        if reverse:
            x = x + jnp.where(row < c - s, pltpu.roll(x, n - s, axis=0), 0.0)
        else:
            x = x + jnp.where(row >= s, pltpu.roll(x, s, axis=0), 0.0)
        s *= 2
    return x


@jax.custom_vjp
def _cumsum_chunks(x):
    return _scan_chunks(x, False)


_cumsum_chunks.defvjp(lambda x: (_scan_chunks(x, False), None), lambda _, ct: (_scan_chunks(ct, True),))


def _gdn_heads(states, qkv, z, gates, norm_w, hk0, hv_total):
    c, h = GDN_CHUNK, states.shape[0]
    g = h // 2

    def cols(src, starts):
        return jnp.stack([src[:, s:s + GDN_DIM] for s in starts])

    q = _l2n(jax.nn.silu(cols(qkv, [4 * GDN_DIM * t for t in range(g)]))) * (GDN_DIM ** -0.5)
    k = _l2n(jax.nn.silu(cols(qkv, [4 * GDN_DIM * t + GDN_DIM for t in range(g)])))
    q, k = jnp.repeat(q, 2, axis=0), jnp.repeat(k, 2, axis=0)
    v = jax.nn.silu(cols(qkv, [4 * GDN_DIM * (t // 2) + (2 + t % 2) * GDN_DIM for t in range(h)]))
    zz = cols(z, [GDN_DIM * t for t in range(h)])
    lane = lax.broadcasted_iota(jnp.int32, gates.shape, 1)

    def col_of(first):
        return jnp.stack([jnp.sum(jnp.where(lane == first + t, gates, 0.0), axis=1, keepdims=True) for t in range(h)])

    beta_col, g_col = col_of(2 * hk0), col_of(hv_total + 2 * hk0)
    i = lax.broadcasted_iota(jnp.int32, (c, c), 0)
    j = lax.broadcasted_iota(jnp.int32, (c, c), 1)
    causal, strict = (i >= j)[None], (i > j)[None]
    gc = _cumsum_chunks(jnp.broadcast_to(g_col, (h, c, GDN_DIM)).reshape(h * c, GDN_DIM)).reshape(h, c, GDN_DIM)
    gc_i = gc[:, :, :c]
    gc_j = jnp.swapaxes(gc_i, 1, 2)
    gc_last = jnp.broadcast_to(gc[:, c - 1:c, :], (h, GDN_DIM, GDN_DIM))
    decay = jnp.where(causal, jnp.exp(jnp.where(causal, gc_i - gc_j, 0.0)), 0.0)
    k_beta = k * beta_col
    lower = jnp.where(strict, _bdot(k_beta, k, _NT) * decay, 0.0)
    eye = (i == j).astype(F32)[None]
    neg = -lower
    inv = eye + neg
    power = neg
    for _ in range(5):
        power = _pdot(power, power)
        inv = _pdot(inv, eye + power)
    sol = _pdot(inv, jnp.concatenate([v * beta_col, k_beta * jnp.exp(gc)], axis=2))
    u, w = sol[:, :, :GDN_DIM], sol[:, :, GDN_DIM:]
    intra = jnp.where(causal, _bdot(q, k, _NT) * decay, 0.0)
    q_dec = q * jnp.exp(gc)
    k_dec = k * jnp.exp(gc_last[:, :c] - gc)
    v_new = u - _bdot(w, states)
    o = _bdot(q_dec, states) + _bdot(intra, v_new)
    new_states = states * jnp.exp(gc_last) + _bdot(k_dec, v_new, _TN)
    y = _rms(o, norm_w) * jax.nn.silu(zz)
    return jnp.concatenate([y[t] for t in range(h)], axis=1), new_states


GDN_KEY_HEADS_PER_STEP = 8


def _key_heads_per_step(hk_total):
    return math.gcd(hk_total, GDN_KEY_HEADS_PER_STEP)


def _gdn_fwd(cq, proj, gates, norm_w, hk_total, name):
    n_rows = cq.shape[0]
    nc, hv_total = n_rows // GDN_CHUNK, 2 * hk_total
    grp = _key_heads_per_step(hk_total)
    heads = 2 * grp
    zblk0 = cq.shape[1] // (heads * GDN_DIM)

    def body(cq_ref, z_ref, g_ref, w_ref, y_ref, save_ref, state):
        n, hg = pl.program_id(0), pl.program_id(1)

        @pl.when(n == 0)
        def _():
            state[pl.ds(heads * hg, heads)] = jnp.zeros((heads, GDN_DIM, GDN_DIM), F32)

        s_in = state[pl.ds(heads * hg, heads)]
        save_ref[0] = s_in
        y, s_out = _gdn_heads(s_in, cq_ref[...], z_ref[...], g_ref[...], w_ref[...], grp * hg, hv_total)
        y_ref[...] = y.astype(y_ref.dtype)
        state[pl.ds(heads * hg, heads)] = s_out

    return pl.pallas_call(
        body, name=name, grid=(nc, hk_total // grp),
        in_specs=[pl.BlockSpec((GDN_CHUNK, 2 * heads * GDN_DIM), lambda n, h: (n, h)),
                  pl.BlockSpec((GDN_CHUNK, heads * GDN_DIM), lambda n, h: (n, zblk0 + h)),
                  pl.BlockSpec((GDN_CHUNK, LANES), lambda n, h: (n, 0)),
                  pl.BlockSpec((1, GDN_DIM), lambda n, h: (0, 0))],
        out_specs=[pl.BlockSpec((GDN_CHUNK, heads * GDN_DIM), lambda n, h: (n, h)),
                   pl.BlockSpec((1, heads, GDN_DIM, GDN_DIM), lambda n, h: (n, h, 0, 0))],
        out_shape=[jax.ShapeDtypeStruct((n_rows, hv_total * GDN_DIM), BF16),
                   jax.ShapeDtypeStruct((nc, hv_total, GDN_DIM, GDN_DIM), F32)],
        scratch_shapes=[pltpu.VMEM((hv_total, GDN_DIM, GDN_DIM), F32)],
        compiler_params=_cparams(("arbitrary", "arbitrary")),
    )(cq, proj, gates, norm_w.reshape(1, GDN_DIM))


def _gdn_bwd(cq, proj, gates, norm_w, saved, dy, hk_total, name):
    n_rows = cq.shape[0]
    nc, hv_total = n_rows // GDN_CHUNK, 2 * hk_total
    grp = _key_heads_per_step(hk_total)
    heads = 2 * grp
    zblk0 = cq.shape[1] // (heads * GDN_DIM)

    def body(cq_ref, z_ref, g_ref, w_ref, save_ref, dy_ref, dcq_ref, dz_ref, dg_ref, dw_ref, dstate):
        n, hg = pl.program_id(0), pl.program_id(1)

        @pl.when(n == 0)
        def _():
            dstate[pl.ds(heads * hg, heads)] = jnp.zeros((heads, GDN_DIM, GDN_DIM), F32)

        @pl.when((n == 0) & (hg == 0))
        def _():
            dw_ref[...] = jnp.zeros_like(dw_ref)

        @pl.when(hg == 0)
        def _():
            dg_ref[...] = jnp.zeros_like(dg_ref)

        f = functools.partial(_gdn_heads, hk0=grp * hg, hv_total=hv_total)
        _, vjp = jax.vjp(f, save_ref[0], cq_ref[...], z_ref[...], g_ref[...], w_ref[...])
        ds, dcq, dz, dg, dw = vjp((dy_ref[...].astype(F32), dstate[pl.ds(heads * hg, heads)]))
        dstate[pl.ds(heads * hg, heads)] = ds
        dcq_ref[...] = dcq
        dz_ref[...] = dz.astype(dz_ref.dtype)
        dg_ref[...] += dg
        dw_ref[...] += dw

    rev = lambda n: nc - 1 - n
    return pl.pallas_call(
        body, name=name, grid=(nc, hk_total // grp),
        in_specs=[pl.BlockSpec((GDN_CHUNK, 2 * heads * GDN_DIM), lambda n, h: (rev(n), h)),
                  pl.BlockSpec((GDN_CHUNK, heads * GDN_DIM), lambda n, h: (rev(n), zblk0 + h)),
                  pl.BlockSpec((GDN_CHUNK, LANES), lambda n, h: (rev(n), 0)),
                  pl.BlockSpec((1, GDN_DIM), lambda n, h: (0, 0)),
                  pl.BlockSpec((1, heads, GDN_DIM, GDN_DIM), lambda n, h: (rev(n), h, 0, 0)),
                  pl.BlockSpec((GDN_CHUNK, heads * GDN_DIM), lambda n, h: (rev(n), h))],
        out_specs=[pl.BlockSpec((GDN_CHUNK, 2 * heads * GDN_DIM), lambda n, h: (rev(n), h)),
                   pl.BlockSpec((GDN_CHUNK, heads * GDN_DIM), lambda n, h: (rev(n), h)),
                   pl.BlockSpec((GDN_CHUNK, LANES), lambda n, h: (rev(n), 0)),
                   pl.BlockSpec((1, GDN_DIM), lambda n, h: (0, 0))],
        out_shape=[jax.ShapeDtypeStruct(cq.shape, F32), jax.ShapeDtypeStruct((n_rows, hv_total * GDN_DIM), BF16),
                   jax.ShapeDtypeStruct((n_rows, LANES), F32), jax.ShapeDtypeStruct((1, GDN_DIM), F32)],
        scratch_shapes=[pltpu.VMEM((hv_total, GDN_DIM, GDN_DIM), F32)],
        compiler_params=_cparams(("arbitrary", "arbitrary")),
    )(cq, proj, gates, norm_w.reshape(1, GDN_DIM), saved, dy)


def _final_loss(h, w, target, name):
    d = h.shape[1]

    def fn(r0, hv, tv, wv):
        def loss_of(hh, ww):
            err = jnp.where(_row_mask(r0, hh.shape[0], d) & (r0 + lax.broadcasted_iota(jnp.int32, hh.shape, 0) >= FRONT),
                            _rms(hh, ww) - tv, 0.0)
            return 0.5 * jnp.sum(jnp.sum(err * err, axis=1, keepdims=True) / d)

        loss, vjp = jax.vjp(loss_of, hv, wv)
        dh, dw = vjp(jnp.ones((), F32))
        return dh, dh, jnp.zeros((1, LANES), F32) + loss, dw

    dh, dh_bf16, loss, dw = _rowwise(fn, [h, target], [w.reshape(1, -1)], [(d, F32), (d, BF16)], [(1, LANES), (1, d)], name)
    return loss[0, 0], dh, dh_bf16, dw[0]


def _rope_tables(n_rows):
    pos = (jnp.arange(n_rows) - PAD_LEN).astype(F32)
    inv_freq = ROPE_THETA ** (-jnp.arange(0, HEAD_DIM, 2, dtype=F32) / HEAD_DIM)
    ang = pos[:, None] * inv_freq[None, :]
    reps = LANES // (HEAD_DIM // 2)
    return jnp.tile(jnp.cos(ang), (1, reps)), jnp.tile(jnp.sin(ang), (1, reps))


def _to_heads(t):
    n_rows, w = t.shape
    return t.reshape(n_rows, w // HEAD_DIM, HEAD_DIM).transpose(1, 0, 2)


def _from_heads(t):
    heads, n_rows, _ = t.shape
    return t.transpose(1, 0, 2).reshape(n_rows, heads * HEAD_DIM)


def _local_step(h0, target, p, tr):
    n_rows, d = h0.shape
    depth = p["norm_mix"].shape[0]
    cos, sin = _rope_tables(n_rows)
    hk_total = d // GDN_DIM
    hv_total = 2 * hk_total
    conv_dim = 4 * hk_total * GDN_DIM
    qw, kw = d, d // GROUP
    saved = []
    h = h0
    mm = functools.partial(tr.mm, True)
    for i in range(depth):
        kind, j = i % 3, i // 3
        s = {"h": h}
        hn = _rms_fwd(h, p["norm_mix"][i], "rms_fwd")
        s["hn"] = hn
        if kind == 0:
            pre = mm(hn, tr.weight("conv_w_pw1", j), "nn", F32, "mm_pw1")
            u1 = tr.call(True, _us_rows(pre), lambda r: _glu_fwd(pre, p["conv_b_pw1"][j], "glu_fwd", ride=r))
            c = tr.call(True, _us_dwconv(n_rows, d, CONV_KERNEL),
                        lambda r: _dwconv_fwd(u1, p["conv_w_dw"][j], p["conv_b_dw"][j], d, "dwconv31_fwd", ride=r))
            sv = _ln_silu_fwd(c, p["conv_ln_g"][j], p["conv_ln_b"][j], "ln_silu_fwd")
            h = mm(sv, tr.weight("conv_w_pw2", j), "nn", F32, "mm_d_d_res", bias=p["conv_b_pw2"][j], residual=h)
            s.update(pre=pre, u1=u1, c=c, sv=sv)
        elif kind == 1:
            pre = mm(hn, tr.weight("attn_w_qkv", j), "nn", F32, "mm_qkv")
            q, k, v = _qkv_post_fwd(pre, p["attn_b_qkv"][j], cos, sin, qw, kw, "qkv_post_fwd")
            qh, kh, vh = _to_heads(q), _to_heads(k), _to_heads(v)
            o = _from_heads(tr.call(True, _us_attn(qh), lambda r: _attn_fwd(qh, kh, vh, p["attn_sinks"][j], "attn_fwd", ride=r)))
            h = mm(o, tr.weight("attn_w_o", j), "nn", F32, "mm_d_d_res", bias=p["attn_b_o"][j], residual=h)
            s.update(qh=qh, kh=kh, vh=vh, o=o)
        else:
            proj = mm(hn, tr.weight("gdn_w_in", j), "nn", F32, "mm_gdn_in")
            cq = tr.call(True, _us_dwconv(n_rows, conv_dim, GDN_CONV),
                         lambda r: _dwconv_fwd(proj, p["gdn_conv_w"][j], None, conv_dim, "dwconv4_fwd", ride=r))
            vec = _gate_vectors(p["gdn_a_log"][j], p["gdn_dt_bias"][j], hv_total)
            ba_blk = (conv_dim + hv_total * GDN_DIM) // LANES
            (gates,) = _rowwise(lambda r0, bav, av, dv: (_gdn_gates(bav, av, dv, r0, hv_total),), [(proj, LANES, ba_blk)],
                                [vec[0], vec[1]], [(LANES, F32)], [], "gdn_gates_fwd")
            y, states = _gdn_fwd(cq, proj, gates, p["gdn_norm_w"][j], hk_total, "gdn_fwd")
            h = mm(y, tr.weight("gdn_w_out", j), "nn", F32, "mm_gdn_out_res", residual=h)
            s.update(proj=proj, cq=cq, gates=gates, y=y, states=states)
        s["h1"] = h
        hn2 = _rms_fwd(h, p["norm_ffn"][i], "rms_fwd")
        w_gate, w_up = tr.weight("ffn_w_gate", i), tr.weight("ffn_w_up", i)
        gate, up, a = tr.call(True, 2 * _us_mm(n_rows, w_gate.shape[1], d), lambda r: _mmx(
            [(hn2, w_gate, 0), (hn2, w_up, 1)], "nn", "mm_ffn_swiglu", [BF16, BF16, BF16],
            lambda accs: [accs[0], accs[1], jax.nn.silu(accs[0]) * accs[1]], n_acc=2, ride=r))
        h = mm(a, tr.weight("ffn_w_down", i), "nn", F32, "mm_ffn_down_res", residual=h)
        s.update(hn2=hn2, gate=gate, up=up, a=a)
        saved.append(s)

    loss, dh, dhb, g_final = _final_loss(h, p["norm_final"], target, "final_loss")
    g = {k: [None] * v.shape[0] for k, v in p.items() if k not in ("norm_final", "meta_tokens")}
    g["norm_final"] = g_final
    mm = functools.partial(tr.mm, False)
    for i in reversed(range(depth)):
        kind, j = i % 3, i // 3
        s = saved[i]
        tr.give("ffn_w_down", i, mm(s["a"], dhb, "tn", BF16, "mm_dw_down"))
        w_gate, w_up, w_down = tr.weight("ffn_w_gate", i), tr.weight("ffn_w_up", i), tr.weight("ffn_w_down", i)

        def swiglu_bwd(accs, gate, up):
            _, vjp = jax.vjp(lambda gv, uv: jax.nn.silu(gv) * uv, gate.astype(F32), up.astype(F32))
            return list(vjp(accs[0]))

        dgate, dup = tr.call(False, _us_mm(n_rows, w_gate.shape[1], d), lambda r: _mmx(
            [(dhb, w_down, 0)], "nt", "mm_da_swiglu", [BF16, BF16], swiglu_bwd, extras=[s["gate"], s["up"]], ride=r))
        g_gate, g_up = tr.call(False, 2 * _us_mm(d, w_gate.shape[1], n_rows), lambda r: _mmx(
            [(s["hn2"], dgate, 0), (s["hn2"], dup, 1)], "tn", "mm_dw_gate_up", [BF16, BF16], lambda accs: accs, n_acc=2, ride=r))
        tr.give("ffn_w_gate", i, g_gate)
        tr.give("ffn_w_up", i, g_up)
        (dhn2,) = tr.call(False, 2 * _us_mm(n_rows, d, w_gate.shape[1]), lambda r: _mmx(
            [(dgate, w_gate, 0), (dup, w_up, 0)], "nt", "mm_dhn2", [F32], lambda accs: accs, ride=r))
        dh, dhb, g["norm_ffn"][i], dh_sums = _rms_bwd(s["h1"], p["norm_ffn"][i], dhn2, dh, "rms_bwd")
        if kind == 0:
            g["conv_b_pw2"][j] = dh_sums
            tr.give("conv_w_pw2", j, mm(s["sv"], dhb, "tn", BF16, "mm_dw_d_d"))
            dsv = mm(dhb, tr.weight("conv_w_pw2", j), "nt", F32, "mm_dx_d_d")
            dc, g["conv_ln_g"][j], g["conv_ln_b"][j] = _ln_silu_bwd(s["c"], p["conv_ln_g"][j], p["conv_ln_b"][j], dsv, "ln_silu_bwd")
            du1, g["conv_w_dw"][j], g["conv_b_dw"][j] = tr.call(
                False, 2 * _us_dwconv(n_rows, d, CONV_KERNEL),
                lambda r: _dwconv_bwd(s["u1"], p["conv_w_dw"][j], dc, d, "dwconv31_bwd", ride=r))
            dpre, g["conv_b_pw1"][j] = _glu_bwd(s["pre"], p["conv_b_pw1"][j], du1, "glu_bwd")
            tr.give("conv_w_pw1", j, mm(s["hn"], dpre, "tn", BF16, "mm_dw_pw1"))
            dhn = mm(dpre, tr.weight("conv_w_pw1", j), "nt", F32, "mm_dx_pw1")
        elif kind == 1:
            g["attn_b_o"][j] = dh_sums
            tr.give("attn_w_o", j, mm(s["o"], dhb, "tn", BF16, "mm_dw_d_d"))
            do = _to_heads(mm(dhb, tr.weight("attn_w_o", j), "nt", BF16, "mm_dx_d_d_bf16"))
            dq, dkp, dkc, dvp, dvc, dsink = _attn_bwd(s["qh"], s["kh"], s["vh"], p["attn_sinks"][j], do, "attn_bwd")
            g["attn_sinks"][j] = dsink.reshape(-1)
            kvh = kw // HEAD_DIM
            dk = _shift_add(dkc, dkp, "attn_shift_add").reshape(kvh, n_rows, HEAD_DIM)
            dv = _shift_add(dvc, dvp, "attn_shift_add").reshape(kvh, n_rows, HEAD_DIM)
            dpre, g["attn_b_qkv"][j] = _qkv_post_bwd(_from_heads(dq), _from_heads(dk), _from_heads(dv), cos, sin, "qkv_post_bwd")
            tr.give("attn_w_qkv", j, mm(s["hn"], dpre, "tn", BF16, "mm_dw_qkv"))
            dhn = mm(dpre, tr.weight("attn_w_qkv", j), "nt", F32, "mm_dx_qkv")
        else:
            tr.give("gdn_w_out", j, mm(s["y"], dhb, "tn", BF16, "mm_dw_gdn_out"))
            dy = mm(dhb, tr.weight("gdn_w_out", j), "nt", BF16, "mm_dx_gdn_out")
            dcq, dz, dgates, g_nw = _gdn_bwd(s["cq"], s["proj"], s["gates"], p["gdn_norm_w"][j], s["states"], dy, hk_total, "gdn_bwd")
            g["gdn_norm_w"][j] = g_nw[0]
            vec = _gate_vectors(p["gdn_a_log"][j], p["gdn_dt_bias"][j], hv_total)
            ba_blk = (conv_dim + hv_total * GDN_DIM) // LANES

            def gates_bwd(r0, bav, dgv, av, dv):
                _, vjp = jax.vjp(functools.partial(_gdn_gates, r0=r0, hv=hv_total), bav, av, dv)
                return vjp(dgv)

            dba, d_alog, d_dt = _rowwise(gates_bwd, [(s["proj"], LANES, ba_blk), dgates], [vec[0], vec[1]], [(LANES, BF16)],
                                         [(1, LANES), (1, LANES)], "gdn_gates_bwd")
            g["gdn_a_log"][j] = d_alog[0, hv_total:2 * hv_total]
            g["gdn_dt_bias"][j] = d_dt[0, hv_total:2 * hv_total]
            dconv_in, g["gdn_conv_w"][j], _ = tr.call(
                False, 2 * _us_dwconv(n_rows, conv_dim, GDN_CONV),
                lambda r: _dwconv_bwd(s["proj"], p["gdn_conv_w"][j], dcq, conv_dim, "dwconv4_bwd", du_dtype=BF16, ride=r))
            w_in = tr.weight("gdn_w_in", j)
            pad = jnp.zeros((n_rows, w_in.shape[1] - conv_dim - hv_total * GDN_DIM - LANES), BF16)
            dproj = jnp.concatenate([dconv_in, dz, dba, pad], axis=1)
            tr.give("gdn_w_in", j, mm(s["hn"], dproj, "tn", BF16, "mm_dw_gdn_in"))
            dhn = mm(dproj, w_in, "nt", F32, "mm_dx_gdn_in")
        dh, dhb, g["norm_mix"][i], _ = _rms_bwd(s["h"], p["norm_mix"][i], dhn, dh, "rms_bwd")
    g = {k: (jnp.stack(v) if isinstance(v, list) else v) for k, v in g.items()}
    return loss, dh, g


def _us_mm(m, n, k):
    return 2.0 * m * n * k / 8.0e8


def _us_rows(t):
    return t.shape[0] * t.shape[1] / 8.0e5


def _us_dwconv(n_rows, c, taps):
    return n_rows * c * (taps + 8) / 2.0e6


def _us_attn(qh):
    return qh.shape[0] * qh.shape[1] / 500.0


def _gate_vectors(a_log, dt_bias, hv):
    def place(t):
        return jnp.concatenate([jnp.zeros((hv,), F32), t, jnp.zeros((LANES - 2 * hv,), F32)]).reshape(1, LANES)

    return place(a_log), place(dt_bias)


GDN_IN_ALIGN = 512


def _gdn_group(w, hk):
    lead, kw = w.shape[:-1], hk * GDN_DIM
    q = w[..., :kw].reshape(*lead, hk, 1, GDN_DIM)
    k = w[..., kw:2 * kw].reshape(*lead, hk, 1, GDN_DIM)
    v = w[..., 2 * kw:4 * kw].reshape(*lead, hk, 2, GDN_DIM)
    return jnp.concatenate([q, k, v], axis=-2).reshape(*lead, 4 * kw)


def _gdn_ungroup(w, hk):
    lead, kw = w.shape[:-1], hk * GDN_DIM
    t = w.reshape(*lead, hk, 4, GDN_DIM)
    return jnp.concatenate([t[..., 0, :].reshape(*lead, kw), t[..., 1, :].reshape(*lead, kw),
                            t[..., 2:, :].reshape(*lead, 2 * kw)], axis=-1)


def _gdn_in_layout(w, hk):
    conv_dim = 4 * hk * GDN_DIM
    width = -(-w.shape[-1] // GDN_IN_ALIGN) * GDN_IN_ALIGN
    pad = jnp.zeros(w.shape[:-1] + (width - w.shape[-1],), w.dtype)
    return jnp.concatenate([_gdn_group(w[..., :conv_dim], hk), w[..., conv_dim:], pad], axis=-1)


def _gdn_in_natural(w, hk, in_width):
    conv_dim = 4 * hk * GDN_DIM
    return jnp.concatenate([_gdn_ungroup(w[..., :conv_dim], hk), w[..., conv_dim:in_width]], axis=-1)


def _exchange(slabs, name):
    n_src = len(slabs)

    def body(*refs):
        sems = refs[2 * n_src:]
        copies = [_exchange_copies(refs[t], refs[n_src + t], *sems[3 * t:3 * t + 3], slab=slabs[t]) for t in range(n_src)]
        for cps in copies:
            _exchange_start(cps)
        for cps in copies:
            _exchange_wait(cps)

    any_spec = pl.BlockSpec(memory_space=pl.ANY)
    return pl.pallas_call(
        body, name=name, out_shape=[slab.out_shape() for slab in slabs], in_specs=[any_spec] * n_src,
        out_specs=[any_spec] * n_src, scratch_shapes=[s for _ in slabs for s in _exchange_sems()],
    )(*[slab.array for slab in slabs])


BIG_COL = ("conv_w_pw1", "attn_w_qkv", "gdn_w_in", "ffn_w_gate", "ffn_w_up")
BIG_ROW = ("conv_w_pw2", "attn_w_o", "gdn_w_out", "ffn_w_down")
EXCHANGE_US_PER_BYTE = 11.4e-6
RIDE_PART_US = 150.0


class _Traffic:
    def __init__(self, shards, hk):
        self.shards, self.hk = shards, hk
        self.in_width = N_DEV * shards["gdn_w_in"].shape[-1]
        self.parts_of = {}
        for k, s in shards.items():
            parts = 1
            while (N_DEV * s.shape[1] * s.shape[2] * 2 * EXCHANGE_US_PER_BYTE / parts > RIDE_PART_US
                   and s.shape[1] % (32 * parts) == 0):
                parts *= 2
            self.parts_of[k] = parts
        depth = shards["ffn_w_down"].shape[0]
        order = []
        for i in range(depth):
            j = i // 3
            order += [[("conv_w_pw1", j), ("conv_w_pw2", j)], [("attn_w_qkv", j), ("attn_w_o", j)],
                      [("gdn_w_in", j), ("gdn_w_out", j)]][i % 3]
            order += [("ffn_w_gate", i), ("ffn_w_up", i), ("ffn_w_down", i)]
        self.wanted = [(k, i, part) for k, i in order for part in range(self.parts_of[k])]
        self.arrived = {}
        self.ready = {}
        self.owed = []
        self.received = {}

    def _us(self, k):
        s = self.shards[k]
        return N_DEV * s.shape[1] * s.shape[2] * 2 * EXCHANGE_US_PER_BYTE / self.parts_of[k]

    def _shard_part(self, item):
        k, i, part = item
        rows = self.shards[k].shape[1] // self.parts_of[k]
        return _Slab(self.shards[k], True, i, part * rows, rows)

    def _pick(self, queue, us_of, room):
        taken = []
        while queue and room >= 0.5 * us_of(queue[0]):
            room -= us_of(queue[0])
            taken.append(queue.pop(0))
        return taken

    def _run(self, forward, room, fn):
        if forward:
            taken = self._pick(self.wanted, lambda it: self._us(it[0]), room)
            items = [self._shard_part(it) for it in taken]
        else:
            taken = self._pick(self.owed, lambda it: self._us(it[0][0]), room)
            items = [slab for _, slab in taken]
        if not taken:
            return fn(None)
        ride = _Ride(items)
        out = fn(ride)
        for it, got in zip(taken, ride.outs):
            if forward:
                self.arrived[it] = got
            else:
                self.received[it[0]] = got
        return out

    def call(self, forward, room, fn):
        return self._run(forward, room, fn)

    def mm(self, forward, a, b, mode, out_dtype, name, **kw):
        m = a.shape[1] if mode == "tn" else a.shape[0]
        k = a.shape[0] if mode == "tn" else a.shape[1]
        n = b.shape[0] if mode == "nt" else b.shape[1]
        return self._run(forward, _us_mm(m, n, k), lambda ride: _mm(a, b, mode, out_dtype, name, ride=ride, **kw))

    def _natural(self, k, i):
        parts = self.parts_of[k]
        missing = [(k, i, part) for part in range(parts) if (k, i, part) not in self.arrived]
        if missing:
            for it in missing:
                self.wanted.remove(it)
            got = _exchange([self._shard_part(it) for it in missing], "gather_weights")
            for it, t in zip(missing, got):
                self.arrived[it] = t
        got = [self.arrived[(k, i, part)] for part in range(parts)]
        rows, c = got[0].shape[1], got[0].shape[2]
        if k in BIG_COL:
            return jnp.concatenate([t.transpose(1, 0, 2).reshape(rows, N_DEV * c) for t in got], axis=0)
        return jnp.stack(got, axis=1).reshape(N_DEV * parts * rows, c)

    def weight(self, k, i):
        if (k, i) not in self.ready:
            w = self._natural(k, i)
            self.ready[(k, i)] = _gdn_in_layout(w, self.hk) if k == "gdn_w_in" else w
        return self.ready[(k, i)]

    def give(self, k, i, grad):
        if k == "gdn_w_in":
            grad = _gdn_in_natural(grad, self.hk, self.in_width)
        r, c = self.shards[k].shape[1:]
        pieces = grad.reshape(r, N_DEV, c).transpose(1, 0, 2) if k in BIG_COL else grad.reshape(N_DEV, r, c)
        rows = r // self.parts_of[k]
        for part in range(self.parts_of[k]):
            self.owed.append(((k, i, part), _Slab(pieces, False, 0, part * rows, rows)))

    def gradient_parts(self, k):
        if self.owed:
            got = _exchange([slab for _, slab in self.owed], "scatter_grads")
            for (it, _), t in zip(self.owed, got):
                self.received[it] = t
            self.owed = []
        layers = self.shards[k].shape[0]
        return jnp.stack([jnp.concatenate([self.received[(k, i, part)] for part in range(self.parts_of[k])], axis=1)
                          for i in range(layers)])


def _cast_bf16(w, name):
    n, r, c = w.shape
    tr = _tile(r, max(16, (1 << 20) // c), 16)

    def body(w_ref, o_ref):
        o_ref[...] = w_ref[...].astype(BF16)

    return pl.pallas_call(
        body, name=name, grid=(n, r // tr), in_specs=[pl.BlockSpec((1, tr, c), lambda l, i: (l, i, 0))],
        out_specs=pl.BlockSpec((1, tr, c), lambda l, i: (l, i, 0)), out_shape=jax.ShapeDtypeStruct(w.shape, BF16),
        compiler_params=_cparams(("parallel", "parallel")),
    )(w)


def _adamw(w, g, m, v):
    m = ADAM_B1 * m + (1.0 - ADAM_B1) * g
    v = ADAM_B2 * v + (1.0 - ADAM_B2) * jnp.square(g)
    m_hat = m / (1.0 - ADAM_B1 ** ADAM_STEP)
    v_hat = v / (1.0 - ADAM_B2 ** ADAM_STEP)
    delta = -ADAM_LR * (m_hat / (jnp.sqrt(v_hat) + ADAM_EPS) + ADAM_WD * w)
    return delta, m, v


def _sum8_adam(parts, w, m, v, name):
    n, _, r, c = parts.shape
    tr = _tile(r, max(16, (1 << 18) // c), 16)
    blk = pl.BlockSpec((1, tr, c), lambda l, i: (l, i, 0))

    def body(p_ref, w_ref, m_ref, v_ref, g_ref, d_ref, mo_ref, vo_ref):
        g = p_ref[0, 0].astype(F32)
        for s in range(1, N_DEV):
            g = g + p_ref[0, s].astype(F32)
        delta, m2, v2 = _adamw(w_ref[0], g, m_ref[0], v_ref[0])
        g_ref[0], d_ref[0], mo_ref[0], vo_ref[0] = g, delta, m2, v2

    shp = jax.ShapeDtypeStruct(w.shape, F32)
    return pl.pallas_call(
        body, name=name, grid=(n, r // tr),
        in_specs=[pl.BlockSpec((1, N_DEV, tr, c), lambda l, i: (l, 0, i, 0)), blk, blk, blk],
        out_specs=[blk, blk, blk, blk], out_shape=[shp, shp, shp, shp],
        compiler_params=_cparams(("parallel", "parallel")),
    )(parts, w, m, v)


PACK_ROWS = 8


def _pack(arrays):
    flat = jnp.concatenate([a.reshape(-1).astype(F32) for a in arrays])
    unit = PACK_ROWS * LANES
    total = -(-flat.shape[0] // unit) * unit
    return jnp.concatenate([flat, jnp.zeros((total - flat.shape[0],), F32)]).reshape(-1, LANES)


def _unpack(packed, shapes):
    flat, out, pos = packed.reshape(-1), [], 0
    for s in shapes:
        size = math.prod(s)
        out.append(flat[pos:pos + size].reshape(s))
        pos += size
    return out


SMALL_SHARDED =("meta_tokens", "conv_b_pw1", "conv_w_dw", "conv_b_dw", "conv_ln_g", "conv_ln_b", "conv_b_pw2", "gdn_conv_w")
REPLICATED = ("norm_mix", "norm_ffn", "norm_final", "attn_b_qkv", "attn_sinks", "attn_b_o", "gdn_a_log", "gdn_dt_bias", "gdn_norm_w")
WEIGHTS = ("meta_tokens", "norm_mix", "norm_ffn", "norm_final", "conv_w_pw1", "conv_b_pw1", "conv_w_dw", "conv_b_dw", "conv_ln_g",
           "conv_ln_b", "conv_w_pw2", "conv_b_pw2", "attn_w_qkv", "attn_b_qkv", "attn_sinks", "attn_w_o", "attn_b_o", "gdn_w_in",
           "gdn_conv_w", "gdn_a_log", "gdn_dt_bias", "gdn_norm_w", "gdn_w_out", "ffn_w_gate", "ffn_w_up", "ffn_w_down")


def kernel(x, meta_tokens, norm_mix, norm_ffn, norm_final, conv_w_pw1, conv_b_pw1, conv_w_dw, conv_b_dw, conv_ln_g, conv_ln_b, conv_w_pw2, conv_b_pw2, attn_w_qkv, attn_b_qkv, attn_sinks, attn_w_o, attn_b_o, gdn_w_in, gdn_conv_w, gdn_a_log, gdn_dt_bias, gdn_norm_w, gdn_w_out, ffn_w_gate, ffn_w_up, ffn_w_down, loss_target, m_meta_tokens, m_norm_mix, m_norm_ffn, m_norm_final, m_conv_w_pw1, m_conv_b_pw1, m_conv_w_dw, m_conv_b_dw, m_conv_ln_g, m_conv_ln_b, m_conv_w_pw2, m_conv_b_pw2, m_attn_w_qkv, m_attn_b_qkv, m_attn_sinks, m_attn_w_o, m_attn_b_o, m_gdn_w_in, m_gdn_conv_w, m_gdn_a_log, m_gdn_dt_bias, m_gdn_norm_w, m_gdn_w_out, m_ffn_w_gate, m_ffn_w_up, m_ffn_w_down, v_meta_tokens, v_norm_mix, v_norm_ffn, v_norm_final, v_conv_w_pw1, v_conv_b_pw1, v_conv_w_dw, v_conv_b_dw, v_conv_ln_g, v_conv_ln_b, v_conv_w_pw2, v_conv_b_pw2, v_attn_w_qkv, v_attn_b_qkv, v_attn_sinks, v_attn_w_o, v_attn_b_o, v_gdn_w_in, v_gdn_conv_w, v_gdn_a_log, v_gdn_dt_bias, v_gdn_norm_w, v_gdn_w_out, v_ffn_w_gate, v_ffn_w_up, v_ffn_w_down):
    a = dict(locals())
    me = 4 * lax.axis_index("x") + 2 * lax.axis_index("y") + lax.axis_index("c")
    d = x.shape[-1]

    hk = d // GDN_DIM
    full = {k: a[k] for k in REPLICATED}
    shard_shapes = [a[k].shape for k in SMALL_SHARDED]
    (got,) = _exchange([_Slab(_pack([a[k] for k in SMALL_SHARDED])[None], True)], "gather_small")
    per_dev = [_unpack(got[s], shard_shapes) for s in range(N_DEV)]
    for i, k in enumerate(SMALL_SHARDED):
        st = jnp.stack([per_dev[s][i] for s in range(N_DEV)], axis=-2)
        full[k] = st.reshape(st.shape[:-2] + (N_DEV * st.shape[-1],))
    traffic = _Traffic({k: _cast_bf16(a[k], "cast_bf16") for k in BIG_COL + BIG_ROW}, hk)

    h0 = jnp.concatenate([jnp.zeros((PAD_LEN, d), F32), full["meta_tokens"], x[0]], axis=0)
    target = jnp.concatenate([jnp.zeros((FRONT, d), F32), loss_target[0]], axis=0)
    loss, dh0, g = _local_step(h0, target, {**full, "gdn_conv_w": _gdn_group(full["gdn_conv_w"], hk)}, traffic)
    g["gdn_conv_w"] = _gdn_ungroup(g["gdn_conv_w"], hk)
    g["meta_tokens"] = dh0[PAD_LEN:FRONT]
    loss = lax.psum(loss, AXES)
    grad_x = dh0[FRONT:][None]

    grads, deltas, new_m, new_v = {}, {}, {}, {}
    for k in BIG_COL + BIG_ROW:
        grads[k], deltas[k], new_m[k], new_v[k] = _sum8_adam(traffic.gradient_parts(k), a[k], a["m_" + k], a["v_" + k], "sum8_adamw")

    small = SMALL_SHARDED + REPLICATED
    full_shapes = [full[k].shape for k in small]
    (got,) = _exchange([_Slab(_pack([g[k] for k in small])[None], True)], "gather_small_grads")
    (total,) = _rowwise(lambda r0, *t: (functools.reduce(lambda p, q: p + q, t),), [got[s] for s in range(N_DEV)], [],
                        [(LANES, F32)], [], "sum8_small", tm=got.shape[1])
    for k, t in zip(small, _unpack(total, full_shapes)):
        if k in SMALL_SHARDED:
            c = a[k].shape[-1]
            t = lax.dynamic_index_in_dim(t.reshape(t.shape[:-1] + (N_DEV, c)), me, axis=t.ndim - 1, keepdims=False)
        grads[k] = t
    shapes = [a[k].shape for k in small]
    packed = [_pack([src[k] for k in small]) for src in (grads, a, {k: a["m_" + k] for k in small}, {k: a["v_" + k] for k in small})]

    def small_adam(r0, gv, wv, mv, vv):
        return _adamw(wv, gv, mv, vv)

    outs = _rowwise(small_adam, packed, [], [(LANES, F32)] * 3, [], "adamw_small", tm=packed[0].shape[0])
    for dst, o in zip((deltas, new_m, new_v), outs):
        for k, t in zip(small, _unpack(o, shapes)):
            dst[k] = t

    return (loss, grad_x, *[grads[k] for k in WEIGHTS], *[deltas[k] for k in WEIGHTS], *[new_m[k] for k in WEIGHTS],
            *[new_v[k] for k in WEIGHTS])
```

```python
import functools
import math

import jax
import jax.numpy as jnp
from jax import lax
from jax.experimental import pallas as pl
from jax.experimental.pallas import tpu as pltpu

F32 = jnp.float32
BF16 = jnp.bfloat16

AXES = ("x", "y", "c")
N_DEV = 8

N_META = 16
FRONT = 128
PAD_LEN = FRONT - N_META
NORM_EPS = 1e-6
LN_EPS = 1e-5
NEG_INF = -1e30
CONV_KERNEL = 31
HEAD_DIM = 64
GROUP = 8
BLOCK = 128
ROPE_THETA = 10000.0
GDN_DIM = 128
GDN_CONV = 4
GDN_CHUNK = 64
ADAM_LR, ADAM_B1, ADAM_B2, ADAM_EPS, ADAM_WD, ADAM_STEP = 0.001, 0.9, 0.999, 1e-08, 0.01, 10

VMEM_LIMIT_BYTES = 52 * 1024 * 1024
LANES = 128
MM_VMEM_BUDGET_BYTES = 40 * 1024 * 1024
MM_ROWS_MAX, MM_COLS_MAX, MM_DEPTH_MAX = 1664, 2048, 2048
MXU_FLOPS = 9.0e14
HBM_BYTES_PER_S = 3.0e12
GRID_STEP_S = 0.4e-6
MM_CHUNK = 256
ROW_TILE_BUDGET_BYTES = 24 * 1024 * 1024
CONV_HALO = 32


def _tile(n, pref, align=128):
    best = None
    for t in range(align, min(n, pref) + 1, align):
        if n % t == 0:
            best = t
    return best if best is not None else n


def _cparams(sem):
    return pltpu.CompilerParams(dimension_semantics=sem, vmem_limit_bytes=VMEM_LIMIT_BYTES)


class _Slab:
    def __init__(self, array, gather, index=0, row0=0, rows=None):
        self.array, self.gather, self.index, self.row0 = array, gather, index, row0
        self.rows = array.shape[1] if rows is None else rows

    def out_shape(self):
        return jax.ShapeDtypeStruct((N_DEV, self.rows, self.array.shape[2]), self.array.dtype)


def _exchange_copies(src_ref, out_ref, send_sems, recv_sems, local_sem, slab):
    x, y, c = lax.axis_index("x"), lax.axis_index("y"), lax.axis_index("c")
    me = 4 * x + 2 * y + c
    rows = pl.ds(slab.row0, slab.rows)

    def src_for(dev):
        return src_ref.at[slab.index, rows] if slab.gather else src_ref.at[dev, rows]

    local = [pltpu.make_async_copy(src_for(me), out_ref.at[me], local_sem.at[0])]
    remote = []
    for k in range(1, N_DEV):
        px = 1 - x if k & 4 else x
        py = 1 - y if k & 2 else y
        pc = 1 - c if k & 1 else c
        remote.append(pltpu.make_async_remote_copy(
            src_ref=src_for(4 * px + 2 * py + pc), dst_ref=out_ref.at[me], send_sem=send_sems.at[k - 1], recv_sem=recv_sems.at[k - 1],
            device_id=(px, py, pc), device_id_type=pl.DeviceIdType.MESH))
    return remote, local


def _exchange_start(copies):
    remote, local = copies
    for cp in local + remote:
        cp.start()


def _exchange_wait(copies):
    remote, local = copies
    for cp in remote:
        cp.wait_send()
    for cp in remote:
        cp.wait_recv()
    for cp in local:
        cp.wait()


def _exchange_sems():
    return [pltpu.SemaphoreType.DMA((N_DEV - 1,)), pltpu.SemaphoreType.DMA((N_DEV - 1,)), pltpu.SemaphoreType.DMA((1,))]


class _Ride:
    def __init__(self, items):
        self.items = items
        self.outs = None


def _pcall(body, args, *, name, grid, in_specs, out_specs, out_shape, sem, scratch_shapes=(), ride=None):
    if ride is None:
        return pl.pallas_call(body, name=name, grid=grid, in_specs=in_specs, out_specs=out_specs, out_shape=out_shape,
                              scratch_shapes=list(scratch_shapes), compiler_params=_cparams(sem))(*args)
    n_in, n_out, n_scr, n_ride = len(in_specs), len(out_specs), len(scratch_shapes), len(ride.items)
    any_spec = pl.BlockSpec(memory_space=pl.ANY)

    def with_ride(*refs):
        pos = 0
        ins, pos = refs[pos:pos + n_in], pos + n_in
        srcs, pos = refs[pos:pos + n_ride], pos + n_ride
        outs, pos = refs[pos:pos + n_out], pos + n_out
        dsts, pos = refs[pos:pos + n_ride], pos + n_ride
        scr, pos = refs[pos:pos + n_scr], pos + n_scr
        sems = refs[pos:]
        first = functools.reduce(lambda p, q: p & q, [pl.program_id(d) == 0 for d in range(len(grid))])
        last = functools.reduce(lambda p, q: p & q, [pl.program_id(d) == grid[d] - 1 for d in range(len(grid))])

        def copies():
            return [_exchange_copies(srcs[t], dsts[t], *sems[3 * t:3 * t + 3], slab=ride.items[t]) for t in range(n_ride)]

        @pl.when(first)
        def _():
            for cps in copies():
                _exchange_start(cps)

        body(*ins, *outs, *scr)

        @pl.when(last)
        def _():
            for cps in copies():
                _exchange_wait(cps)

    res = pl.pallas_call(
        with_ride, name=name, grid=grid, in_specs=list(in_specs) + [any_spec] * n_ride,
        out_specs=list(out_specs) + [any_spec] * n_ride,
        out_shape=list(out_shape) + [slab.out_shape() for slab in ride.items],
        scratch_shapes=list(scratch_shapes) + [s for _ in ride.items for s in _exchange_sems()],
        compiler_params=_cparams(("arbitrary",) * len(grid)),
    )(*args, *[slab.array for slab in ride.items])
    ride.outs = list(res[n_out:])
    return list(res[:n_out])


def _mm_shape(a, b, mode):
    if mode == "nn":
        (m, k), (k2, n) = a.shape, b.shape
    elif mode == "nt":
        (m, k), (n, k2) = a.shape, b.shape
    else:
        (k, m), (k2, n) = a.shape, b.shape
    assert k == k2, (a.shape, b.shape, mode)
    return m, n, k


def _mmx(pairs, mode, name, out_dtypes, epilogue, extras=(), n_acc=1, ride=None):
    m, n, k = _mm_shape(pairs[0][0], pairs[0][1], mode)
    lefts = {id(a): a for a, _, _ in pairs}.values()
    tile_bytes = sum((1 if e.shape[0] == 1 and m != 1 else 0) * e.dtype.itemsize for e in extras)
    full_bytes = sum(e.dtype.itemsize for e in extras if not (e.shape[0] == 1 and m != 1))
    full_bytes += sum(jnp.dtype(dt).itemsize for dt in out_dtypes)

    def plan(tm, tn, tk):
        ni, nj, steps_k = m // tm, n // tn, k // tk
        a_bytes = sum(tm * tk * a.dtype.itemsize for a in lefts)
        b_bytes = sum(tk * tn * b.dtype.itemsize for _, b, _ in pairs)
        io_bytes = tm * tn * full_bytes + tn * tile_bytes
        vmem = 2 * (a_bytes + b_bytes + io_bytes) + (n_acc * tm * tn * 4 if steps_k > 1 else tm * MM_CHUNK * 4 * n_acc)
        stream = (a_bytes if steps_k > 1 else a_bytes / nj) + (b_bytes if nj * steps_k > 1 else b_bytes / ni) + io_bytes / steps_k
        step = max(2.0 * tm * tn * tk * len(pairs) / MXU_FLOPS, stream / HBM_BYTES_PER_S) + GRID_STEP_S
        return ni * nj * steps_k * step, vmem

    def divisors(size, cap):
        return [t for t in range(LANES, min(size, cap) + 1, LANES) if size % t == 0] or [size]

    options = [(tm, tn, tk) for tm in divisors(m, MM_ROWS_MAX) for tn in divisors(n, MM_COLS_MAX) for tk in divisors(k, MM_DEPTH_MAX)
               if tk >= min(k, 512)]
    fitting = [o for o in options if plan(*o)[1] <= MM_VMEM_BUDGET_BYTES]
    tm, tn, tk = min(fitting or options, key=lambda o: (plan(*o)[0] if fitting else plan(*o)[1]))
    nk = k // tk
    dims = {"nn": (((1,), (0,)), ((), ())), "nt": (((1,), (1,)), ((), ())), "tn": (((0,), (0,)), ((), ()))}[mode]
    a_spec = pl.BlockSpec((tk, tm), lambda i, j, kk: (kk, i)) if mode == "tn" else pl.BlockSpec((tm, tk), lambda i, j, kk: (i, kk))
    b_spec = pl.BlockSpec((tn, tk), lambda i, j, kk: (j, kk)) if mode == "nt" else pl.BlockSpec((tk, tn), lambda i, j, kk: (kk, j))
    ins, specs, where = [], [], []
    for a, b, _ in pairs:
        assert _mm_shape(a, b, mode) == (m, n, k)
        ia = next((t for t, x in enumerate(ins) if x is a), None)
        if ia is None:
            ins.append(a)
            specs.append(a_spec)
            ia = len(ins) - 1
        ins.append(b)
        specs.append(b_spec)
        where.append((ia, len(ins) - 1))
    n_ops = len(ins)
    for e in extras:
        ins.append(e)
        specs.append(pl.BlockSpec((1, tn), lambda i, j, kk: (0, j)) if e.shape[0] == 1 and m != 1 else
                     pl.BlockSpec((tm, tn), lambda i, j, kk: (i, j)))
    n_ex, n_out = len(extras), len(out_dtypes)

    def body(*refs):
        ex_refs = refs[n_ops:n_ops + n_ex]
        o_refs = refs[n_ops + n_ex:n_ops + n_ex + n_out]
        accs = refs[n_ops + n_ex + n_out:]
        kk = pl.program_id(2)

        @pl.when(kk == 0)
        def _():
            for acc in accs:
                acc[...] = jnp.zeros_like(acc)

        for (ia, ib), (_, _, which) in zip(where, pairs):
            accs[which][...] += lax.dot_general(refs[ia][...].astype(BF16), refs[ib][...].astype(BF16), dims,
                                                preferred_element_type=F32)

        @pl.when(kk == nk - 1)
        def _():
            tiles = epilogue([acc[...] for acc in accs], *[e[...] for e in ex_refs])
            for o_ref, t in zip(o_refs, tiles):
                o_ref[...] = t.astype(o_ref.dtype)

    def body_one_pass(*refs):
        ex_refs = refs[n_ops:n_ops + n_ex]
        o_refs = refs[n_ops + n_ex:n_ops + n_ex + n_out]
        lefts = {ia: refs[ia][...].astype(BF16) for ia, _ in where}
        for c0 in range(0, tn, MM_CHUNK):
            cw = min(MM_CHUNK, tn - c0)
            accs = [None] * n_acc
            for (ia, ib), (_, _, which) in zip(where, pairs):
                right = refs[ib][c0:c0 + cw, :] if mode == "nt" else refs[ib][:, c0:c0 + cw]
                part = lax.dot_general(lefts[ia], right.astype(BF16), dims, preferred_element_type=F32)
                accs[which] = part if accs[which] is None else accs[which] + part
            tiles = epilogue(accs, *[e[:, c0:c0 + cw] for e in ex_refs])
            for o_ref, t in zip(o_refs, tiles):
                o_ref[:, c0:c0 + cw] = t.astype(o_ref.dtype)

    return _pcall(
        body_one_pass if nk == 1 else body, ins, name=name, grid=(m // tm, n // tn, nk), in_specs=specs,
        out_specs=[pl.BlockSpec((tm, tn), lambda i, j, kk: (i, j))] * n_out,
        out_shape=[jax.ShapeDtypeStruct((m, n), dt) for dt in out_dtypes],
        scratch_shapes=[] if nk == 1 else [pltpu.VMEM((tm, tn), F32)] * n_acc,
        sem=("parallel", "parallel", "arbitrary"), ride=ride)


def _mm(a, b, mode, out_dtype, name, bias=None, residual=None, ride=None):
    extras = ([] if bias is None else [bias.reshape(1, -1).astype(F32)]) + ([] if residual is None else [residual])
    (out,) = _mmx([(a, b, 0)], mode, name, [out_dtype], lambda accs, *ex: [functools.reduce(lambda p, q: p + q.astype(F32), ex, accs[0])],
                  extras=extras, ride=ride)
    return out


def _rowwise(fn, rows, consts, out_rows, out_accs, name, tm=None, ride=None):
    rows = [r if isinstance(r, tuple) else (r, r.shape[1], 0) for r in rows]
    n_rows = rows[0][0].shape[0]
    if tm is None:
        row_bytes = sum(w * r.dtype.itemsize for r, w, _ in rows) + sum(w * jnp.dtype(dt).itemsize for w, dt in out_rows)
        tm = _tile(n_rows, max(8, min(640, ROW_TILE_BUDGET_BYTES // (2 * row_bytes))), 8)
    steps = n_rows // tm
    in_specs = [pl.BlockSpec((tm, w), functools.partial(lambda i, c: (i, c), c=cb)) for _, w, cb in rows]
    in_specs += [pl.BlockSpec(c.shape, lambda i: (0, 0)) for c in consts]
    out_specs = [pl.BlockSpec((tm, w), lambda i: (i, 0)) for w, _ in out_rows]
    out_specs += [pl.BlockSpec(s, lambda i: (0, 0)) for s in out_accs]
    out_shape = [jax.ShapeDtypeStruct((n_rows, w), d) for w, d in out_rows]
    out_shape += [jax.ShapeDtypeStruct(s, F32) for s in out_accs]
    n_in, n_or = len(rows) + len(consts), len(out_rows)

    def body(*refs):
        i = pl.program_id(0)
        vals = fn(i * tm, *[r[...] for r in refs[:n_in]])
        outs = refs[n_in:]
        for o_ref, v in zip(outs[:n_or], vals[:n_or]):
            o_ref[...] = v.astype(o_ref.dtype)
        if out_accs:
            @pl.when(i == 0)
            def _():
                for a_ref in outs[n_or:]:
                    a_ref[...] = jnp.zeros_like(a_ref)

            for a_ref, v in zip(outs[n_or:], vals[n_or:]):
                a_ref[...] += v

    return _pcall(body, [r[0] for r in rows] + list(consts), name=name, grid=(steps,), in_specs=in_specs, out_specs=out_specs,
                  out_shape=out_shape, sem=("arbitrary",) if out_accs else ("parallel",), ride=ride)


def _colsum(v):
    return jnp.sum(v, axis=0, keepdims=True)


def _rms(h, w):
    return h * lax.rsqrt(jnp.mean(h * h, axis=-1, keepdims=True) + NORM_EPS) * w


def _rms_fwd(h, w, name):
    (hn,) = _rowwise(lambda r0, hv, wv: (_rms(hv, wv),), [h], [w.reshape(1, -1)], [(h.shape[1], BF16)], [], name)
    return hn


def _rms_bwd(h, w, dhn, dh_in, name):
    d = h.shape[1]

    def fn(r0, hv, dv, rv, wv):
        _, vjp = jax.vjp(_rms, hv, wv)
        dh, dw = vjp(dv.astype(F32))
        return dh + rv, dh + rv, dw, _colsum(dh + rv)

    dh, dh_bf16, dw, sums = _rowwise(fn, [h, dhn, dh_in], [w.reshape(1, -1)], [(d, F32), (d, BF16)], [(1, d), (1, d)], name)
    return dh, dh_bf16, dw[0], sums[0]


def _glu(p, b):
    t = p + b
    d = t.shape[1] // 2
    return t[:, :d] * jax.nn.sigmoid(t[:, d:])


def _glu_fwd(p, b, name, ride=None):
    (u,) = _rowwise(lambda r0, v, bv: (_glu(v, bv),), [p], [b.reshape(1, -1)], [(p.shape[1] // 2, F32)], [], name, ride=ride)
    return u


def _glu_bwd(p, b, du, name):
    def fn(r0, v, dv, bv):
        _, vjp = jax.vjp(_glu, v, bv)
        dp, db = vjp(dv)
        return dp, db

    dp, db = _rowwise(fn, [p, du], [b.reshape(1, -1)], [(p.shape[1], BF16)], [(1, p.shape[1])], name)
    return dp, db[0]


def _ln_silu(c, g, b):
    mu = jnp.mean(c, axis=-1, keepdims=True)
    xc = c - mu
    var = jnp.mean(xc * xc, axis=-1, keepdims=True)
    return jax.nn.silu(xc * lax.rsqrt(var + LN_EPS) * g + b)


def _ln_silu_fwd(c, g, b, name):
    (s,) = _rowwise(lambda r0, v, gv, bv: (_ln_silu(v, gv, bv),), [c], [g.reshape(1, -1), b.reshape(1, -1)],
                    [(c.shape[1], BF16)], [], name)
    return s


def _ln_silu_bwd(c, g, b, ds, name):
    d = c.shape[1]

    def fn(r0, v, dv, gv, bv):
        _, vjp = jax.vjp(_ln_silu, v, gv, bv)
        return vjp(dv.astype(F32))

    dc, dg, db = _rowwise(fn, [c, ds], [g.reshape(1, -1), b.reshape(1, -1)], [(d, F32)], [(1, d), (1, d)], name)
    return dc, dg[0], db[0]


def _rot_half(x):
    w = x.shape[1]
    lane = lax.broadcasted_iota(jnp.int32, x.shape, 1)
    lo = (lane % HEAD_DIM) < (HEAD_DIM // 2)
    return jnp.where(lo, -pltpu.roll(x, w - HEAD_DIM // 2, axis=1), pltpu.roll(x, HEAD_DIM // 2, axis=1))


def _qkv_post_fwd(pre, b, cos, sin, qw, kw, name):
    reps = (qw + kw) // LANES

    def fn(r0, pv, cv, sv, bv):
        t = pv + bv
        tq = t[:, :qw + kw]
        y = tq * jnp.tile(cv, (1, reps)) + _rot_half(tq) * jnp.tile(sv, (1, reps))
        return y[:, :qw], y[:, qw:], t[:, qw + kw:]

    return _rowwise(fn, [pre, cos, sin], [b.reshape(1, -1)], [(qw, BF16), (kw, BF16), (kw, BF16)], [], name)


def _qkv_post_bwd(dq, dk, dv, cos, sin, name):
    qw, kw = dq.shape[1], dk.shape[1]
    reps = (qw + kw) // LANES
    width = qw + 2 * kw

    def fn(r0, dqv, dkv, dvv, cv, sv):
        dy = jnp.concatenate([dqv.astype(F32), dkv.astype(F32)], axis=1)
        dt = dy * jnp.tile(cv, (1, reps)) - _rot_half(dy * jnp.tile(sv, (1, reps)))
        dpre = jnp.concatenate([dt, dvv.astype(F32)], axis=1)
        return dpre, _colsum(dpre)

    dpre, db = _rowwise(fn, [dq, dk, dv, cos, sin], [], [(width, BF16)], [(1, width)], name)
    return dpre, db[0]


def _dw_tiles(n_rows, c):
    return _tile(n_rows, 640, 8), _tile(c, 512)


class _RowWindow:
    SUBLANES = 8

    def __init__(self, value):
        self.copies = {0: value}

    def rows(self, off, n):
        q, s = divmod(off, self.SUBLANES)
        if s not in self.copies:
            base = self.copies[0]
            self.copies[s] = pltpu.roll(base, base.shape[0] - s, axis=0)
        return self.copies[s][self.SUBLANES * q:self.SUBLANES * q + n, :]


def _row_mask(r0, n, width):
    row = r0 + lax.broadcasted_iota(jnp.int32, (n, width), 0)
    return row >= PAD_LEN


def _dwconv_fwd(u, w, bias, c, name, ride=None):
    n_rows, taps = u.shape[0], w.shape[0]
    tr, cb = _dw_tiles(n_rows, c)
    kp = -(-taps // 8) * 8
    wp = jnp.concatenate([w.astype(F32), jnp.zeros((kp - taps, c), F32)], axis=0)
    bp = jnp.zeros((1, c), F32) if bias is None else bias.reshape(1, c).astype(F32)

    def body(cur_ref, prev_ref, w_ref, b_ref, o_ref):
        r = pl.program_id(1)
        cur = jnp.where(_row_mask(r * tr, tr, cb), cur_ref[...], 0.0)
        tail = jnp.where(_row_mask(r * tr - CONV_HALO, CONV_HALO, cb) & (r > 0), prev_ref[...], 0.0)
        win = _RowWindow(jnp.concatenate([tail, cur], axis=0))
        acc = jnp.zeros((tr, cb), F32) + b_ref[...]
        for k in range(taps):
            acc = acc + w_ref[k:k + 1, :] * win.rows(CONV_HALO - (taps - 1) + k, tr)
        o_ref[...] = acc

    per = tr // CONV_HALO
    (out,) = _pcall(
        body, [u, u, wp, bp], name=name, grid=(c // cb, n_rows // tr),
        in_specs=[pl.BlockSpec((tr, cb), lambda j, r: (r, j)),
                  pl.BlockSpec((CONV_HALO, cb), lambda j, r: (jnp.maximum(r * per - 1, 0), j)),
                  pl.BlockSpec((kp, cb), lambda j, r: (0, j)),
                  pl.BlockSpec((1, cb), lambda j, r: (0, j))],
        out_specs=[pl.BlockSpec((tr, cb), lambda j, r: (r, j))],
        out_shape=[jax.ShapeDtypeStruct((n_rows, c), F32)], sem=("parallel", "parallel"), ride=ride)
    return out


def _dwconv_bwd(u, w, dc, c, name, du_dtype=F32, ride=None):
    n_rows, taps = u.shape[0], w.shape[0]
    tr, cb = _dw_tiles(n_rows, c)
    nr = n_rows // tr
    per = tr // CONV_HALO
    kp = -(-taps // 8) * 8
    wp = jnp.concatenate([w.astype(F32), jnp.zeros((kp - taps, c), F32)], axis=0)

    def body(cur_ref, prev_ref, d_ref, dnext_ref, w_ref, du_ref, dw_ref, db_ref):
        r = pl.program_id(1)
        cur = jnp.where(_row_mask(r * tr, tr, cb), cur_ref[...], 0.0)
        tail = jnp.where(_row_mask(r * tr - CONV_HALO, CONV_HALO, cb) & (r > 0), prev_ref[...], 0.0)
        win_u = _RowWindow(jnp.concatenate([tail, cur], axis=0))
        d = d_ref[...]
        head = jnp.where(r < nr - 1, dnext_ref[...], 0.0)
        win_d = _RowWindow(jnp.concatenate([d, head], axis=0))

        @pl.when(r == 0)
        def _():
            dw_ref[...] = jnp.zeros_like(dw_ref)
            db_ref[...] = jnp.zeros_like(db_ref)

        du = jnp.zeros((tr, cb), F32)
        for k in range(taps):
            du = du + w_ref[k:k + 1, :] * win_d.rows(taps - 1 - k, tr)
            dw_ref[k:k + 1, :] += _colsum(d * win_u.rows(CONV_HALO - (taps - 1) + k, tr))
        du_ref[...] = jnp.where(_row_mask(r * tr, tr, cb), du, 0.0).astype(du_ref.dtype)
        db_ref[...] += _colsum(d)

    du, dw, db = _pcall(
        body, [u, u, dc, dc, wp], name=name, grid=(c // cb, nr),
        in_specs=[pl.BlockSpec((tr, cb), lambda j, r: (r, j)),
                  pl.BlockSpec((CONV_HALO, cb), lambda j, r: (jnp.maximum(r * per - 1, 0), j)),
                  pl.BlockSpec((tr, cb), lambda j, r: (r, j)),
                  pl.BlockSpec((CONV_HALO, cb), lambda j, r: (jnp.minimum((r + 1) * per, nr * per - 1), j)),
                  pl.BlockSpec((kp, cb), lambda j, r: (0, j))],
        out_specs=[pl.BlockSpec((tr, cb), lambda j, r: (r, j)),
                   pl.BlockSpec((kp, cb), lambda j, r: (0, j)),
                   pl.BlockSpec((1, cb), lambda j, r: (0, j))],
        out_shape=[jax.ShapeDtypeStruct((n_rows, c), du_dtype), jax.ShapeDtypeStruct((kp, c), F32),
                   jax.ShapeDtypeStruct((1, c), F32)], sem=("parallel", "arbitrary"), ride=ride)
    return du, dw[:taps], db[0]


def _attn_block(q, kprev, kcur, vprev, vcur, sink, n):
    qf = q.reshape(GROUP * BLOCK, HEAD_DIM).astype(BF16)
    kb = jnp.concatenate([kprev, kcur], axis=0).astype(BF16)
    vb = jnp.concatenate([vprev, vcur], axis=0).astype(BF16)
    s = lax.dot_general(qf, kb, (((1,), (1,)), ((), ())), preferred_element_type=F32) * (HEAD_DIM ** -0.5)
    s = s.reshape(GROUP, BLOCK, 2 * BLOCK)
    qi = lax.broadcasted_iota(jnp.int32, (BLOCK, 2 * BLOCK), 0)
    kj = lax.broadcasted_iota(jnp.int32, (BLOCK, 2 * BLOCK), 1)
    dist = qi + BLOCK - kj
    allowed = (dist >= 0) & (dist < BLOCK) & ((n - 1) * BLOCK + kj >= PAD_LEN)
    s = jnp.where(allowed[None], s, NEG_INF)
    m = lax.stop_gradient(jnp.maximum(jnp.max(s, axis=-1, keepdims=True), sink))
    e = jnp.exp(s - m)
    p = e / (jnp.sum(e, axis=-1, keepdims=True) + jnp.exp(sink - m))
    o = jnp.dot(p.reshape(GROUP * BLOCK, 2 * BLOCK).astype(BF16), vb, preferred_element_type=F32)
    return o.reshape(GROUP, BLOCK, HEAD_DIM)


def _attn_specs(nb):
    q_spec = pl.BlockSpec((GROUP, BLOCK, HEAD_DIM), lambda g, n: (g, n, 0))
    cur = pl.BlockSpec((1, BLOCK, HEAD_DIM), lambda g, n: (g, n, 0))
    prev = pl.BlockSpec((1, BLOCK, HEAD_DIM), lambda g, n: (g, jnp.maximum(n - 1, 0), 0))
    sink = pl.BlockSpec((1, GROUP, 1, 1), lambda g, n: (g, 0, 0, 0))
    return q_spec, cur, prev, sink


def _attn_fwd(q, k, v, sinks, name, ride=None):
    heads, n_rows, _ = q.shape
    nb = n_rows // BLOCK
    q_spec, cur, prev, sink = _attn_specs(nb)

    def body(q_ref, kp_ref, kc_ref, vp_ref, vc_ref, s_ref, o_ref):
        n = pl.program_id(1)
        o = _attn_block(q_ref[...].astype(F32), kp_ref[0].astype(F32), kc_ref[0].astype(F32), vp_ref[0].astype(F32),
                        vc_ref[0].astype(F32), s_ref[0], n)
        o_ref[...] = o.astype(o_ref.dtype)

    (out,) = _pcall(
        body, [q, k, k, v, v, sinks.reshape(heads // GROUP, GROUP, 1, 1)], name=name, grid=(heads // GROUP, nb),
        in_specs=[q_spec, prev, cur, prev, cur, sink], out_specs=[q_spec], out_shape=[jax.ShapeDtypeStruct(q.shape, BF16)],
        sem=("parallel", "parallel"), ride=ride)
    return out


def _attn_bwd(q, k, v, sinks, do, name):
    heads, n_rows, _ = q.shape
    kvh = heads // GROUP
    nb = n_rows // BLOCK
    q_spec, cur, prev, sink = _attn_specs(nb)
    part = pl.BlockSpec((1, 1, BLOCK, HEAD_DIM), lambda g, n: (g, n, 0, 0))
    part_shape = jax.ShapeDtypeStruct((kvh, nb, BLOCK, HEAD_DIM), F32)

    def body(q_ref, kp_ref, kc_ref, vp_ref, vc_ref, s_ref, do_ref, dq_ref, dkp_ref, dkc_ref, dvp_ref, dvc_ref, ds_ref):
        n = pl.program_id(1)
        f = functools.partial(_attn_block, n=n)
        _, vjp = jax.vjp(f, q_ref[...].astype(F32), kp_ref[0].astype(F32), kc_ref[0].astype(F32), vp_ref[0].astype(F32),
                         vc_ref[0].astype(F32), s_ref[0])
        dq, dkp, dkc, dvp, dvc, ds = vjp(do_ref[...].astype(F32))
        dq_ref[...] = dq.astype(dq_ref.dtype)
        dkp_ref[0, 0], dkc_ref[0, 0], dvp_ref[0, 0], dvc_ref[0, 0] = dkp, dkc, dvp, dvc

        @pl.when(n == 0)
        def _():
            ds_ref[...] = jnp.zeros_like(ds_ref)

        ds_ref[0] += ds

    return pl.pallas_call(
        body, name=name, grid=(kvh, nb), in_specs=[q_spec, prev, cur, prev, cur, sink, q_spec],
        out_specs=[q_spec, part, part, part, part, sink],
        out_shape=[jax.ShapeDtypeStruct(q.shape, BF16), part_shape, part_shape, part_shape, part_shape,
                   jax.ShapeDtypeStruct((kvh, GROUP, 1, 1), F32)],
        compiler_params=_cparams(("parallel", "arbitrary")),
    )(q, k, k, v, v, sinks.reshape(kvh, GROUP, 1, 1), do)


def _shift_add(own, to_prev, name):
    kvh, nb = own.shape[:2]
    blk = (1, 1, BLOCK, HEAD_DIM)

    def body(a_ref, b_ref, o_ref):
        n = pl.program_id(1)
        o_ref[...] = (a_ref[...] + jnp.where(n < nb - 1, b_ref[...], 0.0)).astype(o_ref.dtype)

    return pl.pallas_call(
        body, name=name, grid=(kvh, nb),
        in_specs=[pl.BlockSpec(blk, lambda g, n: (g, n, 0, 0)),
                  pl.BlockSpec(blk, lambda g, n: (g, jnp.minimum(n + 1, nb - 1), 0, 0))],
        out_specs=pl.BlockSpec(blk, lambda g, n: (g, n, 0, 0)),
        out_shape=jax.ShapeDtypeStruct(own.shape, BF16), compiler_params=_cparams(("parallel", "parallel")),
    )(own, to_prev)


def _gdn_gates(ba, alog, dt, r0, hv):
    lane = lax.broadcasted_iota(jnp.int32, ba.shape, 1)
    t = ba + dt
    softplus = jnp.maximum(t, 0.0) + jnp.log(1.0 + jnp.exp(-jnp.abs(t)))
    val = jnp.where(lane < hv, jax.nn.sigmoid(ba), jnp.where(lane < 2 * hv, -jnp.exp(alog) * softplus, 0.0))
    return jnp.where(_row_mask(r0, ba.shape[0], ba.shape[1]), val, 0.0)


def _l2n(x):
    return x * lax.rsqrt(jnp.sum(x * x, axis=-1, keepdims=True) + 1e-6)


_NN = (((2,), (1,)), ((0,), (0,)))
_NT = (((2,), (2,)), ((0,), (0,)))
_TN = (((1,), (1,)), ((0,), (0,)))


def _bdot(a, b, dims=_NN):
    return lax.dot_general(a.astype(BF16), b.astype(BF16), dims, preferred_element_type=F32)


def _dot3(a, b, dims):
    ah, bh = a.astype(BF16), b.astype(BF16)
    al, bl = (a - ah.astype(F32)).astype(BF16), (b - bh.astype(F32)).astype(BF16)

    def d(p, q):
        return lax.dot_general(p, q, dims, preferred_element_type=F32)

    return d(ah, bh) + (d(ah, bl) + d(al, bh))


@jax.custom_vjp
def _pdot(a, b):
    return _dot3(a, b, _NN)


def _pdot_fwd(a, b):
    return _dot3(a, b, _NN), (a, b)


def _pdot_bwd(res, ct):
    a, b = res
    return _bdot(ct, b, _NT), _bdot(a, ct, _TN)


_pdot.defvjp(_pdot_fwd, _pdot_bwd)


def _scan_chunks(x, reverse):
    n, c = x.shape[0], GDN_CHUNK
    row = lax.broadcasted_iota(jnp.int32, x.shape, 0) % c
    s = 1
    while s < c:
        if reverse:
            x = x + jnp.where(row < c - s, pltpu.roll(x, n - s, axis=0), 0.0)
        else:
            x = x + jnp.where(row >= s, pltpu.roll(x, s, axis=0), 0.0)
        s *= 2
    return x


@jax.custom_vjp
def _cumsum_chunks(x):
    return _scan_chunks(x, False)


_cumsum_chunks.defvjp(lambda x: (_scan_chunks(x, False), None), lambda _, ct: (_scan_chunks(ct, True),))


def _gdn_heads(states, qkv, z, gates, norm_w, hk0, hv_total):
    c, h = GDN_CHUNK, states.shape[0]
    g = h // 2

    def cols(src, starts):
        return jnp.concatenate([src[:, s:s + GDN_DIM] for s in starts], axis=0).reshape(len(starts), c, GDN_DIM)

    q = _l2n(jax.nn.silu(cols(qkv, [4 * GDN_DIM * t for t in range(g)]))) * (GDN_DIM ** -0.5)
    k = _l2n(jax.nn.silu(cols(qkv, [4 * GDN_DIM * t + GDN_DIM for t in range(g)])))
    q, k = jnp.repeat(q, 2, axis=0), jnp.repeat(k, 2, axis=0)
    v = jax.nn.silu(cols(qkv, [4 * GDN_DIM * (t // 2) + (2 + t % 2) * GDN_DIM for t in range(h)]))
    zz = cols(z, [GDN_DIM * t for t in range(h)])
    lane = lax.broadcasted_iota(jnp.int32, gates.shape, 1)

    def col_of(first):
        return jnp.concatenate([jnp.sum(jnp.where(lane == first + t, gates, 0.0), axis=1, keepdims=True) for t in range(h)],
                               axis=0).reshape(h, c, 1)

    beta_col, g_col = col_of(2 * hk0), col_of(hv_total + 2 * hk0)
    i = lax.broadcasted_iota(jnp.int32, (c, c), 0)
    j = lax.broadcasted_iota(jnp.int32, (c, c), 1)
    causal, strict = (i >= j)[None], (i > j)[None]
    gc = _cumsum_chunks(jnp.broadcast_to(g_col, (h, c, GDN_DIM)).reshape(h * c, GDN_DIM)).reshape(h, c, GDN_DIM)
    gc_i = gc[:, :, :c]
    gc_j = jnp.swapaxes(gc_i, 1, 2)
    gc_last = jnp.broadcast_to(gc[:, c - 1:c, :], (h, GDN_DIM, GDN_DIM))
    decay = jnp.where(causal, jnp.exp(jnp.where(causal, gc_i - gc_j, 0.0)), 0.0)
    k_beta = k * beta_col
    lower = jnp.where(strict, _bdot(k_beta, k, _NT) * decay, 0.0)
    eye = (i == j).astype(F32)[None]
    neg = -lower
    inv = eye + neg
    power = neg
    for _ in range(5):
        power = _pdot(power, power)
        inv = _pdot(inv, eye + power)
    sol = _pdot(inv, jnp.concatenate([v * beta_col, k_beta * jnp.exp(gc)], axis=2))
    u, w = sol[:, :, :GDN_DIM], sol[:, :, GDN_DIM:]
    intra = jnp.where(causal, _bdot(q, k, _NT) * decay, 0.0)
    q_dec = q * jnp.exp(gc)
    k_dec = k * jnp.exp(gc_last[:, :c] - gc)
    v_new = u - _bdot(w, states)
    o = _bdot(q_dec, states) + _bdot(intra, v_new)
    new_states = states * jnp.exp(gc_last) + _bdot(k_dec, v_new, _TN)
    y = _rms(o, norm_w) * jax.nn.silu(zz)
    return jnp.concatenate([y[t] for t in range(h)], axis=1), new_states


GDN_KEY_HEADS_PER_STEP = 8


def _key_heads_per_step(hk_total):
    return math.gcd(hk_total, GDN_KEY_HEADS_PER_STEP)


def _gdn_fwd(cq, proj, gates, norm_w, hk_total, name):
    n_rows = cq.shape[0]
    nc, hv_total = n_rows // GDN_CHUNK, 2 * hk_total
    grp = _key_heads_per_step(hk_total)
    heads = 2 * grp
    zblk0 = cq.shape[1] // (heads * GDN_DIM)

    def body(cq_ref, z_ref, g_ref, w_ref, y_ref, save_ref, state):
        n, hg = pl.program_id(0), pl.program_id(1)

        @pl.when(n == 0)
        def _():
            state[pl.ds(heads * hg, heads)] = jnp.zeros((heads, GDN_DIM, GDN_DIM), F32)

        s_in = state[pl.ds(heads * hg, heads)]
        save_ref[0] = s_in
        y, s_out = _gdn_heads(s_in, cq_ref[...], z_ref[...], g_ref[...], w_ref[...], grp * hg, hv_total)
        y_ref[...] = y.astype(y_ref.dtype)
        state[pl.ds(heads * hg, heads)] = s_out

    return pl.pallas_call(
        body, name=name, grid=(nc, hk_total // grp),
        in_specs=[pl.BlockSpec((GDN_CHUNK, 2 * heads * GDN_DIM), lambda n, h: (n, h)),
                  pl.BlockSpec((GDN_CHUNK, heads * GDN_DIM), lambda n, h: (n, zblk0 + h)),
                  pl.BlockSpec((GDN_CHUNK, LANES), lambda n, h: (n, 0)),
                  pl.BlockSpec((1, GDN_DIM), lambda n, h: (0, 0))],
        out_specs=[pl.BlockSpec((GDN_CHUNK, heads * GDN_DIM), lambda n, h: (n, h)),
                   pl.BlockSpec((1, heads, GDN_DIM, GDN_DIM), lambda n, h: (n, h, 0, 0))],
        out_shape=[jax.ShapeDtypeStruct((n_rows, hv_total * GDN_DIM), BF16),
                   jax.ShapeDtypeStruct((nc, hv_total, GDN_DIM, GDN_DIM), F32)],
        scratch_shapes=[pltpu.VMEM((hv_total, GDN_DIM, GDN_DIM), F32)],
        compiler_params=_cparams(("arbitrary", "arbitrary")),
    )(cq, proj, gates, norm_w.reshape(1, GDN_DIM))


def _gdn_bwd(cq, proj, gates, norm_w, saved, dy, hk_total, name):
    n_rows = cq.shape[0]
    nc, hv_total = n_rows // GDN_CHUNK, 2 * hk_total
    grp = _key_heads_per_step(hk_total)
    heads = 2 * grp
    zblk0 = cq.shape[1] // (heads * GDN_DIM)

    def body(cq_ref, z_ref, g_ref, w_ref, save_ref, dy_ref, dcq_ref, dz_ref, dg_ref, dw_ref, dstate):
        n, hg = pl.program_id(0), pl.program_id(1)

        @pl.when(n == 0)
        def _():
            dstate[pl.ds(heads * hg, heads)] = jnp.zeros((heads, GDN_DIM, GDN_DIM), F32)

        @pl.when((n == 0) & (hg == 0))
        def _():
            dw_ref[...] = jnp.zeros_like(dw_ref)

        @pl.when(hg == 0)
        def _():
            dg_ref[...] = jnp.zeros_like(dg_ref)

        f = functools.partial(_gdn_heads, hk0=grp * hg, hv_total=hv_total)
        _, vjp = jax.vjp(f, save_ref[0], cq_ref[...], z_ref[...], g_ref[...], w_ref[...])
        ds, dcq, dz, dg, dw = vjp((dy_ref[...].astype(F32), dstate[pl.ds(heads * hg, heads)]))
        dstate[pl.ds(heads * hg, heads)] = ds
        dcq_ref[...] = dcq
        dz_ref[...] = dz.astype(dz_ref.dtype)
        dg_ref[...] += dg
        dw_ref[...] += dw

    rev = lambda n: nc - 1 - n
    return pl.pallas_call(
        body, name=name, grid=(nc, hk_total // grp),
        in_specs=[pl.BlockSpec((GDN_CHUNK, 2 * heads * GDN_DIM), lambda n, h: (rev(n), h)),
                  pl.BlockSpec((GDN_CHUNK, heads * GDN_DIM), lambda n, h: (rev(n), zblk0 + h)),
                  pl.BlockSpec((GDN_CHUNK, LANES), lambda n, h: (rev(n), 0)),
                  pl.BlockSpec((1, GDN_DIM), lambda n, h: (0, 0)),
                  pl.BlockSpec((1, heads, GDN_DIM, GDN_DIM), lambda n, h: (rev(n), h, 0, 0)),
                  pl.BlockSpec((GDN_CHUNK, heads * GDN_DIM), lambda n, h: (rev(n), h))],
        out_specs=[pl.BlockSpec((GDN_CHUNK, 2 * heads * GDN_DIM), lambda n, h: (rev(n), h)),
                   pl.BlockSpec((GDN_CHUNK, heads * GDN_DIM), lambda n, h: (rev(n), h)),
                   pl.BlockSpec((GDN_CHUNK, LANES), lambda n, h: (rev(n), 0)),
                   pl.BlockSpec((1, GDN_DIM), lambda n, h: (0, 0))],
        out_shape=[jax.ShapeDtypeStruct(cq.shape, F32), jax.ShapeDtypeStruct((n_rows, hv_total * GDN_DIM), BF16),
                   jax.ShapeDtypeStruct((n_rows, LANES), F32), jax.ShapeDtypeStruct((1, GDN_DIM), F32)],
        scratch_shapes=[pltpu.VMEM((hv_total, GDN_DIM, GDN_DIM), F32)],
        compiler_params=_cparams(("arbitrary", "arbitrary")),
    )(cq, proj, gates, norm_w.reshape(1, GDN_DIM), saved, dy)


def _final_loss(h, w, target, name):
    d = h.shape[1]

    def fn(r0, hv, tv, wv):
        def loss_of(hh, ww):
            err = jnp.where(_row_mask(r0, hh.shape[0], d) & (r0 + lax.broadcasted_iota(jnp.int32, hh.shape, 0) >= FRONT),
                            _rms(hh, ww) - tv, 0.0)
            return 0.5 * jnp.sum(jnp.sum(err * err, axis=1, keepdims=True) / d)

        loss, vjp = jax.vjp(loss_of, hv, wv)
        dh, dw = vjp(jnp.ones((), F32))
        return dh, dh, jnp.zeros((1, LANES), F32) + loss, dw

    dh, dh_bf16, loss, dw = _rowwise(fn, [h, target], [w.reshape(1, -1)], [(d, F32), (d, BF16)], [(1, LANES), (1, d)], name)
    return loss[0, 0], dh, dh_bf16, dw[0]


def _rope_tables(n_rows):
    pos = (jnp.arange(n_rows) - PAD_LEN).astype(F32)
    inv_freq = ROPE_THETA ** (-jnp.arange(0, HEAD_DIM, 2, dtype=F32) / HEAD_DIM)
    ang = pos[:, None] * inv_freq[None, :]
    reps = LANES // (HEAD_DIM // 2)
    return jnp.tile(jnp.cos(ang), (1, reps)), jnp.tile(jnp.sin(ang), (1, reps))


def _to_heads(t):
    n_rows, w = t.shape
    return t.reshape(n_rows, w // HEAD_DIM, HEAD_DIM).transpose(1, 0, 2)


def _from_heads(t):
    heads, n_rows, _ = t.shape
    return t.transpose(1, 0, 2).reshape(n_rows, heads * HEAD_DIM)


def _local_step(h0, target, p, tr):
    n_rows, d = h0.shape
    depth = p["norm_mix"].shape[0]
    cos, sin = _rope_tables(n_rows)
    hk_total = d // GDN_DIM
    hv_total = 2 * hk_total
    conv_dim = 4 * hk_total * GDN_DIM
    qw, kw = d, d // GROUP
    saved = []
    h = h0
    mm = functools.partial(tr.mm, True)
    for i in range(depth):
        kind, j = i % 3, i // 3
        s = {"h": h}
        hn = _rms_fwd(h, p["norm_mix"][i], "rms_fwd")
        s["hn"] = hn
        if kind == 0:
            pre = mm(hn, tr.weight("conv_w_pw1", j), "nn", F32, "mm_pw1")
            u1 = tr.call(True, _us_rows(pre), lambda r: _glu_fwd(pre, p["conv_b_pw1"][j], "glu_fwd", ride=r))
            c = tr.call(True, _us_dwconv(n_rows, d, CONV_KERNEL),
                        lambda r: _dwconv_fwd(u1, p["conv_w_dw"][j], p["conv_b_dw"][j], d, "dwconv31_fwd", ride=r))
            sv = _ln_silu_fwd(c, p["conv_ln_g"][j], p["conv_ln_b"][j], "ln_silu_fwd")
            h = mm(sv, tr.weight("conv_w_pw2", j), "nn", F32, "mm_d_d_res", bias=p["conv_b_pw2"][j], residual=h)
            s.update(pre=pre, u1=u1, c=c, sv=sv)
        elif kind == 1:
            pre = mm(hn, tr.weight("attn_w_qkv", j), "nn", F32, "mm_qkv")
            q, k, v = _qkv_post_fwd(pre, p["attn_b_qkv"][j], cos, sin, qw, kw, "qkv_post_fwd")
            qh, kh, vh = _to_heads(q), _to_heads(k), _to_heads(v)
            o = _from_heads(tr.call(True, _us_attn(qh), lambda r: _attn_fwd(qh, kh, vh, p["attn_sinks"][j], "attn_fwd", ride=r)))
            h = mm(o, tr.weight("attn_w_o", j), "nn", F32, "mm_d_d_res", bias=p["attn_b_o"][j], residual=h)
            s.update(qh=qh, kh=kh, vh=vh, o=o)
        else:
            proj = mm(hn, tr.weight("gdn_w_in", j), "nn", F32, "mm_gdn_in")
            cq = tr.call(True, _us_dwconv(n_rows, conv_dim, GDN_CONV),
                         lambda r: _dwconv_fwd(proj, p["gdn_conv_w"][j], None, conv_dim, "dwconv4_fwd", ride=r))
            vec = _gate_vectors(p["gdn_a_log"][j], p["gdn_dt_bias"][j], hv_total)
            ba_blk = (conv_dim + hv_total * GDN_DIM) // LANES
            (gates,) = _rowwise(lambda r0, bav, av, dv: (_gdn_gates(bav, av, dv, r0, hv_total),), [(proj, LANES, ba_blk)],
                                [vec[0], vec[1]], [(LANES, F32)], [], "gdn_gates_fwd")
            y, states = _gdn_fwd(cq, proj, gates, p["gdn_norm_w"][j], hk_total, "gdn_fwd")
            h = mm(y, tr.weight("gdn_w_out", j), "nn", F32, "mm_gdn_out_res", residual=h)
            s.update(proj=proj, cq=cq, gates=gates, y=y, states=states)
        s["h1"] = h
        hn2 = _rms_fwd(h, p["norm_ffn"][i], "rms_fwd")
        w_gate, w_up = tr.weight("ffn_w_gate", i), tr.weight("ffn_w_up", i)
        gate, up, a = tr.call(True, 2 * _us_mm(n_rows, w_gate.shape[1], d), lambda r: _mmx(
            [(hn2, w_gate, 0), (hn2, w_up, 1)], "nn", "mm_ffn_swiglu", [BF16, BF16, BF16],
            lambda accs: [accs[0], accs[1], jax.nn.silu(accs[0]) * accs[1]], n_acc=2, ride=r))
        h = mm(a, tr.weight("ffn_w_down", i), "nn", F32, "mm_ffn_down_res", residual=h)
        s.update(hn2=hn2, gate=gate, up=up, a=a)
        saved.append(s)

    loss, dh, dhb, g_final = _final_loss(h, p["norm_final"], target, "final_loss")
    g = {k: [None] * v.shape[0] for k, v in p.items() if k not in ("norm_final", "meta_tokens")}
    g["norm_final"] = g_final
    mm = functools.partial(tr.mm, False)
    for i in reversed(range(depth)):
        kind, j = i % 3, i // 3
        s = saved[i]
        tr.give("ffn_w_down", i, mm(s["a"], dhb, "tn", BF16, "mm_dw_down"))
        w_gate, w_up, w_down = tr.weight("ffn_w_gate", i), tr.weight("ffn_w_up", i), tr.weight("ffn_w_down", i)

        def swiglu_bwd(accs, gate, up):
            _, vjp = jax.vjp(lambda gv, uv: jax.nn.silu(gv) * uv, gate.astype(F32), up.astype(F32))
            return list(vjp(accs[0]))

        dgate, dup = tr.call(False, _us_mm(n_rows, w_gate.shape[1], d), lambda r: _mmx(
            [(dhb, w_down, 0)], "nt", "mm_da_swiglu", [BF16, BF16], swiglu_bwd, extras=[s["gate"], s["up"]], ride=r))
        g_gate, g_up = tr.call(False, 2 * _us_mm(d, w_gate.shape[1], n_rows), lambda r: _mmx(
            [(s["hn2"], dgate, 0), (s["hn2"], dup, 1)], "tn", "mm_dw_gate_up", [BF16, BF16], lambda accs: accs, n_acc=2, ride=r))
        tr.give("ffn_w_gate", i, g_gate)
        tr.give("ffn_w_up", i, g_up)
        (dhn2,) = tr.call(False, 2 * _us_mm(n_rows, d, w_gate.shape[1]), lambda r: _mmx(
            [(dgate, w_gate, 0), (dup, w_up, 0)], "nt", "mm_dhn2", [F32], lambda accs: accs, ride=r))
        dh, dhb, g["norm_ffn"][i], dh_sums = _rms_bwd(s["h1"], p["norm_ffn"][i], dhn2, dh, "rms_bwd")
        if kind == 0:
            g["conv_b_pw2"][j] = dh_sums
            tr.give("conv_w_pw2", j, mm(s["sv"], dhb, "tn", BF16, "mm_dw_d_d"))
            dsv = mm(dhb, tr.weight("conv_w_pw2", j), "nt", F32, "mm_dx_d_d")
            dc, g["conv_ln_g"][j], g["conv_ln_b"][j] = _ln_silu_bwd(s["c"], p["conv_ln_g"][j], p["conv_ln_b"][j], dsv, "ln_silu_bwd")
            du1, g["conv_w_dw"][j], g["conv_b_dw"][j] = tr.call(
                False, 2 * _us_dwconv(n_rows, d, CONV_KERNEL),
                lambda r: _dwconv_bwd(s["u1"], p["conv_w_dw"][j], dc, d, "dwconv31_bwd", ride=r))
            dpre, g["conv_b_pw1"][j] = _glu_bwd(s["pre"], p["conv_b_pw1"][j], du1, "glu_bwd")
            tr.give("conv_w_pw1", j, mm(s["hn"], dpre, "tn", BF16, "mm_dw_pw1"))
            dhn = mm(dpre, tr.weight("conv_w_pw1", j), "nt", F32, "mm_dx_pw1")
        elif kind == 1:
            g["attn_b_o"][j] = dh_sums
            tr.give("attn_w_o", j, mm(s["o"], dhb, "tn", BF16, "mm_dw_d_d"))
            do = _to_heads(mm(dhb, tr.weight("attn_w_o", j), "nt", BF16, "mm_dx_d_d_bf16"))
            dq, dkp, dkc, dvp, dvc, dsink = _attn_bwd(s["qh"], s["kh"], s["vh"], p["attn_sinks"][j], do, "attn_bwd")
            g["attn_sinks"][j] = dsink.reshape(-1)
            kvh = kw // HEAD_DIM
            dk = _shift_add(dkc, dkp, "attn_shift_add").reshape(kvh, n_rows, HEAD_DIM)
            dv = _shift_add(dvc, dvp, "attn_shift_add").reshape(kvh, n_rows, HEAD_DIM)
            dpre, g["attn_b_qkv"][j] = _qkv_post_bwd(_from_heads(dq), _from_heads(dk), _from_heads(dv), cos, sin, "qkv_post_bwd")
            tr.give("attn_w_qkv", j, mm(s["hn"], dpre, "tn", BF16, "mm_dw_qkv"))
            dhn = mm(dpre, tr.weight("attn_w_qkv", j), "nt", F32, "mm_dx_qkv")
        else:
            tr.give("gdn_w_out", j, mm(s["y"], dhb, "tn", BF16, "mm_dw_gdn_out"))
            dy = mm(dhb, tr.weight("gdn_w_out", j), "nt", BF16, "mm_dx_gdn_out")
            dcq, dz, dgates, g_nw = _gdn_bwd(s["cq"], s["proj"], s["gates"], p["gdn_norm_w"][j], s["states"], dy, hk_total, "gdn_bwd")
            g["gdn_norm_w"][j] = g_nw[0]
            vec = _gate_vectors(p["gdn_a_log"][j], p["gdn_dt_bias"][j], hv_total)
            ba_blk = (conv_dim + hv_total * GDN_DIM) // LANES

            def gates_bwd(r0, bav, dgv, av, dv):
                _, vjp = jax.vjp(functools.partial(_gdn_gates, r0=r0, hv=hv_total), bav, av, dv)
                return vjp(dgv)

            dba, d_alog, d_dt = _rowwise(gates_bwd, [(s["proj"], LANES, ba_blk), dgates], [vec[0], vec[1]], [(LANES, BF16)],
                                         [(1, LANES), (1, LANES)], "gdn_gates_bwd")
            g["gdn_a_log"][j] = d_alog[0, hv_total:2 * hv_total]
            g["gdn_dt_bias"][j] = d_dt[0, hv_total:2 * hv_total]
            dconv_in, g["gdn_conv_w"][j], _ = tr.call(
                False, 2 * _us_dwconv(n_rows, conv_dim, GDN_CONV),
                lambda r: _dwconv_bwd(s["proj"], p["gdn_conv_w"][j], dcq, conv_dim, "dwconv4_bwd", du_dtype=BF16, ride=r))
            w_in = tr.weight("gdn_w_in", j)
            pad = jnp.zeros((n_rows, w_in.shape[1] - conv_dim - hv_total * GDN_DIM - LANES), BF16)
            dproj = jnp.concatenate([dconv_in, dz, dba, pad], axis=1)
            tr.give("gdn_w_in", j, mm(s["hn"], dproj, "tn", BF16, "mm_dw_gdn_in"))
            dhn = mm(dproj, w_in, "nt", F32, "mm_dx_gdn_in")
        dh, dhb, g["norm_mix"][i], _ = _rms_bwd(s["h"], p["norm_mix"][i], dhn, dh, "rms_bwd")
    g = {k: (jnp.stack(v) if isinstance(v, list) else v) for k, v in g.items()}
    return loss, dh, g


def _us_mm(m, n, k):
    return 2.0 * m * n * k / 8.0e8


def _us_rows(t):
    return t.shape[0] * t.shape[1] / 8.0e5


def _us_dwconv(n_rows, c, taps):
    return n_rows * c * (taps + 8) / 2.0e6


def _us_attn(qh):
    return qh.shape[0] * qh.shape[1] / 500.0


def _gate_vectors(a_log, dt_bias, hv):
    def place(t):
        return jnp.concatenate([jnp.zeros((hv,), F32), t, jnp.zeros((LANES - 2 * hv,), F32)]).reshape(1, LANES)

    return place(a_log), place(dt_bias)


GDN_IN_ALIGN = 512


def _gdn_group(w, hk):
    lead, kw = w.shape[:-1], hk * GDN_DIM
    q = w[..., :kw].reshape(*lead, hk, 1, GDN_DIM)
    k = w[..., kw:2 * kw].reshape(*lead, hk, 1, GDN_DIM)
    v = w[..., 2 * kw:4 * kw].reshape(*lead, hk, 2, GDN_DIM)
    return jnp.concatenate([q, k, v], axis=-2).reshape(*lead, 4 * kw)


def _gdn_ungroup(w, hk):
    lead, kw = w.shape[:-1], hk * GDN_DIM
    t = w.reshape(*lead, hk, 4, GDN_DIM)
    return jnp.concatenate([t[..., 0, :].reshape(*lead, kw), t[..., 1, :].reshape(*lead, kw),
                            t[..., 2:, :].reshape(*lead, 2 * kw)], axis=-1)


def _gdn_in_layout(w, hk):
    conv_dim = 4 * hk * GDN_DIM
    width = -(-w.shape[-1] // GDN_IN_ALIGN) * GDN_IN_ALIGN
    pad = jnp.zeros(w.shape[:-1] + (width - w.shape[-1],), w.dtype)
    return jnp.concatenate([_gdn_group(w[..., :conv_dim], hk), w[..., conv_dim:], pad], axis=-1)


def _gdn_in_natural(w, hk, in_width):
    conv_dim = 4 * hk * GDN_DIM
    return jnp.concatenate([_gdn_ungroup(w[..., :conv_dim], hk), w[..., conv_dim:in_width]], axis=-1)


def _exchange(slabs, name):
    n_src = len(slabs)

    def body(*refs):
        sems = refs[2 * n_src:]
        copies = [_exchange_copies(refs[t], refs[n_src + t], *sems[3 * t:3 * t + 3], slab=slabs[t]) for t in range(n_src)]
        for cps in copies:
            _exchange_start(cps)
        for cps in copies:
            _exchange_wait(cps)

    any_spec = pl.BlockSpec(memory_space=pl.ANY)
    return pl.pallas_call(
        body, name=name, out_shape=[slab.out_shape() for slab in slabs], in_specs=[any_spec] * n_src,
        out_specs=[any_spec] * n_src, scratch_shapes=[s for _ in slabs for s in _exchange_sems()],
    )(*[slab.array for slab in slabs])


BIG_COL = ("conv_w_pw1", "attn_w_qkv", "gdn_w_in", "ffn_w_gate", "ffn_w_up")
BIG_ROW = ("conv_w_pw2", "attn_w_o", "gdn_w_out", "ffn_w_down")
EXCHANGE_US_PER_BYTE = 11.4e-6
RIDE_PART_US = 150.0


class _Traffic:
    def __init__(self, shards, hk):
        self.shards, self.hk = shards, hk
        self.in_width = N_DEV * shards["gdn_w_in"].shape[-1]
        self.parts_of = {}
        for k, s in shards.items():
            parts = 1
            while (N_DEV * s.shape[1] * s.shape[2] * 2 * EXCHANGE_US_PER_BYTE / parts > RIDE_PART_US
                   and s.shape[1] % (32 * parts) == 0):
                parts *= 2
            self.parts_of[k] = parts
        depth = shards["ffn_w_down"].shape[0]
        order = []
        for i in range(depth):
            j = i // 3
            order += [[("conv_w_pw1", j), ("conv_w_pw2", j)], [("attn_w_qkv", j), ("attn_w_o", j)],
                      [("gdn_w_in", j), ("gdn_w_out", j)]][i % 3]
            order += [("ffn_w_gate", i), ("ffn_w_up", i), ("ffn_w_down", i)]
        self.wanted = [(k, i, part) for k, i in order for part in range(self.parts_of[k])]
        self.arrived = {}
        self.ready = {}
        self.owed = []
        self.received = {}

    def _us(self, k):
        s = self.shards[k]
        return N_DEV * s.shape[1] * s.shape[2] * 2 * EXCHANGE_US_PER_BYTE / self.parts_of[k]

    def _shard_part(self, item):
        k, i, part = item
        rows = self.shards[k].shape[1] // self.parts_of[k]
        return _Slab(self.shards[k], True, i, part * rows, rows)

    def _pick(self, queue, us_of, room):
        taken = []
        while queue and room >= 0.5 * us_of(queue[0]):
            room -= us_of(queue[0])
            taken.append(queue.pop(0))
        return taken

    def _run(self, forward, room, fn):
        if forward:
            taken = self._pick(self.wanted, lambda it: self._us(it[0]), room)
            items = [self._shard_part(it) for it in taken]
        else:
            taken = self._pick(self.owed, lambda it: self._us(it[0][0]), room)
            items = [slab for _, slab in taken]
        if not taken:
            return fn(None)
        ride = _Ride(items)
        out = fn(ride)
        for it, got in zip(taken, ride.outs):
            if forward:
                self.arrived[it] = got
            else:
                self.received[it[0]] = got
        return out

    def call(self, forward, room, fn):
        return self._run(forward, room, fn)

    def mm(self, forward, a, b, mode, out_dtype, name, **kw):
        m = a.shape[1] if mode == "tn" else a.shape[0]
        k = a.shape[0] if mode == "tn" else a.shape[1]
        n = b.shape[0] if mode == "nt" else b.shape[1]
        return self._run(forward, _us_mm(m, n, k), lambda ride: _mm(a, b, mode, out_dtype, name, ride=ride, **kw))

    def _natural(self, k, i):
        parts = self.parts_of[k]
        missing = [(k, i, part) for part in range(parts) if (k, i, part) not in self.arrived]
        if missing:
            for it in missing:
                self.wanted.remove(it)
            got = _exchange([self._shard_part(it) for it in missing], "gather_weights")
            for it, t in zip(missing, got):
                self.arrived[it] = t
        got = [self.arrived[(k, i, part)] for part in range(parts)]
        rows, c = got[0].shape[1], got[0].shape[2]
        if k in BIG_COL:
            return jnp.concatenate([t.transpose(1, 0, 2).reshape(rows, N_DEV * c) for t in got], axis=0)
        return jnp.stack(got, axis=1).reshape(N_DEV * parts * rows, c)

    def weight(self, k, i):
        if (k, i) not in self.ready:
            w = self._natural(k, i)
            self.ready[(k, i)] = _gdn_in_layout(w, self.hk) if k == "gdn_w_in" else w
        return self.ready[(k, i)]

    def give(self, k, i, grad):
        if k == "gdn_w_in":
            grad = _gdn_in_natural(grad, self.hk, self.in_width)
        r, c = self.shards[k].shape[1:]
        pieces = grad.reshape(r, N_DEV, c).transpose(1, 0, 2) if k in BIG_COL else grad.reshape(N_DEV, r, c)
        rows = r // self.parts_of[k]
        for part in range(self.parts_of[k]):
            self.owed.append(((k, i, part), _Slab(pieces, False, 0, part * rows, rows)))

    def gradient_parts(self, k):
        if self.owed:
            got = _exchange([slab for _, slab in self.owed], "scatter_grads")
            for (it, _), t in zip(self.owed, got):
                self.received[it] = t
            self.owed = []
        layers = self.shards[k].shape[0]
        return jnp.stack([jnp.concatenate([self.received[(k, i, part)] for part in range(self.parts_of[k])], axis=1)
                          for i in range(layers)])


def _cast_bf16(w, name):
    n, r, c = w.shape
    tr = _tile(r, max(16, (1 << 20) // c), 16)

    def body(w_ref, o_ref):
        o_ref[...] = w_ref[...].astype(BF16)

    return pl.pallas_call(
        body, name=name, grid=(n, r // tr), in_specs=[pl.BlockSpec((1, tr, c), lambda l, i: (l, i, 0))],
        out_specs=pl.BlockSpec((1, tr, c), lambda l, i: (l, i, 0)), out_shape=jax.ShapeDtypeStruct(w.shape, BF16),
        compiler_params=_cparams(("parallel", "parallel")),
    )(w)


def _adamw(w, g, m, v):
    m = ADAM_B1 * m + (1.0 - ADAM_B1) * g
    v = ADAM_B2 * v + (1.0 - ADAM_B2) * jnp.square(g)
    m_hat = m / (1.0 - ADAM_B1 ** ADAM_STEP)
    v_hat = v / (1.0 - ADAM_B2 ** ADAM_STEP)
    delta = -ADAM_LR * (m_hat / (jnp.sqrt(v_hat) + ADAM_EPS) + ADAM_WD * w)
    return delta, m, v


def _sum8_adam(parts, w, m, v, name):
    n, _, r, c = parts.shape
    tr = _tile(r, max(16, (1 << 18) // c), 16)
    blk = pl.BlockSpec((1, tr, c), lambda l, i: (l, i, 0))

    def body(p_ref, w_ref, m_ref, v_ref, g_ref, d_ref, mo_ref, vo_ref):
        g = p_ref[0, 0].astype(F32)
        for s in range(1, N_DEV):
            g = g + p_ref[0, s].astype(F32)
        delta, m2, v2 = _adamw(w_ref[0], g, m_ref[0], v_ref[0])
        g_ref[0], d_ref[0], mo_ref[0], vo_ref[0] = g, delta, m2, v2

    shp = jax.ShapeDtypeStruct(w.shape, F32)
    return pl.pallas_call(
        body, name=name, grid=(n, r // tr),
        in_specs=[pl.BlockSpec((1, N_DEV, tr, c), lambda l, i: (l, 0, i, 0)), blk, blk, blk],
        out_specs=[blk, blk, blk, blk], out_shape=[shp, shp, shp, shp],
        compiler_params=_cparams(("parallel", "parallel")),
    )(parts, w, m, v)


PACK_ROWS = 8


def _pack(arrays):
    flat = jnp.concatenate([a.reshape(-1).astype(F32) for a in arrays])
    unit = PACK_ROWS * LANES
    total = -(-flat.shape[0] // unit) * unit
    return jnp.concatenate([flat, jnp.zeros((total - flat.shape[0],), F32)]).reshape(-1, LANES)


def _unpack(packed, shapes):
    flat, out, pos = packed.reshape(-1), [], 0
    for s in shapes:
        size = math.prod(s)
        out.append(flat[pos:pos + size].reshape(s))
        pos += size
    return out


SMALL_SHARDED =("meta_tokens", "conv_b_pw1", "conv_w_dw", "conv_b_dw", "conv_ln_g", "conv_ln_b", "conv_b_pw2", "gdn_conv_w")
REPLICATED = ("norm_mix", "norm_ffn", "norm_final", "attn_b_qkv", "attn_sinks", "attn_b_o", "gdn_a_log", "gdn_dt_bias", "gdn_norm_w")
WEIGHTS = ("meta_tokens", "norm_mix", "norm_ffn", "norm_final", "conv_w_pw1", "conv_b_pw1", "conv_w_dw", "conv_b_dw", "conv_ln_g",
           "conv_ln_b", "conv_w_pw2", "conv_b_pw2", "attn_w_qkv", "attn_b_qkv", "attn_sinks", "attn_w_o", "attn_b_o", "gdn_w_in",
           "gdn_conv_w", "gdn_a_log", "gdn_dt_bias", "gdn_norm_w", "gdn_w_out", "ffn_w_gate", "ffn_w_up", "ffn_w_down")


def kernel(x, meta_tokens, norm_mix, norm_ffn, norm_final, conv_w_pw1, conv_b_pw1, conv_w_dw, conv_b_dw, conv_ln_g, conv_ln_b, conv_w_pw2, conv_b_pw2, attn_w_qkv, attn_b_qkv, attn_sinks, attn_w_o, attn_b_o, gdn_w_in, gdn_conv_w, gdn_a_log, gdn_dt_bias, gdn_norm_w, gdn_w_out, ffn_w_gate, ffn_w_up, ffn_w_down, loss_target, m_meta_tokens, m_norm_mix, m_norm_ffn, m_norm_final, m_conv_w_pw1, m_conv_b_pw1, m_conv_w_dw, m_conv_b_dw, m_conv_ln_g, m_conv_ln_b, m_conv_w_pw2, m_conv_b_pw2, m_attn_w_qkv, m_attn_b_qkv, m_attn_sinks, m_attn_w_o, m_attn_b_o, m_gdn_w_in, m_gdn_conv_w, m_gdn_a_log, m_gdn_dt_bias, m_gdn_norm_w, m_gdn_w_out, m_ffn_w_gate, m_ffn_w_up, m_ffn_w_down, v_meta_tokens, v_norm_mix, v_norm_ffn, v_norm_final, v_conv_w_pw1, v_conv_b_pw1, v_conv_w_dw, v_conv_b_dw, v_conv_ln_g, v_conv_ln_b, v_conv_w_pw2, v_conv_b_pw2, v_attn_w_qkv, v_attn_b_qkv, v_attn_sinks, v_attn_w_o, v_attn_b_o, v_gdn_w_in, v_gdn_conv_w, v_gdn_a_log, v_gdn_dt_bias, v_gdn_norm_w, v_gdn_w_out, v_ffn_w_gate, v_ffn_w_up, v_ffn_w_down):
    a = dict(locals())
    me = 4 * lax.axis_index("x") + 2 * lax.axis_index("y") + lax.axis_index("c")
    d = x.shape[-1]

    hk = d // GDN_DIM
    full = {k: a[k] for k in REPLICATED}
    shard_shapes = [a[k].shape for k in SMALL_SHARDED]
    (got,) = _exchange([_Slab(_pack([a[k] for k in SMALL_SHARDED])[None], True)], "gather_small")
    per_dev = [_unpack(got[s], shard_shapes) for s in range(N_DEV)]
    for i, k in enumerate(SMALL_SHARDED):
        st = jnp.stack([per_dev[s][i] for s in range(N_DEV)], axis=-2)
        full[k] = st.reshape(st.shape[:-2] + (N_DEV * st.shape[-1],))
    traffic = _Traffic({k: _cast_bf16(a[k], "cast_bf16") for k in BIG_COL + BIG_ROW}, hk)

    h0 = jnp.concatenate([jnp.zeros((PAD_LEN, d), F32), full["meta_tokens"], x[0]], axis=0)
    target = jnp.concatenate([jnp.zeros((FRONT, d), F32), loss_target[0]], axis=0)
    loss, dh0, g = _local_step(h0, target, {**full, "gdn_conv_w": _gdn_group(full["gdn_conv_w"], hk)}, traffic)
    g["gdn_conv_w"] = _gdn_ungroup(g["gdn_conv_w"], hk)
    g["meta_tokens"] = dh0[PAD_LEN:FRONT]
    loss = lax.psum(loss, AXES)
    grad_x = dh0[FRONT:][None]

    grads, deltas, new_m, new_v = {}, {}, {}, {}
    for k in BIG_COL + BIG_ROW:
        grads[k], deltas[k], new_m[k], new_v[k] = _sum8_adam(traffic.gradient_parts(k), a[k], a["m_" + k], a["v_" + k], "sum8_adamw")

    small = SMALL_SHARDED + REPLICATED
    full_shapes = [full[k].shape for k in small]
    (got,) = _exchange([_Slab(_pack([g[k] for k in small])[None], True)], "gather_small_grads")
    (total,) = _rowwise(lambda r0, *t: (functools.reduce(lambda p, q: p + q, t),), [got[s] for s in range(N_DEV)], [],
                        [(LANES, F32)], [], "sum8_small", tm=got.shape[1])
    for k, t in zip(small, _unpack(total, full_shapes)):
        if k in SMALL_SHARDED:
            c = a[k].shape[-1]
            t = lax.dynamic_index_in_dim(t.reshape(t.shape[:-1] + (N_DEV, c)), me, axis=t.ndim - 1, keepdims=False)
        grads[k] = t
    shapes = [a[k].shape for k in small]
    packed = [_pack([src[k] for k in small]) for src in (grads, a, {k: a["m_" + k] for k in small}, {k: a["v_" + k] for k in small})]

    def small_adam(r0, gv, wv, mv, vv):
        return _adamw(wv, gv, mv, vv)

    outs = _rowwise(small_adam, packed, [], [(LANES, F32)] * 3, [], "adamw_small", tm=packed[0].shape[0])
    for dst, o in zip((deltas, new_m, new_v), outs):
        for k, t in zip(small, _unpack(o, shapes)):
            dst[k] = t

    return (loss, grad_x, *[grads[k] for k in WEIGHTS], *[deltas[k] for k in WEIGHTS], *[new_m[k] for k in WEIGHTS],
            *[new_v[k] for k in WEIGHTS])
```
